```python
import math
import jax, jax.numpy as jnp
from jax import lax
import numpy as np

D_MODEL = 1024
BATCH = 2
SEQ = 8192
DEPTH = 4


N_EVEN = (DEPTH + 1) // 2
N_ODD = DEPTH // 2
N_MEM = 256
RMS_EPS = 1e-6

D_POOL = D_MODEL // 2
POOL_WINDOWS = (2, 4, 8, 16)
N_POOL_GROUPS = len(POOL_WINDOWS)
POOL_GROUP_DIM = D_POOL // N_POOL_GROUPS
D_CONV = D_MODEL // 2
CONV_HEADS = 8
CONV_WIDTH = 3
EVEN_IN = D_POOL + 3 * D_CONV
EVEN_MIX = D_POOL + D_CONV

D_S5 = D_MODEL // 2
S5_GROUP_DIM = 16
S5_GROUPS = D_S5 // S5_GROUP_DIM
S5_STATE = 64
S5_DT_MIN = 1e-3
S5_DT_MAX = 1e-1
D_HYENA = D_MODEL // 2
HYENA_HEADS = 8
HYENA_ORDER = 2
HYENA_BANDS = 16
HYENA_EMB = 2 * HYENA_BANDS + 1
HYENA_FFN = 64
HYENA_TARGET = 1e-2
HYENA_SHORT_DECAY_PCT = 0.3
HYENA_LONG_DECAY_PCT = 1.5
ODD_IN = D_S5 + (HYENA_ORDER + 1) * D_HYENA
ODD_MIX = D_S5 + D_HYENA

XA_HEADS = 4
XA_HEAD_DIM = D_MODEL // XA_HEADS
D_FF = 4 * D_MODEL

kernel_name = 'hybrid_pool_conv_s5_hyena_encoder'


def rms_norm(x, g):
    xf = x.astype(jnp.float32)
    y = xf * lax.rsqrt(jnp.mean(xf * xf, axis=-1, keepdims=True) + RMS_EPS)
    return (y * g.astype(jnp.float32)).astype(x.dtype)


def centred_conv3(u, w):
    up = jnp.pad(u, ((0, 0), (1, 1), (0, 0)))
    return w[0] * up[:, :-2] + w[1] * up[:, 1:-1] + w[2] * up[:, 2:]


def centred_window_mean(u, window):
    L = u.shape[1]
    h = window // 2
    uf = u.astype(jnp.float32)
    cs = jnp.cumsum(jnp.pad(uf, ((0, 0), (h + 1, h), (0, 0))), axis=1)
    s = cs[:, window:window + L] - cs[:, :L]
    t = jnp.arange(L)
    cnt = (jnp.minimum(t + h, L) - jnp.maximum(t - h, 0)).astype(jnp.float32)
    return (s / cnt[None, :, None]).astype(u.dtype)


def pool_mixer(u, w_group, scale):
    B, L, _ = u.shape
    ug = u.reshape(B, L, N_POOL_GROUPS, POOL_GROUP_DIM)
    pooled = jnp.stack([centred_window_mean(ug[:, :, g], win) for g, win in enumerate(POOL_WINDOWS)], axis=2) - ug
    y = jnp.einsum('blgc,gcd->blgd', pooled, w_group).reshape(B, L, D_POOL)
    return y * scale


def short_conv_mixer(b_gate, c_gate, h, conv_w):
    return b_gate * centred_conv3(c_gate * h, conv_w)


def _linear_recurrence(e1, e2):
    a1, b1 = e1
    a2, b2 = e2
    return a1 * a2, a2 * b1 + b2


def s5_mixer(u, lam_re, lam_im, log_dt, b_re, b_im, c_re, c_im, d_skip, w_glu):
    B, L, _ = u.shape
    f32 = jnp.float32
    uf = u.astype(f32)
    lam = lax.complex(jnp.minimum(lam_re.astype(f32), -1e-4), lam_im.astype(f32))
    dt = jnp.exp(log_dt.astype(f32))[..., None]
    lam_bar = jnp.exp(lam * dt)
    coef = (lam_bar - 1.0) / lam
    b_mat = lax.complex(b_re.astype(f32), b_im.astype(f32))
    c_mat = lax.complex(c_re.astype(f32), c_im.astype(f32))
    ug = uf.reshape(B, L, S5_GROUPS, S5_GROUP_DIM).astype(jnp.complex64)
    bu = jnp.einsum('blgh,gnh->blgn', ug, b_mat)
    ys = []
    for direction in range(2):
        a = jnp.broadcast_to(lam_bar[direction], bu.shape)
        _, states = lax.associative_scan(_linear_recurrence, (a, coef[direction] * bu), reverse=(direction == 1), axis=1)
        ys.append(jnp.einsum('blgn,ghn->blgh', states, c_mat[direction]).real)
    y = (ys[0] + ys[1]).reshape(B, L, D_S5) + d_skip.astype(f32) * uf
    g = jax.nn.gelu(y)
    out = g * jax.nn.sigmoid(g @ w_glu.astype(f32))
    return out.astype(u.dtype)


def hyena_filters(L, w1, b1, w2, b2, w3, freq):
    f32 = jnp.float32
    t_norm = jnp.linspace(0.0, 1.0, L, dtype=f32)[:, None]
    bands = jnp.linspace(1e-4, HYENA_BANDS - 1, HYENA_BANDS, dtype=f32)[None, :]
    ang = (2.0 * math.pi / L) * jnp.arange(L, dtype=f32)[:, None] * bands
    z = jnp.concatenate([t_norm, jnp.cos(ang), -jnp.sin(ang)], axis=-1)
    fr = freq.astype(f32)
    h = jnp.sin(fr * (z @ w1.astype(f32) + b1.astype(f32)))
    h = jnp.sin(fr * (h @ w2.astype(f32) + b2.astype(f32)))
    h = (h @ w3.astype(f32)).reshape(L, HYENA_ORDER, 2, D_HYENA)
    deltas = jnp.abs(jnp.linspace(math.log(HYENA_TARGET) / HYENA_LONG_DECAY_PCT, math.log(HYENA_TARGET) / HYENA_SHORT_DECAY_PCT, D_HYENA, dtype=f32))
    h = h * jnp.exp(-t_norm * deltas)[:, None, None, :]
    h = h / (jnp.sum(jnp.abs(h), axis=(0, 2), keepdims=True) + 1e-6)
    fwd = h[:, :, 0]
    bwd = h[:, :, 1]
    k = jnp.concatenate([fwd[:1] + bwd[:1], fwd[1:], jnp.zeros_like(fwd[:1]), bwd[:0:-1]], axis=0)
    return jnp.fft.rfft(k, n=2 * L, axis=0)


def fft_long_conv(u, k_f, bias):
    L = u.shape[1]
    uf = u.astype(jnp.float32)
    y = jnp.fft.irfft(jnp.fft.rfft(uf, n=2 * L, axis=1) * k_f[None], n=2 * L, axis=1)[:, :L]
    return (y + uf * bias.astype(jnp.float32)).astype(u.dtype)


def hyena_mixer(p, short_w, short_b, w1, b1, w2, b2, w3, freq, bias):
    L = p.shape[1]
    p = centred_conv3(p, short_w) + short_b
    g_out, g_mid, v = jnp.split(p, 3, axis=-1)
    k_f = hyena_filters(L, w1, b1, w2, b2, w3, freq)
    z = g_mid * fft_long_conv(v, k_f[:, 0], bias[0])
    return g_out * fft_long_conv(z, k_f[:, 1], bias[1])


def memory_cross_attention(xn, memn, wq, wk, wv, wo):
    B, L, _ = xn.shape
    q = (xn @ wq).reshape(B, L, XA_HEADS, XA_HEAD_DIM)
    k = (memn @ wk).reshape(B, -1, XA_HEADS, XA_HEAD_DIM)
    v = (memn @ wv).reshape(B, -1, XA_HEADS, XA_HEAD_DIM)
    s = jnp.einsum('blhd,bmhd->bhlm', q, k).astype(jnp.float32) * (XA_HEAD_DIM ** -0.5)
    pr = jax.nn.softmax(s, axis=-1).astype(v.dtype)
    o = jnp.einsum('bhlm,bmhd->blhd', pr, v).reshape(B, L, D_MODEL)
    return o @ wo


def squared_relu_mlp(xn, w1, w2):
    h = jax.nn.relu(xn @ w1)
    return (h * h) @ w2


def setup_inputs(seed: int = 0) -> dict:
    key = jax.random.key(seed)
    keys = iter(jax.random.split(key, 48))
    f32 = jnp.float32

    def normal(shape, scale):
        return scale * jax.random.normal(next(keys), shape, f32)

    def gain(shape):
        return 1.0 + normal(shape, 0.05)

    x = normal((BATCH, SEQ, D_MODEL), 1.0)
    mem = normal((BATCH, N_MEM, D_MODEL), 1.0)
    norm_mix = gain((DEPTH, 2, D_MODEL))
    norm_xattn = gain((DEPTH, 2, D_MODEL))
    norm_mem = gain((DEPTH, D_MODEL))
    norm_mlp = gain((DEPTH, 2, D_MODEL))
    xa_wq = normal((DEPTH, D_MODEL, D_MODEL), D_MODEL ** -0.5)
    xa_wk = normal((DEPTH, D_MODEL, D_MODEL), D_MODEL ** -0.5)
    xa_wv = normal((DEPTH, D_MODEL, D_MODEL), D_MODEL ** -0.5)
    xa_wo = normal((DEPTH, D_MODEL, D_MODEL), D_MODEL ** -0.5)
    mlp_w1 = normal((DEPTH, D_MODEL, D_FF), D_MODEL ** -0.5)
    mlp_w2 = normal((DEPTH, D_FF, D_MODEL), D_FF ** -0.5)
    ev_w_in = normal((N_EVEN, D_MODEL, EVEN_IN), D_MODEL ** -0.5)
    ev_pool_w = normal((N_EVEN, N_POOL_GROUPS, POOL_GROUP_DIM, POOL_GROUP_DIM), POOL_GROUP_DIM ** -0.5)
    ev_pool_scale = gain((N_EVEN, D_POOL))
    ev_conv_w = normal((N_EVEN, CONV_WIDTH, D_CONV), CONV_WIDTH ** -0.5)
    ev_w_out = normal((N_EVEN, EVEN_MIX, D_MODEL), EVEN_MIX ** -0.5)
    od_w_in = normal((N_ODD, D_MODEL, ODD_IN), D_MODEL ** -0.5)
    od_s5_lambda_re = -0.5 + normal((N_ODD, 2, S5_GROUPS, S5_STATE), 0.01)
    od_s5_lambda_im = math.pi * jnp.arange(S5_STATE, dtype=f32) + normal((N_ODD, 2, S5_GROUPS, S5_STATE), 0.01)
    od_s5_log_dt = jax.random.uniform(next(keys), (N_ODD, 2, S5_GROUPS), f32, math.log(S5_DT_MIN), math.log(S5_DT_MAX))
    od_s5_b_re = normal((N_ODD, S5_GROUPS, S5_STATE, S5_GROUP_DIM), (2 * S5_GROUP_DIM) ** -0.5)
    od_s5_b_im = normal((N_ODD, S5_GROUPS, S5_STATE, S5_GROUP_DIM), (2 * S5_GROUP_DIM) ** -0.5)
    od_s5_c_re = normal((N_ODD, 2, S5_GROUPS, S5_GROUP_DIM, S5_STATE), (2 * S5_STATE) ** -0.5)
    od_s5_c_im = normal((N_ODD, 2, S5_GROUPS, S5_GROUP_DIM, S5_STATE), (2 * S5_STATE) ** -0.5)
    od_s5_d = normal((N_ODD, D_S5), 1.0)
    od_s5_w_glu = normal((N_ODD, D_S5, D_S5), D_S5 ** -0.5)
    od_hy_short_w = normal((N_ODD, CONV_WIDTH, (HYENA_ORDER + 1) * D_HYENA), CONV_WIDTH ** -0.5)
    od_hy_short_b = normal((N_ODD, (HYENA_ORDER + 1) * D_HYENA), 0.02)
    od_hy_w1 = normal((N_ODD, HYENA_EMB, HYENA_FFN), HYENA_EMB ** -0.5)
    od_hy_b1 = normal((N_ODD, HYENA_FFN), 0.1)
    od_hy_w2 = normal((N_ODD, HYENA_FFN, HYENA_FFN), HYENA_FFN ** -0.5)
    od_hy_b2 = normal((N_ODD, HYENA_FFN), 0.1)
    od_hy_w3 = normal((N_ODD, HYENA_FFN, HYENA_ORDER * 2 * D_HYENA), HYENA_FFN ** -0.5)
    od_hy_freq = gain((N_ODD, HYENA_FFN))
    od_hy_bias = normal((N_ODD, HYENA_ORDER, D_HYENA), 1.0)
    od_w_out = normal((N_ODD, ODD_MIX, D_MODEL), ODD_MIX ** -0.5)
    return {
        'x': x, 'mem': mem,
        'norm_mix': norm_mix, 'norm_xattn': norm_xattn, 'norm_mem': norm_mem, 'norm_mlp': norm_mlp,
        'xa_wq': xa_wq, 'xa_wk': xa_wk, 'xa_wv': xa_wv, 'xa_wo': xa_wo,
        'mlp_w1': mlp_w1, 'mlp_w2': mlp_w2,
        'ev_w_in': ev_w_in, 'ev_pool_w': ev_pool_w, 'ev_pool_scale': ev_pool_scale,
        'ev_conv_w': ev_conv_w, 'ev_w_out': ev_w_out,
        'od_w_in': od_w_in, 'od_s5_lambda_re': od_s5_lambda_re, 'od_s5_lambda_im': od_s5_lambda_im,
        'od_s5_log_dt': od_s5_log_dt, 'od_s5_b_re': od_s5_b_re, 'od_s5_b_im': od_s5_b_im,
        'od_s5_c_re': od_s5_c_re, 'od_s5_c_im': od_s5_c_im, 'od_s5_d': od_s5_d, 'od_s5_w_glu': od_s5_w_glu,
        'od_hy_short_w': od_hy_short_w, 'od_hy_short_b': od_hy_short_b,
        'od_hy_w1': od_hy_w1, 'od_hy_b1': od_hy_b1, 'od_hy_w2': od_hy_w2, 'od_hy_b2': od_hy_b2,
        'od_hy_w3': od_hy_w3, 'od_hy_freq': od_hy_freq, 'od_hy_bias': od_hy_bias,
        'od_w_out': od_w_out,
    }


def reference(x, mem, norm_mix, norm_xattn, norm_mem, norm_mlp, xa_wq, xa_wk, xa_wv, xa_wo, mlp_w1, mlp_w2, ev_w_in, ev_pool_w, ev_pool_scale, ev_conv_w, ev_w_out, od_w_in, od_s5_lambda_re, od_s5_lambda_im, od_s5_log_dt, od_s5_b_re, od_s5_b_im, od_s5_c_re, od_s5_c_im, od_s5_d, od_s5_w_glu, od_hy_short_w, od_hy_short_b, od_hy_w1, od_hy_b1, od_hy_w2, od_hy_b2, od_hy_w3, od_hy_freq, od_hy_bias, od_w_out):
    for i in range(DEPTH):
        j = i // 2
        h = rms_norm(x, norm_mix[i, 0])
        if i % 2 == 0:
            p = h @ ev_w_in[j]
            b_gate, c_gate, hv = jnp.split(p[..., D_POOL:], 3, axis=-1)
            y_pool = pool_mixer(p[..., :D_POOL], ev_pool_w[j], ev_pool_scale[j])
            y_conv = short_conv_mixer(b_gate, c_gate, hv, ev_conv_w[j])
            mix = jnp.concatenate([y_pool, y_conv], axis=-1) @ ev_w_out[j]
        else:
            p = h @ od_w_in[j]
            y_s5 = s5_mixer(p[..., :D_S5], od_s5_lambda_re[j], od_s5_lambda_im[j], od_s5_log_dt[j], od_s5_b_re[j], od_s5_b_im[j], od_s5_c_re[j], od_s5_c_im[j], od_s5_d[j], od_s5_w_glu[j])
            y_hy = hyena_mixer(p[..., D_S5:], od_hy_short_w[j], od_hy_short_b[j], od_hy_w1[j], od_hy_b1[j], od_hy_w2[j], od_hy_b2[j], od_hy_w3[j], od_hy_freq[j], od_hy_bias[j])
            mix = jnp.concatenate([y_s5, y_hy], axis=-1) @ od_w_out[j]
        x = x + rms_norm(mix, norm_mix[i, 1])
        h = rms_norm(x, norm_xattn[i, 0])
        m = rms_norm(mem, norm_mem[i])
        x = x + rms_norm(memory_cross_attention(h, m, xa_wq[i], xa_wk[i], xa_wv[i], xa_wo[i]), norm_xattn[i, 1])
        h = rms_norm(x, norm_mlp[i, 0])
        x = x + rms_norm(squared_relu_mlp(h, mlp_w1[i], mlp_w2[i]), norm_mlp[i, 1])
    return x
```

```python
import functools
import math

import numpy as np
import jax
import jax.numpy as jnp
from jax import lax
from jax.experimental import pallas as pl
from jax.experimental.pallas import tpu as pltpu

F32 = jnp.float32
BF16 = jnp.bfloat16

D_MODEL = 1024
N_MEM = 256
RMS_EPS = 1e-6
D_POOL = 512
POOL_WINDOWS = (2, 4, 8, 16)
POOL_GROUP_DIM = 128
POOL_HALO = 8
D_CONV = 512
D_S5 = 512
S5_GROUP_DIM = 16
S5_GROUPS = 32
S5_STATE = 64
S5_CHUNK = 16
D_HYENA = 512
HYENA_BANDS = 16
HYENA_FFN = 64
HYENA_TARGET = 1e-2
HYENA_SHORT_DECAY_PCT = 0.3
HYENA_LONG_DECAY_PCT = 1.5
XA_HEADS = 4
XA_HEAD_DIM = 256
D_FF = 4096

LANES = 128
SUBLANES = 8
DFT_FAST = 128
VMEM_LIMIT = 56 * 1024 * 1024


def _cparams(*sem):
    return pltpu.CompilerParams(dimension_semantics=sem, vmem_limit_bytes=VMEM_LIMIT)


def _rms(xf, g):
    ms = jnp.mean(xf * xf, axis=-1, keepdims=True)
    return xf * lax.rsqrt(ms + RMS_EPS) * g


def _dot(a, b):
    return jnp.dot(a, b, preferred_element_type=F32)


def _norm_matmul_kernel(x_ref, g_ref, w_ref, o_ref):
    xn = _rms(x_ref[...], g_ref[...]).astype(BF16)
    o_ref[...] = _dot(xn, w_ref[...]).astype(o_ref.dtype)


def norm_matmul(x2d, g, w, *, tm, out_dtype=F32):
    m, d = x2d.shape
    n = w.shape[1]
    return pl.pallas_call(
        _norm_matmul_kernel,
        grid=(m // tm,),
        in_specs=[
            pl.BlockSpec((tm, d), lambda i: (i, 0)),
            pl.BlockSpec((1, d), lambda i: (0, 0)),
            pl.BlockSpec((d, n), lambda i: (0, 0)),
        ],
        out_specs=pl.BlockSpec((tm, n), lambda i: (i, 0)),
        out_shape=jax.ShapeDtypeStruct((m, n), out_dtype),
        compiler_params=_cparams("parallel"),
        name="norm_matmul",
    )(x2d, g.reshape(1, d), w)


def _xattn_kernel(x_ref, kv_ref, wq_ref, wo_ref, g1_ref, g2_ref, o_ref):
    x = x_ref[0]
    xn = _rms(x, g1_ref[...]).astype(BF16)
    q = (_dot(xn, wq_ref[...]) * (XA_HEAD_DIM ** -0.5)).astype(BF16)
    heads = []
    for h in range(XA_HEADS):
        lo = h * XA_HEAD_DIM
        qh = q[:, lo:lo + XA_HEAD_DIM]
        kh = kv_ref[0, :, lo:lo + XA_HEAD_DIM]
        vh = kv_ref[0, :, D_MODEL + lo:D_MODEL + lo + XA_HEAD_DIM]
        s = lax.dot_general(qh, kh, (((1,), (1,)), ((), ())), preferred_element_type=F32)
        e = jnp.exp(s - jnp.max(s, axis=-1, keepdims=True))
        p = e / jnp.sum(e, axis=-1, keepdims=True)
        heads.append(_dot(p.astype(BF16), vh).astype(BF16))
    o = jnp.concatenate(heads, axis=-1)
    y = _dot(o, wo_ref[...])
    o_ref[0] = x + _rms(y, g2_ref[...])


def xattn_block(x, kv, wq, wo, g1, g2, *, tm):
    b, l, d = x.shape
    return pl.pallas_call(
        _xattn_kernel,
        grid=(b, l // tm),
        in_specs=[
            pl.BlockSpec((1, tm, d), lambda bi, i: (bi, i, 0)),
            pl.BlockSpec((1, N_MEM, 2 * d), lambda bi, i: (bi, 0, 0)),
            pl.BlockSpec((d, d), lambda bi, i: (0, 0)),
            pl.BlockSpec((d, d), lambda bi, i: (0, 0)),
            pl.BlockSpec((1, d), lambda bi, i: (0, 0)),
            pl.BlockSpec((1, d), lambda bi, i: (0, 0)),
        ],
        out_specs=pl.BlockSpec((1, tm, d), lambda bi, i: (bi, i, 0)),
        out_shape=jax.ShapeDtypeStruct(x.shape, F32),
        compiler_params=_cparams("parallel", "parallel"),
        name="xattn_block",
    )(x, kv, wq, wo, g1.reshape(1, d), g2.reshape(1, d))


def _mlp_kernel(x_ref, w1_ref, w2_ref, g1_ref, g2_ref, o_ref, xn_ref, acc_ref):
    j = pl.program_id(1)

    @pl.when(j == 0)
    def _():
        xn_ref[...] = _rms(x_ref[...], g1_ref[...]).astype(BF16)
        acc_ref[...] = jnp.zeros_like(acc_ref)

    h = jnp.maximum(_dot(xn_ref[...], w1_ref[...]), 0.0)
    acc_ref[...] += _dot((h * h).astype(BF16), w2_ref[...])

    @pl.when(j == pl.num_programs(1) - 1)
    def _():
        o_ref[...] = x_ref[...] + _rms(acc_ref[...], g2_ref[...])


def mlp_block(x2d, w1, w2, g1, g2, *, tm, tf):
    m, d = x2d.shape
    ff = w1.shape[1]
    return pl.pallas_call(
        _mlp_kernel,
        grid=(m // tm, ff // tf),
        in_specs=[
            pl.BlockSpec((tm, d), lambda i, j: (i, 0)),
            pl.BlockSpec((d, tf), lambda i, j: (0, j)),
            pl.BlockSpec((tf, d), lambda i, j: (j, 0)),
            pl.BlockSpec((1, d), lambda i, j: (0, 0)),
            pl.BlockSpec((1, d), lambda i, j: (0, 0)),
        ],
        out_specs=pl.BlockSpec((tm, d), lambda i, j: (i, 0)),
        out_shape=jax.ShapeDtypeStruct((m, d), F32),
        scratch_shapes=[pltpu.VMEM((tm, d), BF16), pltpu.VMEM((tm, d), F32)],
        compiler_params=_cparams("parallel", "arbitrary"),
        name="mlp_block",
    )(x2d, w1, w2, g1.reshape(1, d), g2.reshape(1, d))


def _halo_specs(tm, seq_len, width, col):
    r = tm // POOL_HALO
    last = seq_len // POOL_HALO - 1

    def prev_map(bi, i, *_):
        return (bi, jnp.maximum(i * r - 1, 0), col(*_))

    def main_map(bi, i, *_):
        return (bi, i, col(*_))

    def next_map(bi, i, *_):
        return (bi, jnp.minimum((i + 1) * r, last), col(*_))

    return [
        pl.BlockSpec((1, POOL_HALO, width), prev_map),
        pl.BlockSpec((1, tm, width), main_map),
        pl.BlockSpec((1, POOL_HALO, width), next_map),
    ]


def _with_halo(prev_ref, main_ref, next_ref):
    i = pl.program_id(1)
    prev = jnp.where(i > 0, prev_ref[0], 0.0)
    nxt = jnp.where(i < pl.num_programs(1) - 1, next_ref[0], 0.0)
    return jnp.concatenate([prev, main_ref[0], nxt], axis=0)


def _shift_rows(v, k):
    return pltpu.roll(v, k % v.shape[0], 0)


def _even_mixer_kernel(pp_ref, p_ref, pn_ref, x_ref, wg_ref, ps_ref, cw_ref, wo_ref, g_ref, o_ref,
                       *, tm, seq_len):
    ext = _with_halo(pp_ref, p_ref, pn_ref)
    lo, hi = POOL_HALO, POOL_HALO + tm
    t = pl.program_id(1) * tm + lax.broadcasted_iota(jnp.int32, (tm, 1), 0)
    parts = []
    for gi, win in enumerate(POOL_WINDOWS):
        half = win // 2
        u = ext[:, gi * POOL_GROUP_DIM:(gi + 1) * POOL_GROUP_DIM]
        s = u + _shift_rows(u, 1)
        step = 1
        while 2 * step < win:
            s = _shift_rows(s, step) + _shift_rows(s, -step)
            step *= 2
        cnt = (jnp.minimum(t + half, seq_len) - jnp.maximum(t - half, 0)).astype(F32)
        pooled = s[lo:hi] / cnt - u[lo:hi]
        y = _dot(pooled.astype(BF16), wg_ref[gi])
        parts.append((y * ps_ref[:, gi * POOL_GROUP_DIM:(gi + 1) * POOL_GROUP_DIM]).astype(BF16))
    b_gate = ext[lo:hi, D_POOL:D_POOL + D_CONV]
    ch = ext[:, D_POOL + D_CONV:D_POOL + 2 * D_CONV] * ext[:, D_POOL + 2 * D_CONV:D_POOL + 3 * D_CONV]
    conv = cw_ref[0:1, :] * _shift_rows(ch, 1) + cw_ref[1:2, :] * ch + cw_ref[2:3, :] * _shift_rows(ch, -1)
    parts.append((b_gate * conv[lo:hi]).astype(BF16))
    mix = _dot(jnp.concatenate(parts, axis=-1), wo_ref[...])
    o_ref[0] = x_ref[0] + _rms(mix, g_ref[...])


def even_mixer(p, x, w_group, pool_scale, conv_w, w_out, g, *, tm):
    b, l, d = x.shape
    width = p.shape[-1]
    kern = functools.partial(_even_mixer_kernel, tm=tm, seq_len=l)
    return pl.pallas_call(
        kern,
        grid=(b, l // tm),
        in_specs=_halo_specs(tm, l, width, lambda: 0) + [
            pl.BlockSpec((1, tm, d), lambda bi, i: (bi, i, 0)),
            pl.BlockSpec(w_group.shape, lambda bi, i: (0, 0, 0)),
            pl.BlockSpec((1, D_POOL), lambda bi, i: (0, 0)),
            pl.BlockSpec((3, D_CONV), lambda bi, i: (0, 0)),
            pl.BlockSpec((d, d), lambda bi, i: (0, 0)),
            pl.BlockSpec((1, d), lambda bi, i: (0, 0)),
        ],
        out_specs=pl.BlockSpec((1, tm, d), lambda bi, i: (bi, i, 0)),
        out_shape=jax.ShapeDtypeStruct(x.shape, F32),
        compiler_params=_cparams("parallel", "parallel"),
        name="even_mixer",
    )(p, p, p, x, w_group, pool_scale.reshape(1, D_POOL), conv_w, w_out, g.reshape(1, d))


def s5_matrices(lam_re, lam_im, log_dt, b_re, b_im, c_re, c_im):
    tc, hd, n = S5_CHUNK, S5_GROUP_DIM, S5_STATE
    hp = lax.Precision.HIGHEST
    lam = lax.complex(jnp.minimum(lam_re, -1e-4), lam_im)
    dt = jnp.exp(log_dt)[..., None]
    lam_bar = jnp.exp(lam * dt)
    coef = (lam_bar - 1.0) / lam
    taus = jnp.arange(tc + 1, dtype=F32)[:, None, None, None]
    pw = jnp.exp(lam[None] * dt[None] * taus)
    b_mat = lax.complex(b_re, b_im)
    c_mat = lax.complex(c_re, c_im)
    cb = coef[..., None] * b_mat[None]
    m = jnp.einsum('dghn,tdgn,dgnk->dtghk', c_mat, pw[:tc], cb, precision=hp).real
    m_fwd, m_bwd = m[0], m[1]
    diag = m_fwd[0] + m_bwd[0]
    lag_table = jnp.concatenate([m_bwd[:0:-1], diag[None], m_fwd[1:]], axis=0)
    lag = np.arange(tc)[None, :] - np.arange(tc)[:, None] + tc - 1
    intra = lag_table[lag]
    intra = jnp.transpose(intra, (2, 0, 4, 1, 3)).reshape(S5_GROUPS, tc * hd, tc * hd)
    sf = pw[tc - 1::-1, 0][:tc, :, :, None] * cb[0][None]
    sb = pw[:tc, 1][:, :, :, None] * cb[1][None]

    def rows(z):
        return jnp.transpose(z, (1, 0, 3, 2)).reshape(S5_GROUPS, tc * hd, n)

    summ = jnp.concatenate([rows(sf.real), rows(sb.real), rows(sf.imag), rows(sb.imag)], axis=-1)
    qf = c_mat[0][None] * pw[1:tc + 1, 0][:, :, None, :]
    qb = c_mat[1][None] * pw[tc:0:-1, 1][:, :, None, :]

    def cols(z):
        return jnp.transpose(z, (1, 3, 0, 2)).reshape(S5_GROUPS, n, tc * hd)

    carry = jnp.concatenate([cols(qf.real), cols(qb.real), cols(-qf.imag), cols(-qb.imag)], axis=1)
    a_t = pw[tc]
    a_re = jnp.concatenate([a_t[0].real, a_t[1].real], axis=-1)
    a_im = jnp.concatenate([a_t[0].imag, a_t[1].imag], axis=-1)
    return intra, summ, carry, a_re, a_im


def _s5_summary_kernel(u_ref, w_ref, sr_ref, si_ref):
    n2 = sr_ref.shape[-1]
    for j in range(SUBLANES):
        s = _dot(u_ref[j, 0].astype(BF16), w_ref[j])
        sr_ref[:, j, :] = s[:, :n2]
        si_ref[:, j, :] = s[:, n2:]


def _s5_state_spec(c, n2, gb):
    return pl.BlockSpec((c, SUBLANES, n2), lambda bi, gi: (0, bi * gb + gi, 0))


def s5_summary(u_t, summ):
    g, b, c, k = u_t.shape
    n2 = summ.shape[-1] // 2
    gb = g // SUBLANES
    shape = jax.ShapeDtypeStruct((c, b * g, n2), F32)
    return pl.pallas_call(
        _s5_summary_kernel,
        grid=(b, gb),
        in_specs=[
            pl.BlockSpec((SUBLANES, 1, c, k), lambda bi, gi: (gi, bi, 0, 0)),
            pl.BlockSpec((SUBLANES, k, 2 * n2), lambda bi, gi: (gi, 0, 0)),
        ],
        out_specs=[_s5_state_spec(c, n2, gb)] * 2,
        out_shape=[shape, shape],
        compiler_params=_cparams("parallel", "parallel"),
        name="s5_summary",
    )(u_t, summ)


def _s5_scan_kernel(sr_ref, si_ref, are_ref, aim_ref, or_ref, oi_ref, *, n_chunks):
    n2 = 2 * S5_STATE
    a_re = are_ref[...]
    a_im = aim_ref[...]
    fwd_lane = lax.broadcasted_iota(jnp.int32, (SUBLANES, n2), 1) < S5_STATE

    def body(i, state):
        x_re, x_im = state
        r = n_chunks - 1 - i
        or_ref[i, :, 0:S5_STATE] = x_re[:, :S5_STATE]
        oi_ref[i, :, 0:S5_STATE] = x_im[:, :S5_STATE]
        or_ref[r, :, S5_STATE:n2] = x_re[:, S5_STATE:]
        oi_ref[r, :, S5_STATE:n2] = x_im[:, S5_STATE:]
        s_re = jnp.where(fwd_lane, sr_ref[i], sr_ref[r])
        s_im = jnp.where(fwd_lane, si_ref[i], si_ref[r])
        return (a_re * x_re - a_im * x_im + s_re, a_re * x_im + a_im * x_re + s_im)

    zero = jnp.zeros((SUBLANES, n2), F32)
    lax.fori_loop(0, n_chunks, body, (zero, zero))


def s5_scan(s_re, s_im, a_re, a_im):
    c, rows, n2 = s_re.shape
    kern = functools.partial(_s5_scan_kernel, n_chunks=c)
    st = pl.BlockSpec((c, SUBLANES, n2), lambda i: (0, i, 0))
    av = pl.BlockSpec((SUBLANES, n2), lambda i: (i, 0))
    shape = jax.ShapeDtypeStruct(s_re.shape, F32)
    return pl.pallas_call(
        kern,
        grid=(rows // SUBLANES,),
        in_specs=[st, st, av, av],
        out_specs=[st, st],
        out_shape=[shape, shape],
        compiler_params=_cparams("parallel"),
        name="s5_scan",
    )(s_re, s_im, a_re, a_im)


def _s5_output_kernel(u_ref, xr_ref, xi_ref, wi_ref, wc_ref, o_ref):
    for j in range(SUBLANES):
        y = _dot(u_ref[j, 0].astype(BF16), wi_ref[j])
        xin = jnp.concatenate([xr_ref[:, j, :], xi_ref[:, j, :]], axis=-1)
        y += _dot(xin.astype(BF16), wc_ref[j])
        o_ref[j, 0] = y


def s5_output(u_t, xin_re, xin_im, intra, carry):
    g, b, c, k = u_t.shape
    n2 = xin_re.shape[-1]
    gb = g // SUBLANES
    return pl.pallas_call(
        _s5_output_kernel,
        grid=(b, gb),
        in_specs=[
            pl.BlockSpec((SUBLANES, 1, c, k), lambda bi, gi: (gi, bi, 0, 0)),
            _s5_state_spec(c, n2, gb),
            _s5_state_spec(c, n2, gb),
            pl.BlockSpec((SUBLANES, k, k), lambda bi, gi: (gi, 0, 0)),
            pl.BlockSpec((SUBLANES, 2 * n2, k), lambda bi, gi: (gi, 0, 0)),
        ],
        out_specs=pl.BlockSpec((SUBLANES, 1, c, k), lambda bi, gi: (gi, bi, 0, 0)),
        out_shape=jax.ShapeDtypeStruct(u_t.shape, F32),
        compiler_params=_cparams("parallel", "parallel"),
        name="s5_output",
    )(u_t, xin_re, xin_im, intra, carry)


def s5_core(u, mats):
    b, l, _ = u.shape
    c = l // S5_CHUNK
    intra, summ, carry, a_re, a_im = mats
    u_t = u.reshape(b, c, S5_CHUNK, S5_GROUPS, S5_GROUP_DIM)
    u_t = jnp.transpose(u_t, (3, 0, 1, 2, 4)).reshape(S5_GROUPS, b, c, S5_CHUNK * S5_GROUP_DIM)
    s_re, s_im = s5_summary(u_t, summ.astype(BF16))
    xin_re, xin_im = s5_scan(s_re, s_im, jnp.tile(a_re, (b, 1)), jnp.tile(a_im, (b, 1)))
    y_t = s5_output(u_t, xin_re, xin_im, intra.astype(BF16), carry.astype(BF16))
    y = y_t.reshape(S5_GROUPS, b, c, S5_CHUNK, S5_GROUP_DIM)
    return jnp.transpose(y, (1, 2, 3, 0, 4)).reshape(b, l, D_S5)


def _short_conv_kernel(pp_ref, p_ref, pn_ref, w_ref, b_ref, o_ref, *, tm):
    ext = _with_halo(pp_ref, p_ref, pn_ref)
    y = w_ref[0:1, :] * _shift_rows(ext, 1) + w_ref[1:2, :] * ext + w_ref[2:3, :] * _shift_rows(ext, -1)
    o_ref[0, 0] = y[POOL_HALO:POOL_HALO + tm] + b_ref[...]


def hyena_short_conv(p, w, bias, *, tm):
    b, l, _ = p.shape
    kern = functools.partial(_short_conv_kernel, tm=tm)
    return pl.pallas_call(
        kern,
        grid=(b, l // tm, 3),
        in_specs=_halo_specs(tm, l, D_HYENA, lambda j: j + 1) + [
            pl.BlockSpec((3, D_HYENA), lambda bi, i, j: (0, j)),
            pl.BlockSpec((1, D_HYENA), lambda bi, i, j: (0, j)),
        ],
        out_specs=pl.BlockSpec((1, 1, tm, D_HYENA), lambda bi, i, j: (j, bi, i, 0)),
        out_shape=jax.ShapeDtypeStruct((3, b, l, D_HYENA), F32),
        compiler_params=_cparams("parallel", "parallel", "parallel"),
        name="hyena_short_conv",
    )(p, p, p, w, bias.reshape(1, -1))


def _filter_mlp_kernel(bands_ref, w1t_ref, w1cs_ref, b1_ref, w2_ref, b2_ref, w3_ref, fr_ref, dl_ref,
                       h_ref, asum_ref, *, tl, seq_len):
    i = pl.program_id(0)
    hp = lax.Precision.HIGHEST
    t = (i * tl + lax.broadcasted_iota(jnp.int32, (tl, 1), 0)).astype(F32)
    t_norm = t / (seq_len - 1.0)
    ang = (2.0 * math.pi / seq_len) * t * bands_ref[...]
    is_cos = lax.broadcasted_iota(jnp.int32, ang.shape, 1) < HYENA_BANDS
    cs = jnp.where(is_cos, jnp.cos(ang), -jnp.sin(ang))
    fr = fr_ref[...]
    z = t_norm * w1t_ref[...] + jnp.dot(cs, w1cs_ref[...], precision=hp, preferred_element_type=F32)
    h = jnp.sin(fr * (z + b1_ref[...]))
    h = jnp.sin(fr * (jnp.dot(h, w2_ref[...], precision=hp, preferred_element_type=F32) + b2_ref[...]))
    h = jnp.dot(h, w3_ref[...], precision=hp, preferred_element_type=F32)
    decay = jnp.exp(-t_norm * dl_ref[...])
    h = h * jnp.concatenate([decay] * (h.shape[1] // decay.shape[1]), axis=-1)
    h_ref[...] = h

    @pl.when(i == 0)
    def _():
        asum_ref[...] = jnp.zeros_like(asum_ref)

    asum_ref[...] += jnp.sum(jnp.abs(h), axis=0, keepdims=True)


def hyena_filter_mlp(seq_len, w1, b1, w2, b2, w3, freq, *, tl):
    n_out = w3.shape[1]
    bands = jnp.linspace(1e-4, HYENA_BANDS - 1, HYENA_BANDS, dtype=F32)
    bands2 = jnp.concatenate([bands, bands])[None, :]
    deltas = jnp.abs(jnp.linspace(math.log(HYENA_TARGET) / HYENA_LONG_DECAY_PCT,
                                  math.log(HYENA_TARGET) / HYENA_SHORT_DECAY_PCT, D_HYENA, dtype=F32))[None, :]
    kern = functools.partial(_filter_mlp_kernel, tl=tl, seq_len=seq_len)
    full = lambda a: pl.BlockSpec(a.shape, lambda i: (0,) * a.ndim)
    args = [bands2, w1[0:1], w1[1:], b1[None, :], w2, b2[None, :], w3, freq[None, :], deltas]
    return pl.pallas_call(
        kern,
        grid=(seq_len // tl,),
        in_specs=[full(a) for a in args],
        out_specs=[pl.BlockSpec((tl, n_out), lambda i: (i, 0)), pl.BlockSpec((1, n_out), lambda i: (0, 0))],
        out_shape=[jax.ShapeDtypeStruct((seq_len, n_out), F32), jax.ShapeDtypeStruct((1, n_out), F32)],
        compiler_params=_cparams("arbitrary"),
        name="hyena_filter_mlp",
    )(*args)


def dft_tables(seq_len):
    n = 2 * seq_len
    nf = DFT_FAST
    ns = n // nf
    k_a = np.arange(ns)[:, None]
    n_s = np.arange(ns // 2)[None, :]
    w_s = np.exp(-2j * np.pi * ((k_a * n_s) % ns) / ns)
    n_f = np.arange(nf)[:, None]
    tw = np.exp(-2j * np.pi * ((n_f * np.arange(ns)[None, :]) % n) / n)
    c32 = lambda z: (jnp.asarray(z.real, F32), jnp.asarray(z.imag, F32))
    ws_re, ws_im = c32(w_s)
    tw_re, tw_im = c32(tw)
    g_re = tw_re[:, :, None] * ws_re[None] - tw_im[:, :, None] * ws_im[None]
    g_im = tw_re[:, :, None] * ws_im[None] + tw_im[:, :, None] * ws_re[None]
    fwd_a = jnp.concatenate([jnp.concatenate([g_re, -g_im], axis=2),
                             jnp.concatenate([g_im, g_re], axis=2)], axis=1)
    gt_re = jnp.swapaxes(g_re, 1, 2) * (1.0 / n)
    gt_im = jnp.swapaxes(g_im, 1, 2) * (1.0 / n)
    inv_a = jnp.concatenate([jnp.concatenate([gt_re, gt_im], axis=2),
                             jnp.concatenate([-gt_im, gt_re], axis=2)], axis=1)
    kk = np.arange(nf)
    w_f = np.exp(-2j * np.pi * ((kk[:, None] * kk[None, :]) % nf) / nf)
    f_re, f_im = c32(w_f)
    fwd_b = jnp.concatenate([jnp.concatenate([f_re, -f_im], axis=1),
                             jnp.concatenate([f_im, f_re], axis=1)], axis=0)
    inv_b = jnp.concatenate([jnp.concatenate([f_re, f_im], axis=1),
                             jnp.concatenate([-f_im, f_re], axis=1)], axis=0)
    return dict(fwd_a=fwd_a.astype(BF16), fwd_a_real=fwd_a[:, :, :ns // 2].astype(BF16),
                inv_a=inv_a.astype(BF16), fwd_b=fwd_b.astype(BF16), inv_b=inv_b.astype(BF16))


def _dft_a_kernel(*refs, nfb, complex_in):
    if complex_in:
        xr_ref, xi_ref, g_ref, ar_ref, ai_ref = refs
    else:
        xr_ref, g_ref, ar_ref, ai_ref = refs
    ns = ar_ref.shape[0]
    for j in range(nfb):
        x = xr_ref[:, j, :]
        if complex_in:
            x = jnp.concatenate([x, xi_ref[:, j, :]], axis=0)
        a = _dot(g_ref[j], x.astype(BF16))
        ar_ref[:, j, :] = a[:ns]
        ai_ref[:, j, :] = a[ns:]


def dft_stage_a(xr, xi, table, *, nfb):
    l, c = xr.shape
    nf = DFT_FAST
    ns = 2 * l // nf
    complex_in = xi is not None
    xs = [xr.reshape(ns // 2, nf, c)] + ([xi.reshape(ns // 2, nf, c)] if complex_in else [])
    kern = functools.partial(_dft_a_kernel, nfb=nfb, complex_in=complex_in)
    x_spec = pl.BlockSpec((ns // 2, nfb, LANES), lambda i, j: (0, i, j))
    o_spec = pl.BlockSpec((ns, nfb, LANES), lambda i, j: (0, i, j))
    o_shape = jax.ShapeDtypeStruct((ns, nf, c), F32)
    ar, ai = pl.pallas_call(
        kern,
        grid=(nf // nfb, c // LANES),
        in_specs=[x_spec] * len(xs) + [pl.BlockSpec((nfb,) + table.shape[1:], lambda i, j: (i, 0, 0))],
        out_specs=[o_spec, o_spec],
        out_shape=[o_shape, o_shape],
        compiler_params=_cparams("parallel", "parallel"),
        name="dft_stage_a",
    )(*xs, table)
    return ar.reshape(2 * l, c), ai.reshape(2 * l, c)


def _dft_b_conv_kernel(ar_ref, ai_ref, kr_ref, ki_ref, fb_ref, ib_ref, cr_ref, ci_ref, *, slabs):
    nf = DFT_FAST
    for s in range(slabs):
        rows = slice(s * nf, (s + 1) * nf)
        a = jnp.concatenate([ar_ref[rows, :], ai_ref[rows, :]], axis=0).astype(BF16)
        x = _dot(fb_ref[...], a)
        xr, xi = x[:nf], x[nf:]
        kr, ki = kr_ref[rows, :], ki_ref[rows, :]
        y = jnp.concatenate([xr * kr - xi * ki, xr * ki + xi * kr], axis=0).astype(BF16)
        cc = _dot(ib_ref[...], y)
        cr_ref[rows, :] = cc[:nf]
        ci_ref[rows, :] = cc[nf:]


def dft_stage_b_conv(ar, ai, kr, ki, fwd_b, inv_b, *, slabs, tc):
    n, c = ar.shape
    rows = slabs * DFT_FAST
    kern = functools.partial(_dft_b_conv_kernel, slabs=slabs)
    spec = pl.BlockSpec((rows, tc), lambda i, j: (i, j))
    mat = pl.BlockSpec(fwd_b.shape, lambda i, j: (0, 0))
    shape = jax.ShapeDtypeStruct((n, c), F32)
    return pl.pallas_call(
        kern,
        grid=(n // rows, c // tc),
        in_specs=[spec, spec, spec, spec, mat, mat],
        out_specs=[spec, spec],
        out_shape=[shape, shape],
        compiler_params=_cparams("parallel", "parallel"),
        name="dft_stage_b_conv",
    )(ar, ai, kr, ki, fwd_b, inv_b)


def _dft_b_filter_kernel(fr_ref, fi_ref, br_ref, bi_ref, sf_ref, sb_ref, fb_ref, kr_ref, ki_ref, *, slabs):
    nf = DFT_FAST
    inv_norm = 1.0 / (sf_ref[...] + sb_ref[...] + 1e-6)
    for s in range(slabs):
        rows = slice(s * nf, (s + 1) * nf)
        zf = _dot(fb_ref[...], jnp.concatenate([fr_ref[rows, :], fi_ref[rows, :]], axis=0).astype(BF16))
        zb = _dot(fb_ref[...], jnp.concatenate([br_ref[rows, :], bi_ref[rows, :]], axis=0).astype(BF16))
        kr_ref[0, rows, :] = (zf[:nf] + zb[:nf]) * inv_norm
        ki_ref[0, rows, :] = (zf[nf:] - zb[nf:]) * inv_norm


def dft_stage_b_filter(ar, ai, asum, fwd_b, *, slabs):
    n, cols = ar.shape
    c = D_HYENA
    orders = cols // (2 * c)
    rows = slabs * DFT_FAST
    kern = functools.partial(_dft_b_filter_kernel, slabs=slabs)
    f_spec = pl.BlockSpec((rows, c), lambda o, i: (i, 2 * o))
    b_spec = pl.BlockSpec((rows, c), lambda o, i: (i, 2 * o + 1))
    o_spec = pl.BlockSpec((1, rows, c), lambda o, i: (o, i, 0))
    shape = jax.ShapeDtypeStruct((orders, n, c), F32)
    return pl.pallas_call(
        kern,
        grid=(orders, n // rows),
        in_specs=[f_spec, f_spec, b_spec, b_spec,
                  pl.BlockSpec((1, c), lambda o, i: (0, 2 * o)), pl.BlockSpec((1, c), lambda o, i: (0, 2 * o + 1)),
                  pl.BlockSpec(fwd_b.shape, lambda o, i: (0, 0))],
        out_specs=[o_spec, o_spec],
        out_shape=[shape, shape],
        compiler_params=_cparams("parallel", "parallel"),
        name="dft_stage_b_filter",
    )(ar, ai, ar, ai, asum, asum, fwd_b)


def _dft_a_inv_kernel(cr_ref, ci_ref, g_ref, gate_ref, u_ref, bias_ref, o_ref, *, nfb):
    half = o_ref.shape[1]
    for j in range(nfb):
        cc = jnp.concatenate([cr_ref[:, j, :], ci_ref[:, j, :]], axis=0).astype(BF16)
        y = _dot(g_ref[j], cc)
        for bi in range(2):
            u = u_ref[bi, :, j, :]
            o_ref[bi, :, j, :] = gate_ref[bi, :, j, :] * (y[bi * half:(bi + 1) * half] + u * bias_ref[...])


def dft_stage_a_inv(cr, ci, table, gate, u, bias, *, nfb):
    _, l, c = u.shape
    nf = DFT_FAST
    ns = 2 * l // nf
    kern = functools.partial(_dft_a_inv_kernel, nfb=nfb)
    c_spec = pl.BlockSpec((ns, nfb, LANES), lambda i, j: (0, i, j))
    t_spec = pl.BlockSpec((2, ns // 2, nfb, LANES), lambda i, j: (0, 0, i, j))
    out = pl.pallas_call(
        kern,
        grid=(nf // nfb, c // LANES),
        in_specs=[c_spec, c_spec, pl.BlockSpec((nfb,) + table.shape[1:], lambda i, j: (i, 0, 0)),
                  t_spec, t_spec, pl.BlockSpec((1, LANES), lambda i, j: (0, j))],
        out_specs=t_spec,
        out_shape=jax.ShapeDtypeStruct((2, ns // 2, nf, c), F32),
        compiler_params=_cparams("parallel", "parallel"),
        name="dft_stage_a_inv",
    )(cr.reshape(ns, nf, c), ci.reshape(ns, nf, c), table,
      gate.reshape(2, ns // 2, nf, c), u.reshape(2, ns // 2, nf, c), bias.reshape(1, c))
    return out.reshape(2, l, c)


def hyena_filter_spectra(seq_len, tables, w1, b1, w2, b2, w3, freq):
    h, asum = hyena_filter_mlp(seq_len, w1, b1, w2, b2, w3, freq, tl=512)
    ar, ai = dft_stage_a(h, None, tables['fwd_a_real'], nfb=SUBLANES)
    return dft_stage_b_filter(ar, ai, asum, tables['fwd_b'], slabs=4)


def hyena_long_conv(u, gate, kr, ki, bias, tables):
    ar, ai = dft_stage_a(u[0], u[1], tables['fwd_a'], nfb=SUBLANES)
    cr, ci = dft_stage_b_conv(ar, ai, kr, ki, tables['fwd_b'], tables['inv_b'], slabs=4, tc=512)
    return dft_stage_a_inv(cr, ci, tables['inv_a'], gate, u, bias, nfb=SUBLANES)


def _odd_out_kernel(ys_ref, u_ref, hy_ref, x_ref, d_ref, wglu_ref, wo_ref, g_ref, o_ref):
    y = ys_ref[0] + d_ref[...] * u_ref[0]
    c0 = math.sqrt(2.0 / math.pi)
    gl = 0.5 * y * (1.0 + jnp.tanh(c0 * (y + 0.044715 * (y * y * y))))
    z = _dot(gl.astype(BF16), wglu_ref[...])
    s5 = gl * (1.0 / (1.0 + jnp.exp(-z)))
    mix_in = jnp.concatenate([s5.astype(BF16), hy_ref[0].astype(BF16)], axis=-1)
    mix = _dot(mix_in, wo_ref[...])
    o_ref[0] = x_ref[0] + _rms(mix, g_ref[...])


def odd_out(ys, p, hy, x, d_skip, w_glu, w_out, g, *, tm):
    b, l, d = x.shape
    tok = lambda w: pl.BlockSpec((1, tm, w), lambda bi, i: (bi, i, 0))
    return pl.pallas_call(
        _odd_out_kernel,
        grid=(b, l // tm),
        in_specs=[tok(D_S5), tok(D_S5), tok(D_HYENA), tok(d),
                  pl.BlockSpec((1, D_S5), lambda bi, i: (0, 0)),
                  pl.BlockSpec((D_S5, D_S5), lambda bi, i: (0, 0)),
                  pl.BlockSpec((d, d), lambda bi, i: (0, 0)),
                  pl.BlockSpec((1, d), lambda bi, i: (0, 0))],
        out_specs=tok(d),
        out_shape=jax.ShapeDtypeStruct(x.shape, F32),
        compiler_params=_cparams("parallel", "parallel"),
        name="odd_out",
    )(ys, p, hy, x, d_skip.reshape(1, D_S5), w_glu, w_out, g.reshape(1, d))


def kernel(x, mem, norm_mix, norm_xattn, norm_mem, norm_mlp, xa_wq, xa_wk, xa_wv, xa_wo, mlp_w1, mlp_w2, ev_w_in, ev_pool_w, ev_pool_scale, ev_conv_w, ev_w_out, od_w_in, od_s5_lambda_re, od_s5_lambda_im, od_s5_log_dt, od_s5_b_re, od_s5_b_im, od_s5_c_re, od_s5_c_im, od_s5_d, od_s5_w_glu, od_hy_short_w, od_hy_short_b, od_hy_w1, od_hy_b1, od_hy_w2, od_hy_b2, od_hy_w3, od_hy_freq, od_hy_bias, od_w_out):
    b, l, d = x.shape
    depth = norm_mix.shape[0]
    assert b == 2, "the long convolution packs the two batch rows as one complex signal"
    tables = dft_tables(l)
    mem2d = mem.reshape(b * N_MEM, d)
    for i in range(depth):
        j = i // 2
        if i % 2 == 0:
            p = norm_matmul(x.reshape(b * l, d), norm_mix[i, 0], ev_w_in[j].astype(BF16), tm=512)
            x = even_mixer(p.reshape(b, l, -1), x, ev_pool_w[j].astype(BF16), ev_pool_scale[j], ev_conv_w[j],
                           ev_w_out[j].astype(BF16), norm_mix[i, 1], tm=512)
        else:
            p = norm_matmul(x.reshape(b * l, d), norm_mix[i, 0], od_w_in[j].astype(BF16), tm=512)
            p = p.reshape(b, l, -1)
            mats = s5_matrices(od_s5_lambda_re[j], od_s5_lambda_im[j], od_s5_log_dt[j], od_s5_b_re[j],
                               od_s5_b_im[j], od_s5_c_re[j], od_s5_c_im[j])
            ys = s5_core(p[..., :D_S5], mats)
            kr, ki = hyena_filter_spectra(l, tables, od_hy_w1[j], od_hy_b1[j], od_hy_w2[j], od_hy_b2[j],
                                          od_hy_w3[j], od_hy_freq[j])
            gates = hyena_short_conv(p, od_hy_short_w[j], od_hy_short_b[j], tm=512)
            z = hyena_long_conv(gates[2], gates[1], kr[0], ki[0], od_hy_bias[j, 0], tables)
            hy = hyena_long_conv(z, gates[0], kr[1], ki[1], od_hy_bias[j, 1], tables)
            x = odd_out(ys, p, hy, x, od_s5_d[j], od_s5_w_glu[j].astype(BF16), od_w_out[j].astype(BF16),
                        norm_mix[i, 1], tm=512)
        kv_w = jnp.concatenate([xa_wk[i], xa_wv[i]], axis=1).astype(BF16)
        kv = norm_matmul(mem2d, norm_mem[i], kv_w, tm=b * N_MEM, out_dtype=BF16).reshape(b, N_MEM, 2 * d)
        x = xattn_block(x, kv, xa_wq[i].astype(BF16), xa_wo[i].astype(BF16), norm_xattn[i, 0], norm_xattn[i, 1],
                        tm=512)
        x = mlp_block(x.reshape(b * l, d), mlp_w1[i].astype(BF16), mlp_w2[i].astype(BF16), norm_mlp[i, 0],
                      norm_mlp[i, 1], tm=1024, tf=512).reshape(b, l, d)
    return x
```

```python
import functools
import math

import numpy as np
import jax
import jax.numpy as jnp
from jax import lax
from jax.experimental import pallas as pl
from jax.experimental.pallas import tpu as pltpu

F32 = jnp.float32
BF16 = jnp.bfloat16

D_MODEL = 1024
N_MEM = 256
RMS_EPS = 1e-6
D_POOL = 512
POOL_WINDOWS = (2, 4, 8, 16)
POOL_GROUP_DIM = 128
POOL_HALO = 8
D_CONV = 512
D_S5 = 512
S5_GROUP_DIM = 16
S5_GROUPS = 32
S5_STATE = 64
S5_CHUNK = 16
D_HYENA = 512
HYENA_BANDS = 16
HYENA_FFN = 64
HYENA_TARGET = 1e-2
HYENA_SHORT_DECAY_PCT = 0.3
HYENA_LONG_DECAY_PCT = 1.5
XA_HEADS = 4
XA_HEAD_DIM = 256
D_FF = 4096

LANES = 128
SUBLANES = 8
S5_SUPER = LANES // S5_GROUP_DIM
DFT_FAST = 128
DFT_TILES = DFT_FAST // SUBLANES
DFT_PITCH = DFT_FAST + SUBLANES
VMEM_LIMIT = 56 * 1024 * 1024


def _cparams(*sem):
    return pltpu.CompilerParams(dimension_semantics=sem, vmem_limit_bytes=VMEM_LIMIT)


def _rms(xf, g):
    ms = jnp.mean(xf * xf, axis=-1, keepdims=True)
    return xf * lax.rsqrt(ms + RMS_EPS) * g


def _dot(a, b):
    return jnp.dot(a, b, preferred_element_type=F32)


def _layer_spec(block, layer, tail_map):
    return pl.BlockSpec((None,) + block, lambda *idx: (layer,) + tail_map(*idx))


def _norm_matmul_kernel(x_ref, g_ref, w_ref, o_ref):
    xn = _rms(x_ref[...], g_ref[...]).astype(BF16)
    o_ref[...] = _dot(xn, w_ref[...]).astype(o_ref.dtype)


def norm_matmul(x2d, g, w, layer, *, tm, out_dtype=F32):
    m, d = x2d.shape
    n = w.shape[-1]
    return pl.pallas_call(
        _norm_matmul_kernel,
        grid=(m // tm,),
        in_specs=[
            pl.BlockSpec((tm, d), lambda i: (i, 0)),
            pl.BlockSpec((1, d), lambda i: (0, 0)),
            _layer_spec((d, n), layer, lambda i: (0, 0)),
        ],
        out_specs=pl.BlockSpec((tm, n), lambda i: (i, 0)),
        out_shape=jax.ShapeDtypeStruct((m, n), out_dtype),
        compiler_params=_cparams("parallel"),
        name="norm_matmul",
    )(x2d, g.reshape(1, d), w)


def _xattn_kernel(x_ref, kv_ref, wq_ref, wo_ref, g1_ref, g2_ref, o_ref):
    x = x_ref[0]
    xn = _rms(x, g1_ref[...]).astype(BF16)
    q = (_dot(xn, wq_ref[...]) * (XA_HEAD_DIM ** -0.5)).astype(BF16)
    heads = []
    for h in range(XA_HEADS):
        lo = h * XA_HEAD_DIM
        qh = q[:, lo:lo + XA_HEAD_DIM]
        kh = kv_ref[0, :, lo:lo + XA_HEAD_DIM]
        vh = kv_ref[0, :, D_MODEL + lo:D_MODEL + lo + XA_HEAD_DIM]
        s = lax.dot_general(qh, kh, (((1,), (1,)), ((), ())), preferred_element_type=F32)
        e = jnp.exp(s - jnp.max(s, axis=-1, keepdims=True))
        p = e / jnp.sum(e, axis=-1, keepdims=True)
        heads.append(_dot(p.astype(BF16), vh).astype(BF16))
    o = jnp.concatenate(heads, axis=-1)
    y = _dot(o, wo_ref[...])
    o_ref[0] = x + _rms(y, g2_ref[...])


def xattn_block(x, kv, wq, wo, layer, g1, g2, *, tm):
    b, l, d = x.shape
    return pl.pallas_call(
        _xattn_kernel,
        grid=(b, l // tm),
        in_specs=[
            pl.BlockSpec((1, tm, d), lambda bi, i: (bi, i, 0)),
            pl.BlockSpec((1, N_MEM, 2 * d), lambda bi, i: (bi, 0, 0)),
            _layer_spec((d, d), layer, lambda bi, i: (0, 0)),
            _layer_spec((d, d), layer, lambda bi, i: (0, 0)),
            pl.BlockSpec((1, d), lambda bi, i: (0, 0)),
            pl.BlockSpec((1, d), lambda bi, i: (0, 0)),
        ],
        out_specs=pl.BlockSpec((1, tm, d), lambda bi, i: (bi, i, 0)),
        out_shape=jax.ShapeDtypeStruct(x.shape, F32),
        compiler_params=_cparams("parallel", "parallel"),
        name="xattn_block",
    )(x, kv, wq, wo, g1.reshape(1, d), g2.reshape(1, d))


def _mlp_kernel(x_ref, w1_ref, w2_ref, g1_ref, g2_ref, o_ref, xn_ref, acc_ref):
    j = pl.program_id(1)

    @pl.when(j == 0)
    def _():
        xn_ref[...] = _rms(x_ref[...], g1_ref[...]).astype(BF16)
        acc_ref[...] = jnp.zeros_like(acc_ref)

    h = jnp.maximum(_dot(xn_ref[...], w1_ref[...]), 0.0)
    acc_ref[...] += _dot((h * h).astype(BF16), w2_ref[...])

    @pl.when(j == pl.num_programs(1) - 1)
    def _():
        o_ref[...] = x_ref[...] + _rms(acc_ref[...], g2_ref[...])


def mlp_block(x2d, w1, w2, layer, g1, g2, *, tm, tf):
    m, d = x2d.shape
    ff = w1.shape[-1]
    return pl.pallas_call(
        _mlp_kernel,
        grid=(m // tm, ff // tf),
        in_specs=[
            pl.BlockSpec((tm, d), lambda i, j: (i, 0)),
            _layer_spec((d, tf), layer, lambda i, j: (0, j)),
            _layer_spec((tf, d), layer, lambda i, j: (j, 0)),
            pl.BlockSpec((1, d), lambda i, j: (0, 0)),
            pl.BlockSpec((1, d), lambda i, j: (0, 0)),
        ],
        out_specs=pl.BlockSpec((tm, d), lambda i, j: (i, 0)),
        out_shape=jax.ShapeDtypeStruct((m, d), F32),
        scratch_shapes=[pltpu.VMEM((tm, d), BF16), pltpu.VMEM((tm, d), F32)],
        compiler_params=_cparams("parallel", "arbitrary"),
        name="mlp_block",
    )(x2d, w1, w2, g1.reshape(1, d), g2.reshape(1, d))


def _halo_specs(tm, seq_len, width, col):
    r = tm // POOL_HALO
    last = seq_len // POOL_HALO - 1

    def prev_map(bi, i, *_):
        return (bi, jnp.maximum(i * r - 1, 0), col(*_))

    def main_map(bi, i, *_):
        return (bi, i, col(*_))

    def next_map(bi, i, *_):
        return (bi, jnp.minimum((i + 1) * r, last), col(*_))

    return [
        pl.BlockSpec((1, POOL_HALO, width), prev_map),
        pl.BlockSpec((1, tm, width), main_map),
        pl.BlockSpec((1, POOL_HALO, width), next_map),
    ]


def _with_halo(prev_ref, main_ref, next_ref):
    i = pl.program_id(1)
    prev = jnp.where(i > 0, prev_ref[0], 0.0)
    nxt = jnp.where(i < pl.num_programs(1) - 1, next_ref[0], 0.0)
    return jnp.concatenate([prev, main_ref[0], nxt], axis=0)


def _shift_rows(v, k):
    return pltpu.roll(v, k % v.shape[0], 0)


def _even_mixer_kernel(pp_ref, p_ref, pn_ref, x_ref, wg_ref, ps_ref, cw_ref, wo_ref, g_ref, o_ref,
                       *, tm, seq_len):
    ext = _with_halo(pp_ref, p_ref, pn_ref)
    lo, hi = POOL_HALO, POOL_HALO + tm
    t = pl.program_id(1) * tm + lax.broadcasted_iota(jnp.int32, (tm, 1), 0)
    parts = []
    for gi, win in enumerate(POOL_WINDOWS):
        half = win // 2
        u = ext[:, gi * POOL_GROUP_DIM:(gi + 1) * POOL_GROUP_DIM]
        s = u + _shift_rows(u, 1)
        step = 1
        while 2 * step < win:
            s = _shift_rows(s, step) + _shift_rows(s, -step)
            step *= 2
        cnt = (jnp.minimum(t + half, seq_len) - jnp.maximum(t - half, 0)).astype(F32)
        pooled = s[lo:hi] / cnt - u[lo:hi]
        y = _dot(pooled.astype(BF16), wg_ref[gi])
        parts.append((y * ps_ref[:, gi * POOL_GROUP_DIM:(gi + 1) * POOL_GROUP_DIM]).astype(BF16))
    b_gate = ext[lo:hi, D_POOL:D_POOL + D_CONV]
    ch = ext[:, D_POOL + D_CONV:D_POOL + 2 * D_CONV] * ext[:, D_POOL + 2 * D_CONV:D_POOL + 3 * D_CONV]
    conv = cw_ref[0:1, :] * _shift_rows(ch, 1) + cw_ref[1:2, :] * ch + cw_ref[2:3, :] * _shift_rows(ch, -1)
    parts.append((b_gate * conv[lo:hi]).astype(BF16))
    mix = _dot(jnp.concatenate(parts, axis=-1), wo_ref[...])
    o_ref[0] = x_ref[0] + _rms(mix, g_ref[...])


def even_mixer(p, x, w_group, pool_scale, conv_w, w_out, layer, g, *, tm):
    b, l, d = x.shape
    width = p.shape[-1]
    kern = functools.partial(_even_mixer_kernel, tm=tm, seq_len=l)
    return pl.pallas_call(
        kern,
        grid=(b, l // tm),
        in_specs=_halo_specs(tm, l, width, lambda: 0) + [
            pl.BlockSpec((1, tm, d), lambda bi, i: (bi, i, 0)),
            _layer_spec(w_group.shape[1:], layer, lambda bi, i: (0, 0, 0)),
            pl.BlockSpec((1, D_POOL), lambda bi, i: (0, 0)),
            pl.BlockSpec((3, D_CONV), lambda bi, i: (0, 0)),
            _layer_spec((d, d), layer, lambda bi, i: (0, 0)),
            pl.BlockSpec((1, d), lambda bi, i: (0, 0)),
        ],
        out_specs=pl.BlockSpec((1, tm, d), lambda bi, i: (bi, i, 0)),
        out_shape=jax.ShapeDtypeStruct(x.shape, F32),
        compiler_params=_cparams("parallel", "parallel"),
        name="even_mixer",
    )(p, p, p, x, w_group, pool_scale.reshape(1, D_POOL), conv_w, w_out, g.reshape(1, d))


def _cmul(ar, ai, br, bi):
    return ar * br - ai * bi, ar * bi + ai * br


def s5_matrices(lam_re, lam_im, log_dt, b_re, b_im, c_re, c_im):
    tc, hd, n, sup = S5_CHUNK, S5_GROUP_DIM, S5_STATE, S5_SUPER
    n_tiles = S5_GROUPS // sup
    hp = lax.Precision.HIGHEST
    lr = jnp.minimum(lam_re, -1e-4)
    li = lam_im
    dt = jnp.exp(log_dt)[..., None]
    taus = jnp.arange(tc + 1, dtype=F32)[:, None, None, None]
    rmag = jnp.exp(lr[None] * dt[None] * (tc - taus))
    rang = li[None] * dt[None] * (tc - taus)
    pwr_re, pwr_im = rmag * jnp.cos(rang), rmag * jnp.sin(rang)
    mag = jnp.exp(lr[None] * dt[None] * taus)
    ang = li[None] * dt[None] * taus
    pw_re, pw_im = mag * jnp.cos(ang), mag * jnp.sin(ang)
    nr, ni = pw_re[1] - 1.0, pw_im[1]
    den = lr * lr + li * li
    coef_re, coef_im = (nr * lr + ni * li) / den, (ni * lr - nr * li) / den
    cb_re, cb_im = _cmul(coef_re[..., None], coef_im[..., None], b_re[None], b_im[None])
    q_re, q_im = _cmul(pw_re[:tc, :, :, :, None], pw_im[:tc, :, :, :, None], cb_re[None], cb_im[None])
    m = (jnp.einsum('dghn,tdgnk->dtghk', c_re, q_re, precision=hp)
         - jnp.einsum('dghn,tdgnk->dtghk', c_im, q_im, precision=hp))
    m_fwd, m_bwd = m[0], m[1]
    diag = m_fwd[0] + m_bwd[0]
    lag_table = jnp.concatenate([m_bwd[:0:-1], diag[None], m_fwd[1:]], axis=0)
    intra = jnp.stack([lag_table[tc - 1 - s:2 * tc - 1 - s] for s in range(tc)], axis=0)
    eye = jnp.eye(sup, dtype=F32)
    intra = intra.reshape(tc, tc, n_tiles, sup, hd, hd)
    intra = jnp.transpose(intra, (2, 0, 3, 5, 1, 4))[:, :, :, :, :, None, :] * eye.reshape(1, 1, sup, 1, 1, sup, 1)
    intra = intra.reshape(n_tiles, tc * LANES, tc * LANES)
    sf_re, sf_im = _cmul(pwr_re[1:tc + 1, 0][:, :, :, None], pwr_im[1:tc + 1, 0][:, :, :, None],
                         cb_re[0][None], cb_im[0][None])
    sb_re, sb_im = _cmul(pw_re[:tc, 1][:, :, :, None], pw_im[:tc, 1][:, :, :, None],
                         cb_re[1][None], cb_im[1][None])
    summ = jnp.concatenate([sf_re, sb_re, sf_im, sb_im], axis=2)
    summ = summ.reshape(tc, n_tiles, sup, 4 * n, hd)
    summ = jnp.transpose(summ, (1, 0, 2, 4, 3))[:, :, :, :, None, :] * eye.reshape(1, 1, sup, 1, sup, 1)
    summ = summ.reshape(n_tiles, tc * LANES, sup * 4 * n)
    qf_re, qf_im = _cmul(c_re[0][None], c_im[0][None], pw_re[1:tc + 1, 0][:, :, None, :], pw_im[1:tc + 1, 0][:, :, None, :])
    qb_re, qb_im = _cmul(c_re[1][None], c_im[1][None], pwr_re[:tc, 1][:, :, None, :], pwr_im[:tc, 1][:, :, None, :])
    carry = jnp.concatenate([qf_re, qb_re, -qf_im, -qb_im], axis=-1)
    carry = carry.reshape(tc, n_tiles, sup, hd, 4 * n)
    carry = jnp.transpose(carry, (1, 2, 4, 0, 3))[:, :, :, :, None, :] * eye.reshape(1, sup, 1, 1, sup, 1)
    carry = carry.reshape(n_tiles, sup * 4 * n, tc * LANES)
    a_re = jnp.concatenate([pw_re[tc, 0], pw_re[tc, 1]], axis=-1)
    a_im = jnp.concatenate([pw_im[tc, 0], pw_im[tc, 1]], axis=-1)
    return intra, summ, carry, a_re, a_im


def _s5_gather_chunks(p_ref, n_chunks):
    cols = [p_ref[pl.ds(t, n_chunks, stride=S5_CHUNK), :].astype(BF16) for t in range(S5_CHUNK)]
    return jnp.concatenate(cols, axis=-1)


def _s5_summary_kernel(p_ref, w_ref, sr_ref, si_ref, *, n_chunks):
    n2 = 2 * S5_STATE
    u = _s5_gather_chunks(p_ref, n_chunks)
    for gl in range(S5_SUPER):
        s = _dot(u, w_ref[:, gl * 2 * n2:(gl + 1) * 2 * n2])
        sr_ref[pl.ds(gl, n_chunks, stride=S5_SUPER), :] = s[:, :n2]
        si_ref[pl.ds(gl, n_chunks, stride=S5_SUPER), :] = s[:, n2:]


def _s5_state_spec(c, n2, n_tiles):
    return pl.BlockSpec((None, c * S5_SUPER, n2), lambda a, bi: (bi * n_tiles + a, 0, 0))


def s5_summary(p, summ):
    b, l, _ = p.shape
    c = l // S5_CHUNK
    n_tiles = summ.shape[0]
    n2 = 2 * S5_STATE
    shape = jax.ShapeDtypeStruct((b * n_tiles, c * S5_SUPER, n2), F32)
    kern = functools.partial(_s5_summary_kernel, n_chunks=c)
    return pl.pallas_call(
        kern,
        grid=(n_tiles, b),
        in_specs=[
            pl.BlockSpec((None, l, LANES), lambda a, bi: (bi, 0, a)),
            pl.BlockSpec((None,) + summ.shape[1:], lambda a, bi: (a, 0, 0)),
        ],
        out_specs=[_s5_state_spec(c, n2, n_tiles)] * 2,
        out_shape=[shape, shape],
        compiler_params=_cparams("parallel", "parallel"),
        name="s5_summary",
    )(p, summ)


def _s5_scan_kernel(sr_ref, si_ref, are_ref, aim_ref, or_ref, oi_ref, *, n_chunks):
    n2 = 2 * S5_STATE
    a_re = are_ref[...]
    a_im = aim_ref[...]
    fwd_lane = lax.broadcasted_iota(jnp.int32, (SUBLANES, n2), 1) < S5_STATE

    def body(i, state):
        x_re, x_im = state
        rows_i = pl.ds(pl.multiple_of(i * SUBLANES, SUBLANES), SUBLANES)
        rows_r = pl.ds(pl.multiple_of((n_chunks - 1 - i) * SUBLANES, SUBLANES), SUBLANES)
        or_ref[rows_i, 0:S5_STATE] = x_re[:, :S5_STATE]
        oi_ref[rows_i, 0:S5_STATE] = x_im[:, :S5_STATE]
        or_ref[rows_r, S5_STATE:n2] = x_re[:, S5_STATE:]
        oi_ref[rows_r, S5_STATE:n2] = x_im[:, S5_STATE:]
        s_re = jnp.where(fwd_lane, sr_ref[rows_i, :], sr_ref[rows_r, :])
        s_im = jnp.where(fwd_lane, si_ref[rows_i, :], si_ref[rows_r, :])
        return (a_re * x_re - a_im * x_im + s_re, a_re * x_im + a_im * x_re + s_im)

    zero = jnp.zeros((SUBLANES, n2), F32)
    lax.fori_loop(0, n_chunks, body, (zero, zero))


def s5_scan(s_re, s_im, a_re, a_im):
    tiles, rows, n2 = s_re.shape
    kern = functools.partial(_s5_scan_kernel, n_chunks=rows // S5_SUPER)
    st = pl.BlockSpec((None, rows, n2), lambda i: (i, 0, 0))
    av = pl.BlockSpec((SUBLANES, n2), lambda i: (i, 0))
    shape = jax.ShapeDtypeStruct(s_re.shape, F32)
    return pl.pallas_call(
        kern,
        grid=(tiles,),
        in_specs=[st, st, av, av],
        out_specs=[st, st],
        out_shape=[shape, shape],
        compiler_params=_cparams("parallel"),
        name="s5_scan",
    )(s_re, s_im, a_re, a_im)


def _s5_output_kernel(p_ref, xr_ref, xi_ref, wi_ref, wc_ref, o_ref, *, n_chunks):
    n2 = 2 * S5_STATE
    u = _s5_gather_chunks(p_ref, n_chunks)
    xs = []
    for gl in range(S5_SUPER):
        xs.append(xr_ref[pl.ds(gl, n_chunks, stride=S5_SUPER), :].astype(BF16))
        xs.append(xi_ref[pl.ds(gl, n_chunks, stride=S5_SUPER), :].astype(BF16))
    xin = jnp.concatenate(xs, axis=-1)
    per_dot = 4
    for q in range(S5_CHUNK // per_dot):
        cols = slice(q * per_dot * LANES, (q + 1) * per_dot * LANES)
        y = _dot(u, wi_ref[:, cols]) + _dot(xin, wc_ref[:, cols])
        for tt in range(per_dot):
            o_ref[pl.ds(q * per_dot + tt, n_chunks, stride=S5_CHUNK), :] = y[:, tt * LANES:(tt + 1) * LANES]


def s5_output(p, xin_re, xin_im, intra, carry):
    b, l, _ = p.shape
    c = l // S5_CHUNK
    n_tiles = intra.shape[0]
    n2 = 2 * S5_STATE
    kern = functools.partial(_s5_output_kernel, n_chunks=c)
    once = dict(pipeline_mode=pl.Buffered(1))
    return pl.pallas_call(
        kern,
        grid=(n_tiles, b),
        in_specs=[
            pl.BlockSpec((None, l, LANES), lambda a, bi: (bi, 0, a)),
            _s5_state_spec(c, n2, n_tiles),
            _s5_state_spec(c, n2, n_tiles),
            pl.BlockSpec((None,) + intra.shape[1:], lambda a, bi: (a, 0, 0), **once),
            pl.BlockSpec((None,) + carry.shape[1:], lambda a, bi: (a, 0, 0), **once),
        ],
        out_specs=pl.BlockSpec((None, l, LANES), lambda a, bi: (bi, 0, a)),
        out_shape=jax.ShapeDtypeStruct((b, l, D_S5), F32),
        compiler_params=_cparams("parallel", "parallel"),
        name="s5_output",
    )(p, xin_re, xin_im, intra, carry)


def s5_core(p, mats):
    b = p.shape[0]
    intra, summ, carry, a_re, a_im = mats
    s_re, s_im = s5_summary(p, summ.astype(BF16))
    xin_re, xin_im = s5_scan(s_re, s_im, jnp.tile(a_re, (b, 1)), jnp.tile(a_im, (b, 1)))
    return s5_output(p, xin_re, xin_im, intra.astype(BF16), carry.astype(BF16))


def _to_time_tiles(y):
    nf, c = DFT_FAST, y.shape[-1]
    slabs = [y[s * nf:(s + 1) * nf].reshape(DFT_TILES, SUBLANES, c) for s in range(y.shape[0] // nf)]
    return jnp.concatenate(slabs, axis=1)


def _from_time_tiles(ref):
    c = ref.shape[-1]
    slabs = [ref[:, s * SUBLANES:(s + 1) * SUBLANES, :].reshape(DFT_FAST, c) for s in range(ref.shape[1] // SUBLANES)]
    return jnp.concatenate(slabs, axis=0)


def _short_conv_kernel(pp_ref, p_ref, pn_ref, w_ref, b_ref, o_ref, *, tm):
    ext = _with_halo(pp_ref, p_ref, pn_ref)
    y = w_ref[0:1, :] * _shift_rows(ext, 1) + w_ref[1:2, :] * ext + w_ref[2:3, :] * _shift_rows(ext, -1)
    o_ref[...] = _to_time_tiles(y[POOL_HALO:POOL_HALO + tm] + b_ref[...])


def hyena_short_conv(p, w, bias, *, tm):
    b, l, _ = p.shape
    kern = functools.partial(_short_conv_kernel, tm=tm)
    rows = tm // DFT_FAST * SUBLANES
    return pl.pallas_call(
        kern,
        grid=(b, l // tm, 3),
        in_specs=_halo_specs(tm, l, D_HYENA, lambda j: j + 1) + [
            pl.BlockSpec((3, D_HYENA), lambda bi, i, j: (0, j)),
            pl.BlockSpec((1, D_HYENA), lambda bi, i, j: (0, j)),
        ],
        out_specs=pl.BlockSpec((None, None, DFT_TILES, rows, D_HYENA), lambda bi, i, j: (j, bi, 0, i, 0)),
        out_shape=jax.ShapeDtypeStruct((3, b, DFT_TILES, l // DFT_FAST * SUBLANES, D_HYENA), F32),
        compiler_params=_cparams("parallel", "parallel", "parallel"),
        name="hyena_short_conv",
    )(p, p, p, w, bias.reshape(1, -1))


def _filter_mlp_kernel(bands_ref, w1t_ref, w1cs_ref, b1_ref, w2_ref, b2_ref, w3_ref, fr_ref, dl_ref,
                       eo_ref, asum_ref, *, tl, seq_len):
    i = pl.program_id(0)
    hp = lax.Precision.HIGHEST
    c = D_HYENA
    t = (i * tl + lax.broadcasted_iota(jnp.int32, (tl, 1), 0)).astype(F32)
    t_norm = t / (seq_len - 1.0)
    ang = (2.0 * math.pi / seq_len) * t * bands_ref[...]
    is_cos = lax.broadcasted_iota(jnp.int32, ang.shape, 1) < HYENA_BANDS
    cs = jnp.where(is_cos, jnp.cos(ang), -jnp.sin(ang))
    fr = fr_ref[...]
    z = t_norm * w1t_ref[...] + jnp.dot(cs, w1cs_ref[...], precision=hp, preferred_element_type=F32)
    h = jnp.sin(fr * (z + b1_ref[...]))
    h = jnp.sin(fr * (jnp.dot(h, w2_ref[...], precision=hp, preferred_element_type=F32) + b2_ref[...]))
    h = jnp.dot(h, w3_ref[...], precision=hp, preferred_element_type=F32)
    decay = jnp.exp(-t_norm * dl_ref[...])

    @pl.when(i == 0)
    def _():
        asum_ref[...] = jnp.zeros_like(asum_ref)

    for o in range(h.shape[1] // (2 * c)):
        fwd = h[:, 2 * o * c:(2 * o + 1) * c] * decay
        bwd = h[:, (2 * o + 1) * c:(2 * o + 2) * c] * decay
        eo_ref[:, 2 * o * c:(2 * o + 1) * c] = fwd + bwd
        eo_ref[:, (2 * o + 1) * c:(2 * o + 2) * c] = fwd - bwd
        asum_ref[:, o * c:(o + 1) * c] += jnp.sum(jnp.abs(fwd) + jnp.abs(bwd), axis=0, keepdims=True)


def hyena_filter_mlp(seq_len, w1, b1, w2, b2, w3, freq, *, tl):
    n_out = w3.shape[1]
    bands = jnp.linspace(1e-4, HYENA_BANDS - 1, HYENA_BANDS, dtype=F32)
    bands2 = jnp.concatenate([bands, bands])[None, :]
    deltas = jnp.abs(jnp.linspace(math.log(HYENA_TARGET) / HYENA_LONG_DECAY_PCT,
                                  math.log(HYENA_TARGET) / HYENA_SHORT_DECAY_PCT, D_HYENA, dtype=F32))[None, :]
    kern = functools.partial(_filter_mlp_kernel, tl=tl, seq_len=seq_len)
    full = lambda a: pl.BlockSpec(a.shape, lambda i: (0,) * a.ndim)
    args = [bands2, w1[0:1], w1[1:], b1[None, :], w2, b2[None, :], w3, freq[None, :], deltas]
    return pl.pallas_call(
        kern,
        grid=(seq_len // tl,),
        in_specs=[full(a) for a in args],
        out_specs=[pl.BlockSpec((tl, n_out), lambda i: (i, 0)), pl.BlockSpec((1, n_out // 2), lambda i: (0, 0))],
        out_shape=[jax.ShapeDtypeStruct((seq_len, n_out), F32), jax.ShapeDtypeStruct((1, n_out // 2), F32)],
        compiler_params=_cparams("arbitrary"),
        name="hyena_filter_mlp",
    )(*args)


def dft_tables(seq_len):
    n = 2 * seq_len
    nf = DFT_FAST
    ns = n // nf
    k_a = np.arange(ns)[:, None]
    n_s = np.arange(ns // 2)[None, :]
    w_s = np.exp(-2j * np.pi * ((k_a * n_s) % ns) / ns)
    n_f = np.arange(nf)[:, None]
    tw = np.exp(-2j * np.pi * ((n_f * np.arange(ns)[None, :]) % n) / n)
    c32 = lambda z: (jnp.asarray(z.real, F32), jnp.asarray(z.imag, F32))
    ws_re, ws_im = c32(w_s)
    tw_re, tw_im = c32(tw)
    g_re = tw_re[:, :, None] * ws_re[None] - tw_im[:, :, None] * ws_im[None]
    g_im = tw_re[:, :, None] * ws_im[None] + tw_im[:, :, None] * ws_re[None]
    fwd_a = jnp.concatenate([jnp.concatenate([g_re, -g_im], axis=2),
                             jnp.concatenate([g_im, g_re], axis=2)], axis=1)
    gt_re = jnp.swapaxes(g_re, 1, 2) * (1.0 / n)
    gt_im = jnp.swapaxes(g_im, 1, 2) * (1.0 / n)
    inv_a = jnp.concatenate([jnp.concatenate([gt_re, gt_im], axis=2),
                             jnp.concatenate([-gt_im, gt_re], axis=2)], axis=1)
    kk = np.arange(nf)
    w_f = np.exp(-2j * np.pi * ((kk[:, None] * kk[None, :]) % nf) / nf)
    f_re, f_im = c32(w_f)
    fwd_b = jnp.concatenate([jnp.concatenate([f_re, -f_im], axis=1),
                             jnp.concatenate([f_im, f_re], axis=1)], axis=0)
    inv_b = jnp.concatenate([jnp.concatenate([f_re, f_im], axis=1),
                             jnp.concatenate([-f_im, f_re], axis=1)], axis=0)
    return dict(fwd_a=fwd_a.astype(BF16), fwd_a_real=fwd_a[:, :, :ns // 2].astype(BF16),
                inv_a=inv_a.astype(BF16), fwd_b=fwd_b.astype(BF16), inv_b=inv_b.astype(BF16),
                fwd_b_part=fwd_b.reshape(2, nf, 2 * nf).astype(BF16))


def _filter_dft_kernel(x_ref, ga_ref, fb_ref, nrm_ref, k_ref, ar_scr, ai_scr):
    nf = DFT_FAST
    ns = ga_ref.shape[1] // 2

    def stage_a(j, carry):
        x = x_ref[pl.ds(j, ns // 2, stride=nf), :].astype(BF16)
        a = _dot(ga_ref[j], x)
        ar_scr[pl.ds(j, ns, stride=DFT_PITCH), :] = a[:ns]
        ai_scr[pl.ds(j, ns, stride=DFT_PITCH), :] = a[ns:]
        return carry

    lax.fori_loop(0, nf, stage_a, 0)
    inv_norm = 1.0 / (nrm_ref[...] + 1e-6)

    def stage_b(k, carry):
        src = pl.multiple_of(k * DFT_PITCH, SUBLANES)
        a = jnp.concatenate([ar_scr[pl.ds(src, nf), :], ai_scr[pl.ds(src, nf), :]], axis=0).astype(BF16)
        dst = pl.multiple_of(k * nf, nf)
        k_ref[pl.ds(dst, nf), :] = _dot(fb_ref[...], a) * inv_norm
        return carry

    lax.fori_loop(0, ns, stage_b, 0)


def hyena_filter_dft(eo, asum, tables):
    l, cols = eo.shape
    c = D_HYENA
    orders = cols // (2 * c)
    ct = c // LANES
    ga, fb = tables['fwd_a_real'], tables['fwd_b_part']
    ns = ga.shape[1] // 2
    scr = pltpu.VMEM((ns * DFT_PITCH, LANES), F32)
    return pl.pallas_call(
        _filter_dft_kernel,
        grid=(orders, ct, 2),
        in_specs=[
            pl.BlockSpec((l, LANES), lambda o, j, part: (0, (2 * o + part) * ct + j)),
            pl.BlockSpec(ga.shape, lambda o, j, part: (0, 0, 0)),
            pl.BlockSpec((None,) + fb.shape[1:], lambda o, j, part: (part, 0, 0)),
            pl.BlockSpec((1, LANES), lambda o, j, part: (0, o * ct + j)),
        ],
        out_specs=pl.BlockSpec((None, None, 2 * l, LANES), lambda o, j, part: (o, part, 0, j)),
        out_shape=jax.ShapeDtypeStruct((orders, 2, 2 * l, c), F32),
        scratch_shapes=[scr, scr],
        compiler_params=_cparams("parallel", "parallel", "parallel"),
        name="hyena_filter_dft",
    )(eo, ga, fb, asum)


def _dft_a_kernel(xr_ref, xi_ref, g_ref, ar_ref, ai_ref):
    half = xr_ref.shape[0] // SUBLANES
    ns = 2 * half
    for j in range(SUBLANES):
        rows = pl.ds(j, half, stride=SUBLANES)
        x = jnp.concatenate([xr_ref[rows, :], xi_ref[rows, :]], axis=0)
        a = _dot(g_ref[j], x.astype(BF16))
        ar_ref[pl.ds(j, ns, stride=SUBLANES), :] = a[:ns]
        ai_ref[pl.ds(j, ns, stride=SUBLANES), :] = a[ns:]


def dft_stage_a(u5, which, table):
    _, _, tiles, rows, c = u5.shape
    kern = functools.partial(_dft_a_kernel)
    x_spec = lambda bi: pl.BlockSpec((None, None, None, rows, LANES), lambda i, j: (which, bi, i, 0, j))
    o_spec = pl.BlockSpec((None, 2 * rows, LANES), lambda i, j: (i, 0, j))
    o_shape = jax.ShapeDtypeStruct((tiles, 2 * rows, c), F32)
    return pl.pallas_call(
        kern,
        grid=(tiles, c // LANES),
        in_specs=[x_spec(0), x_spec(1), pl.BlockSpec((SUBLANES,) + table.shape[1:], lambda i, j: (i, 0, 0))],
        out_specs=[o_spec, o_spec],
        out_shape=[o_shape, o_shape],
        compiler_params=_cparams("parallel", "parallel"),
        name="dft_stage_a",
    )(u5, u5, table)


def _dft_b_conv_kernel(ar_ref, ai_ref, kr_ref, ki_ref, fb_ref, ib_ref, cr_ref, ci_ref, *, slabs):
    nf = DFT_FAST
    tc = ar_ref.shape[-1]
    for s in range(slabs):
        tile_rows = slice(s * SUBLANES, (s + 1) * SUBLANES)
        a = jnp.concatenate([ar_ref[:, tile_rows, :].reshape(nf, tc), ai_ref[:, tile_rows, :].reshape(nf, tc)], axis=0)
        x = _dot(fb_ref[...], a.astype(BF16))
        xr, xi = x[:nf], x[nf:]
        kr, ki = kr_ref[s * nf:(s + 1) * nf, :], ki_ref[s * nf:(s + 1) * nf, :]
        y = jnp.concatenate([xr * kr - xi * ki, xr * ki + xi * kr], axis=0).astype(BF16)
        cc = _dot(ib_ref[...], y)
        cr_ref[:, tile_rows, :] = cc[:nf].reshape(DFT_TILES, SUBLANES, tc)
        ci_ref[:, tile_rows, :] = cc[nf:].reshape(DFT_TILES, SUBLANES, tc)


def dft_stage_b_conv(ar, ai, kf, order, fwd_b, inv_b, *, slabs, tc):
    tiles, rows, c = ar.shape
    kern = functools.partial(_dft_b_conv_kernel, slabs=slabs)
    spec = pl.BlockSpec((tiles, slabs * SUBLANES, tc), lambda i, j: (0, i, j))
    k_spec = lambda part: pl.BlockSpec((None, None, slabs * DFT_FAST, tc), lambda i, j: (order, part, i, j))
    mat = pl.BlockSpec(fwd_b.shape, lambda i, j: (0, 0))
    shape = jax.ShapeDtypeStruct(ar.shape, F32)
    return pl.pallas_call(
        kern,
        grid=(rows // (slabs * SUBLANES), c // tc),
        in_specs=[spec, spec, k_spec(0), k_spec(1), mat, mat],
        out_specs=[spec, spec],
        out_shape=[shape, shape],
        compiler_params=_cparams("parallel", "parallel"),
        name="dft_stage_b_conv",
    )(ar, ai, kf, kf, fwd_b, inv_b)


def _dft_a_inv_kernel(cr_ref, ci_ref, g_ref, gate_ref, u_ref, bias_ref, o_ref):
    ns = cr_ref.shape[0] // SUBLANES
    half = ns // 2
    for j in range(SUBLANES):
        rows = pl.ds(j, ns, stride=SUBLANES)
        cc = jnp.concatenate([cr_ref[rows, :], ci_ref[rows, :]], axis=0)
        y = _dot(g_ref[j], cc.astype(BF16))
        o_ref[0, pl.ds(j, half, stride=SUBLANES), :] = y[:half]
        o_ref[1, pl.ds(j, half, stride=SUBLANES), :] = y[half:]
    o_ref[...] = gate_ref[...] * (o_ref[...] + u_ref[...] * bias_ref[...])


def dft_stage_a_inv(cr, ci, table, gate5, gate_idx, u5, u_idx, bias):
    _, _, tiles, rows, c = u5.shape
    c_spec = pl.BlockSpec((None, 2 * rows, LANES), lambda i, j: (i, 0, j))
    t_spec = lambda k: pl.BlockSpec((None, 2, None, rows, LANES), lambda i, j: (k, 0, i, 0, j))
    return pl.pallas_call(
        _dft_a_inv_kernel,
        grid=(tiles, c // LANES),
        in_specs=[c_spec, c_spec, pl.BlockSpec((SUBLANES,) + table.shape[1:], lambda i, j: (i, 0, 0)),
                  t_spec(gate_idx), t_spec(u_idx), pl.BlockSpec((1, LANES), lambda i, j: (0, j))],
        out_specs=pl.BlockSpec((2, None, rows, LANES), lambda i, j: (0, i, 0, j)),
        out_shape=jax.ShapeDtypeStruct((2, tiles, rows, c), F32),
        compiler_params=_cparams("parallel", "parallel"),
        name="dft_stage_a_inv",
    )(cr, ci, table, gate5, u5, bias.reshape(1, c))


def hyena_filter_spectra(seq_len, tables, w1, b1, w2, b2, w3, freq):
    eo, asum = hyena_filter_mlp(seq_len, w1, b1, w2, b2, w3, freq, tl=512)
    return hyena_filter_dft(eo, asum, tables)


def hyena_long_conv(u5, u_idx, gate5, gate_idx, kf, order, bias, tables):
    ar, ai = dft_stage_a(u5, u_idx, tables['fwd_a'])
    cr, ci = dft_stage_b_conv(ar, ai, kf, order, tables['fwd_b'], tables['inv_b'], slabs=4, tc=512)
    return dft_stage_a_inv(cr, ci, tables['inv_a'], gate5, gate_idx, u5, u_idx, bias)


def hyena_mixer(p, short_w, short_b, kf, bias, tables):
    gates5 = hyena_short_conv(p, short_w, short_b, tm=512)
    z = hyena_long_conv(gates5, 2, gates5, 1, kf, 0, bias[0], tables)
    return hyena_long_conv(z[None], 0, gates5, 0, kf, 1, bias[1], tables)


def _odd_out_kernel(ys_ref, u_ref, hy_ref, x_ref, d_ref, wglu_ref, wo_ref, g_ref, o_ref):
    y = ys_ref[0] + d_ref[...] * u_ref[0]
    c0 = math.sqrt(2.0 / math.pi)
    gl = 0.5 * y * (1.0 + jnp.tanh(c0 * (y + 0.044715 * (y * y * y))))
    z = _dot(gl.astype(BF16), wglu_ref[...])
    s5 = gl * (1.0 / (1.0 + jnp.exp(-z)))
    mix_in = jnp.concatenate([s5.astype(BF16), _from_time_tiles(hy_ref).astype(BF16)], axis=-1)
    mix = _dot(mix_in, wo_ref[...])
    o_ref[0] = x_ref[0] + _rms(mix, g_ref[...])


def odd_out(ys, p, hy, x, d_skip, w_glu, w_out, layer, g, *, tm):
    b, l, d = x.shape
    tok = lambda w: pl.BlockSpec((1, tm, w), lambda bi, i: (bi, i, 0))
    hy_spec = pl.BlockSpec((None, DFT_TILES, tm // DFT_FAST * SUBLANES, D_HYENA), lambda bi, i: (bi, 0, i, 0))
    return pl.pallas_call(
        _odd_out_kernel,
        grid=(b, l // tm),
        in_specs=[tok(D_S5), tok(D_S5), hy_spec, tok(d),
                  pl.BlockSpec((1, D_S5), lambda bi, i: (0, 0)),
                  _layer_spec((D_S5, D_S5), layer, lambda bi, i: (0, 0)),
                  _layer_spec((d, d), layer, lambda bi, i: (0, 0)),
                  pl.BlockSpec((1, d), lambda bi, i: (0, 0))],
        out_specs=tok(d),
        out_shape=jax.ShapeDtypeStruct(x.shape, F32),
        compiler_params=_cparams("parallel", "parallel"),
        name="odd_out",
    )(ys, p, hy, x, d_skip.reshape(1, D_S5), w_glu, w_out, g.reshape(1, d))


def kernel(x, mem, norm_mix, norm_xattn, norm_mem, norm_mlp, xa_wq, xa_wk, xa_wv, xa_wo, mlp_w1, mlp_w2, ev_w_in, ev_pool_w, ev_pool_scale, ev_conv_w, ev_w_out, od_w_in, od_s5_lambda_re, od_s5_lambda_im, od_s5_log_dt, od_s5_b_re, od_s5_b_im, od_s5_c_re, od_s5_c_im, od_s5_d, od_s5_w_glu, od_hy_short_w, od_hy_short_b, od_hy_w1, od_hy_b1, od_hy_w2, od_hy_b2, od_hy_w3, od_hy_freq, od_hy_bias, od_w_out):
    b, l, d = x.shape
    depth = norm_mix.shape[0]
    assert b == 2, "the long convolution packs the two batch rows as one complex signal"
    tables = dft_tables(l)
    mem2d = mem.reshape(b * N_MEM, d)
    wq, wo = xa_wq.astype(BF16), xa_wo.astype(BF16)
    wkv = jnp.concatenate([xa_wk, xa_wv], axis=2).astype(BF16)
    w1, w2 = mlp_w1.astype(BF16), mlp_w2.astype(BF16)
    ev_in, ev_out, ev_pool = ev_w_in.astype(BF16), ev_w_out.astype(BF16), ev_pool_w.astype(BF16)
    od_in, od_out, od_glu = od_w_in.astype(BF16), od_w_out.astype(BF16), od_s5_w_glu.astype(BF16)
    for i in range(depth):
        j = i // 2
        if i % 2 == 0:
            p = norm_matmul(x.reshape(b * l, d), norm_mix[i, 0], ev_in, j, tm=512)
            x = even_mixer(p.reshape(b, l, -1), x, ev_pool, ev_pool_scale[j], ev_conv_w[j], ev_out, j,
                           norm_mix[i, 1], tm=512)
        else:
            p = norm_matmul(x.reshape(b * l, d), norm_mix[i, 0], od_in, j, tm=512).reshape(b, l, -1)
            mats = s5_matrices(od_s5_lambda_re[j], od_s5_lambda_im[j], od_s5_log_dt[j], od_s5_b_re[j],
                               od_s5_b_im[j], od_s5_c_re[j], od_s5_c_im[j])
            ys = s5_core(p, mats)
            kf = hyena_filter_spectra(l, tables, od_hy_w1[j], od_hy_b1[j], od_hy_w2[j], od_hy_b2[j],
                                      od_hy_w3[j], od_hy_freq[j])
            hy = hyena_mixer(p, od_hy_short_w[j], od_hy_short_b[j], kf, od_hy_bias[j], tables)
            x = odd_out(ys, p, hy, x, od_s5_d[j], od_glu, od_out, j, norm_mix[i, 1], tm=512)
        kv = norm_matmul(mem2d, norm_mem[i], wkv, i, tm=b * N_MEM, out_dtype=BF16).reshape(b, N_MEM, 2 * d)
        x = xattn_block(x, kv, wq, wo, i, norm_xattn[i, 0], norm_xattn[i, 1], tm=512)
        x = mlp_block(x.reshape(b * l, d), w1, w2, i, norm_mlp[i, 0], norm_mlp[i, 1],
                      tm=1024, tf=512).reshape(b, l, d)
    return x
```

```python
import functools
import math

import numpy as np
import jax
import jax.numpy as jnp
from jax import lax
from jax.experimental import pallas as pl
from jax.experimental.pallas import tpu as pltpu

F32 = jnp.float32
BF16 = jnp.bfloat16

D_MODEL = 1024
N_MEM = 256
RMS_EPS = 1e-6
D_POOL = 512
POOL_WINDOWS = (2, 4, 8, 16)
POOL_GROUP_DIM = 128
POOL_HALO = 8
D_CONV = 512
D_S5 = 512
S5_GROUP_DIM = 16
S5_GROUPS = 32
S5_STATE = 64
S5_CHUNK = 16
D_HYENA = 512
HYENA_BANDS = 16
HYENA_FFN = 64
HYENA_TARGET = 1e-2
HYENA_SHORT_DECAY_PCT = 0.3
HYENA_LONG_DECAY_PCT = 1.5
XA_HEADS = 4
XA_HEAD_DIM = 256
D_FF = 4096

LANES = 128
SUBLANES = 8
S5_SUPER = LANES // S5_GROUP_DIM
DFT_FAST = 128
DFT_TILES = DFT_FAST // SUBLANES
DFT_PITCH = DFT_FAST + SUBLANES
DFT_UNROLL = 16
VMEM_LIMIT = 56 * 1024 * 1024


def _cparams(*sem):
    return pltpu.CompilerParams(dimension_semantics=sem, vmem_limit_bytes=VMEM_LIMIT)


def _rms(xf, g):
    ms = jnp.mean(xf * xf, axis=-1, keepdims=True)
    return xf * lax.rsqrt(ms + RMS_EPS) * g


def _dot(a, b):
    return jnp.dot(a, b, preferred_element_type=F32)


def _layer_spec(block, layer, tail_map):
    return pl.BlockSpec((None,) + block, lambda *idx: (layer,) + tail_map(*idx))


def _norm_matmul_kernel(x_ref, g_ref, w_ref, o_ref):
    xn = _rms(x_ref[...], g_ref[...]).astype(BF16)
    o_ref[...] = _dot(xn, w_ref[...]).astype(o_ref.dtype)


def norm_matmul(x2d, g, w, layer, *, tm, out_dtype=F32):
    m, d = x2d.shape
    n = w.shape[-1]
    return pl.pallas_call(
        _norm_matmul_kernel,
        grid=(m // tm,),
        in_specs=[
            pl.BlockSpec((tm, d), lambda i: (i, 0)),
            pl.BlockSpec((1, d), lambda i: (0, 0)),
            _layer_spec((d, n), layer, lambda i: (0, 0)),
        ],
        out_specs=pl.BlockSpec((tm, n), lambda i: (i, 0)),
        out_shape=jax.ShapeDtypeStruct((m, n), out_dtype),
        compiler_params=_cparams("parallel"),
        name="norm_matmul",
    )(x2d, g.reshape(1, d), w)


def _xattn_kernel(x_ref, kv_ref, wq_ref, wo_ref, g1_ref, g2_ref, o_ref):
    x = x_ref[0]
    xn = _rms(x, g1_ref[...]).astype(BF16)
    q = (_dot(xn, wq_ref[...]) * (XA_HEAD_DIM ** -0.5)).astype(BF16)
    heads = []
    for h in range(XA_HEADS):
        lo = h * XA_HEAD_DIM
        qh = q[:, lo:lo + XA_HEAD_DIM]
        kh = kv_ref[0, :, lo:lo + XA_HEAD_DIM]
        vh = kv_ref[0, :, D_MODEL + lo:D_MODEL + lo + XA_HEAD_DIM]
        s = lax.dot_general(qh, kh, (((1,), (1,)), ((), ())), preferred_element_type=F32)
        e = jnp.exp(s - jnp.max(s, axis=-1, keepdims=True))
        p = e / jnp.sum(e, axis=-1, keepdims=True)
        heads.append(_dot(p.astype(BF16), vh).astype(BF16))
    o = jnp.concatenate(heads, axis=-1)
    y = _dot(o, wo_ref[...])
    o_ref[0] = x + _rms(y, g2_ref[...])


def xattn_block(x, kv, wq, wo, layer, g1, g2, *, tm):
    b, l, d = x.shape
    return pl.pallas_call(
        _xattn_kernel,
        grid=(b, l // tm),
        in_specs=[
            pl.BlockSpec((1, tm, d), lambda bi, i: (bi, i, 0)),
            pl.BlockSpec((1, N_MEM, 2 * d), lambda bi, i: (bi, 0, 0)),
            _layer_spec((d, d), layer, lambda bi, i: (0, 0)),
            _layer_spec((d, d), layer, lambda bi, i: (0, 0)),
            pl.BlockSpec((1, d), lambda bi, i: (0, 0)),
            pl.BlockSpec((1, d), lambda bi, i: (0, 0)),
        ],
        out_specs=pl.BlockSpec((1, tm, d), lambda bi, i: (bi, i, 0)),
        out_shape=jax.ShapeDtypeStruct(x.shape, F32),
        compiler_params=_cparams("parallel", "parallel"),
        name="xattn_block",
    )(x, kv, wq, wo, g1.reshape(1, d), g2.reshape(1, d))


def _mlp_kernel(x_ref, w1_ref, w2_ref, g1_ref, g2_ref, o_ref, xn_ref, acc_ref):
    j = pl.program_id(1)

    @pl.when(j == 0)
    def _():
        xn_ref[...] = _rms(x_ref[...], g1_ref[...]).astype(BF16)
        acc_ref[...] = jnp.zeros_like(acc_ref)

    h = jnp.maximum(_dot(xn_ref[...], w1_ref[...]), 0.0)
    acc_ref[...] += _dot((h * h).astype(BF16), w2_ref[...])

    @pl.when(j == pl.num_programs(1) - 1)
    def _():
        o_ref[...] = x_ref[...] + _rms(acc_ref[...], g2_ref[...])


def mlp_block(x2d, w1, w2, layer, g1, g2, *, tm, tf):
    m, d = x2d.shape
    ff = w1.shape[-1]
    return pl.pallas_call(
        _mlp_kernel,
        grid=(m // tm, ff // tf),
        in_specs=[
            pl.BlockSpec((tm, d), lambda i, j: (i, 0)),
            _layer_spec((d, tf), layer, lambda i, j: (0, j)),
            _layer_spec((tf, d), layer, lambda i, j: (j, 0)),
            pl.BlockSpec((1, d), lambda i, j: (0, 0)),
            pl.BlockSpec((1, d), lambda i, j: (0, 0)),
        ],
        out_specs=pl.BlockSpec((tm, d), lambda i, j: (i, 0)),
        out_shape=jax.ShapeDtypeStruct((m, d), F32),
        scratch_shapes=[pltpu.VMEM((tm, d), BF16), pltpu.VMEM((tm, d), F32)],
        compiler_params=_cparams("parallel", "arbitrary"),
        name="mlp_block",
    )(x2d, w1, w2, g1.reshape(1, d), g2.reshape(1, d))


def _halo_specs(tm, seq_len, width, col):
    r = tm // POOL_HALO
    last = seq_len // POOL_HALO - 1

    def prev_map(bi, i, *_):
        return (bi, jnp.maximum(i * r - 1, 0), col(*_))

    def main_map(bi, i, *_):
        return (bi, i, col(*_))

    def next_map(bi, i, *_):
        return (bi, jnp.minimum((i + 1) * r, last), col(*_))

    return [
        pl.BlockSpec((1, POOL_HALO, width), prev_map),
        pl.BlockSpec((1, tm, width), main_map),
        pl.BlockSpec((1, POOL_HALO, width), next_map),
    ]


def _with_halo(prev_ref, main_ref, next_ref):
    i = pl.program_id(1)
    prev = jnp.where(i > 0, prev_ref[0], 0.0)
    nxt = jnp.where(i < pl.num_programs(1) - 1, next_ref[0], 0.0)
    return jnp.concatenate([prev, main_ref[0], nxt], axis=0)


def _shift_rows(v, k):
    return pltpu.roll(v, k % v.shape[0], 0)


def _even_mixer_kernel(pp_ref, p_ref, pn_ref, x_ref, wg_ref, ps_ref, cw_ref, wo_ref, g_ref, o_ref,
                       *, tm, seq_len):
    ext = _with_halo(pp_ref, p_ref, pn_ref)
    lo, hi = POOL_HALO, POOL_HALO + tm
    t = pl.program_id(1) * tm + lax.broadcasted_iota(jnp.int32, (tm, 1), 0)
    parts = []
    for gi, win in enumerate(POOL_WINDOWS):
        half = win // 2
        u = ext[:, gi * POOL_GROUP_DIM:(gi + 1) * POOL_GROUP_DIM]
        s = u + _shift_rows(u, 1)
        step = 1
        while 2 * step < win:
            s = _shift_rows(s, step) + _shift_rows(s, -step)
            step *= 2
        cnt = (jnp.minimum(t + half, seq_len) - jnp.maximum(t - half, 0)).astype(F32)
        pooled = s[lo:hi] / cnt - u[lo:hi]
        y = _dot(pooled.astype(BF16), wg_ref[gi])
        parts.append((y * ps_ref[:, gi * POOL_GROUP_DIM:(gi + 1) * POOL_GROUP_DIM]).astype(BF16))
    b_gate = ext[lo:hi, D_POOL:D_POOL + D_CONV]
    ch = ext[:, D_POOL + D_CONV:D_POOL + 2 * D_CONV] * ext[:, D_POOL + 2 * D_CONV:D_POOL + 3 * D_CONV]
    conv = cw_ref[0:1, :] * _shift_rows(ch, 1) + cw_ref[1:2, :] * ch + cw_ref[2:3, :] * _shift_rows(ch, -1)
    parts.append((b_gate * conv[lo:hi]).astype(BF16))
    mix = _dot(jnp.concatenate(parts, axis=-1), wo_ref[...])
    o_ref[0] = x_ref[0] + _rms(mix, g_ref[...])


def even_mixer(p, x, w_group, pool_scale, conv_w, w_out, layer, g, *, tm):
    b, l, d = x.shape
    width = p.shape[-1]
    kern = functools.partial(_even_mixer_kernel, tm=tm, seq_len=l)
    return pl.pallas_call(
        kern,
        grid=(b, l // tm),
        in_specs=_halo_specs(tm, l, width, lambda: 0) + [
            pl.BlockSpec((1, tm, d), lambda bi, i: (bi, i, 0)),
            _layer_spec(w_group.shape[1:], layer, lambda bi, i: (0, 0, 0)),
            pl.BlockSpec((1, D_POOL), lambda bi, i: (0, 0)),
            pl.BlockSpec((3, D_CONV), lambda bi, i: (0, 0)),
            _layer_spec((d, d), layer, lambda bi, i: (0, 0)),
            pl.BlockSpec((1, d), lambda bi, i: (0, 0)),
        ],
        out_specs=pl.BlockSpec((1, tm, d), lambda bi, i: (bi, i, 0)),
        out_shape=jax.ShapeDtypeStruct(x.shape, F32),
        compiler_params=_cparams("parallel", "parallel"),
        name="even_mixer",
    )(p, p, p, x, w_group, pool_scale.reshape(1, D_POOL), conv_w, w_out, g.reshape(1, d))


def _cmul(ar, ai, br, bi):
    return ar * br - ai * bi, ar * bi + ai * br


def s5_matrices(lam_re, lam_im, log_dt, b_re, b_im, c_re, c_im):
    tc, hd, n, sup = S5_CHUNK, S5_GROUP_DIM, S5_STATE, S5_SUPER
    n_tiles = S5_GROUPS // sup
    hp = lax.Precision.HIGHEST
    lr = jnp.minimum(lam_re, -1e-4)
    li = lam_im
    dt = jnp.exp(log_dt)[..., None]
    taus = jnp.arange(tc + 1, dtype=F32)[:, None, None, None]
    rmag = jnp.exp(lr[None] * dt[None] * (tc - taus))
    rang = li[None] * dt[None] * (tc - taus)
    pwr_re, pwr_im = rmag * jnp.cos(rang), rmag * jnp.sin(rang)
    mag = jnp.exp(lr[None] * dt[None] * taus)
    ang = li[None] * dt[None] * taus
    pw_re, pw_im = mag * jnp.cos(ang), mag * jnp.sin(ang)
    nr, ni = pw_re[1] - 1.0, pw_im[1]
    den = lr * lr + li * li
    coef_re, coef_im = (nr * lr + ni * li) / den, (ni * lr - nr * li) / den
    cb_re, cb_im = _cmul(coef_re[..., None], coef_im[..., None], b_re[None], b_im[None])
    cbt_re, cbt_im = _cmul(coef_re[:, :, None, :], coef_im[:, :, None, :],
                           jnp.swapaxes(b_re, 1, 2)[None], jnp.swapaxes(b_im, 1, 2)[None])
    spread = jnp.asarray(np.tile(np.eye(hd), (1, sup)), F32)
    same_group = jnp.asarray(np.kron(np.eye(sup), np.ones((hd, hd))), F32)
    spread_lanes = lambda z: jnp.einsum('...k,kc->...c', z, spread, precision=hp)
    q_re, q_im = _cmul(pw_re[:tc, :, :, :, None], pw_im[:tc, :, :, :, None], cb_re[None], cb_im[None])
    m = (jnp.einsum('dgkn,tdgnj->dtgjk', c_re, q_re, precision=hp)
         - jnp.einsum('dgkn,tdgnj->dtgjk', c_im, q_im, precision=hp))
    m_fwd, m_bwd = m[0], m[1]
    diag = m_fwd[0] + m_bwd[0]
    lag_table = jnp.concatenate([m_bwd[:0:-1], diag[None], m_fwd[1:]], axis=0)
    lag_blocks = spread_lanes(lag_table.reshape(2 * tc - 1, n_tiles, LANES, hd)) * same_group
    lag_blocks = jnp.transpose(lag_blocks, (1, 2, 0, 3))
    intra = jnp.stack([lag_blocks[:, :, tc - 1 - s:2 * tc - 1 - s, :].reshape(n_tiles, LANES, tc * LANES)
                       for s in range(tc)], axis=1)
    intra = intra.reshape(n_tiles, tc * LANES, tc * LANES)
    sf_re, sf_im = _cmul(pwr_re[1:tc + 1, 0][:, :, None, :], pwr_im[1:tc + 1, 0][:, :, None, :],
                         cbt_re[0][None], cbt_im[0][None])
    sb_re, sb_im = _cmul(pw_re[:tc, 1][:, :, None, :], pw_im[:tc, 1][:, :, None, :],
                         cbt_re[1][None], cbt_im[1][None])
    summ = jnp.concatenate([sf_re, sb_re, sf_im, sb_im], axis=-1)
    summ = jnp.swapaxes(summ.reshape(tc, n_tiles, LANES, 4 * n), 0, 1)
    row_group = jnp.asarray(np.kron(np.eye(sup), np.ones((hd, 1))), F32)
    summ = summ[:, :, :, None, :] * row_group[:, :, None]
    summ = summ.reshape(n_tiles, tc * LANES, sup * 4 * n)
    ct_re, ct_im = jnp.swapaxes(c_re, 2, 3), jnp.swapaxes(c_im, 2, 3)
    gnt = lambda z: jnp.transpose(z, (1, 2, 0))[..., None]
    qf_re, qf_im = _cmul(ct_re[0][:, :, None, :], ct_im[0][:, :, None, :],
                         gnt(pw_re[1:tc + 1, 0]), gnt(pw_im[1:tc + 1, 0]))
    qb_re, qb_im = _cmul(ct_re[1][:, :, None, :], ct_im[1][:, :, None, :], gnt(pwr_re[:tc, 1]), gnt(pwr_im[:tc, 1]))
    carry = jnp.concatenate([qf_re, qb_re, -qf_im, -qb_im], axis=1)
    col_group = jnp.asarray(np.kron(np.eye(sup), np.ones((1, hd))), F32)
    carry = spread_lanes(carry.reshape(n_tiles, sup, 4 * n, tc, hd)) * col_group[:, None, None, :]
    carry = carry.reshape(n_tiles, sup * 4 * n, tc * LANES)
    a_re = jnp.concatenate([pw_re[tc, 0], pw_re[tc, 1]], axis=-1)
    a_im = jnp.concatenate([pw_im[tc, 0], pw_im[tc, 1]], axis=-1)
    return intra, summ, carry, a_re, a_im


def _s5_gather_chunks(p_ref, n_chunks):
    cols = [p_ref[pl.ds(t, n_chunks, stride=S5_CHUNK), :].astype(BF16) for t in range(S5_CHUNK)]
    return jnp.concatenate(cols, axis=-1)


def _s5_summary_kernel(p_ref, w_ref, sr_ref, si_ref, *, n_chunks):
    n2 = 2 * S5_STATE
    u = _s5_gather_chunks(p_ref, n_chunks)
    for gl in range(S5_SUPER):
        s = _dot(u, w_ref[:, gl * 2 * n2:(gl + 1) * 2 * n2])
        sr_ref[pl.ds(gl, n_chunks, stride=S5_SUPER), :] = s[:, :n2]
        si_ref[pl.ds(gl, n_chunks, stride=S5_SUPER), :] = s[:, n2:]


def _s5_state_spec(c, n2, n_tiles):
    return pl.BlockSpec((None, c * S5_SUPER, n2), lambda a, bi: (bi * n_tiles + a, 0, 0))


def s5_summary(p, summ):
    b, l, _ = p.shape
    c = l // S5_CHUNK
    n_tiles = summ.shape[0]
    n2 = 2 * S5_STATE
    shape = jax.ShapeDtypeStruct((b * n_tiles, c * S5_SUPER, n2), F32)
    kern = functools.partial(_s5_summary_kernel, n_chunks=c)
    return pl.pallas_call(
        kern,
        grid=(n_tiles, b),
        in_specs=[
            pl.BlockSpec((None, l, LANES), lambda a, bi: (bi, 0, a)),
            pl.BlockSpec((None,) + summ.shape[1:], lambda a, bi: (a, 0, 0)),
        ],
        out_specs=[_s5_state_spec(c, n2, n_tiles)] * 2,
        out_shape=[shape, shape],
        compiler_params=_cparams("parallel", "parallel"),
        name="s5_summary",
    )(p, summ)


def _s5_scan_kernel(sr_ref, si_ref, are_ref, aim_ref, or_ref, oi_ref, *, n_chunks):
    n2 = 2 * S5_STATE
    a_re = are_ref[...]
    a_im = aim_ref[...]
    fwd_lane = lax.broadcasted_iota(jnp.int32, (SUBLANES, n2), 1) < S5_STATE

    def body(i, state):
        x_re, x_im = state
        rows_i = pl.ds(pl.multiple_of(i * SUBLANES, SUBLANES), SUBLANES)
        rows_r = pl.ds(pl.multiple_of((n_chunks - 1 - i) * SUBLANES, SUBLANES), SUBLANES)
        or_ref[rows_i, 0:S5_STATE] = x_re[:, :S5_STATE]
        oi_ref[rows_i, 0:S5_STATE] = x_im[:, :S5_STATE]
        or_ref[rows_r, S5_STATE:n2] = x_re[:, S5_STATE:]
        oi_ref[rows_r, S5_STATE:n2] = x_im[:, S5_STATE:]
        s_re = jnp.where(fwd_lane, sr_ref[rows_i, :], sr_ref[rows_r, :])
        s_im = jnp.where(fwd_lane, si_ref[rows_i, :], si_ref[rows_r, :])
        return (a_re * x_re - a_im * x_im + s_re, a_re * x_im + a_im * x_re + s_im)

    zero = jnp.zeros((SUBLANES, n2), F32)
    lax.fori_loop(0, n_chunks, body, (zero, zero))


def s5_scan(s_re, s_im, a_re, a_im):
    tiles, rows, n2 = s_re.shape
    kern = functools.partial(_s5_scan_kernel, n_chunks=rows // S5_SUPER)
    st = pl.BlockSpec((None, rows, n2), lambda i: (i, 0, 0))
    av = pl.BlockSpec((SUBLANES, n2), lambda i: (i, 0))
    shape = jax.ShapeDtypeStruct(s_re.shape, F32)
    return pl.pallas_call(
        kern,
        grid=(tiles,),
        in_specs=[st, st, av, av],
        out_specs=[st, st],
        out_shape=[shape, shape],
        compiler_params=_cparams("parallel"),
        name="s5_scan",
    )(s_re, s_im, a_re, a_im)


def _s5_output_kernel(p_ref, xr_ref, xi_ref, wi_ref, wc_ref, o_ref, *, n_chunks):
    n2 = 2 * S5_STATE
    u = _s5_gather_chunks(p_ref, n_chunks)
    xs = []
    for gl in range(S5_SUPER):
        xs.append(xr_ref[pl.ds(gl, n_chunks, stride=S5_SUPER), :].astype(BF16))
        xs.append(xi_ref[pl.ds(gl, n_chunks, stride=S5_SUPER), :].astype(BF16))
    xin = jnp.concatenate(xs, axis=-1)
    per_dot = 4
    for q in range(S5_CHUNK // per_dot):
        cols = slice(q * per_dot * LANES, (q + 1) * per_dot * LANES)
        y = _dot(u, wi_ref[:, cols]) + _dot(xin, wc_ref[:, cols])
        for tt in range(per_dot):
            o_ref[pl.ds(q * per_dot + tt, n_chunks, stride=S5_CHUNK), :] = y[:, tt * LANES:(tt + 1) * LANES]


def s5_output(p, xin_re, xin_im, intra, carry):
    b, l, _ = p.shape
    c = l // S5_CHUNK
    n_tiles = intra.shape[0]
    n2 = 2 * S5_STATE
    kern = functools.partial(_s5_output_kernel, n_chunks=c)
    once = dict(pipeline_mode=pl.Buffered(1))
    return pl.pallas_call(
        kern,
        grid=(n_tiles, b),
        in_specs=[
            pl.BlockSpec((None, l, LANES), lambda a, bi: (bi, 0, a)),
            _s5_state_spec(c, n2, n_tiles),
            _s5_state_spec(c, n2, n_tiles),
            pl.BlockSpec((None,) + intra.shape[1:], lambda a, bi: (a, 0, 0), **once),
            pl.BlockSpec((None,) + carry.shape[1:], lambda a, bi: (a, 0, 0), **once),
        ],
        out_specs=pl.BlockSpec((None, l, LANES), lambda a, bi: (bi, 0, a)),
        out_shape=jax.ShapeDtypeStruct((b, l, D_S5), F32),
        compiler_params=_cparams("parallel", "parallel"),
        name="s5_output",
    )(p, xin_re, xin_im, intra, carry)


def s5_core(p, mats):
    b = p.shape[0]
    intra, summ, carry, a_re, a_im = mats
    s_re, s_im = s5_summary(p, summ.astype(BF16))
    xin_re, xin_im = s5_scan(s_re, s_im, jnp.tile(a_re, (b, 1)), jnp.tile(a_im, (b, 1)))
    return s5_output(p, xin_re, xin_im, intra.astype(BF16), carry.astype(BF16))


def _to_time_tiles(y):
    nf, c = DFT_FAST, y.shape[-1]
    slabs = [y[s * nf:(s + 1) * nf].reshape(DFT_TILES, SUBLANES, c) for s in range(y.shape[0] // nf)]
    return jnp.concatenate(slabs, axis=1)


def _from_time_tiles(ref):
    c = ref.shape[-1]
    slabs = [ref[:, s * SUBLANES:(s + 1) * SUBLANES, :].reshape(DFT_FAST, c) for s in range(ref.shape[1] // SUBLANES)]
    return jnp.concatenate(slabs, axis=0)


def _short_conv_kernel(pp_ref, p_ref, pn_ref, w_ref, b_ref, o_ref, *, tm):
    ext = _with_halo(pp_ref, p_ref, pn_ref)
    y = w_ref[0:1, :] * _shift_rows(ext, 1) + w_ref[1:2, :] * ext + w_ref[2:3, :] * _shift_rows(ext, -1)
    o_ref[...] = _to_time_tiles(y[POOL_HALO:POOL_HALO + tm] + b_ref[...])


def hyena_short_conv(p, w, bias, *, tm):
    b, l, _ = p.shape
    kern = functools.partial(_short_conv_kernel, tm=tm)
    rows = tm // DFT_FAST * SUBLANES
    return pl.pallas_call(
        kern,
        grid=(b, l // tm, 3),
        in_specs=_halo_specs(tm, l, D_HYENA, lambda j: j + 1) + [
            pl.BlockSpec((3, D_HYENA), lambda bi, i, j: (0, j)),
            pl.BlockSpec((1, D_HYENA), lambda bi, i, j: (0, j)),
        ],
        out_specs=pl.BlockSpec((None, None, DFT_TILES, rows, D_HYENA), lambda bi, i, j: (j, bi, 0, i, 0)),
        out_shape=jax.ShapeDtypeStruct((3, b, DFT_TILES, l // DFT_FAST * SUBLANES, D_HYENA), F32),
        compiler_params=_cparams("parallel", "parallel", "parallel"),
        name="hyena_short_conv",
    )(p, p, p, w, bias.reshape(1, -1))


def _filter_mlp_kernel(bands_ref, w1t_ref, w1cs_ref, b1_ref, w2_ref, b2_ref, w3_ref, fr_ref, dl_ref,
                       eo_ref, asum_ref, *, tl, seq_len):
    i = pl.program_id(0)
    hp = lax.Precision.HIGHEST
    c = D_HYENA
    t = (i * tl + lax.broadcasted_iota(jnp.int32, (tl, 1), 0)).astype(F32)
    t_norm = t / (seq_len - 1.0)
    ang = (2.0 * math.pi / seq_len) * t * bands_ref[...]
    is_cos = lax.broadcasted_iota(jnp.int32, ang.shape, 1) < HYENA_BANDS
    cs = jnp.where(is_cos, jnp.cos(ang), -jnp.sin(ang))
    fr = fr_ref[...]
    z = t_norm * w1t_ref[...] + jnp.dot(cs, w1cs_ref[...], precision=hp, preferred_element_type=F32)
    h = jnp.sin(fr * (z + b1_ref[...]))
    h = jnp.sin(fr * (jnp.dot(h, w2_ref[...], precision=hp, preferred_element_type=F32) + b2_ref[...]))
    h = jnp.dot(h, w3_ref[...], precision=hp, preferred_element_type=F32)
    decay = jnp.exp(-t_norm * dl_ref[...])

    @pl.when(i == 0)
    def _():
        asum_ref[...] = jnp.zeros_like(asum_ref)

    for o in range(h.shape[1] // (2 * c)):
        fwd = h[:, 2 * o * c:(2 * o + 1) * c] * decay
        bwd = h[:, (2 * o + 1) * c:(2 * o + 2) * c] * decay
        eo_ref[:, 2 * o * c:(2 * o + 1) * c] = fwd + bwd
        eo_ref[:, (2 * o + 1) * c:(2 * o + 2) * c] = fwd - bwd
        asum_ref[:, o * c:(o + 1) * c] += jnp.sum(jnp.abs(fwd) + jnp.abs(bwd), axis=0, keepdims=True)


def hyena_filter_mlp(seq_len, w1, b1, w2, b2, w3, freq, *, tl):
    n_out = w3.shape[1]
    bands = jnp.linspace(1e-4, HYENA_BANDS - 1, HYENA_BANDS, dtype=F32)
    bands2 = jnp.concatenate([bands, bands])[None, :]
    deltas = jnp.abs(jnp.linspace(math.log(HYENA_TARGET) / HYENA_LONG_DECAY_PCT,
                                  math.log(HYENA_TARGET) / HYENA_SHORT_DECAY_PCT, D_HYENA, dtype=F32))[None, :]
    kern = functools.partial(_filter_mlp_kernel, tl=tl, seq_len=seq_len)
    full = lambda a: pl.BlockSpec(a.shape, lambda i: (0,) * a.ndim)
    args = [bands2, w1[0:1], w1[1:], b1[None, :], w2, b2[None, :], w3, freq[None, :], deltas]
    return pl.pallas_call(
        kern,
        grid=(seq_len // tl,),
        in_specs=[full(a) for a in args],
        out_specs=[pl.BlockSpec((tl, n_out), lambda i: (i, 0)), pl.BlockSpec((1, n_out // 2), lambda i: (0, 0))],
        out_shape=[jax.ShapeDtypeStruct((seq_len, n_out), F32), jax.ShapeDtypeStruct((1, n_out // 2), F32)],
        compiler_params=_cparams("arbitrary"),
        name="hyena_filter_mlp",
    )(*args)


def dft_tables(seq_len):
    n = 2 * seq_len
    nf = DFT_FAST
    ns = n // nf
    k_a = np.arange(ns)[:, None]
    n_s = np.arange(ns // 2)[None, :]
    w_s = np.exp(-2j * np.pi * ((k_a * n_s) % ns) / ns)
    n_f = np.arange(nf)[:, None]
    tw = np.exp(-2j * np.pi * ((n_f * np.arange(ns)[None, :]) % n) / n)
    c32 = lambda z: (jnp.asarray(z.real, F32), jnp.asarray(z.imag, F32))
    ws_re, ws_im = c32(w_s)
    tw_re, tw_im = c32(tw)
    g_re = tw_re[:, :, None] * ws_re[None] - tw_im[:, :, None] * ws_im[None]
    g_im = tw_re[:, :, None] * ws_im[None] + tw_im[:, :, None] * ws_re[None]
    fwd_a = jnp.concatenate([jnp.concatenate([g_re, -g_im], axis=2),
                             jnp.concatenate([g_im, g_re], axis=2)], axis=1)
    gt_re = jnp.swapaxes(g_re, 1, 2) * (1.0 / n)
    gt_im = jnp.swapaxes(g_im, 1, 2) * (1.0 / n)
    inv_a = jnp.concatenate([jnp.concatenate([gt_re, gt_im], axis=2),
                             jnp.concatenate([-gt_im, gt_re], axis=2)], axis=1)
    kk = np.arange(nf)
    w_f = np.exp(-2j * np.pi * ((kk[:, None] * kk[None, :]) % nf) / nf)
    f_re, f_im = c32(w_f)
    fwd_b = jnp.concatenate([jnp.concatenate([f_re, -f_im], axis=1),
                             jnp.concatenate([f_im, f_re], axis=1)], axis=0)
    inv_b = jnp.concatenate([jnp.concatenate([f_re, f_im], axis=1),
                             jnp.concatenate([-f_im, f_re], axis=1)], axis=0)
    return dict(fwd_a=fwd_a.astype(BF16), fwd_a_real=fwd_a[:, :, :ns // 2].astype(BF16),
                inv_a=inv_a.astype(BF16), fwd_b=fwd_b.astype(BF16), inv_b=inv_b.astype(BF16),
                fwd_b_part=fwd_b.reshape(2, nf, 2 * nf).astype(BF16))


def _filter_dft_kernel(x_ref, ga_ref, fb_ref, nrm_ref, k_ref, ar_scr, ai_scr):
    nf = DFT_FAST
    ns = ga_ref.shape[1] // 2

    def stage_a(j, carry):
        x = x_ref[pl.ds(j, ns // 2, stride=nf), :].astype(BF16)
        a = _dot(ga_ref[j], x)
        ar_scr[pl.ds(j, ns, stride=DFT_PITCH), :] = a[:ns]
        ai_scr[pl.ds(j, ns, stride=DFT_PITCH), :] = a[ns:]
        return carry

    lax.fori_loop(0, nf, stage_a, 0, unroll=DFT_UNROLL)
    inv_norm = 1.0 / (nrm_ref[...] + 1e-6)

    def stage_b(k, carry):
        src = pl.multiple_of(k * DFT_PITCH, SUBLANES)
        a = jnp.concatenate([ar_scr[pl.ds(src, nf), :], ai_scr[pl.ds(src, nf), :]], axis=0).astype(BF16)
        dst = pl.multiple_of(k * nf, nf)
        k_ref[pl.ds(dst, nf), :] = _dot(fb_ref[...], a) * inv_norm
        return carry

    lax.fori_loop(0, ns, stage_b, 0, unroll=DFT_UNROLL)


def hyena_filter_dft(eo, asum, tables):
    l, cols = eo.shape
    c = D_HYENA
    orders = cols // (2 * c)
    ct = c // LANES
    ga, fb = tables['fwd_a_real'], tables['fwd_b_part']
    ns = ga.shape[1] // 2
    scr = pltpu.VMEM((ns * DFT_PITCH, LANES), F32)
    return pl.pallas_call(
        _filter_dft_kernel,
        grid=(orders, ct, 2),
        in_specs=[
            pl.BlockSpec((l, LANES), lambda o, j, part: (0, (2 * o + part) * ct + j)),
            pl.BlockSpec(ga.shape, lambda o, j, part: (0, 0, 0)),
            pl.BlockSpec((None,) + fb.shape[1:], lambda o, j, part: (part, 0, 0)),
            pl.BlockSpec((1, LANES), lambda o, j, part: (0, o * ct + j)),
        ],
        out_specs=pl.BlockSpec((None, None, 2 * l, LANES), lambda o, j, part: (o, part, 0, j)),
        out_shape=jax.ShapeDtypeStruct((orders, 2, 2 * l, c), F32),
        scratch_shapes=[scr, scr],
        compiler_params=_cparams("parallel", "parallel", "parallel"),
        name="hyena_filter_dft",
    )(eo, ga, fb, asum)


def _dft_a_kernel(xr_ref, xi_ref, g_ref, ar_ref, ai_ref):
    half = xr_ref.shape[0] // SUBLANES
    ns = 2 * half
    for j in range(SUBLANES):
        rows = pl.ds(j, half, stride=SUBLANES)
        x = jnp.concatenate([xr_ref[rows, :], xi_ref[rows, :]], axis=0)
        a = _dot(g_ref[j], x.astype(BF16))
        ar_ref[pl.ds(j, ns, stride=SUBLANES), :] = a[:ns]
        ai_ref[pl.ds(j, ns, stride=SUBLANES), :] = a[ns:]


def dft_stage_a(u5, which, table):
    _, _, tiles, rows, c = u5.shape
    kern = functools.partial(_dft_a_kernel)
    x_spec = lambda bi: pl.BlockSpec((None, None, None, rows, LANES), lambda i, j: (which, bi, i, 0, j))
    o_spec = pl.BlockSpec((None, 2 * rows, LANES), lambda i, j: (i, 0, j))
    o_shape = jax.ShapeDtypeStruct((tiles, 2 * rows, c), F32)
    return pl.pallas_call(
        kern,
        grid=(tiles, c // LANES),
        in_specs=[x_spec(0), x_spec(1), pl.BlockSpec((SUBLANES,) + table.shape[1:], lambda i, j: (i, 0, 0))],
        out_specs=[o_spec, o_spec],
        out_shape=[o_shape, o_shape],
        compiler_params=_cparams("parallel", "parallel"),
        name="dft_stage_a",
    )(u5, u5, table)


def _dft_b_conv_kernel(ar_ref, ai_ref, kr_ref, ki_ref, fb_ref, ib_ref, cr_ref, ci_ref, *, slabs):
    nf = DFT_FAST
    tc = ar_ref.shape[-1]
    for s in range(slabs):
        tile_rows = slice(s * SUBLANES, (s + 1) * SUBLANES)
        a = jnp.concatenate([ar_ref[:, tile_rows, :].reshape(nf, tc), ai_ref[:, tile_rows, :].reshape(nf, tc)], axis=0)
        x = _dot(fb_ref[...], a.astype(BF16))
        xr, xi = x[:nf], x[nf:]
        kr, ki = kr_ref[s * nf:(s + 1) * nf, :], ki_ref[s * nf:(s + 1) * nf, :]
        y = jnp.concatenate([xr * kr - xi * ki, xr * ki + xi * kr], axis=0).astype(BF16)
        cc = _dot(ib_ref[...], y)
        cr_ref[:, tile_rows, :] = cc[:nf].reshape(DFT_TILES, SUBLANES, tc)
        ci_ref[:, tile_rows, :] = cc[nf:].reshape(DFT_TILES, SUBLANES, tc)


def dft_stage_b_conv(ar, ai, kf, order, fwd_b, inv_b, *, slabs, tc):
    tiles, rows, c = ar.shape
    kern = functools.partial(_dft_b_conv_kernel, slabs=slabs)
    spec = pl.BlockSpec((tiles, slabs * SUBLANES, tc), lambda i, j: (0, i, j))
    k_spec = lambda part: pl.BlockSpec((None, None, slabs * DFT_FAST, tc), lambda i, j: (order, part, i, j))
    mat = pl.BlockSpec(fwd_b.shape, lambda i, j: (0, 0))
    shape = jax.ShapeDtypeStruct(ar.shape, F32)
    return pl.pallas_call(
        kern,
        grid=(rows // (slabs * SUBLANES), c // tc),
        in_specs=[spec, spec, k_spec(0), k_spec(1), mat, mat],
        out_specs=[spec, spec],
        out_shape=[shape, shape],
        compiler_params=_cparams("parallel", "parallel"),
        name="dft_stage_b_conv",
    )(ar, ai, kf, kf, fwd_b, inv_b)


def _dft_a_inv_kernel(cr_ref, ci_ref, g_ref, gate_ref, u_ref, bias_ref, o_ref):
    ns = cr_ref.shape[0] // SUBLANES
    half = ns // 2
    for j in range(SUBLANES):
        rows = pl.ds(j, ns, stride=SUBLANES)
        cc = jnp.concatenate([cr_ref[rows, :], ci_ref[rows, :]], axis=0)
        y = _dot(g_ref[j], cc.astype(BF16))
        o_ref[0, pl.ds(j, half, stride=SUBLANES), :] = y[:half]
        o_ref[1, pl.ds(j, half, stride=SUBLANES), :] = y[half:]
    o_ref[...] = gate_ref[...] * (o_ref[...] + u_ref[...] * bias_ref[...])


def dft_stage_a_inv(cr, ci, table, gate5, gate_idx, u5, u_idx, bias):
    _, _, tiles, rows, c = u5.shape
    c_spec = pl.BlockSpec((None, 2 * rows, LANES), lambda i, j: (i, 0, j))
    t_spec = lambda k: pl.BlockSpec((None, 2, None, rows, LANES), lambda i, j: (k, 0, i, 0, j))
    return pl.pallas_call(
        _dft_a_inv_kernel,
        grid=(tiles, c // LANES),
        in_specs=[c_spec, c_spec, pl.BlockSpec((SUBLANES,) + table.shape[1:], lambda i, j: (i, 0, 0)),
                  t_spec(gate_idx), t_spec(u_idx), pl.BlockSpec((1, LANES), lambda i, j: (0, j))],
        out_specs=pl.BlockSpec((2, None, rows, LANES), lambda i, j: (0, i, 0, j)),
        out_shape=jax.ShapeDtypeStruct((2, tiles, rows, c), F32),
        compiler_params=_cparams("parallel", "parallel"),
        name="dft_stage_a_inv",
    )(cr, ci, table, gate5, u5, bias.reshape(1, c))


def hyena_filter_spectra(seq_len, tables, w1, b1, w2, b2, w3, freq):
    eo, asum = hyena_filter_mlp(seq_len, w1, b1, w2, b2, w3, freq, tl=512)
    return hyena_filter_dft(eo, asum, tables)


def hyena_long_conv(u5, u_idx, gate5, gate_idx, kf, order, bias, tables):
    ar, ai = dft_stage_a(u5, u_idx, tables['fwd_a'])
    cr, ci = dft_stage_b_conv(ar, ai, kf, order, tables['fwd_b'], tables['inv_b'], slabs=4, tc=512)
    return dft_stage_a_inv(cr, ci, tables['inv_a'], gate5, gate_idx, u5, u_idx, bias)


def hyena_mixer(p, short_w, short_b, kf, bias, tables):
    gates5 = hyena_short_conv(p, short_w, short_b, tm=512)
    z = hyena_long_conv(gates5, 2, gates5, 1, kf, 0, bias[0], tables)
    return hyena_long_conv(z[None], 0, gates5, 0, kf, 1, bias[1], tables)


def _odd_out_kernel(ys_ref, u_ref, hy_ref, x_ref, d_ref, wglu_ref, wo_ref, g_ref, o_ref):
    y = ys_ref[0] + d_ref[...] * u_ref[0]
    c0 = math.sqrt(2.0 / math.pi)
    gl = 0.5 * y * (1.0 + jnp.tanh(c0 * (y + 0.044715 * (y * y * y))))
    z = _dot(gl.astype(BF16), wglu_ref[...])
    s5 = gl * (1.0 / (1.0 + jnp.exp(-z)))
    mix_in = jnp.concatenate([s5.astype(BF16), _from_time_tiles(hy_ref).astype(BF16)], axis=-1)
    mix = _dot(mix_in, wo_ref[...])
    o_ref[0] = x_ref[0] + _rms(mix, g_ref[...])


def odd_out(ys, p, hy, x, d_skip, w_glu, w_out, layer, g, *, tm):
    b, l, d = x.shape
    tok = lambda w: pl.BlockSpec((1, tm, w), lambda bi, i: (bi, i, 0))
    hy_spec = pl.BlockSpec((None, DFT_TILES, tm // DFT_FAST * SUBLANES, D_HYENA), lambda bi, i: (bi, 0, i, 0))
    return pl.pallas_call(
        _odd_out_kernel,
        grid=(b, l // tm),
        in_specs=[tok(D_S5), tok(D_S5), hy_spec, tok(d),
                  pl.BlockSpec((1, D_S5), lambda bi, i: (0, 0)),
                  _layer_spec((D_S5, D_S5), layer, lambda bi, i: (0, 0)),
                  _layer_spec((d, d), layer, lambda bi, i: (0, 0)),
                  pl.BlockSpec((1, d), lambda bi, i: (0, 0))],
        out_specs=tok(d),
        out_shape=jax.ShapeDtypeStruct(x.shape, F32),
        compiler_params=_cparams("parallel", "parallel"),
        name="odd_out",
    )(ys, p, hy, x, d_skip.reshape(1, D_S5), w_glu, w_out, g.reshape(1, d))


def kernel(x, mem, norm_mix, norm_xattn, norm_mem, norm_mlp, xa_wq, xa_wk, xa_wv, xa_wo, mlp_w1, mlp_w2, ev_w_in, ev_pool_w, ev_pool_scale, ev_conv_w, ev_w_out, od_w_in, od_s5_lambda_re, od_s5_lambda_im, od_s5_log_dt, od_s5_b_re, od_s5_b_im, od_s5_c_re, od_s5_c_im, od_s5_d, od_s5_w_glu, od_hy_short_w, od_hy_short_b, od_hy_w1, od_hy_b1, od_hy_w2, od_hy_b2, od_hy_w3, od_hy_freq, od_hy_bias, od_w_out):
    b, l, d = x.shape
    depth = norm_mix.shape[0]
    assert b == 2, "the long convolution packs the two batch rows as one complex signal"
    tables = dft_tables(l)
    mem2d = mem.reshape(b * N_MEM, d)
    wq, wo = xa_wq.astype(BF16), xa_wo.astype(BF16)
    wkv = jnp.concatenate([xa_wk, xa_wv], axis=2).astype(BF16)
    w1, w2 = mlp_w1.astype(BF16), mlp_w2.astype(BF16)
    ev_in, ev_out, ev_pool = ev_w_in.astype(BF16), ev_w_out.astype(BF16), ev_pool_w.astype(BF16)
    od_in, od_out, od_glu = od_w_in.astype(BF16), od_w_out.astype(BF16), od_s5_w_glu.astype(BF16)
    for i in range(depth):
        j = i // 2
        if i % 2 == 0:
            p = norm_matmul(x.reshape(b * l, d), norm_mix[i, 0], ev_in, j, tm=512)
            x = even_mixer(p.reshape(b, l, -1), x, ev_pool, ev_pool_scale[j], ev_conv_w[j], ev_out, j,
                           norm_mix[i, 1], tm=512)
        else:
            p = norm_matmul(x.reshape(b * l, d), norm_mix[i, 0], od_in, j, tm=512).reshape(b, l, -1)
            mats = s5_matrices(od_s5_lambda_re[j], od_s5_lambda_im[j], od_s5_log_dt[j], od_s5_b_re[j],
                               od_s5_b_im[j], od_s5_c_re[j], od_s5_c_im[j])
            ys = s5_core(p, mats)
            kf = hyena_filter_spectra(l, tables, od_hy_w1[j], od_hy_b1[j], od_hy_w2[j], od_hy_b2[j],
                                      od_hy_w3[j], od_hy_freq[j])
            hy = hyena_mixer(p, od_hy_short_w[j], od_hy_short_b[j], kf, od_hy_bias[j], tables)
            x = odd_out(ys, p, hy, x, od_s5_d[j], od_glu, od_out, j, norm_mix[i, 1], tm=512)
        kv = norm_matmul(mem2d, norm_mem[i], wkv, i, tm=b * N_MEM, out_dtype=BF16).reshape(b, N_MEM, 2 * d)
        x = xattn_block(x, kv, wq, wo, i, norm_xattn[i, 0], norm_xattn[i, 1], tm=512)
        x = mlp_block(x.reshape(b * l, d), w1, w2, i, norm_mlp[i, 0], norm_mlp[i, 1],
                      tm=1024, tf=512).reshape(b, l, d)
    return x
```

```python
import functools
import math

import numpy as np
import jax
import jax.numpy as jnp
from jax import lax
from jax.experimental import pallas as pl
from jax.experimental.pallas import tpu as pltpu

F32 = jnp.float32
BF16 = jnp.bfloat16

D_MODEL = 1024
N_MEM = 256
RMS_EPS = 1e-6
D_POOL = 512
POOL_WINDOWS = (2, 4, 8, 16)
POOL_GROUP_DIM = 128
POOL_HALO = 8
D_CONV = 512
D_S5 = 512
S5_GROUP_DIM = 16
S5_GROUPS = 32
S5_STATE = 64
S5_CHUNK = 16
D_HYENA = 512
HYENA_BANDS = 16
HYENA_FFN = 64
HYENA_TARGET = 1e-2
HYENA_SHORT_DECAY_PCT = 0.3
HYENA_LONG_DECAY_PCT = 1.5
XA_HEADS = 4
XA_HEAD_DIM = 256
D_FF = 4096

LANES = 128
SUBLANES = 8
S5_SUPER = LANES // S5_GROUP_DIM
DFT_FAST = 128
DFT_TILES = DFT_FAST // SUBLANES
DFT_PITCH = DFT_FAST + SUBLANES
DFT_UNROLL = 16
VMEM_LIMIT = 56 * 1024 * 1024


def _cparams(*sem):
    return pltpu.CompilerParams(dimension_semantics=sem, vmem_limit_bytes=VMEM_LIMIT)


def _rms(xf, g):
    ms = jnp.mean(xf * xf, axis=-1, keepdims=True)
    return xf * lax.rsqrt(ms + RMS_EPS) * g


def _dot(a, b):
    return jnp.dot(a, b, preferred_element_type=F32)


def _layer_spec(block, layer, tail_map):
    return pl.BlockSpec((None,) + block, lambda *idx: (layer,) + tail_map(*idx))


def _norm_matmul_kernel(x_ref, g_ref, w_ref, o_ref):
    xn = _rms(x_ref[...], g_ref[...]).astype(BF16)
    o_ref[...] = _dot(xn, w_ref[...]).astype(o_ref.dtype)


def norm_matmul(x2d, g, w, layer, *, tm, out_dtype=F32):
    m, d = x2d.shape
    n = w.shape[-1]
    return pl.pallas_call(
        _norm_matmul_kernel,
        grid=(m // tm,),
        in_specs=[
            pl.BlockSpec((tm, d), lambda i: (i, 0)),
            pl.BlockSpec((1, d), lambda i: (0, 0)),
            _layer_spec((d, n), layer, lambda i: (0, 0)),
        ],
        out_specs=pl.BlockSpec((tm, n), lambda i: (i, 0)),
        out_shape=jax.ShapeDtypeStruct((m, n), out_dtype),
        compiler_params=_cparams("parallel"),
        name="norm_matmul",
    )(x2d, g.reshape(1, d), w)


def _xattn_kernel(x_ref, kv_ref, wq_ref, wo_ref, g1_ref, g2_ref, o_ref):
    x = x_ref[0]
    xn = _rms(x, g1_ref[...]).astype(BF16)
    q = (_dot(xn, wq_ref[...]) * (XA_HEAD_DIM ** -0.5)).astype(BF16)
    heads = []
    for h in range(XA_HEADS):
        lo = h * XA_HEAD_DIM
        qh = q[:, lo:lo + XA_HEAD_DIM]
        kh = kv_ref[0, :, lo:lo + XA_HEAD_DIM]
        vh = kv_ref[0, :, D_MODEL + lo:D_MODEL + lo + XA_HEAD_DIM]
        s = lax.dot_general(qh, kh, (((1,), (1,)), ((), ())), preferred_element_type=F32)
        e = jnp.exp(s - jnp.max(s, axis=-1, keepdims=True))
        p = e / jnp.sum(e, axis=-1, keepdims=True)
        heads.append(_dot(p.astype(BF16), vh).astype(BF16))
    o = jnp.concatenate(heads, axis=-1)
    y = _dot(o, wo_ref[...])
    o_ref[0] = x + _rms(y, g2_ref[...])


def xattn_block(x, kv, wq, wo, layer, g1, g2, *, tm):
    b, l, d = x.shape
    return pl.pallas_call(
        _xattn_kernel,
        grid=(b, l // tm),
        in_specs=[
            pl.BlockSpec((1, tm, d), lambda bi, i: (bi, i, 0)),
            pl.BlockSpec((1, N_MEM, 2 * d), lambda bi, i: (bi, 0, 0)),
            _layer_spec((d, d), layer, lambda bi, i: (0, 0)),
            _layer_spec((d, d), layer, lambda bi, i: (0, 0)),
            pl.BlockSpec((1, d), lambda bi, i: (0, 0)),
            pl.BlockSpec((1, d), lambda bi, i: (0, 0)),
        ],
        out_specs=pl.BlockSpec((1, tm, d), lambda bi, i: (bi, i, 0)),
        out_shape=jax.ShapeDtypeStruct(x.shape, F32),
        compiler_params=_cparams("parallel", "parallel"),
        name="xattn_block",
    )(x, kv, wq, wo, g1.reshape(1, d), g2.reshape(1, d))


def _mlp_kernel(x_ref, w1_ref, w2_ref, g1_ref, g2_ref, o_ref, h_ref, *, tf):
    x = x_ref[...]
    xn = _rms(x, g1_ref[...]).astype(BF16)
    for c in range(h_ref.shape[1] // tf):
        cols = slice(c * tf, (c + 1) * tf)
        h = jnp.maximum(_dot(xn, w1_ref[:, cols]), 0.0)
        h_ref[:, cols] = (h * h).astype(BF16)
    o_ref[...] = x + _rms(_dot(h_ref[...], w2_ref[...]), g2_ref[...])


def mlp_block(x2d, w1, w2, layer, g1, g2, *, tm, tf):
    m, d = x2d.shape
    ff = w1.shape[-1]
    resident = dict(pipeline_mode=pl.Buffered(1))
    return pl.pallas_call(
        functools.partial(_mlp_kernel, tf=tf),
        grid=(m // tm,),
        in_specs=[
            pl.BlockSpec((tm, d), lambda i: (i, 0)),
            pl.BlockSpec((None, d, ff), lambda i: (layer, 0, 0), **resident),
            pl.BlockSpec((None, ff, d), lambda i: (layer, 0, 0), **resident),
            pl.BlockSpec((1, d), lambda i: (0, 0)),
            pl.BlockSpec((1, d), lambda i: (0, 0)),
        ],
        out_specs=pl.BlockSpec((tm, d), lambda i: (i, 0)),
        out_shape=jax.ShapeDtypeStruct((m, d), F32),
        scratch_shapes=[pltpu.VMEM((tm, ff), BF16)],
        compiler_params=_cparams("parallel"),
        name="mlp_block",
    )(x2d, w1, w2, g1.reshape(1, d), g2.reshape(1, d))


def _halo_specs(tm, seq_len, width, col):
    r = tm // POOL_HALO
    last = seq_len // POOL_HALO - 1

    def prev_map(bi, i, *_):
        return (bi, jnp.maximum(i * r - 1, 0), col(*_))

    def main_map(bi, i, *_):
        return (bi, i, col(*_))

    def next_map(bi, i, *_):
        return (bi, jnp.minimum((i + 1) * r, last), col(*_))

    return [
        pl.BlockSpec((1, POOL_HALO, width), prev_map),
        pl.BlockSpec((1, tm, width), main_map),
        pl.BlockSpec((1, POOL_HALO, width), next_map),
    ]


def _with_halo(prev_ref, main_ref, next_ref):
    i = pl.program_id(1)
    prev = jnp.where(i > 0, prev_ref[0], 0.0)
    nxt = jnp.where(i < pl.num_programs(1) - 1, next_ref[0], 0.0)
    return jnp.concatenate([prev, main_ref[0], nxt], axis=0)


def _shift_rows(v, k):
    return pltpu.roll(v, k % v.shape[0], 0)


def _even_mixer_kernel(pp_ref, p_ref, pn_ref, x_ref, wg_ref, ps_ref, cw_ref, wo_ref, g_ref, o_ref,
                       *, tm, seq_len):
    ext = _with_halo(pp_ref, p_ref, pn_ref)
    lo, hi = POOL_HALO, POOL_HALO + tm
    t = pl.program_id(1) * tm + lax.broadcasted_iota(jnp.int32, (tm, 1), 0)
    parts = []
    for gi, win in enumerate(POOL_WINDOWS):
        half = win // 2
        u = ext[:, gi * POOL_GROUP_DIM:(gi + 1) * POOL_GROUP_DIM]
        s = u + _shift_rows(u, 1)
        step = 1
        while 2 * step < win:
            s = _shift_rows(s, step) + _shift_rows(s, -step)
            step *= 2
        cnt = (jnp.minimum(t + half, seq_len) - jnp.maximum(t - half, 0)).astype(F32)
        pooled = s[lo:hi] / cnt - u[lo:hi]
        y = _dot(pooled.astype(BF16), wg_ref[gi])
        parts.append((y * ps_ref[:, gi * POOL_GROUP_DIM:(gi + 1) * POOL_GROUP_DIM]).astype(BF16))
    b_gate = ext[lo:hi, D_POOL:D_POOL + D_CONV]
    ch = ext[:, D_POOL + D_CONV:D_POOL + 2 * D_CONV] * ext[:, D_POOL + 2 * D_CONV:D_POOL + 3 * D_CONV]
    conv = cw_ref[0:1, :] * _shift_rows(ch, 1) + cw_ref[1:2, :] * ch + cw_ref[2:3, :] * _shift_rows(ch, -1)
    parts.append((b_gate * conv[lo:hi]).astype(BF16))
    mix = _dot(jnp.concatenate(parts, axis=-1), wo_ref[...])
    o_ref[0] = x_ref[0] + _rms(mix, g_ref[...])


def even_mixer(p, x, w_group, pool_scale, conv_w, w_out, layer, g, *, tm):
    b, l, d = x.shape
    width = p.shape[-1]
    kern = functools.partial(_even_mixer_kernel, tm=tm, seq_len=l)
    return pl.pallas_call(
        kern,
        grid=(b, l // tm),
        in_specs=_halo_specs(tm, l, width, lambda: 0) + [
            pl.BlockSpec((1, tm, d), lambda bi, i: (bi, i, 0)),
            _layer_spec(w_group.shape[1:], layer, lambda bi, i: (0, 0, 0)),
            pl.BlockSpec((1, D_POOL), lambda bi, i: (0, 0)),
            pl.BlockSpec((3, D_CONV), lambda bi, i: (0, 0)),
            _layer_spec((d, d), layer, lambda bi, i: (0, 0)),
            pl.BlockSpec((1, d), lambda bi, i: (0, 0)),
        ],
        out_specs=pl.BlockSpec((1, tm, d), lambda bi, i: (bi, i, 0)),
        out_shape=jax.ShapeDtypeStruct(x.shape, F32),
        compiler_params=_cparams("parallel", "parallel"),
        name="even_mixer",
    )(p, p, p, x, w_group, pool_scale.reshape(1, D_POOL), conv_w, w_out, g.reshape(1, d))


def _cmul(ar, ai, br, bi):
    return ar * br - ai * bi, ar * bi + ai * br


def s5_matrices(lam_re, lam_im, log_dt, b_re, b_im, c_re, c_im):
    tc, hd, n, sup = S5_CHUNK, S5_GROUP_DIM, S5_STATE, S5_SUPER
    n_tiles = S5_GROUPS // sup
    hp = lax.Precision.HIGHEST
    lr = jnp.minimum(lam_re, -1e-4)
    li = lam_im
    dt = jnp.exp(log_dt)[..., None]
    taus = jnp.arange(tc + 1, dtype=F32)[:, None, None, None]
    rmag = jnp.exp(lr[None] * dt[None] * (tc - taus))
    rang = li[None] * dt[None] * (tc - taus)
    pwr_re, pwr_im = rmag * jnp.cos(rang), rmag * jnp.sin(rang)
    mag = jnp.exp(lr[None] * dt[None] * taus)
    ang = li[None] * dt[None] * taus
    pw_re, pw_im = mag * jnp.cos(ang), mag * jnp.sin(ang)
    nr, ni = pw_re[1] - 1.0, pw_im[1]
    den = lr * lr + li * li
    coef_re, coef_im = (nr * lr + ni * li) / den, (ni * lr - nr * li) / den
    cb_re, cb_im = _cmul(coef_re[..., None], coef_im[..., None], b_re[None], b_im[None])
    cbt_re, cbt_im = _cmul(coef_re[:, :, None, :], coef_im[:, :, None, :],
                           jnp.swapaxes(b_re, 1, 2)[None], jnp.swapaxes(b_im, 1, 2)[None])
    q_re, q_im = _cmul(pw_re[:tc, :, :, :, None], pw_im[:tc, :, :, :, None], cb_re[None], cb_im[None])
    m = (jnp.einsum('dgkn,tdgnj->dtgjk', c_re, q_re, precision=hp)
         - jnp.einsum('dgkn,tdgnj->dtgjk', c_im, q_im, precision=hp))
    m_fwd, m_bwd = m[0], m[1]
    diag = m_fwd[0] + m_bwd[0]
    lag_table = jnp.concatenate([m_bwd[:0:-1], diag[None], m_fwd[1:]], axis=0)
    intra = jnp.stack([lag_table[tc - 1 - s:2 * tc - 1 - s] for s in range(tc)], axis=0)
    intra = jnp.transpose(intra, (2, 0, 3, 1, 4)).reshape(S5_GROUPS, tc * hd, tc * hd)
    sf_re, sf_im = _cmul(pwr_re[1:tc + 1, 0][:, :, None, :], pwr_im[1:tc + 1, 0][:, :, None, :],
                         cbt_re[0][None], cbt_im[0][None])
    sb_re, sb_im = _cmul(pw_re[:tc, 1][:, :, None, :], pw_im[:tc, 1][:, :, None, :],
                         cbt_re[1][None], cbt_im[1][None])
    summ = jnp.concatenate([sf_re, sb_re, sf_im, sb_im], axis=-1)
    summ = jnp.swapaxes(summ, 0, 1).reshape(S5_GROUPS, tc * hd, 4 * n)
    ct_re, ct_im = jnp.swapaxes(c_re, 2, 3), jnp.swapaxes(c_im, 2, 3)
    gnt = lambda z: jnp.transpose(z, (1, 2, 0))[..., None]
    qf_re, qf_im = _cmul(ct_re[0][:, :, None, :], ct_im[0][:, :, None, :],
                         gnt(pw_re[1:tc + 1, 0]), gnt(pw_im[1:tc + 1, 0]))
    qb_re, qb_im = _cmul(ct_re[1][:, :, None, :], ct_im[1][:, :, None, :], gnt(pwr_re[:tc, 1]), gnt(pwr_im[:tc, 1]))
    carry = jnp.concatenate([qf_re, qb_re, -qf_im, -qb_im], axis=1)
    carry = carry.reshape(S5_GROUPS, 4 * n, tc * hd)
    a_re = jnp.concatenate([pw_re[tc, 0], pw_re[tc, 1]], axis=-1)
    a_im = jnp.concatenate([pw_im[tc, 0], pw_im[tc, 1]], axis=-1)
    return intra, summ, carry, a_re, a_im


def s5_regroup_matrix():
    n = S5_SUPER * LANES
    src = np.arange(n)
    tb, g, h = src // LANES, (src % LANES) // S5_GROUP_DIM, src % S5_GROUP_DIM
    perm = np.zeros((n, n), np.float32)
    perm[src, g * LANES + tb * S5_GROUP_DIM + h] = 1.0
    return jnp.asarray(perm, BF16)


def _s5_regroup_kernel(p_ref, w_ref, perm_ref, u_ref, sr_ref, si_ref, *, n_chunks):
    n2 = 2 * S5_STATE
    for q in range(S5_CHUNK // S5_SUPER):
        rows_t = [p_ref[pl.ds(q * S5_SUPER + tb, n_chunks, stride=S5_CHUNK), :].astype(BF16) for tb in range(S5_SUPER)]
        grouped = _dot(jnp.concatenate(rows_t, axis=-1), perm_ref[...]).astype(BF16)
        for g in range(S5_SUPER):
            u_ref[g, :, q * LANES:(q + 1) * LANES] = grouped[:, g * LANES:(g + 1) * LANES]
    for g in range(S5_SUPER):
        s = _dot(u_ref[g], w_ref[g])
        sr_ref[pl.ds(g, n_chunks, stride=S5_SUPER), :] = s[:, :n2]
        si_ref[pl.ds(g, n_chunks, stride=S5_SUPER), :] = s[:, n2:]


def _s5_state_spec(c, n2, n_tiles):
    return pl.BlockSpec((None, c * S5_SUPER, n2), lambda a, bi: (bi * n_tiles + a, 0, 0))


def _s5_group_spec(rows, cols):
    return pl.BlockSpec((S5_SUPER, rows, cols), lambda a, bi: (a, 0, 0))


def s5_summary(p, summ):
    b, l, _ = p.shape
    c = l // S5_CHUNK
    n_tiles = S5_GROUPS // S5_SUPER
    k = S5_CHUNK * S5_GROUP_DIM
    n2 = 2 * S5_STATE
    shape = jax.ShapeDtypeStruct((b * n_tiles, c * S5_SUPER, n2), F32)
    kern = functools.partial(_s5_regroup_kernel, n_chunks=c)
    perm = s5_regroup_matrix()
    return pl.pallas_call(
        kern,
        grid=(n_tiles, b),
        in_specs=[
            pl.BlockSpec((None, l, LANES), lambda a, bi: (bi, 0, a)),
            _s5_group_spec(k, 2 * n2),
            pl.BlockSpec(perm.shape, lambda a, bi: (0, 0)),
        ],
        out_specs=[pl.BlockSpec((None, S5_SUPER, c, k), lambda a, bi: (bi * n_tiles + a, 0, 0, 0)),
                   _s5_state_spec(c, n2, n_tiles), _s5_state_spec(c, n2, n_tiles)],
        out_shape=[jax.ShapeDtypeStruct((b * n_tiles, S5_SUPER, c, k), BF16), shape, shape],
        compiler_params=_cparams("parallel", "parallel"),
        name="s5_summary",
    )(p, summ, perm)


def _s5_scan_kernel(sr_ref, si_ref, are_ref, aim_ref, or_ref, oi_ref, *, n_chunks):
    n2 = 2 * S5_STATE
    a_re = are_ref[...]
    a_im = aim_ref[...]
    fwd_lane = lax.broadcasted_iota(jnp.int32, (SUBLANES, n2), 1) < S5_STATE

    def body(i, state):
        x_re, x_im = state
        rows_i = pl.ds(pl.multiple_of(i * SUBLANES, SUBLANES), SUBLANES)
        rows_r = pl.ds(pl.multiple_of((n_chunks - 1 - i) * SUBLANES, SUBLANES), SUBLANES)
        or_ref[rows_i, 0:S5_STATE] = x_re[:, :S5_STATE]
        oi_ref[rows_i, 0:S5_STATE] = x_im[:, :S5_STATE]
        or_ref[rows_r, S5_STATE:n2] = x_re[:, S5_STATE:]
        oi_ref[rows_r, S5_STATE:n2] = x_im[:, S5_STATE:]
        s_re = jnp.where(fwd_lane, sr_ref[rows_i, :], sr_ref[rows_r, :])
        s_im = jnp.where(fwd_lane, si_ref[rows_i, :], si_ref[rows_r, :])
        return (a_re * x_re - a_im * x_im + s_re, a_re * x_im + a_im * x_re + s_im)

    zero = jnp.zeros((SUBLANES, n2), F32)
    lax.fori_loop(0, n_chunks, body, (zero, zero))


def s5_scan(s_re, s_im, a_re, a_im):
    tiles, rows, n2 = s_re.shape
    kern = functools.partial(_s5_scan_kernel, n_chunks=rows // S5_SUPER)
    st = pl.BlockSpec((None, rows, n2), lambda i: (i, 0, 0))
    av = pl.BlockSpec((SUBLANES, n2), lambda i: (i, 0))
    shape = jax.ShapeDtypeStruct(s_re.shape, F32)
    return pl.pallas_call(
        kern,
        grid=(tiles,),
        in_specs=[st, st, av, av],
        out_specs=[st, st],
        out_shape=[shape, shape],
        compiler_params=_cparams("parallel"),
        name="s5_scan",
    )(s_re, s_im, a_re, a_im)


def _s5_output_kernel(u_ref, xr_ref, xi_ref, wi_ref, wc_ref, perm_ref, o_ref, y_ref, *, n_chunks):
    for g in range(S5_SUPER):
        rows = pl.ds(g, n_chunks, stride=S5_SUPER)
        xin = jnp.concatenate([xr_ref[rows, :], xi_ref[rows, :]], axis=-1).astype(BF16)
        y_ref[g] = _dot(u_ref[g], wi_ref[g]) + _dot(xin, wc_ref[g])
    back = (((1,), (1,)), ((), ()))
    for q in range(S5_CHUNK // S5_SUPER):
        y = jnp.concatenate([y_ref[g, :, q * LANES:(q + 1) * LANES] for g in range(S5_SUPER)], axis=-1)
        hi = y.astype(BF16)
        lo = (y - hi.astype(F32)).astype(BF16)
        tok = (lax.dot_general(hi, perm_ref[...], back, preferred_element_type=F32)
               + lax.dot_general(lo, perm_ref[...], back, preferred_element_type=F32))
        for tb in range(S5_SUPER):
            o_ref[pl.ds(q * S5_SUPER + tb, n_chunks, stride=S5_CHUNK), :] = tok[:, tb * LANES:(tb + 1) * LANES]


def s5_output(u, xin_re, xin_im, intra, carry, *, batch):
    tiles, _, c, k = u.shape
    n_tiles = tiles // batch
    n2 = 2 * S5_STATE
    l = c * S5_CHUNK
    kern = functools.partial(_s5_output_kernel, n_chunks=c)
    perm = s5_regroup_matrix()
    return pl.pallas_call(
        kern,
        grid=(n_tiles, batch),
        in_specs=[
            pl.BlockSpec((None, S5_SUPER, c, k), lambda a, bi: (bi * n_tiles + a, 0, 0, 0)),
            _s5_state_spec(c, n2, n_tiles),
            _s5_state_spec(c, n2, n_tiles),
            _s5_group_spec(k, k),
            _s5_group_spec(2 * n2, k),
            pl.BlockSpec(perm.shape, lambda a, bi: (0, 0)),
        ],
        out_specs=pl.BlockSpec((None, l, LANES), lambda a, bi: (bi, 0, a)),
        out_shape=jax.ShapeDtypeStruct((batch, l, D_S5), F32),
        scratch_shapes=[pltpu.VMEM((S5_SUPER, c, k), F32)],
        compiler_params=_cparams("parallel", "parallel"),
        name="s5_output",
    )(u, xin_re, xin_im, intra, carry, perm)


def s5_core(p, mats):
    b = p.shape[0]
    intra, summ, carry, a_re, a_im = mats
    u, s_re, s_im = s5_summary(p, summ.astype(BF16))
    xin_re, xin_im = s5_scan(s_re, s_im, jnp.tile(a_re, (b, 1)), jnp.tile(a_im, (b, 1)))
    return s5_output(u, xin_re, xin_im, intra.astype(BF16), carry.astype(BF16), batch=b)


def _to_time_tiles(y):
    nf, c = DFT_FAST, y.shape[-1]
    slabs = [y[s * nf:(s + 1) * nf].reshape(DFT_TILES, SUBLANES, c) for s in range(y.shape[0] // nf)]
    return jnp.concatenate(slabs, axis=1)


def _from_time_tiles(ref):
    c = ref.shape[-1]
    slabs = [ref[:, s * SUBLANES:(s + 1) * SUBLANES, :].reshape(DFT_FAST, c) for s in range(ref.shape[1] // SUBLANES)]
    return jnp.concatenate(slabs, axis=0)


def _short_conv_kernel(pp_ref, p_ref, pn_ref, w_ref, b_ref, o_ref, *, tm):
    ext = _with_halo(pp_ref, p_ref, pn_ref)
    y = w_ref[0:1, :] * _shift_rows(ext, 1) + w_ref[1:2, :] * ext + w_ref[2:3, :] * _shift_rows(ext, -1)
    o_ref[...] = _to_time_tiles(y[POOL_HALO:POOL_HALO + tm] + b_ref[...])


def hyena_short_conv(p, w, bias, *, tm):
    b, l, _ = p.shape
    kern = functools.partial(_short_conv_kernel, tm=tm)
    rows = tm // DFT_FAST * SUBLANES
    return pl.pallas_call(
        kern,
        grid=(b, l // tm, 3),
        in_specs=_halo_specs(tm, l, D_HYENA, lambda j: j + 1) + [
            pl.BlockSpec((3, D_HYENA), lambda bi, i, j: (0, j)),
            pl.BlockSpec((1, D_HYENA), lambda bi, i, j: (0, j)),
        ],
        out_specs=pl.BlockSpec((None, None, DFT_TILES, rows, D_HYENA), lambda bi, i, j: (j, bi, 0, i, 0)),
        out_shape=jax.ShapeDtypeStruct((3, b, DFT_TILES, l // DFT_FAST * SUBLANES, D_HYENA), F32),
        compiler_params=_cparams("parallel", "parallel", "parallel"),
        name="hyena_short_conv",
    )(p, p, p, w, bias.reshape(1, -1))


def _filter_mlp_kernel(bands_ref, w1t_ref, w1cs_ref, b1_ref, w2_ref, b2_ref, w3_ref, fr_ref, dl_ref,
                       eo_ref, asum_ref, *, tl, seq_len):
    i = pl.program_id(0)
    hp = lax.Precision.HIGHEST
    c = D_HYENA
    t = (i * tl + lax.broadcasted_iota(jnp.int32, (tl, 1), 0)).astype(F32)
    t_norm = t / (seq_len - 1.0)
    ang = (2.0 * math.pi / seq_len) * t * bands_ref[...]
    is_cos = lax.broadcasted_iota(jnp.int32, ang.shape, 1) < HYENA_BANDS
    cs = jnp.where(is_cos, jnp.cos(ang), -jnp.sin(ang))
    fr = fr_ref[...]
    z = t_norm * w1t_ref[...] + jnp.dot(cs, w1cs_ref[...], precision=hp, preferred_element_type=F32)
    h = jnp.sin(fr * (z + b1_ref[...]))
    h = jnp.sin(fr * (jnp.dot(h, w2_ref[...], precision=hp, preferred_element_type=F32) + b2_ref[...]))
    h = jnp.dot(h, w3_ref[...], precision=hp, preferred_element_type=F32)
    decay = jnp.exp(-t_norm * dl_ref[...])

    @pl.when(i == 0)
    def _():
        asum_ref[...] = jnp.zeros_like(asum_ref)

    for o in range(h.shape[1] // (2 * c)):
        fwd = h[:, 2 * o * c:(2 * o + 1) * c] * decay
        bwd = h[:, (2 * o + 1) * c:(2 * o + 2) * c] * decay
        eo_ref[:, 2 * o * c:(2 * o + 1) * c] = fwd + bwd
        eo_ref[:, (2 * o + 1) * c:(2 * o + 2) * c] = fwd - bwd
        asum_ref[:, o * c:(o + 1) * c] += jnp.sum(jnp.abs(fwd) + jnp.abs(bwd), axis=0, keepdims=True)


def hyena_filter_mlp(seq_len, w1, b1, w2, b2, w3, freq, *, tl):
    n_out = w3.shape[1]
    bands = jnp.linspace(1e-4, HYENA_BANDS - 1, HYENA_BANDS, dtype=F32)
    bands2 = jnp.concatenate([bands, bands])[None, :]
    deltas = jnp.abs(jnp.linspace(math.log(HYENA_TARGET) / HYENA_LONG_DECAY_PCT,
                                  math.log(HYENA_TARGET) / HYENA_SHORT_DECAY_PCT, D_HYENA, dtype=F32))[None, :]
    kern = functools.partial(_filter_mlp_kernel, tl=tl, seq_len=seq_len)
    full = lambda a: pl.BlockSpec(a.shape, lambda i: (0,) * a.ndim)
    args = [bands2, w1[0:1], w1[1:], b1[None, :], w2, b2[None, :], w3, freq[None, :], deltas]
    return pl.pallas_call(
        kern,
        grid=(seq_len // tl,),
        in_specs=[full(a) for a in args],
        out_specs=[pl.BlockSpec((tl, n_out), lambda i: (i, 0)), pl.BlockSpec((1, n_out // 2), lambda i: (0, 0))],
        out_shape=[jax.ShapeDtypeStruct((seq_len, n_out), F32), jax.ShapeDtypeStruct((1, n_out // 2), F32)],
        compiler_params=_cparams("arbitrary"),
        name="hyena_filter_mlp",
    )(*args)


def dft_tables(seq_len):
    n = 2 * seq_len
    nf = DFT_FAST
    ns = n // nf
    k_a = np.arange(ns)[:, None]
    n_s = np.arange(ns // 2)[None, :]
    w_s = np.exp(-2j * np.pi * ((k_a * n_s) % ns) / ns)
    n_f = np.arange(nf)[:, None]
    tw = np.exp(-2j * np.pi * ((n_f * np.arange(ns)[None, :]) % n) / n)
    c32 = lambda z: (jnp.asarray(z.real, F32), jnp.asarray(z.imag, F32))
    ws_re, ws_im = c32(w_s)
    tw_re, tw_im = c32(tw)
    g_re = tw_re[:, :, None] * ws_re[None] - tw_im[:, :, None] * ws_im[None]
    g_im = tw_re[:, :, None] * ws_im[None] + tw_im[:, :, None] * ws_re[None]
    fwd_a = jnp.concatenate([jnp.concatenate([g_re, -g_im], axis=2),
                             jnp.concatenate([g_im, g_re], axis=2)], axis=1)
    gt_re = jnp.swapaxes(g_re, 1, 2) * (1.0 / n)
    gt_im = jnp.swapaxes(g_im, 1, 2) * (1.0 / n)
    inv_a = jnp.concatenate([jnp.concatenate([gt_re, gt_im], axis=2),
                             jnp.concatenate([-gt_im, gt_re], axis=2)], axis=1)
    kk = np.arange(nf)
    w_f = np.exp(-2j * np.pi * ((kk[:, None] * kk[None, :]) % nf) / nf)
    f_re, f_im = c32(w_f)
    fwd_b = jnp.concatenate([jnp.concatenate([f_re, -f_im], axis=1),
                             jnp.concatenate([f_im, f_re], axis=1)], axis=0)
    inv_b = jnp.concatenate([jnp.concatenate([f_re, f_im], axis=1),
                             jnp.concatenate([-f_im, f_re], axis=1)], axis=0)
    return dict(fwd_a=fwd_a.astype(BF16), fwd_a_real=fwd_a[:, :, :ns // 2].astype(BF16),
                inv_a=inv_a.astype(BF16), fwd_b=fwd_b.astype(BF16), inv_b=inv_b.astype(BF16),
                fwd_b_part=fwd_b.reshape(2, nf, 2 * nf).astype(BF16))


def _filter_dft_kernel(x_ref, ga_ref, fb_ref, nrm_ref, k_ref, ar_scr, ai_scr):
    nf = DFT_FAST
    ns = ga_ref.shape[1] // 2

    def stage_a(j, carry):
        x = x_ref[pl.ds(j, ns // 2, stride=nf), :].astype(BF16)
        a = _dot(ga_ref[j], x)
        ar_scr[pl.ds(j, ns, stride=DFT_PITCH), :] = a[:ns]
        ai_scr[pl.ds(j, ns, stride=DFT_PITCH), :] = a[ns:]
        return carry

    lax.fori_loop(0, nf, stage_a, 0, unroll=DFT_UNROLL)
    inv_norm = 1.0 / (nrm_ref[...] + 1e-6)

    def stage_b(k, carry):
        src = pl.multiple_of(k * DFT_PITCH, SUBLANES)
        a = jnp.concatenate([ar_scr[pl.ds(src, nf), :], ai_scr[pl.ds(src, nf), :]], axis=0).astype(BF16)
        dst = pl.multiple_of(k * nf, nf)
        k_ref[pl.ds(dst, nf), :] = _dot(fb_ref[...], a) * inv_norm
        return carry

    lax.fori_loop(0, ns, stage_b, 0, unroll=DFT_UNROLL)


def hyena_filter_dft(eo, asum, tables):
    l, cols = eo.shape
    c = D_HYENA
    orders = cols // (2 * c)
    ct = c // LANES
    ga, fb = tables['fwd_a_real'], tables['fwd_b_part']
    ns = ga.shape[1] // 2
    scr = pltpu.VMEM((ns * DFT_PITCH, LANES), F32)
    return pl.pallas_call(
        _filter_dft_kernel,
        grid=(orders, ct, 2),
        in_specs=[
            pl.BlockSpec((l, LANES), lambda o, j, part: (0, (2 * o + part) * ct + j)),
            pl.BlockSpec(ga.shape, lambda o, j, part: (0, 0, 0)),
            pl.BlockSpec((None,) + fb.shape[1:], lambda o, j, part: (part, 0, 0)),
            pl.BlockSpec((1, LANES), lambda o, j, part: (0, o * ct + j)),
        ],
        out_specs=pl.BlockSpec((None, None, 2 * l, LANES), lambda o, j, part: (o, part, 0, j)),
        out_shape=jax.ShapeDtypeStruct((orders, 2, 2 * l, c), F32),
        scratch_shapes=[scr, scr],
        compiler_params=_cparams("parallel", "parallel", "parallel"),
        name="hyena_filter_dft",
    )(eo, ga, fb, asum)


def _dft_a_kernel(xr_ref, xi_ref, g_ref, ar_ref, ai_ref):
    half = xr_ref.shape[0] // SUBLANES
    ns = 2 * half
    for j in range(SUBLANES):
        rows = pl.ds(j, half, stride=SUBLANES)
        x = jnp.concatenate([xr_ref[rows, :], xi_ref[rows, :]], axis=0)
        a = _dot(g_ref[j], x.astype(BF16))
        ar_ref[pl.ds(j, ns, stride=SUBLANES), :] = a[:ns]
        ai_ref[pl.ds(j, ns, stride=SUBLANES), :] = a[ns:]


def dft_stage_a(u5, which, table):
    _, _, tiles, rows, c = u5.shape
    kern = functools.partial(_dft_a_kernel)
    x_spec = lambda bi: pl.BlockSpec((None, None, None, rows, LANES), lambda i, j: (which, bi, i, 0, j))
    o_spec = pl.BlockSpec((None, 2 * rows, LANES), lambda i, j: (i, 0, j))
    o_shape = jax.ShapeDtypeStruct((tiles, 2 * rows, c), F32)
    return pl.pallas_call(
        kern,
        grid=(tiles, c // LANES),
        in_specs=[x_spec(0), x_spec(1), pl.BlockSpec((SUBLANES,) + table.shape[1:], lambda i, j: (i, 0, 0))],
        out_specs=[o_spec, o_spec],
        out_shape=[o_shape, o_shape],
        compiler_params=_cparams("parallel", "parallel"),
        name="dft_stage_a",
    )(u5, u5, table)


def _dft_b_conv_kernel(ar_ref, ai_ref, kr_ref, ki_ref, fb_ref, ib_ref, cr_ref, ci_ref, *, slabs):
    nf = DFT_FAST
    tc = ar_ref.shape[-1]
    for s in range(slabs):
        tile_rows = slice(s * SUBLANES, (s + 1) * SUBLANES)
        a = jnp.concatenate([ar_ref[:, tile_rows, :].reshape(nf, tc), ai_ref[:, tile_rows, :].reshape(nf, tc)], axis=0)
        x = _dot(fb_ref[...], a.astype(BF16))
        xr, xi = x[:nf], x[nf:]
        kr, ki = kr_ref[s * nf:(s + 1) * nf, :], ki_ref[s * nf:(s + 1) * nf, :]
        y = jnp.concatenate([xr * kr - xi * ki, xr * ki + xi * kr], axis=0).astype(BF16)
        cc = _dot(ib_ref[...], y)
        cr_ref[:, tile_rows, :] = cc[:nf].reshape(DFT_TILES, SUBLANES, tc)
        ci_ref[:, tile_rows, :] = cc[nf:].reshape(DFT_TILES, SUBLANES, tc)


def dft_stage_b_conv(ar, ai, kf, order, fwd_b, inv_b, *, slabs, tc):
    tiles, rows, c = ar.shape
    kern = functools.partial(_dft_b_conv_kernel, slabs=slabs)
    spec = pl.BlockSpec((tiles, slabs * SUBLANES, tc), lambda i, j: (0, i, j))
    k_spec = lambda part: pl.BlockSpec((None, None, slabs * DFT_FAST, tc), lambda i, j: (order, part, i, j))
    mat = pl.BlockSpec(fwd_b.shape, lambda i, j: (0, 0))
    shape = jax.ShapeDtypeStruct(ar.shape, F32)
    return pl.pallas_call(
        kern,
        grid=(rows // (slabs * SUBLANES), c // tc),
        in_specs=[spec, spec, k_spec(0), k_spec(1), mat, mat],
        out_specs=[spec, spec],
        out_shape=[shape, shape],
        compiler_params=_cparams("parallel", "parallel"),
        name="dft_stage_b_conv",
    )(ar, ai, kf, kf, fwd_b, inv_b)


def _dft_a_inv_kernel(cr_ref, ci_ref, g_ref, gate_ref, u_ref, bias_ref, o_ref):
    ns = cr_ref.shape[0] // SUBLANES
    half = ns // 2
    for j in range(SUBLANES):
        rows = pl.ds(j, ns, stride=SUBLANES)
        cc = jnp.concatenate([cr_ref[rows, :], ci_ref[rows, :]], axis=0)
        y = _dot(g_ref[j], cc.astype(BF16))
        o_ref[0, pl.ds(j, half, stride=SUBLANES), :] = y[:half]
        o_ref[1, pl.ds(j, half, stride=SUBLANES), :] = y[half:]
    o_ref[...] = gate_ref[...] * (o_ref[...] + u_ref[...] * bias_ref[...])


def dft_stage_a_inv(cr, ci, table, gate5, gate_idx, u5, u_idx, bias):
    _, _, tiles, rows, c = u5.shape
    c_spec = pl.BlockSpec((None, 2 * rows, LANES), lambda i, j: (i, 0, j))
    t_spec = lambda k: pl.BlockSpec((None, 2, None, rows, LANES), lambda i, j: (k, 0, i, 0, j))
    return pl.pallas_call(
        _dft_a_inv_kernel,
        grid=(tiles, c // LANES),
        in_specs=[c_spec, c_spec, pl.BlockSpec((SUBLANES,) + table.shape[1:], lambda i, j: (i, 0, 0)),
                  t_spec(gate_idx), t_spec(u_idx), pl.BlockSpec((1, LANES), lambda i, j: (0, j))],
        out_specs=pl.BlockSpec((2, None, rows, LANES), lambda i, j: (0, i, 0, j)),
        out_shape=jax.ShapeDtypeStruct((2, tiles, rows, c), F32),
        compiler_params=_cparams("parallel", "parallel"),
        name="dft_stage_a_inv",
    )(cr, ci, table, gate5, u5, bias.reshape(1, c))


def hyena_filter_spectra(seq_len, tables, w1, b1, w2, b2, w3, freq):
    eo, asum = hyena_filter_mlp(seq_len, w1, b1, w2, b2, w3, freq, tl=512)
    return hyena_filter_dft(eo, asum, tables)


def hyena_long_conv(u5, u_idx, gate5, gate_idx, kf, order, bias, tables):
    ar, ai = dft_stage_a(u5, u_idx, tables['fwd_a'])
    cr, ci = dft_stage_b_conv(ar, ai, kf, order, tables['fwd_b'], tables['inv_b'], slabs=4, tc=512)
    return dft_stage_a_inv(cr, ci, tables['inv_a'], gate5, gate_idx, u5, u_idx, bias)


def hyena_mixer(p, short_w, short_b, kf, bias, tables):
    gates5 = hyena_short_conv(p, short_w, short_b, tm=512)
    z = hyena_long_conv(gates5, 2, gates5, 1, kf, 0, bias[0], tables)
    return hyena_long_conv(z[None], 0, gates5, 0, kf, 1, bias[1], tables)


def _odd_out_kernel(ys_ref, u_ref, hy_ref, x_ref, d_ref, wglu_ref, wo_ref, g_ref, o_ref):
    y = ys_ref[0] + d_ref[...] * u_ref[0]
    c0 = math.sqrt(2.0 / math.pi)
    gl = 0.5 * y * (1.0 + jnp.tanh(c0 * (y + 0.044715 * (y * y * y))))
    z = _dot(gl.astype(BF16), wglu_ref[...])
    s5 = gl * (1.0 / (1.0 + jnp.exp(-z)))
    mix_in = jnp.concatenate([s5.astype(BF16), _from_time_tiles(hy_ref).astype(BF16)], axis=-1)
    mix = _dot(mix_in, wo_ref[...])
    o_ref[0] = x_ref[0] + _rms(mix, g_ref[...])


def odd_out(ys, p, hy, x, d_skip, w_glu, w_out, layer, g, *, tm):
    b, l, d = x.shape
    tok = lambda w: pl.BlockSpec((1, tm, w), lambda bi, i: (bi, i, 0))
    hy_spec = pl.BlockSpec((None, DFT_TILES, tm // DFT_FAST * SUBLANES, D_HYENA), lambda bi, i: (bi, 0, i, 0))
    return pl.pallas_call(
        _odd_out_kernel,
        grid=(b, l // tm),
        in_specs=[tok(D_S5), tok(D_S5), hy_spec, tok(d),
                  pl.BlockSpec((1, D_S5), lambda bi, i: (0, 0)),
                  _layer_spec((D_S5, D_S5), layer, lambda bi, i: (0, 0)),
                  _layer_spec((d, d), layer, lambda bi, i: (0, 0)),
                  pl.BlockSpec((1, d), lambda bi, i: (0, 0))],
        out_specs=tok(d),
        out_shape=jax.ShapeDtypeStruct(x.shape, F32),
        compiler_params=_cparams("parallel", "parallel"),
        name="odd_out",
    )(ys, p, hy, x, d_skip.reshape(1, D_S5), w_glu, w_out, g.reshape(1, d))


def kernel(x, mem, norm_mix, norm_xattn, norm_mem, norm_mlp, xa_wq, xa_wk, xa_wv, xa_wo, mlp_w1, mlp_w2, ev_w_in, ev_pool_w, ev_pool_scale, ev_conv_w, ev_w_out, od_w_in, od_s5_lambda_re, od_s5_lambda_im, od_s5_log_dt, od_s5_b_re, od_s5_b_im, od_s5_c_re, od_s5_c_im, od_s5_d, od_s5_w_glu, od_hy_short_w, od_hy_short_b, od_hy_w1, od_hy_b1, od_hy_w2, od_hy_b2, od_hy_w3, od_hy_freq, od_hy_bias, od_w_out):
    b, l, d = x.shape
    depth = norm_mix.shape[0]
    assert b == 2, "the long convolution packs the two batch rows as one complex signal"
    tables = dft_tables(l)
    mem2d = mem.reshape(b * N_MEM, d)
    wq, wo = xa_wq.astype(BF16), xa_wo.astype(BF16)
    wkv = jnp.concatenate([xa_wk, xa_wv], axis=2).astype(BF16)
    w1, w2 = mlp_w1.astype(BF16), mlp_w2.astype(BF16)
    ev_in, ev_out, ev_pool = ev_w_in.astype(BF16), ev_w_out.astype(BF16), ev_pool_w.astype(BF16)
    od_in, od_out, od_glu = od_w_in.astype(BF16), od_w_out.astype(BF16), od_s5_w_glu.astype(BF16)
    for i in range(depth):
        j = i // 2
        if i % 2 == 0:
            p = norm_matmul(x.reshape(b * l, d), norm_mix[i, 0], ev_in, j, tm=512)
            x = even_mixer(p.reshape(b, l, -1), x, ev_pool, ev_pool_scale[j], ev_conv_w[j], ev_out, j,
                           norm_mix[i, 1], tm=512)
        else:
            p = norm_matmul(x.reshape(b * l, d), norm_mix[i, 0], od_in, j, tm=512).reshape(b, l, -1)
            mats = s5_matrices(od_s5_lambda_re[j], od_s5_lambda_im[j], od_s5_log_dt[j], od_s5_b_re[j],
                               od_s5_b_im[j], od_s5_c_re[j], od_s5_c_im[j])
            ys = s5_core(p, mats)
            kf = hyena_filter_spectra(l, tables, od_hy_w1[j], od_hy_b1[j], od_hy_w2[j], od_hy_b2[j],
                                      od_hy_w3[j], od_hy_freq[j])
            hy = hyena_mixer(p, od_hy_short_w[j], od_hy_short_b[j], kf, od_hy_bias[j], tables)
            x = odd_out(ys, p, hy, x, od_s5_d[j], od_glu, od_out, j, norm_mix[i, 1], tm=512)
        kv = norm_matmul(mem2d, norm_mem[i], wkv, i, tm=b * N_MEM, out_dtype=BF16).reshape(b, N_MEM, 2 * d)
        x = xattn_block(x, kv, wq, wo, i, norm_xattn[i, 0], norm_xattn[i, 1], tm=512)
        x = mlp_block(x.reshape(b * l, d), w1, w2, i, norm_mlp[i, 0], norm_mlp[i, 1],
                      tm=1024, tf=512).reshape(b, l, d)
    return x
```

```python
import functools
import math

import numpy as np
import jax
import jax.numpy as jnp
from jax import lax
from jax.experimental import pallas as pl
from jax.experimental.pallas import tpu as pltpu

F32 = jnp.float32
BF16 = jnp.bfloat16

D_MODEL = 1024
N_MEM = 256
RMS_EPS = 1e-6
D_POOL = 512
POOL_WINDOWS = (2, 4, 8, 16)
POOL_GROUP_DIM = 128
POOL_HALO = 8
D_CONV = 512
D_S5 = 512
S5_GROUP_DIM = 16
S5_GROUPS = 32
S5_STATE = 64
S5_CHUNK = 16
D_HYENA = 512
HYENA_BANDS = 16
HYENA_FFN = 64
HYENA_TARGET = 1e-2
HYENA_SHORT_DECAY_PCT = 0.3
HYENA_LONG_DECAY_PCT = 1.5
XA_HEADS = 4
XA_HEAD_DIM = 256
D_FF = 4096

LANES = 128
SUBLANES = 8
S5_SUPER = LANES // S5_GROUP_DIM
DFT_FAST = 128
DFT_TILES = DFT_FAST // SUBLANES
DFT_PITCH = DFT_FAST + SUBLANES
DFT_UNROLL = 16
VMEM_LIMIT = 56 * 1024 * 1024


def _cparams(*sem):
    return pltpu.CompilerParams(dimension_semantics=sem, vmem_limit_bytes=VMEM_LIMIT)


def _rms(xf, g):
    ms = jnp.mean(xf * xf, axis=-1, keepdims=True)
    return xf * lax.rsqrt(ms + RMS_EPS) * g


def _dot(a, b):
    return jnp.dot(a, b, preferred_element_type=F32)


def _layer_spec(block, layer, tail_map):
    return pl.BlockSpec((None,) + block, lambda *idx: (layer,) + tail_map(*idx))


def _norm_matmul_kernel(x_ref, g_ref, w_ref, o_ref):
    xn = _rms(x_ref[...], g_ref[...]).astype(BF16)
    o_ref[...] = _dot(xn, w_ref[...]).astype(o_ref.dtype)


def norm_matmul(x2d, g, w, layer, *, tm, out_dtype=F32):
    m, d = x2d.shape
    n = w.shape[-1]
    return pl.pallas_call(
        _norm_matmul_kernel,
        grid=(m // tm,),
        in_specs=[
            pl.BlockSpec((tm, d), lambda i: (i, 0)),
            pl.BlockSpec((1, d), lambda i: (0, 0)),
            _layer_spec((d, n), layer, lambda i: (0, 0)),
        ],
        out_specs=pl.BlockSpec((tm, n), lambda i: (i, 0)),
        out_shape=jax.ShapeDtypeStruct((m, n), out_dtype),
        compiler_params=_cparams("parallel"),
        name="norm_matmul",
    )(x2d, g.reshape(1, d), w)


def _xattn_kernel(x_ref, kv_ref, wq_ref, wo_ref, g1_ref, g2_ref, o_ref):
    x = x_ref[0]
    xn = _rms(x, g1_ref[...]).astype(BF16)
    q = (_dot(xn, wq_ref[...]) * (XA_HEAD_DIM ** -0.5)).astype(BF16)
    heads = []
    for h in range(XA_HEADS):
        lo = h * XA_HEAD_DIM
        qh = q[:, lo:lo + XA_HEAD_DIM]
        kh = kv_ref[0, :, lo:lo + XA_HEAD_DIM]
        vh = kv_ref[0, :, D_MODEL + lo:D_MODEL + lo + XA_HEAD_DIM]
        s = lax.dot_general(qh, kh, (((1,), (1,)), ((), ())), preferred_element_type=F32)
        e = jnp.exp(s - jnp.max(s, axis=-1, keepdims=True))
        p = e / jnp.sum(e, axis=-1, keepdims=True)
        heads.append(_dot(p.astype(BF16), vh).astype(BF16))
    o = jnp.concatenate(heads, axis=-1)
    y = _dot(o, wo_ref[...])
    o_ref[0] = x + _rms(y, g2_ref[...])


def xattn_block(x, kv, wq, wo, layer, g1, g2, *, tm):
    b, l, d = x.shape
    return pl.pallas_call(
        _xattn_kernel,
        grid=(b, l // tm),
        in_specs=[
            pl.BlockSpec((1, tm, d), lambda bi, i: (bi, i, 0)),
            pl.BlockSpec((1, N_MEM, 2 * d), lambda bi, i: (bi, 0, 0)),
            _layer_spec((d, d), layer, lambda bi, i: (0, 0)),
            _layer_spec((d, d), layer, lambda bi, i: (0, 0)),
            pl.BlockSpec((1, d), lambda bi, i: (0, 0)),
            pl.BlockSpec((1, d), lambda bi, i: (0, 0)),
        ],
        out_specs=pl.BlockSpec((1, tm, d), lambda bi, i: (bi, i, 0)),
        out_shape=jax.ShapeDtypeStruct(x.shape, F32),
        compiler_params=_cparams("parallel", "parallel"),
        name="xattn_block",
    )(x, kv, wq, wo, g1.reshape(1, d), g2.reshape(1, d))


def _mlp_kernel(x_ref, w1_ref, w2_ref, g1_ref, g2_ref, o_ref, h_ref, *, tf):
    x = x_ref[...]
    xn = _rms(x, g1_ref[...]).astype(BF16)
    for c in range(h_ref.shape[1] // tf):
        cols = slice(c * tf, (c + 1) * tf)
        h = jnp.maximum(_dot(xn, w1_ref[:, cols]), 0.0)
        h_ref[:, cols] = (h * h).astype(BF16)
    o_ref[...] = x + _rms(_dot(h_ref[...], w2_ref[...]), g2_ref[...])


def mlp_block(x2d, w1, w2, layer, g1, g2, *, tm, tf):
    m, d = x2d.shape
    ff = w1.shape[-1]
    resident = dict(pipeline_mode=pl.Buffered(1))
    return pl.pallas_call(
        functools.partial(_mlp_kernel, tf=tf),
        grid=(m // tm,),
        in_specs=[
            pl.BlockSpec((tm, d), lambda i: (i, 0)),
            pl.BlockSpec((None, d, ff), lambda i: (layer, 0, 0), **resident),
            pl.BlockSpec((None, ff, d), lambda i: (layer, 0, 0), **resident),
            pl.BlockSpec((1, d), lambda i: (0, 0)),
            pl.BlockSpec((1, d), lambda i: (0, 0)),
        ],
        out_specs=pl.BlockSpec((tm, d), lambda i: (i, 0)),
        out_shape=jax.ShapeDtypeStruct((m, d), F32),
        scratch_shapes=[pltpu.VMEM((tm, ff), BF16)],
        compiler_params=_cparams("parallel"),
        name="mlp_block",
    )(x2d, w1, w2, g1.reshape(1, d), g2.reshape(1, d))


def _halo_specs(tm, seq_len, width, col):
    r = tm // POOL_HALO
    last = seq_len // POOL_HALO - 1

    def prev_map(bi, i, *_):
        return (bi, jnp.maximum(i * r - 1, 0), col(*_))

    def main_map(bi, i, *_):
        return (bi, i, col(*_))

    def next_map(bi, i, *_):
        return (bi, jnp.minimum((i + 1) * r, last), col(*_))

    return [
        pl.BlockSpec((1, POOL_HALO, width), prev_map),
        pl.BlockSpec((1, tm, width), main_map),
        pl.BlockSpec((1, POOL_HALO, width), next_map),
    ]


def _with_halo(prev_ref, main_ref, next_ref):
    i = pl.program_id(1)
    prev = jnp.where(i > 0, prev_ref[0], 0.0)
    nxt = jnp.where(i < pl.num_programs(1) - 1, next_ref[0], 0.0)
    return jnp.concatenate([prev, main_ref[0], nxt], axis=0)


def _shift_rows(v, k):
    return pltpu.roll(v, k % v.shape[0], 0)


def _even_mixer_kernel(pp_ref, p_ref, pn_ref, x_ref, wg_ref, ps_ref, cw_ref, wo_ref, g_ref, o_ref,
                       *, tm, seq_len):
    ext = _with_halo(pp_ref, p_ref, pn_ref)
    lo, hi = POOL_HALO, POOL_HALO + tm
    t = pl.program_id(1) * tm + lax.broadcasted_iota(jnp.int32, (tm, 1), 0)
    parts = []
    for gi, win in enumerate(POOL_WINDOWS):
        half = win // 2
        u = ext[:, gi * POOL_GROUP_DIM:(gi + 1) * POOL_GROUP_DIM]
        s = u + _shift_rows(u, 1)
        step = 1
        while 2 * step < win:
            s = _shift_rows(s, step) + _shift_rows(s, -step)
            step *= 2
        cnt = (jnp.minimum(t + half, seq_len) - jnp.maximum(t - half, 0)).astype(F32)
        pooled = s[lo:hi] / cnt - u[lo:hi]
        y = _dot(pooled.astype(BF16), wg_ref[gi])
        parts.append((y * ps_ref[:, gi * POOL_GROUP_DIM:(gi + 1) * POOL_GROUP_DIM]).astype(BF16))
    b_gate = ext[lo:hi, D_POOL:D_POOL + D_CONV]
    ch = ext[:, D_POOL + D_CONV:D_POOL + 2 * D_CONV] * ext[:, D_POOL + 2 * D_CONV:D_POOL + 3 * D_CONV]
    conv = cw_ref[0:1, :] * _shift_rows(ch, 1) + cw_ref[1:2, :] * ch + cw_ref[2:3, :] * _shift_rows(ch, -1)
    parts.append((b_gate * conv[lo:hi]).astype(BF16))
    mix = _dot(jnp.concatenate(parts, axis=-1), wo_ref[...])
    o_ref[0] = x_ref[0] + _rms(mix, g_ref[...])


def even_mixer(p, x, w_group, pool_scale, conv_w, w_out, layer, g, *, tm):
    b, l, d = x.shape
    width = p.shape[-1]
    kern = functools.partial(_even_mixer_kernel, tm=tm, seq_len=l)
    return pl.pallas_call(
        kern,
        grid=(b, l // tm),
        in_specs=_halo_specs(tm, l, width, lambda: 0) + [
            pl.BlockSpec((1, tm, d), lambda bi, i: (bi, i, 0)),
            _layer_spec(w_group.shape[1:], layer, lambda bi, i: (0, 0, 0)),
            pl.BlockSpec((1, D_POOL), lambda bi, i: (0, 0)),
            pl.BlockSpec((3, D_CONV), lambda bi, i: (0, 0)),
            _layer_spec((d, d), layer, lambda bi, i: (0, 0)),
            pl.BlockSpec((1, d), lambda bi, i: (0, 0)),
        ],
        out_specs=pl.BlockSpec((1, tm, d), lambda bi, i: (bi, i, 0)),
        out_shape=jax.ShapeDtypeStruct(x.shape, F32),
        compiler_params=_cparams("parallel", "parallel"),
        name="even_mixer",
    )(p, p, p, x, w_group, pool_scale.reshape(1, D_POOL), conv_w, w_out, g.reshape(1, d))


def _cmul(ar, ai, br, bi):
    return ar * br - ai * bi, ar * bi + ai * br


def s5_matrices(lam_re, lam_im, log_dt, b_re, b_im, c_re, c_im):
    tc, hd, n, sup = S5_CHUNK, S5_GROUP_DIM, S5_STATE, S5_SUPER
    n_tiles = S5_GROUPS // sup
    hp = lax.Precision.HIGHEST
    lr = jnp.minimum(lam_re, -1e-4)
    li = lam_im
    dt = jnp.exp(log_dt)[..., None]
    taus = jnp.arange(tc + 1, dtype=F32)[:, None, None, None]
    rmag = jnp.exp(lr[None] * dt[None] * (tc - taus))
    rang = li[None] * dt[None] * (tc - taus)
    pwr_re, pwr_im = rmag * jnp.cos(rang), rmag * jnp.sin(rang)
    mag = jnp.exp(lr[None] * dt[None] * taus)
    ang = li[None] * dt[None] * taus
    pw_re, pw_im = mag * jnp.cos(ang), mag * jnp.sin(ang)
    nr, ni = pw_re[1] - 1.0, pw_im[1]
    den = lr * lr + li * li
    coef_re, coef_im = (nr * lr + ni * li) / den, (ni * lr - nr * li) / den
    cb_re, cb_im = _cmul(coef_re[..., None], coef_im[..., None], b_re[None], b_im[None])
    cbt_re, cbt_im = _cmul(coef_re[:, :, None, :], coef_im[:, :, None, :],
                           jnp.swapaxes(b_re, 1, 2)[None], jnp.swapaxes(b_im, 1, 2)[None])
    q_re, q_im = _cmul(pw_re[:tc, :, :, :, None], pw_im[:tc, :, :, :, None], cb_re[None], cb_im[None])
    m = (jnp.einsum('dgkn,tdgnj->dtgjk', c_re, q_re, precision=hp)
         - jnp.einsum('dgkn,tdgnj->dtgjk', c_im, q_im, precision=hp))
    m_fwd, m_bwd = m[0], m[1]
    diag = m_fwd[0] + m_bwd[0]
    lag_table = jnp.concatenate([m_bwd[:0:-1], diag[None], m_fwd[1:]], axis=0)
    intra = jnp.stack([lag_table[tc - 1 - s:2 * tc - 1 - s] for s in range(tc)], axis=0)
    intra = jnp.transpose(intra, (2, 0, 3, 1, 4)).reshape(S5_GROUPS, tc * hd, tc * hd)
    sf_re, sf_im = _cmul(pwr_re[1:tc + 1, 0][:, :, None, :], pwr_im[1:tc + 1, 0][:, :, None, :],
                         cbt_re[0][None], cbt_im[0][None])
    sb_re, sb_im = _cmul(pw_re[:tc, 1][:, :, None, :], pw_im[:tc, 1][:, :, None, :],
                         cbt_re[1][None], cbt_im[1][None])
    summ = jnp.concatenate([sf_re, sb_re, sf_im, sb_im], axis=-1)
    summ = jnp.swapaxes(summ, 0, 1).reshape(S5_GROUPS, tc * hd, 4 * n)
    ct_re, ct_im = jnp.swapaxes(c_re, 2, 3), jnp.swapaxes(c_im, 2, 3)
    gnt = lambda z: jnp.transpose(z, (1, 2, 0))[..., None]
    qf_re, qf_im = _cmul(ct_re[0][:, :, None, :], ct_im[0][:, :, None, :],
                         gnt(pw_re[1:tc + 1, 0]), gnt(pw_im[1:tc + 1, 0]))
    qb_re, qb_im = _cmul(ct_re[1][:, :, None, :], ct_im[1][:, :, None, :], gnt(pwr_re[:tc, 1]), gnt(pwr_im[:tc, 1]))
    carry = jnp.concatenate([qf_re, qb_re, -qf_im, -qb_im], axis=1)
    carry = carry.reshape(S5_GROUPS, 4 * n, tc * hd)
    a_re = jnp.concatenate([pw_re[tc, 0], pw_re[tc, 1]], axis=-1)
    a_im = jnp.concatenate([pw_im[tc, 0], pw_im[tc, 1]], axis=-1)
    return intra, summ, carry, a_re, a_im


def s5_regroup_matrix():
    n = S5_SUPER * LANES
    src = np.arange(n)
    tb, g, h = src // LANES, (src % LANES) // S5_GROUP_DIM, src % S5_GROUP_DIM
    perm = np.zeros((n, n), np.float32)
    perm[src, g * LANES + tb * S5_GROUP_DIM + h] = 1.0
    return jnp.asarray(perm, BF16)


def _s5_regroup_kernel(p_ref, w_ref, perm_ref, u_ref, sr_ref, si_ref, *, n_chunks):
    n2 = 2 * S5_STATE
    for q in range(S5_CHUNK // S5_SUPER):
        rows_t = [p_ref[pl.ds(q * S5_SUPER + tb, n_chunks, stride=S5_CHUNK), :].astype(BF16) for tb in range(S5_SUPER)]
        grouped = _dot(jnp.concatenate(rows_t, axis=-1), perm_ref[...]).astype(BF16)
        for g in range(S5_SUPER):
            u_ref[g, :, q * LANES:(q + 1) * LANES] = grouped[:, g * LANES:(g + 1) * LANES]
    for g in range(S5_SUPER):
        s = _dot(u_ref[g], w_ref[g])
        sr_ref[pl.ds(g, n_chunks, stride=S5_SUPER), :] = s[:, :n2]
        si_ref[pl.ds(g, n_chunks, stride=S5_SUPER), :] = s[:, n2:]


def _s5_state_spec(c, n2, n_tiles):
    return pl.BlockSpec((None, c * S5_SUPER, n2), lambda a, bi: (bi * n_tiles + a, 0, 0))


def _s5_group_spec(rows, cols):
    return pl.BlockSpec((S5_SUPER, rows, cols), lambda a, bi: (a, 0, 0))


def s5_summary(p, summ):
    b, l, _ = p.shape
    c = l // S5_CHUNK
    n_tiles = S5_GROUPS // S5_SUPER
    k = S5_CHUNK * S5_GROUP_DIM
    n2 = 2 * S5_STATE
    shape = jax.ShapeDtypeStruct((b * n_tiles, c * S5_SUPER, n2), F32)
    kern = functools.partial(_s5_regroup_kernel, n_chunks=c)
    perm = s5_regroup_matrix()
    return pl.pallas_call(
        kern,
        grid=(n_tiles, b),
        in_specs=[
            pl.BlockSpec((None, l, LANES), lambda a, bi: (bi, 0, a)),
            _s5_group_spec(k, 2 * n2),
            pl.BlockSpec(perm.shape, lambda a, bi: (0, 0)),
        ],
        out_specs=[pl.BlockSpec((None, S5_SUPER, c, k), lambda a, bi: (bi * n_tiles + a, 0, 0, 0)),
                   _s5_state_spec(c, n2, n_tiles), _s5_state_spec(c, n2, n_tiles)],
        out_shape=[jax.ShapeDtypeStruct((b * n_tiles, S5_SUPER, c, k), BF16), shape, shape],
        compiler_params=_cparams("parallel", "parallel"),
        name="s5_summary",
    )(p, summ, perm)


def _s5_scan_kernel(sr_ref, si_ref, are_ref, aim_ref, or_ref, oi_ref, *, n_chunks):
    n2 = 2 * S5_STATE
    a_re = are_ref[...]
    a_im = aim_ref[...]
    fwd_lane = lax.broadcasted_iota(jnp.int32, (SUBLANES, n2), 1) < S5_STATE

    def body(i, state):
        x_re, x_im = state
        rows_i = pl.ds(pl.multiple_of(i * SUBLANES, SUBLANES), SUBLANES)
        rows_r = pl.ds(pl.multiple_of((n_chunks - 1 - i) * SUBLANES, SUBLANES), SUBLANES)
        or_ref[rows_i, 0:S5_STATE] = x_re[:, :S5_STATE]
        oi_ref[rows_i, 0:S5_STATE] = x_im[:, :S5_STATE]
        or_ref[rows_r, S5_STATE:n2] = x_re[:, S5_STATE:]
        oi_ref[rows_r, S5_STATE:n2] = x_im[:, S5_STATE:]
        s_re = jnp.where(fwd_lane, sr_ref[rows_i, :], sr_ref[rows_r, :])
        s_im = jnp.where(fwd_lane, si_ref[rows_i, :], si_ref[rows_r, :])
        return (a_re * x_re - a_im * x_im + s_re, a_re * x_im + a_im * x_re + s_im)

    zero = jnp.zeros((SUBLANES, n2), F32)
    lax.fori_loop(0, n_chunks, body, (zero, zero))


def s5_scan(s_re, s_im, a_re, a_im):
    tiles, rows, n2 = s_re.shape
    kern = functools.partial(_s5_scan_kernel, n_chunks=rows // S5_SUPER)
    st = pl.BlockSpec((None, rows, n2), lambda i: (i, 0, 0))
    av = pl.BlockSpec((SUBLANES, n2), lambda i: (i, 0))
    shape = jax.ShapeDtypeStruct(s_re.shape, F32)
    return pl.pallas_call(
        kern,
        grid=(tiles,),
        in_specs=[st, st, av, av],
        out_specs=[st, st],
        out_shape=[shape, shape],
        compiler_params=_cparams("parallel"),
        name="s5_scan",
    )(s_re, s_im, a_re, a_im)


def _s5_output_kernel(u_ref, xr_ref, xi_ref, wi_ref, wc_ref, perm_ref, o_ref, y_ref, *, n_chunks):
    for g in range(S5_SUPER):
        rows = pl.ds(g, n_chunks, stride=S5_SUPER)
        xin = jnp.concatenate([xr_ref[rows, :], xi_ref[rows, :]], axis=-1).astype(BF16)
        y_ref[g] = _dot(u_ref[g], wi_ref[g]) + _dot(xin, wc_ref[g])
    back = (((1,), (1,)), ((), ()))
    for q in range(S5_CHUNK // S5_SUPER):
        y = jnp.concatenate([y_ref[g, :, q * LANES:(q + 1) * LANES] for g in range(S5_SUPER)], axis=-1)
        hi = y.astype(BF16)
        lo = (y - hi.astype(F32)).astype(BF16)
        tok = (lax.dot_general(hi, perm_ref[...], back, preferred_element_type=F32)
               + lax.dot_general(lo, perm_ref[...], back, preferred_element_type=F32))
        for tb in range(S5_SUPER):
            o_ref[pl.ds(q * S5_SUPER + tb, n_chunks, stride=S5_CHUNK), :] = tok[:, tb * LANES:(tb + 1) * LANES]


def s5_output(u, xin_re, xin_im, intra, carry, *, batch):
    tiles, _, c, k = u.shape
    n_tiles = tiles // batch
    n2 = 2 * S5_STATE
    l = c * S5_CHUNK
    kern = functools.partial(_s5_output_kernel, n_chunks=c)
    perm = s5_regroup_matrix()
    return pl.pallas_call(
        kern,
        grid=(n_tiles, batch),
        in_specs=[
            pl.BlockSpec((None, S5_SUPER, c, k), lambda a, bi: (bi * n_tiles + a, 0, 0, 0)),
            _s5_state_spec(c, n2, n_tiles),
            _s5_state_spec(c, n2, n_tiles),
            _s5_group_spec(k, k),
            _s5_group_spec(2 * n2, k),
            pl.BlockSpec(perm.shape, lambda a, bi: (0, 0)),
        ],
        out_specs=pl.BlockSpec((None, l, LANES), lambda a, bi: (bi, 0, a)),
        out_shape=jax.ShapeDtypeStruct((batch, l, D_S5), F32),
        scratch_shapes=[pltpu.VMEM((S5_SUPER, c, k), F32)],
        compiler_params=_cparams("parallel", "parallel"),
        name="s5_output",
    )(u, xin_re, xin_im, intra, carry, perm)


def s5_core(p, mats):
    b = p.shape[0]
    intra, summ, carry, a_re, a_im = mats
    u, s_re, s_im = s5_summary(p, summ.astype(BF16))
    xin_re, xin_im = s5_scan(s_re, s_im, jnp.tile(a_re, (b, 1)), jnp.tile(a_im, (b, 1)))
    return s5_output(u, xin_re, xin_im, intra.astype(BF16), carry.astype(BF16), batch=b)


def _to_time_tiles(y):
    nf, c = DFT_FAST, y.shape[-1]
    slabs = [y[s * nf:(s + 1) * nf].reshape(DFT_TILES, SUBLANES, c) for s in range(y.shape[0] // nf)]
    return jnp.concatenate(slabs, axis=1)


def _from_time_tiles(v):
    c = v.shape[-1]
    slabs = [v[:, s * SUBLANES:(s + 1) * SUBLANES, :].reshape(DFT_FAST, c) for s in range(v.shape[1] // SUBLANES)]
    return jnp.concatenate(slabs, axis=0)


def _odd_in_kernel(xp_ref, x_ref, xn_ref, g_ref, w_ref, cw_ref, cb_ref, s5_ref, gates_ref, *, tm):
    ext = _with_halo(xp_ref, x_ref, xn_ref)
    p = _dot(_rms(ext, g_ref[...]).astype(BF16), w_ref[...])
    lo, hi = POOL_HALO, POOL_HALO + tm
    s5_ref[0] = p[lo:hi, :D_S5]
    ph = p[:, D_S5:]
    y = cw_ref[0:1, :] * _shift_rows(ph, 1) + cw_ref[1:2, :] * ph + cw_ref[2:3, :] * _shift_rows(ph, -1)
    y = y[lo:hi] + cb_ref[...]
    for k in range(3):
        for ct in range(D_HYENA // LANES):
            col = k * D_HYENA + ct * LANES
            gates_ref[k, ct] = _to_time_tiles(y[:, col:col + LANES])


def odd_in(x, g, w, layer, conv_w, conv_b, *, tm):
    b, l, d = x.shape
    n = w.shape[-1]
    rows = tm // DFT_FAST * SUBLANES
    cts = D_HYENA // LANES
    kern = functools.partial(_odd_in_kernel, tm=tm)
    return pl.pallas_call(
        kern,
        grid=(b, l // tm),
        in_specs=_halo_specs(tm, l, d, lambda: 0) + [
            pl.BlockSpec((1, d), lambda bi, i: (0, 0)),
            _layer_spec((d, n), layer, lambda bi, i: (0, 0)),
            pl.BlockSpec((3, n - D_S5), lambda bi, i: (0, 0)),
            pl.BlockSpec((1, n - D_S5), lambda bi, i: (0, 0)),
        ],
        out_specs=[pl.BlockSpec((1, tm, D_S5), lambda bi, i: (bi, i, 0)),
                   pl.BlockSpec((3, None, cts, DFT_TILES, rows, LANES), lambda bi, i: (0, bi, 0, 0, i, 0))],
        out_shape=[jax.ShapeDtypeStruct((b, l, D_S5), F32),
                   jax.ShapeDtypeStruct((3, b, cts, DFT_TILES, l // DFT_FAST * SUBLANES, LANES), F32)],
        compiler_params=_cparams("parallel", "parallel"),
        name="odd_in",
    )(x, x, x, g.reshape(1, d), w, conv_w, conv_b.reshape(1, -1))


def _filter_mlp_kernel(bands_ref, w1t_ref, w1cs_ref, b1_ref, w2_ref, b2_ref, w3_ref, fr_ref, dl_ref,
                       eo_ref, asum_ref, *, tl, seq_len):
    i = pl.program_id(0)
    hp = lax.Precision.HIGHEST
    c = D_HYENA
    t = (i * tl + lax.broadcasted_iota(jnp.int32, (tl, 1), 0)).astype(F32)
    t_norm = t / (seq_len - 1.0)
    ang = (2.0 * math.pi / seq_len) * t * bands_ref[...]
    is_cos = lax.broadcasted_iota(jnp.int32, ang.shape, 1) < HYENA_BANDS
    cs = jnp.where(is_cos, jnp.cos(ang), -jnp.sin(ang))
    fr = fr_ref[...]
    z = t_norm * w1t_ref[...] + jnp.dot(cs, w1cs_ref[...], precision=hp, preferred_element_type=F32)
    h = jnp.sin(fr * (z + b1_ref[...]))
    h = jnp.sin(fr * (jnp.dot(h, w2_ref[...], precision=hp, preferred_element_type=F32) + b2_ref[...]))
    h = jnp.dot(h, w3_ref[...], precision=hp, preferred_element_type=F32)
    decay = jnp.exp(-t_norm * dl_ref[...])

    @pl.when(i == 0)
    def _():
        asum_ref[...] = jnp.zeros_like(asum_ref)

    for o in range(h.shape[1] // (2 * c)):
        fwd = h[:, 2 * o * c:(2 * o + 1) * c] * decay
        bwd = h[:, (2 * o + 1) * c:(2 * o + 2) * c] * decay
        eo_ref[:, 2 * o * c:(2 * o + 1) * c] = fwd + bwd
        eo_ref[:, (2 * o + 1) * c:(2 * o + 2) * c] = fwd - bwd
        asum_ref[:, o * c:(o + 1) * c] += jnp.sum(jnp.abs(fwd) + jnp.abs(bwd), axis=0, keepdims=True)


def hyena_filter_mlp(seq_len, w1, b1, w2, b2, w3, freq, *, tl):
    n_out = w3.shape[1]
    bands = jnp.linspace(1e-4, HYENA_BANDS - 1, HYENA_BANDS, dtype=F32)
    bands2 = jnp.concatenate([bands, bands])[None, :]
    deltas = jnp.abs(jnp.linspace(math.log(HYENA_TARGET) / HYENA_LONG_DECAY_PCT,
                                  math.log(HYENA_TARGET) / HYENA_SHORT_DECAY_PCT, D_HYENA, dtype=F32))[None, :]
    kern = functools.partial(_filter_mlp_kernel, tl=tl, seq_len=seq_len)
    full = lambda a: pl.BlockSpec(a.shape, lambda i: (0,) * a.ndim)
    args = [bands2, w1[0:1], w1[1:], b1[None, :], w2, b2[None, :], w3, freq[None, :], deltas]
    return pl.pallas_call(
        kern,
        grid=(seq_len // tl,),
        in_specs=[full(a) for a in args],
        out_specs=[pl.BlockSpec((tl, n_out), lambda i: (i, 0)), pl.BlockSpec((1, n_out // 2), lambda i: (0, 0))],
        out_shape=[jax.ShapeDtypeStruct((seq_len, n_out), F32), jax.ShapeDtypeStruct((1, n_out // 2), F32)],
        compiler_params=_cparams("arbitrary"),
        name="hyena_filter_mlp",
    )(*args)


def dft_tables(seq_len):
    n = 2 * seq_len
    nf = DFT_FAST
    ns = n // nf
    k_a = np.arange(ns)[:, None]
    n_s = np.arange(ns // 2)[None, :]
    w_s = np.exp(-2j * np.pi * ((k_a * n_s) % ns) / ns)
    n_f = np.arange(nf)[:, None]
    tw = np.exp(-2j * np.pi * ((n_f * np.arange(ns)[None, :]) % n) / n)
    c32 = lambda z: (jnp.asarray(z.real, F32), jnp.asarray(z.imag, F32))
    ws_re, ws_im = c32(w_s)
    tw_re, tw_im = c32(tw)
    g_re = tw_re[:, :, None] * ws_re[None] - tw_im[:, :, None] * ws_im[None]
    g_im = tw_re[:, :, None] * ws_im[None] + tw_im[:, :, None] * ws_re[None]
    fwd_a = jnp.concatenate([jnp.concatenate([g_re, -g_im], axis=2),
                             jnp.concatenate([g_im, g_re], axis=2)], axis=1)
    gt_re = jnp.swapaxes(g_re, 1, 2) * (1.0 / n)
    gt_im = jnp.swapaxes(g_im, 1, 2) * (1.0 / n)
    inv_a = jnp.concatenate([jnp.concatenate([gt_re, gt_im], axis=2),
                             jnp.concatenate([-gt_im, gt_re], axis=2)], axis=1)
    kk = np.arange(nf)
    w_f = np.exp(-2j * np.pi * ((kk[:, None] * kk[None, :]) % nf) / nf)
    f_re, f_im = c32(w_f)
    fwd_b = jnp.concatenate([jnp.concatenate([f_re, -f_im], axis=1),
                             jnp.concatenate([f_im, f_re], axis=1)], axis=0)
    inv_b = jnp.concatenate([jnp.concatenate([f_re, f_im], axis=1),
                             jnp.concatenate([-f_im, f_re], axis=1)], axis=0)
    return dict(fwd_a=fwd_a.astype(BF16), fwd_a_real=fwd_a[:, :, :ns // 2].astype(BF16),
                inv_a=inv_a.astype(BF16), fwd_b=fwd_b.astype(BF16), inv_b=inv_b.astype(BF16),
                fwd_b_part=fwd_b.reshape(2, nf, 2 * nf).astype(BF16))


def _filter_dft_kernel(x_ref, ga_ref, fb_ref, nrm_ref, k_ref, ar_scr, ai_scr):
    nf = DFT_FAST
    ns = ga_ref.shape[1] // 2

    def stage_a(j, carry):
        x = x_ref[pl.ds(j, ns // 2, stride=nf), :].astype(BF16)
        a = _dot(ga_ref[j], x)
        ar_scr[pl.ds(j, ns, stride=DFT_PITCH), :] = a[:ns]
        ai_scr[pl.ds(j, ns, stride=DFT_PITCH), :] = a[ns:]
        return carry

    lax.fori_loop(0, nf, stage_a, 0, unroll=DFT_UNROLL)
    inv_norm = 1.0 / (nrm_ref[...] + 1e-6)

    def stage_b(k, carry):
        src = pl.multiple_of(k * DFT_PITCH, SUBLANES)
        a = jnp.concatenate([ar_scr[pl.ds(src, nf), :], ai_scr[pl.ds(src, nf), :]], axis=0).astype(BF16)
        dst = pl.multiple_of(k * nf, nf)
        k_ref[pl.ds(dst, nf), :] = _dot(fb_ref[...], a) * inv_norm
        return carry

    lax.fori_loop(0, ns, stage_b, 0, unroll=DFT_UNROLL)


def hyena_filter_dft(eo, asum, tables):
    l, cols = eo.shape
    c = D_HYENA
    orders = cols // (2 * c)
    ct = c // LANES
    ga, fb = tables['fwd_a_real'], tables['fwd_b_part']
    ns = ga.shape[1] // 2
    scr = pltpu.VMEM((ns * DFT_PITCH, LANES), F32)
    return pl.pallas_call(
        _filter_dft_kernel,
        grid=(orders, ct, 2),
        in_specs=[
            pl.BlockSpec((l, LANES), lambda o, j, part: (0, (2 * o + part) * ct + j)),
            pl.BlockSpec(ga.shape, lambda o, j, part: (0, 0, 0)),
            pl.BlockSpec((None,) + fb.shape[1:], lambda o, j, part: (part, 0, 0)),
            pl.BlockSpec((1, LANES), lambda o, j, part: (0, o * ct + j)),
        ],
        out_specs=pl.BlockSpec((None, None, None, 2 * l, LANES), lambda o, j, part: (o, part, j, 0, 0)),
        out_shape=jax.ShapeDtypeStruct((orders, 2, ct, 2 * l, LANES), F32),
        scratch_shapes=[scr, scr],
        compiler_params=_cparams("parallel", "parallel", "parallel"),
        name="hyena_filter_dft",
    )(eo, ga, fb, asum)


def _lane_cat(parts):
    return parts[0] if len(parts) == 1 else jnp.concatenate(parts, axis=-1)


def _hyena_conv_kernel(x_ref, ga_ref, k_ref, fb_ref, ib_ref, gi_ref, gate_ref, u_ref, bias_ref, o_ref, *scr,
                       n_a, n_b, slabs):
    nf = DFT_FAST
    tiles = x_ref.shape[1]
    half = x_ref.shape[2] // SUBLANES
    ns = 2 * half
    step = pl.program_id(1)

    @pl.when(step < n_a)
    def _():
        for j in range(SUBLANES):
            rows = pl.ds(j, half, stride=SUBLANES)
            x = _lane_cat([jnp.concatenate([x_ref[0, t, rows, :], x_ref[1, t, rows, :]], axis=0) for t in range(tiles)])
            a = _dot(ga_ref[j], x.astype(BF16))
            dst = pl.ds(step * SUBLANES + j, ns, stride=DFT_PITCH)
            for t in range(tiles):
                scr[2 * t][dst, :] = a[:ns, t * LANES:(t + 1) * LANES]
                scr[2 * t + 1][dst, :] = a[ns:, t * LANES:(t + 1) * LANES]

    @pl.when((step >= n_a) & (step < n_a + n_b))
    def _():
        for s in range(slabs):
            k_a = (step - n_a) * slabs + s
            rows = pl.ds(pl.multiple_of(k_a * DFT_PITCH, SUBLANES), nf)
            a = jnp.concatenate([_lane_cat([scr[2 * t + part][rows, :] for t in range(tiles)]) for part in range(2)], axis=0)
            x = _dot(fb_ref[...], a.astype(BF16))
            xr, xi = x[:nf], x[nf:]
            kr = _lane_cat([k_ref[0, t, s * nf:(s + 1) * nf, :] for t in range(tiles)])
            ki = _lane_cat([k_ref[1, t, s * nf:(s + 1) * nf, :] for t in range(tiles)])
            y = jnp.concatenate([xr * kr - xi * ki, xr * ki + xi * kr], axis=0).astype(BF16)
            cc = _dot(ib_ref[...], y)
            for t in range(tiles):
                scr[2 * t][rows, :] = cc[:nf, t * LANES:(t + 1) * LANES]
                scr[2 * t + 1][rows, :] = cc[nf:, t * LANES:(t + 1) * LANES]

    @pl.when(step >= n_a + n_b)
    def _():
        q = step - (n_a + n_b)
        for j in range(SUBLANES):
            src = pl.ds(q * SUBLANES + j, ns, stride=DFT_PITCH)
            cc = jnp.concatenate([_lane_cat([scr[2 * t + part][src, :] for t in range(tiles)]) for part in range(2)], axis=0)
            y = _dot(gi_ref[j], cc.astype(BF16))
            dst = pl.ds(j, half, stride=SUBLANES)
            for t in range(tiles):
                o_ref[0, t, dst, :] = y[:half, t * LANES:(t + 1) * LANES]
                o_ref[1, t, dst, :] = y[half:, t * LANES:(t + 1) * LANES]
        for t in range(tiles):
            bias = bias_ref[:, t * LANES:(t + 1) * LANES]
            for bi in range(2):
                o_ref[bi, t] = gate_ref[bi, t] * (o_ref[bi, t] + u_ref[bi, t] * bias)


def hyena_long_conv(u6, u_idx, gate6, gate_idx, kf, order, bias, tables, *, slabs=4, tiles=2):
    _, _, cts, n_a, rows, _ = u6.shape
    ns = 2 * rows // SUBLANES
    n_b = ns // slabs
    ga, gi, fb, ib = tables['fwd_a'], tables['inv_a'], tables['fwd_b'], tables['inv_b']
    q_fwd = lambda s: jnp.minimum(s, n_a - 1)
    q_mid = lambda s: jnp.clip(s - n_a, 0, n_b - 1)
    q_inv = lambda s: jnp.clip(s - n_a - n_b, 0, n_a - 1)

    def seq_spec(idx, q_of):
        return pl.BlockSpec((None, 2, tiles, None, rows, LANES), lambda c, s: (idx, 0, c, q_of(s), 0, 0))

    scr = pltpu.VMEM((ns * DFT_PITCH, LANES), F32)
    kern = functools.partial(_hyena_conv_kernel, n_a=n_a, n_b=n_b, slabs=slabs)
    return pl.pallas_call(
        kern,
        grid=(cts // tiles, 2 * n_a + n_b),
        in_specs=[
            seq_spec(u_idx, q_fwd),
            pl.BlockSpec((SUBLANES,) + ga.shape[1:], lambda c, s: (q_fwd(s), 0, 0)),
            pl.BlockSpec((None, 2, tiles, slabs * DFT_FAST, LANES), lambda c, s: (order, 0, c, q_mid(s), 0)),
            pl.BlockSpec(fb.shape, lambda c, s: (0, 0)),
            pl.BlockSpec(ib.shape, lambda c, s: (0, 0)),
            pl.BlockSpec((SUBLANES,) + gi.shape[1:], lambda c, s: (q_inv(s), 0, 0)),
            seq_spec(gate_idx, q_inv),
            seq_spec(u_idx, q_inv),
            pl.BlockSpec((1, tiles * LANES), lambda c, s: (0, c)),
        ],
        out_specs=pl.BlockSpec((2, tiles, None, rows, LANES), lambda c, s: (0, c, q_inv(s), 0, 0)),
        out_shape=jax.ShapeDtypeStruct((2, cts, n_a, rows, LANES), F32),
        scratch_shapes=[scr] * (2 * tiles),
        compiler_params=_cparams("arbitrary", "arbitrary"),
        name="hyena_long_conv",
    )(u6, ga, kf, fb, ib, gi, gate6, u6, bias.reshape(1, -1))


def hyena_filter_spectra(seq_len, tables, w1, b1, w2, b2, w3, freq):
    eo, asum = hyena_filter_mlp(seq_len, w1, b1, w2, b2, w3, freq, tl=512)
    return hyena_filter_dft(eo, asum, tables)


def hyena_mixer(gates, kf, bias, tables):
    z = hyena_long_conv(gates, 2, gates, 1, kf, 0, bias[0], tables)
    return hyena_long_conv(z[None], 0, gates, 0, kf, 1, bias[1], tables)


def _odd_out_kernel(ys_ref, u_ref, hy_ref, x_ref, d_ref, wglu_ref, wo_ref, g_ref, o_ref):
    y = ys_ref[0] + d_ref[...] * u_ref[0]
    c0 = math.sqrt(2.0 / math.pi)
    gl = 0.5 * y * (1.0 + jnp.tanh(c0 * (y + 0.044715 * (y * y * y))))
    z = _dot(gl.astype(BF16), wglu_ref[...])
    s5 = gl * (1.0 / (1.0 + jnp.exp(-z)))
    hy = [_from_time_tiles(hy_ref[ct]).astype(BF16) for ct in range(hy_ref.shape[0])]
    mix = _dot(jnp.concatenate([s5.astype(BF16)] + hy, axis=-1), wo_ref[...])
    o_ref[0] = x_ref[0] + _rms(mix, g_ref[...])


def odd_out(ys, p, hy, x, d_skip, w_glu, w_out, layer, g, *, tm):
    b, l, d = x.shape
    tok = lambda w: pl.BlockSpec((1, tm, w), lambda bi, i: (bi, i, 0))
    hy_spec = pl.BlockSpec((None, D_HYENA // LANES, DFT_TILES, tm // DFT_FAST * SUBLANES, LANES),
                           lambda bi, i: (bi, 0, 0, i, 0))
    return pl.pallas_call(
        _odd_out_kernel,
        grid=(b, l // tm),
        in_specs=[tok(D_S5), tok(D_S5), hy_spec, tok(d),
                  pl.BlockSpec((1, D_S5), lambda bi, i: (0, 0)),
                  _layer_spec((D_S5, D_S5), layer, lambda bi, i: (0, 0)),
                  _layer_spec((d, d), layer, lambda bi, i: (0, 0)),
                  pl.BlockSpec((1, d), lambda bi, i: (0, 0))],
        out_specs=tok(d),
        out_shape=jax.ShapeDtypeStruct(x.shape, F32),
        compiler_params=_cparams("parallel", "parallel"),
        name="odd_out",
    )(ys, p, hy, x, d_skip.reshape(1, D_S5), w_glu, w_out, g.reshape(1, d))


def kernel(x, mem, norm_mix, norm_xattn, norm_mem, norm_mlp, xa_wq, xa_wk, xa_wv, xa_wo, mlp_w1, mlp_w2, ev_w_in, ev_pool_w, ev_pool_scale, ev_conv_w, ev_w_out, od_w_in, od_s5_lambda_re, od_s5_lambda_im, od_s5_log_dt, od_s5_b_re, od_s5_b_im, od_s5_c_re, od_s5_c_im, od_s5_d, od_s5_w_glu, od_hy_short_w, od_hy_short_b, od_hy_w1, od_hy_b1, od_hy_w2, od_hy_b2, od_hy_w3, od_hy_freq, od_hy_bias, od_w_out):
    b, l, d = x.shape
    depth = norm_mix.shape[0]
    assert b == 2, "the long convolution packs the two batch rows as one complex signal"
    tables = dft_tables(l)
    mem2d = mem.reshape(b * N_MEM, d)
    wq, wo = xa_wq.astype(BF16), xa_wo.astype(BF16)
    wkv = jnp.concatenate([xa_wk, xa_wv], axis=2).astype(BF16)
    w1, w2 = mlp_w1.astype(BF16), mlp_w2.astype(BF16)
    ev_in, ev_out, ev_pool = ev_w_in.astype(BF16), ev_w_out.astype(BF16), ev_pool_w.astype(BF16)
    od_in, od_out, od_glu = od_w_in.astype(BF16), od_w_out.astype(BF16), od_s5_w_glu.astype(BF16)
    for i in range(depth):
        j = i // 2
        if i % 2 == 0:
            p = norm_matmul(x.reshape(b * l, d), norm_mix[i, 0], ev_in, j, tm=512)
            x = even_mixer(p.reshape(b, l, -1), x, ev_pool, ev_pool_scale[j], ev_conv_w[j], ev_out, j,
                           norm_mix[i, 1], tm=512)
        else:
            u_s5, gates = odd_in(x, norm_mix[i, 0], od_in, j, od_hy_short_w[j], od_hy_short_b[j], tm=512)
            mats = s5_matrices(od_s5_lambda_re[j], od_s5_lambda_im[j], od_s5_log_dt[j], od_s5_b_re[j],
                               od_s5_b_im[j], od_s5_c_re[j], od_s5_c_im[j])
            ys = s5_core(u_s5, mats)
            kf = hyena_filter_spectra(l, tables, od_hy_w1[j], od_hy_b1[j], od_hy_w2[j], od_hy_b2[j],
                                      od_hy_w3[j], od_hy_freq[j])
            hy = hyena_mixer(gates, kf, od_hy_bias[j], tables)
            x = odd_out(ys, u_s5, hy, x, od_s5_d[j], od_glu, od_out, j, norm_mix[i, 1], tm=512)
        kv = norm_matmul(mem2d, norm_mem[i], wkv, i, tm=b * N_MEM, out_dtype=BF16).reshape(b, N_MEM, 2 * d)
        x = xattn_block(x, kv, wq, wo, i, norm_xattn[i, 0], norm_xattn[i, 1], tm=512)
        x = mlp_block(x.reshape(b * l, d), w1, w2, i, norm_mlp[i, 0], norm_mlp[i, 1],
                      tm=1024, tf=512).reshape(b, l, d)
    return x
```

```python
import functools
import math

import numpy as np
import jax
import jax.numpy as jnp
from jax import lax
from jax.experimental import pallas as pl
from jax.experimental.pallas import tpu as pltpu

F32 = jnp.float32
BF16 = jnp.bfloat16

D_MODEL = 1024
N_MEM = 256
RMS_EPS = 1e-6
D_POOL = 512
POOL_WINDOWS = (2, 4, 8, 16)
POOL_GROUP_DIM = 128
POOL_HALO = 8
D_CONV = 512
D_S5 = 512
S5_GROUP_DIM = 16
S5_GROUPS = 32
S5_STATE = 64
S5_CHUNK = 16
D_HYENA = 512
HYENA_BANDS = 16
HYENA_FFN = 64
HYENA_TARGET = 1e-2
HYENA_SHORT_DECAY_PCT = 0.3
HYENA_LONG_DECAY_PCT = 1.5
XA_HEADS = 4
XA_HEAD_DIM = 256
D_FF = 4096

LANES = 128
SUBLANES = 8
S5_SUPER = LANES // S5_GROUP_DIM
DFT_FAST = 128
DFT_TILES = DFT_FAST // SUBLANES
DFT_PITCH = DFT_FAST + SUBLANES
DFT_UNROLL = 16
VMEM_LIMIT = 56 * 1024 * 1024
VMEM_LIMIT_CONV = 62 * 1024 * 1024


def _cparams(*sem, vmem=VMEM_LIMIT):
    return pltpu.CompilerParams(dimension_semantics=sem, vmem_limit_bytes=vmem)


def _rms(xf, g):
    ms = jnp.mean(xf * xf, axis=-1, keepdims=True)
    return xf * lax.rsqrt(ms + RMS_EPS) * g


def _dot(a, b):
    return jnp.dot(a, b, preferred_element_type=F32)


def _layer_spec(block, layer, tail_map):
    return pl.BlockSpec((None,) + block, lambda *idx: (layer,) + tail_map(*idx))


def _norm_matmul_kernel(x_ref, g_ref, w_ref, o_ref):
    xn = _rms(x_ref[...], g_ref[...]).astype(BF16)
    o_ref[...] = _dot(xn, w_ref[...]).astype(o_ref.dtype)


def norm_matmul(x2d, g, w, layer, *, tm, out_dtype=F32):
    m, d = x2d.shape
    n = w.shape[-1]
    return pl.pallas_call(
        _norm_matmul_kernel,
        grid=(m // tm,),
        in_specs=[
            pl.BlockSpec((tm, d), lambda i: (i, 0)),
            pl.BlockSpec((1, d), lambda i: (0, 0)),
            _layer_spec((d, n), layer, lambda i: (0, 0)),
        ],
        out_specs=pl.BlockSpec((tm, n), lambda i: (i, 0)),
        out_shape=jax.ShapeDtypeStruct((m, n), out_dtype),
        compiler_params=_cparams("parallel"),
        name="norm_matmul",
    )(x2d, g.reshape(1, d), w)


def _xattn_kernel(x_ref, kv_ref, wq_ref, wo_ref, g1_ref, g2_ref, o_ref):
    x = x_ref[0]
    xn = _rms(x, g1_ref[...]).astype(BF16)
    q = (_dot(xn, wq_ref[...]) * (XA_HEAD_DIM ** -0.5)).astype(BF16)
    heads = []
    for h in range(XA_HEADS):
        lo = h * XA_HEAD_DIM
        qh = q[:, lo:lo + XA_HEAD_DIM]
        kh = kv_ref[0, :, lo:lo + XA_HEAD_DIM]
        vh = kv_ref[0, :, D_MODEL + lo:D_MODEL + lo + XA_HEAD_DIM]
        s = lax.dot_general(qh, kh, (((1,), (1,)), ((), ())), preferred_element_type=F32)
        e = jnp.exp(s - jnp.max(s, axis=-1, keepdims=True))
        p = e / jnp.sum(e, axis=-1, keepdims=True)
        heads.append(_dot(p.astype(BF16), vh).astype(BF16))
    o = jnp.concatenate(heads, axis=-1)
    y = _dot(o, wo_ref[...])
    o_ref[0] = x + _rms(y, g2_ref[...])


def xattn_block(x, kv, wq, wo, layer, g1, g2, *, tm):
    b, l, d = x.shape
    return pl.pallas_call(
        _xattn_kernel,
        grid=(b, l // tm),
        in_specs=[
            pl.BlockSpec((1, tm, d), lambda bi, i: (bi, i, 0)),
            pl.BlockSpec((1, N_MEM, 2 * d), lambda bi, i: (bi, 0, 0)),
            _layer_spec((d, d), layer, lambda bi, i: (0, 0)),
            _layer_spec((d, d), layer, lambda bi, i: (0, 0)),
            pl.BlockSpec((1, d), lambda bi, i: (0, 0)),
            pl.BlockSpec((1, d), lambda bi, i: (0, 0)),
        ],
        out_specs=pl.BlockSpec((1, tm, d), lambda bi, i: (bi, i, 0)),
        out_shape=jax.ShapeDtypeStruct(x.shape, F32),
        compiler_params=_cparams("parallel", "parallel"),
        name="xattn_block",
    )(x, kv, wq, wo, g1.reshape(1, d), g2.reshape(1, d))


def _mlp_kernel(x_ref, w1_ref, w2_ref, g1_ref, g2_ref, o_ref, h_ref, *, tf):
    x = x_ref[...]
    xn = _rms(x, g1_ref[...]).astype(BF16)
    for c in range(h_ref.shape[1] // tf):
        cols = slice(c * tf, (c + 1) * tf)
        h = jnp.maximum(_dot(xn, w1_ref[:, cols]), 0.0)
        h_ref[:, cols] = (h * h).astype(BF16)
    o_ref[...] = x + _rms(_dot(h_ref[...], w2_ref[...]), g2_ref[...])


def mlp_block(x2d, w1, w2, layer, g1, g2, *, tm, tf):
    m, d = x2d.shape
    ff = w1.shape[-1]
    resident = dict(pipeline_mode=pl.Buffered(1))
    return pl.pallas_call(
        functools.partial(_mlp_kernel, tf=tf),
        grid=(m // tm,),
        in_specs=[
            pl.BlockSpec((tm, d), lambda i: (i, 0)),
            pl.BlockSpec((None, d, ff), lambda i: (layer, 0, 0), **resident),
            pl.BlockSpec((None, ff, d), lambda i: (layer, 0, 0), **resident),
            pl.BlockSpec((1, d), lambda i: (0, 0)),
            pl.BlockSpec((1, d), lambda i: (0, 0)),
        ],
        out_specs=pl.BlockSpec((tm, d), lambda i: (i, 0)),
        out_shape=jax.ShapeDtypeStruct((m, d), F32),
        scratch_shapes=[pltpu.VMEM((tm, ff), BF16)],
        compiler_params=_cparams("parallel"),
        name="mlp_block",
    )(x2d, w1, w2, g1.reshape(1, d), g2.reshape(1, d))


def _halo_specs(tm, seq_len, width, col):
    r = tm // POOL_HALO
    last = seq_len // POOL_HALO - 1

    def prev_map(bi, i, *_):
        return (bi, jnp.maximum(i * r - 1, 0), col(*_))

    def main_map(bi, i, *_):
        return (bi, i, col(*_))

    def next_map(bi, i, *_):
        return (bi, jnp.minimum((i + 1) * r, last), col(*_))

    return [
        pl.BlockSpec((1, POOL_HALO, width), prev_map),
        pl.BlockSpec((1, tm, width), main_map),
        pl.BlockSpec((1, POOL_HALO, width), next_map),
    ]


def _with_halo(prev_ref, main_ref, next_ref):
    i = pl.program_id(1)
    prev = jnp.where(i > 0, prev_ref[0], 0.0)
    nxt = jnp.where(i < pl.num_programs(1) - 1, next_ref[0], 0.0)
    return jnp.concatenate([prev, main_ref[0], nxt], axis=0)


def _shift_rows(v, k):
    return pltpu.roll(v, k % v.shape[0], 0)


def _even_mixer_kernel(xp_ref, x_ref, xn_ref, gi_ref, wi_ref, wg_ref, ps_ref, cw_ref, wo_ref, g_ref, o_ref,
                       *, tm, seq_len):
    x_ext = _with_halo(xp_ref, x_ref, xn_ref)
    ext = _dot(_rms(x_ext, gi_ref[...]).astype(BF16), wi_ref[...])
    lo, hi = POOL_HALO, POOL_HALO + tm
    t = pl.program_id(1) * tm + lax.broadcasted_iota(jnp.int32, (tm, 1), 0)
    parts = []
    for gi, win in enumerate(POOL_WINDOWS):
        half = win // 2
        u = ext[:, gi * POOL_GROUP_DIM:(gi + 1) * POOL_GROUP_DIM]
        s = u + _shift_rows(u, 1)
        step = 1
        while 2 * step < win:
            s = _shift_rows(s, step) + _shift_rows(s, -step)
            step *= 2
        cnt = (jnp.minimum(t + half, seq_len) - jnp.maximum(t - half, 0)).astype(F32)
        pooled = s[lo:hi] / cnt - u[lo:hi]
        y = _dot(pooled.astype(BF16), wg_ref[gi])
        parts.append((y * ps_ref[:, gi * POOL_GROUP_DIM:(gi + 1) * POOL_GROUP_DIM]).astype(BF16))
    b_gate = ext[lo:hi, D_POOL:D_POOL + D_CONV]
    ch = ext[:, D_POOL + D_CONV:D_POOL + 2 * D_CONV] * ext[:, D_POOL + 2 * D_CONV:D_POOL + 3 * D_CONV]
    conv = cw_ref[0:1, :] * _shift_rows(ch, 1) + cw_ref[1:2, :] * ch + cw_ref[2:3, :] * _shift_rows(ch, -1)
    parts.append((b_gate * conv[lo:hi]).astype(BF16))
    mix = _dot(jnp.concatenate(parts, axis=-1), wo_ref[...])
    o_ref[0] = x_ref[0] + _rms(mix, g_ref[...])


def even_mixer(x, g_in, w_in, w_group, pool_scale, conv_w, w_out, layer, g, *, tm):
    b, l, d = x.shape
    kern = functools.partial(_even_mixer_kernel, tm=tm, seq_len=l)
    return pl.pallas_call(
        kern,
        grid=(b, l // tm),
        in_specs=_halo_specs(tm, l, d, lambda: 0) + [
            pl.BlockSpec((1, d), lambda bi, i: (0, 0)),
            _layer_spec(w_in.shape[1:], layer, lambda bi, i: (0, 0)),
            _layer_spec(w_group.shape[1:], layer, lambda bi, i: (0, 0, 0)),
            pl.BlockSpec((1, D_POOL), lambda bi, i: (0, 0)),
            pl.BlockSpec((3, D_CONV), lambda bi, i: (0, 0)),
            _layer_spec((d, d), layer, lambda bi, i: (0, 0)),
            pl.BlockSpec((1, d), lambda bi, i: (0, 0)),
        ],
        out_specs=pl.BlockSpec((1, tm, d), lambda bi, i: (bi, i, 0)),
        out_shape=jax.ShapeDtypeStruct(x.shape, F32),
        compiler_params=_cparams("parallel", "parallel"),
        name="even_mixer",
    )(x, x, x, g_in.reshape(1, d), w_in, w_group, pool_scale.reshape(1, D_POOL), conv_w, w_out, g.reshape(1, d))


def _cmul(ar, ai, br, bi):
    return ar * br - ai * bi, ar * bi + ai * br


def s5_matrices(lam_re, lam_im, log_dt, b_re, b_im, c_re, c_im):
    tc, hd, n, sup = S5_CHUNK, S5_GROUP_DIM, S5_STATE, S5_SUPER
    n_tiles = S5_GROUPS // sup
    hp = lax.Precision.HIGHEST
    lr = jnp.minimum(lam_re, -1e-4)
    li = lam_im
    dt = jnp.exp(log_dt)[..., None]
    taus = jnp.arange(tc + 1, dtype=F32)[:, None, None, None]
    rmag = jnp.exp(lr[None] * dt[None] * (tc - taus))
    rang = li[None] * dt[None] * (tc - taus)
    pwr_re, pwr_im = rmag * jnp.cos(rang), rmag * jnp.sin(rang)
    mag = jnp.exp(lr[None] * dt[None] * taus)
    ang = li[None] * dt[None] * taus
    pw_re, pw_im = mag * jnp.cos(ang), mag * jnp.sin(ang)
    nr, ni = pw_re[1] - 1.0, pw_im[1]
    den = lr * lr + li * li
    coef_re, coef_im = (nr * lr + ni * li) / den, (ni * lr - nr * li) / den
    cb_re, cb_im = _cmul(coef_re[..., None], coef_im[..., None], b_re[None], b_im[None])
    cbt_re, cbt_im = _cmul(coef_re[:, :, None, :], coef_im[:, :, None, :],
                           jnp.swapaxes(b_re, 1, 2)[None], jnp.swapaxes(b_im, 1, 2)[None])
    q_re, q_im = _cmul(pw_re[:tc, :, :, :, None], pw_im[:tc, :, :, :, None], cb_re[None], cb_im[None])
    m = (jnp.einsum('dgkn,tdgnj->dtgjk', c_re, q_re, precision=hp)
         - jnp.einsum('dgkn,tdgnj->dtgjk', c_im, q_im, precision=hp))
    m_fwd, m_bwd = m[0], m[1]
    diag = m_fwd[0] + m_bwd[0]
    lag_table = jnp.concatenate([m_bwd[:0:-1], diag[None], m_fwd[1:]], axis=0)
    intra = jnp.stack([lag_table[tc - 1 - s:2 * tc - 1 - s] for s in range(tc)], axis=0)
    intra = jnp.transpose(intra, (2, 0, 3, 1, 4)).reshape(S5_GROUPS, tc * hd, tc * hd)
    sf_re, sf_im = _cmul(pwr_re[1:tc + 1, 0][:, :, None, :], pwr_im[1:tc + 1, 0][:, :, None, :],
                         cbt_re[0][None], cbt_im[0][None])
    sb_re, sb_im = _cmul(pw_re[:tc, 1][:, :, None, :], pw_im[:tc, 1][:, :, None, :],
                         cbt_re[1][None], cbt_im[1][None])
    summ = jnp.concatenate([sf_re, sb_re, sf_im, sb_im], axis=-1)
    summ = jnp.swapaxes(summ, 0, 1).reshape(S5_GROUPS, tc * hd, 4 * n)
    ct_re, ct_im = jnp.swapaxes(c_re, 2, 3), jnp.swapaxes(c_im, 2, 3)
    gnt = lambda z: jnp.transpose(z, (1, 2, 0))[..., None]
    qf_re, qf_im = _cmul(ct_re[0][:, :, None, :], ct_im[0][:, :, None, :],
                         gnt(pw_re[1:tc + 1, 0]), gnt(pw_im[1:tc + 1, 0]))
    qb_re, qb_im = _cmul(ct_re[1][:, :, None, :], ct_im[1][:, :, None, :], gnt(pwr_re[:tc, 1]), gnt(pwr_im[:tc, 1]))
    carry = jnp.concatenate([qf_re, qb_re, -qf_im, -qb_im], axis=1)
    carry = carry.reshape(S5_GROUPS, 4 * n, tc * hd)
    a_re = jnp.concatenate([pw_re[tc, 0], pw_re[tc, 1]], axis=-1)
    a_im = jnp.concatenate([pw_im[tc, 0], pw_im[tc, 1]], axis=-1)
    return intra, summ, carry, a_re, a_im


def s5_regroup_matrix():
    n = S5_SUPER * LANES
    src = np.arange(n)
    tb, g, h = src // LANES, (src % LANES) // S5_GROUP_DIM, src % S5_GROUP_DIM
    perm = np.zeros((n, n), np.float32)
    perm[src, g * LANES + tb * S5_GROUP_DIM + h] = 1.0
    return jnp.asarray(perm, BF16)


def _s5_regroup_kernel(p_ref, w_ref, perm_ref, u_ref, sr_ref, si_ref, *, n_chunks):
    n2 = 2 * S5_STATE
    for q in range(S5_CHUNK // S5_SUPER):
        rows_t = [p_ref[pl.ds(q * S5_SUPER + tb, n_chunks, stride=S5_CHUNK), :].astype(BF16) for tb in range(S5_SUPER)]
        grouped = _dot(jnp.concatenate(rows_t, axis=-1), perm_ref[...]).astype(BF16)
        for g in range(S5_SUPER):
            u_ref[g, :, q * LANES:(q + 1) * LANES] = grouped[:, g * LANES:(g + 1) * LANES]
    for g in range(S5_SUPER):
        s = _dot(u_ref[g], w_ref[g])
        sr_ref[pl.ds(g, n_chunks, stride=S5_SUPER), :] = s[:, :n2]
        si_ref[pl.ds(g, n_chunks, stride=S5_SUPER), :] = s[:, n2:]


def _s5_state_spec(c, n2, n_tiles):
    return pl.BlockSpec((None, c * S5_SUPER, n2), lambda a, bi: (bi * n_tiles + a, 0, 0))


def _s5_group_spec(rows, cols):
    return pl.BlockSpec((S5_SUPER, rows, cols), lambda a, bi: (a, 0, 0))


def s5_summary(p, summ):
    b, l, _ = p.shape
    c = l // S5_CHUNK
    n_tiles = S5_GROUPS // S5_SUPER
    k = S5_CHUNK * S5_GROUP_DIM
    n2 = 2 * S5_STATE
    shape = jax.ShapeDtypeStruct((b * n_tiles, c * S5_SUPER, n2), F32)
    kern = functools.partial(_s5_regroup_kernel, n_chunks=c)
    perm = s5_regroup_matrix()
    return pl.pallas_call(
        kern,
        grid=(n_tiles, b),
        in_specs=[
            pl.BlockSpec((None, l, LANES), lambda a, bi: (bi, 0, a)),
            _s5_group_spec(k, 2 * n2),
            pl.BlockSpec(perm.shape, lambda a, bi: (0, 0)),
        ],
        out_specs=[pl.BlockSpec((None, S5_SUPER, c, k), lambda a, bi: (bi * n_tiles + a, 0, 0, 0)),
                   _s5_state_spec(c, n2, n_tiles), _s5_state_spec(c, n2, n_tiles)],
        out_shape=[jax.ShapeDtypeStruct((b * n_tiles, S5_SUPER, c, k), BF16), shape, shape],
        compiler_params=_cparams("parallel", "parallel"),
        name="s5_summary",
    )(p, summ, perm)


def _s5_scan_kernel(sr_ref, si_ref, are_ref, aim_ref, or_ref, oi_ref, *, n_chunks):
    n2 = 2 * S5_STATE
    a_re = are_ref[...]
    a_im = aim_ref[...]
    fwd_lane = lax.broadcasted_iota(jnp.int32, (SUBLANES, n2), 1) < S5_STATE

    def body(i, state):
        x_re, x_im = state
        rows_i = pl.ds(pl.multiple_of(i * SUBLANES, SUBLANES), SUBLANES)
        rows_r = pl.ds(pl.multiple_of((n_chunks - 1 - i) * SUBLANES, SUBLANES), SUBLANES)
        or_ref[rows_i, 0:S5_STATE] = x_re[:, :S5_STATE]
        oi_ref[rows_i, 0:S5_STATE] = x_im[:, :S5_STATE]
        or_ref[rows_r, S5_STATE:n2] = x_re[:, S5_STATE:]
        oi_ref[rows_r, S5_STATE:n2] = x_im[:, S5_STATE:]
        s_re = jnp.where(fwd_lane, sr_ref[rows_i, :], sr_ref[rows_r, :])
        s_im = jnp.where(fwd_lane, si_ref[rows_i, :], si_ref[rows_r, :])
        return (a_re * x_re - a_im * x_im + s_re, a_re * x_im + a_im * x_re + s_im)

    zero = jnp.zeros((SUBLANES, n2), F32)
    lax.fori_loop(0, n_chunks, body, (zero, zero))


def s5_scan(s_re, s_im, a_re, a_im):
    tiles, rows, n2 = s_re.shape
    kern = functools.partial(_s5_scan_kernel, n_chunks=rows // S5_SUPER)
    st = pl.BlockSpec((None, rows, n2), lambda i: (i, 0, 0))
    av = pl.BlockSpec((SUBLANES, n2), lambda i: (i, 0))
    shape = jax.ShapeDtypeStruct(s_re.shape, F32)
    return pl.pallas_call(
        kern,
        grid=(tiles,),
        in_specs=[st, st, av, av],
        out_specs=[st, st],
        out_shape=[shape, shape],
        compiler_params=_cparams("parallel"),
        name="s5_scan",
    )(s_re, s_im, a_re, a_im)


def _s5_output_kernel(u_ref, xr_ref, xi_ref, wi_ref, wc_ref, perm_ref, o_ref, y_ref, *, n_chunks):
    for g in range(S5_SUPER):
        rows = pl.ds(g, n_chunks, stride=S5_SUPER)
        xin = jnp.concatenate([xr_ref[rows, :], xi_ref[rows, :]], axis=-1).astype(BF16)
        y_ref[g] = _dot(u_ref[g], wi_ref[g]) + _dot(xin, wc_ref[g])
    back = (((1,), (1,)), ((), ()))
    for q in range(S5_CHUNK // S5_SUPER):
        y = jnp.concatenate([y_ref[g, :, q * LANES:(q + 1) * LANES] for g in range(S5_SUPER)], axis=-1)
        hi = y.astype(BF16)
        lo = (y - hi.astype(F32)).astype(BF16)
        tok = (lax.dot_general(hi, perm_ref[...], back, preferred_element_type=F32)
               + lax.dot_general(lo, perm_ref[...], back, preferred_element_type=F32))
        for tb in range(S5_SUPER):
            o_ref[pl.ds(q * S5_SUPER + tb, n_chunks, stride=S5_CHUNK), :] = tok[:, tb * LANES:(tb + 1) * LANES]


def s5_output(u, xin_re, xin_im, intra, carry, *, batch):
    tiles, _, c, k = u.shape
    n_tiles = tiles // batch
    n2 = 2 * S5_STATE
    l = c * S5_CHUNK
    kern = functools.partial(_s5_output_kernel, n_chunks=c)
    perm = s5_regroup_matrix()
    return pl.pallas_call(
        kern,
        grid=(n_tiles, batch),
        in_specs=[
            pl.BlockSpec((None, S5_SUPER, c, k), lambda a, bi: (bi * n_tiles + a, 0, 0, 0)),
            _s5_state_spec(c, n2, n_tiles),
            _s5_state_spec(c, n2, n_tiles),
            _s5_group_spec(k, k),
            _s5_group_spec(2 * n2, k),
            pl.BlockSpec(perm.shape, lambda a, bi: (0, 0)),
        ],
        out_specs=pl.BlockSpec((None, l, LANES), lambda a, bi: (bi, 0, a)),
        out_shape=jax.ShapeDtypeStruct((batch, l, D_S5), F32),
        scratch_shapes=[pltpu.VMEM((S5_SUPER, c, k), F32)],
        compiler_params=_cparams("parallel", "parallel"),
        name="s5_output",
    )(u, xin_re, xin_im, intra, carry, perm)


def s5_core(p, mats):
    b = p.shape[0]
    intra, summ, carry, a_re, a_im = mats
    u, s_re, s_im = s5_summary(p, summ.astype(BF16))
    xin_re, xin_im = s5_scan(s_re, s_im, jnp.tile(a_re, (b, 1)), jnp.tile(a_im, (b, 1)))
    return s5_output(u, xin_re, xin_im, intra.astype(BF16), carry.astype(BF16), batch=b)


def _to_time_tiles(y):
    nf, c = DFT_FAST, y.shape[-1]
    slabs = [y[s * nf:(s + 1) * nf].reshape(DFT_TILES, SUBLANES, c) for s in range(y.shape[0] // nf)]
    return jnp.concatenate(slabs, axis=1)


def _from_time_tiles(v):
    c = v.shape[-1]
    slabs = [v[:, s * SUBLANES:(s + 1) * SUBLANES, :].reshape(DFT_FAST, c) for s in range(v.shape[1] // SUBLANES)]
    return jnp.concatenate(slabs, axis=0)


def _odd_in_kernel(xp_ref, x_ref, xn_ref, g_ref, w_ref, cw_ref, cb_ref, s5_ref, gates_ref, *, tm):
    ext = _with_halo(xp_ref, x_ref, xn_ref)
    p = _dot(_rms(ext, g_ref[...]).astype(BF16), w_ref[...])
    lo, hi = POOL_HALO, POOL_HALO + tm
    s5_ref[0] = p[lo:hi, :D_S5]
    ph = p[:, D_S5:]
    y = cw_ref[0:1, :] * _shift_rows(ph, 1) + cw_ref[1:2, :] * ph + cw_ref[2:3, :] * _shift_rows(ph, -1)
    y = y[lo:hi] + cb_ref[...]
    for k in range(3):
        for ct in range(D_HYENA // LANES):
            col = k * D_HYENA + ct * LANES
            gates_ref[k, ct] = _to_time_tiles(y[:, col:col + LANES])


def odd_in(x, g, w, layer, conv_w, conv_b, *, tm):
    b, l, d = x.shape
    n = w.shape[-1]
    rows = tm // DFT_FAST * SUBLANES
    cts = D_HYENA // LANES
    kern = functools.partial(_odd_in_kernel, tm=tm)
    return pl.pallas_call(
        kern,
        grid=(b, l // tm),
        in_specs=_halo_specs(tm, l, d, lambda: 0) + [
            pl.BlockSpec((1, d), lambda bi, i: (0, 0)),
            _layer_spec((d, n), layer, lambda bi, i: (0, 0)),
            pl.BlockSpec((3, n - D_S5), lambda bi, i: (0, 0)),
            pl.BlockSpec((1, n - D_S5), lambda bi, i: (0, 0)),
        ],
        out_specs=[pl.BlockSpec((1, tm, D_S5), lambda bi, i: (bi, i, 0)),
                   pl.BlockSpec((3, None, cts, DFT_TILES, rows, LANES), lambda bi, i: (0, bi, 0, 0, i, 0))],
        out_shape=[jax.ShapeDtypeStruct((b, l, D_S5), F32),
                   jax.ShapeDtypeStruct((3, b, cts, DFT_TILES, l // DFT_FAST * SUBLANES, LANES), F32)],
        compiler_params=_cparams("parallel", "parallel"),
        name="odd_in",
    )(x, x, x, g.reshape(1, d), w, conv_w, conv_b.reshape(1, -1))


def _filter_mlp_kernel(bands_ref, w1t_ref, w1cs_ref, b1_ref, w2_ref, b2_ref, w3_ref, fr_ref, dl_ref,
                       eo_ref, asum_ref, *, tl, seq_len):
    i = pl.program_id(0)
    hp = lax.Precision.HIGHEST
    c = D_HYENA
    t = (i * tl + lax.broadcasted_iota(jnp.int32, (tl, 1), 0)).astype(F32)
    t_norm = t / (seq_len - 1.0)
    ang = (2.0 * math.pi / seq_len) * t * bands_ref[...]
    is_cos = lax.broadcasted_iota(jnp.int32, ang.shape, 1) < HYENA_BANDS
    cs = jnp.where(is_cos, jnp.cos(ang), -jnp.sin(ang))
    fr = fr_ref[...]
    z = t_norm * w1t_ref[...] + jnp.dot(cs, w1cs_ref[...], precision=hp, preferred_element_type=F32)
    h = jnp.sin(fr * (z + b1_ref[...]))
    h = jnp.sin(fr * (jnp.dot(h, w2_ref[...], precision=hp, preferred_element_type=F32) + b2_ref[...]))
    h = jnp.dot(h, w3_ref[...], precision=hp, preferred_element_type=F32)
    decay = jnp.exp(-t_norm * dl_ref[...])

    @pl.when(i == 0)
    def _():
        asum_ref[...] = jnp.zeros_like(asum_ref)

    for o in range(h.shape[1] // (2 * c)):
        fwd = h[:, 2 * o * c:(2 * o + 1) * c] * decay
        bwd = h[:, (2 * o + 1) * c:(2 * o + 2) * c] * decay
        eo_ref[:, 2 * o * c:(2 * o + 1) * c] = fwd + bwd
        eo_ref[:, (2 * o + 1) * c:(2 * o + 2) * c] = fwd - bwd
        asum_ref[:, o * c:(o + 1) * c] += jnp.sum(jnp.abs(fwd) + jnp.abs(bwd), axis=0, keepdims=True)


def hyena_filter_mlp(seq_len, w1, b1, w2, b2, w3, freq, *, tl):
    n_out = w3.shape[1]
    bands = jnp.linspace(1e-4, HYENA_BANDS - 1, HYENA_BANDS, dtype=F32)
    bands2 = jnp.concatenate([bands, bands])[None, :]
    deltas = jnp.abs(jnp.linspace(math.log(HYENA_TARGET) / HYENA_LONG_DECAY_PCT,
                                  math.log(HYENA_TARGET) / HYENA_SHORT_DECAY_PCT, D_HYENA, dtype=F32))[None, :]
    kern = functools.partial(_filter_mlp_kernel, tl=tl, seq_len=seq_len)
    full = lambda a: pl.BlockSpec(a.shape, lambda i: (0,) * a.ndim)
    args = [bands2, w1[0:1], w1[1:], b1[None, :], w2, b2[None, :], w3, freq[None, :], deltas]
    return pl.pallas_call(
        kern,
        grid=(seq_len // tl,),
        in_specs=[full(a) for a in args],
        out_specs=[pl.BlockSpec((tl, n_out), lambda i: (i, 0)), pl.BlockSpec((1, n_out // 2), lambda i: (0, 0))],
        out_shape=[jax.ShapeDtypeStruct((seq_len, n_out), F32), jax.ShapeDtypeStruct((1, n_out // 2), F32)],
        compiler_params=_cparams("arbitrary"),
        name="hyena_filter_mlp",
    )(*args)


def dft_tables(seq_len):
    n = 2 * seq_len
    nf = DFT_FAST
    ns = n // nf
    k_a = np.arange(ns)[:, None]
    n_s = np.arange(ns // 2)[None, :]
    w_s = np.exp(-2j * np.pi * ((k_a * n_s) % ns) / ns)
    n_f = np.arange(nf)[:, None]
    tw = np.exp(-2j * np.pi * ((n_f * np.arange(ns)[None, :]) % n) / n)
    c32 = lambda z: (jnp.asarray(z.real, F32), jnp.asarray(z.imag, F32))
    ws_re, ws_im = c32(w_s)
    tw_re, tw_im = c32(tw)
    g_re = tw_re[:, :, None] * ws_re[None] - tw_im[:, :, None] * ws_im[None]
    g_im = tw_re[:, :, None] * ws_im[None] + tw_im[:, :, None] * ws_re[None]
    fwd_a = jnp.concatenate([jnp.concatenate([g_re, -g_im], axis=2),
                             jnp.concatenate([g_im, g_re], axis=2)], axis=1)
    gt_re = jnp.swapaxes(g_re, 1, 2) * (1.0 / n)
    gt_im = jnp.swapaxes(g_im, 1, 2) * (1.0 / n)
    inv_a = jnp.concatenate([jnp.concatenate([gt_re, gt_im], axis=2),
                             jnp.concatenate([-gt_im, gt_re], axis=2)], axis=1)
    kk = np.arange(nf)
    w_f = np.exp(-2j * np.pi * ((kk[:, None] * kk[None, :]) % nf) / nf)
    f_re, f_im = c32(w_f)
    fwd_b = jnp.concatenate([jnp.concatenate([f_re, -f_im], axis=1),
                             jnp.concatenate([f_im, f_re], axis=1)], axis=0)
    inv_b = jnp.concatenate([jnp.concatenate([f_re, f_im], axis=1),
                             jnp.concatenate([-f_im, f_re], axis=1)], axis=0)
    return dict(fwd_a=fwd_a.astype(BF16), fwd_a_real=fwd_a[:, :, :ns // 2].astype(BF16),
                inv_a=inv_a.astype(BF16), fwd_b=fwd_b.astype(BF16), inv_b=inv_b.astype(BF16),
                fwd_b_part=fwd_b.reshape(2, nf, 2 * nf).astype(BF16))


def _filter_dft_kernel(x_ref, ga_ref, fb_ref, nrm_ref, k_ref, ar_scr, ai_scr):
    nf = DFT_FAST
    ns = ga_ref.shape[1] // 2

    def stage_a(j, carry):
        x = x_ref[pl.ds(j, ns // 2, stride=nf), :].astype(BF16)
        a = _dot(ga_ref[j], x)
        ar_scr[pl.ds(j, ns, stride=DFT_PITCH), :] = a[:ns]
        ai_scr[pl.ds(j, ns, stride=DFT_PITCH), :] = a[ns:]
        return carry

    lax.fori_loop(0, nf, stage_a, 0, unroll=DFT_UNROLL)
    inv_norm = 1.0 / (nrm_ref[...] + 1e-6)

    def stage_b(k, carry):
        src = pl.multiple_of(k * DFT_PITCH, SUBLANES)
        a = jnp.concatenate([ar_scr[pl.ds(src, nf), :], ai_scr[pl.ds(src, nf), :]], axis=0).astype(BF16)
        dst = pl.multiple_of(k * nf, nf)
        k_ref[pl.ds(dst, nf), :] = _dot(fb_ref[...], a) * inv_norm
        return carry

    lax.fori_loop(0, ns, stage_b, 0, unroll=DFT_UNROLL)


def hyena_filter_dft(eo, asum, tables):
    l, cols = eo.shape
    c = D_HYENA
    orders = cols // (2 * c)
    ct = c // LANES
    ga, fb = tables['fwd_a_real'], tables['fwd_b_part']
    ns = ga.shape[1] // 2
    scr = pltpu.VMEM((ns * DFT_PITCH, LANES), F32)
    return pl.pallas_call(
        _filter_dft_kernel,
        grid=(orders, ct, 2),
        in_specs=[
            pl.BlockSpec((l, LANES), lambda o, j, part: (0, (2 * o + part) * ct + j)),
            pl.BlockSpec(ga.shape, lambda o, j, part: (0, 0, 0)),
            pl.BlockSpec((None,) + fb.shape[1:], lambda o, j, part: (part, 0, 0)),
            pl.BlockSpec((1, LANES), lambda o, j, part: (0, o * ct + j)),
        ],
        out_specs=pl.BlockSpec((None, None, None, 2 * l, LANES), lambda o, j, part: (o, part, j, 0, 0)),
        out_shape=jax.ShapeDtypeStruct((orders, 2, ct, 2 * l, LANES), F32),
        scratch_shapes=[scr, scr],
        compiler_params=_cparams("parallel", "parallel", "parallel"),
        name="hyena_filter_dft",
    )(eo, ga, fb, asum)


def _lane_cat(parts):
    return parts[0] if len(parts) == 1 else jnp.concatenate(parts, axis=-1)


def _hyena_conv_kernel(x_ref, ga_ref, k_ref, fb_ref, ib_ref, gi_ref, gate_ref, u_ref, bias_ref, o_ref, *scr,
                       n_a, n_b, slabs):
    nf = DFT_FAST
    tiles, qt = x_ref.shape[1], x_ref.shape[2]
    half = x_ref.shape[3] // SUBLANES
    ns = 2 * half
    step = pl.program_id(1)

    @pl.when(step < n_a)
    def _():
        for qj in range(qt * SUBLANES):
            q, j = divmod(qj, SUBLANES)
            rows = pl.ds(j, half, stride=SUBLANES)
            x = _lane_cat([jnp.concatenate([x_ref[0, t, q, rows, :], x_ref[1, t, q, rows, :]], axis=0)
                           for t in range(tiles)])
            a = _dot(ga_ref[qj], x.astype(BF16))
            dst = pl.ds(step * (qt * SUBLANES) + qj, ns, stride=DFT_PITCH)
            for t in range(tiles):
                scr[2 * t][dst, :] = a[:ns, t * LANES:(t + 1) * LANES]
                scr[2 * t + 1][dst, :] = a[ns:, t * LANES:(t + 1) * LANES]

    @pl.when((step >= n_a) & (step < n_a + n_b))
    def _():
        for s in range(slabs):
            k_a = (step - n_a) * slabs + s
            rows = pl.ds(pl.multiple_of(k_a * DFT_PITCH, SUBLANES), nf)
            a = jnp.concatenate([_lane_cat([scr[2 * t + part][rows, :] for t in range(tiles)]) for part in range(2)], axis=0)
            x = _dot(fb_ref[...], a.astype(BF16))
            xr, xi = x[:nf], x[nf:]
            kr = _lane_cat([k_ref[0, t, s * nf:(s + 1) * nf, :] for t in range(tiles)])
            ki = _lane_cat([k_ref[1, t, s * nf:(s + 1) * nf, :] for t in range(tiles)])
            y = jnp.concatenate([xr * kr - xi * ki, xr * ki + xi * kr], axis=0).astype(BF16)
            cc = _dot(ib_ref[...], y)
            for t in range(tiles):
                scr[2 * t][rows, :] = cc[:nf, t * LANES:(t + 1) * LANES]
                scr[2 * t + 1][rows, :] = cc[nf:, t * LANES:(t + 1) * LANES]

    @pl.when(step >= n_a + n_b)
    def _():
        first = (step - (n_a + n_b)) * (qt * SUBLANES)
        for qj in range(qt * SUBLANES):
            q, j = divmod(qj, SUBLANES)
            src = pl.ds(first + qj, ns, stride=DFT_PITCH)
            cc = jnp.concatenate([_lane_cat([scr[2 * t + part][src, :] for t in range(tiles)]) for part in range(2)], axis=0)
            y = _dot(gi_ref[qj], cc.astype(BF16))
            dst = pl.ds(j, half, stride=SUBLANES)
            for t in range(tiles):
                o_ref[0, t, q, dst, :] = y[:half, t * LANES:(t + 1) * LANES]
                o_ref[1, t, q, dst, :] = y[half:, t * LANES:(t + 1) * LANES]
        for t in range(tiles):
            bias = bias_ref[:, t * LANES:(t + 1) * LANES]
            for bi in range(2):
                o_ref[bi, t] = gate_ref[bi, t] * (o_ref[bi, t] + u_ref[bi, t] * bias)


def hyena_long_conv(u6, u_idx, gate6, gate_idx, kf, order, bias, tables, *, slabs=4, tiles=2, qt=2):
    _, _, cts, q_tiles, rows, _ = u6.shape
    ns = 2 * rows // SUBLANES
    n_a = q_tiles // qt
    n_b = ns // slabs
    ga, gi, fb, ib = tables['fwd_a'], tables['inv_a'], tables['fwd_b'], tables['inv_b']
    q_fwd = lambda s: jnp.minimum(s, n_a - 1)
    q_mid = lambda s: jnp.clip(s - n_a, 0, n_b - 1)
    q_inv = lambda s: jnp.clip(s - n_a - n_b, 0, n_a - 1)

    def seq_spec(idx, q_of):
        return pl.BlockSpec((None, 2, tiles, qt, rows, LANES), lambda c, s: (idx, 0, c, q_of(s), 0, 0))

    scr = pltpu.VMEM((ns * DFT_PITCH, LANES), F32)
    kern = functools.partial(_hyena_conv_kernel, n_a=n_a, n_b=n_b, slabs=slabs)
    return pl.pallas_call(
        kern,
        grid=(cts // tiles, 2 * n_a + n_b),
        in_specs=[
            seq_spec(u_idx, q_fwd),
            pl.BlockSpec((qt * SUBLANES,) + ga.shape[1:], lambda c, s: (q_fwd(s), 0, 0)),
            pl.BlockSpec((None, 2, tiles, slabs * DFT_FAST, LANES), lambda c, s: (order, 0, c, q_mid(s), 0)),
            pl.BlockSpec(fb.shape, lambda c, s: (0, 0)),
            pl.BlockSpec(ib.shape, lambda c, s: (0, 0)),
            pl.BlockSpec((qt * SUBLANES,) + gi.shape[1:], lambda c, s: (q_inv(s), 0, 0)),
            seq_spec(gate_idx, q_inv),
            seq_spec(u_idx, q_inv),
            pl.BlockSpec((1, tiles * LANES), lambda c, s: (0, c)),
        ],
        out_specs=pl.BlockSpec((2, tiles, qt, rows, LANES), lambda c, s: (0, c, q_inv(s), 0, 0)),
        out_shape=jax.ShapeDtypeStruct((2, cts, q_tiles, rows, LANES), F32),
        scratch_shapes=[scr] * (2 * tiles),
        compiler_params=_cparams("arbitrary", "arbitrary", vmem=VMEM_LIMIT_CONV),
        name="hyena_long_conv",
    )(u6, ga, kf, fb, ib, gi, gate6, u6, bias.reshape(1, -1))


def hyena_filter_spectra(seq_len, tables, w1, b1, w2, b2, w3, freq):
    eo, asum = hyena_filter_mlp(seq_len, w1, b1, w2, b2, w3, freq, tl=512)
    return hyena_filter_dft(eo, asum, tables)


def hyena_mixer(gates, kf, bias, tables):
    z = hyena_long_conv(gates, 2, gates, 1, kf, 0, bias[0], tables)
    return hyena_long_conv(z[None], 0, gates, 0, kf, 1, bias[1], tables)


def _odd_out_kernel(ys_ref, u_ref, hy_ref, x_ref, d_ref, wglu_ref, wo_ref, g_ref, o_ref):
    y = ys_ref[0] + d_ref[...] * u_ref[0]
    c0 = math.sqrt(2.0 / math.pi)
    gl = 0.5 * y * (1.0 + jnp.tanh(c0 * (y + 0.044715 * (y * y * y))))
    z = _dot(gl.astype(BF16), wglu_ref[...])
    s5 = gl * (1.0 / (1.0 + jnp.exp(-z)))
    hy = [_from_time_tiles(hy_ref[ct]).astype(BF16) for ct in range(hy_ref.shape[0])]
    mix = _dot(jnp.concatenate([s5.astype(BF16)] + hy, axis=-1), wo_ref[...])
    o_ref[0] = x_ref[0] + _rms(mix, g_ref[...])


def odd_out(ys, p, hy, x, d_skip, w_glu, w_out, layer, g, *, tm):
    b, l, d = x.shape
    tok = lambda w: pl.BlockSpec((1, tm, w), lambda bi, i: (bi, i, 0))
    hy_spec = pl.BlockSpec((None, D_HYENA // LANES, DFT_TILES, tm // DFT_FAST * SUBLANES, LANES),
                           lambda bi, i: (bi, 0, 0, i, 0))
    return pl.pallas_call(
        _odd_out_kernel,
        grid=(b, l // tm),
        in_specs=[tok(D_S5), tok(D_S5), hy_spec, tok(d),
                  pl.BlockSpec((1, D_S5), lambda bi, i: (0, 0)),
                  _layer_spec((D_S5, D_S5), layer, lambda bi, i: (0, 0)),
                  _layer_spec((d, d), layer, lambda bi, i: (0, 0)),
                  pl.BlockSpec((1, d), lambda bi, i: (0, 0))],
        out_specs=tok(d),
        out_shape=jax.ShapeDtypeStruct(x.shape, F32),
        compiler_params=_cparams("parallel", "parallel"),
        name="odd_out",
    )(ys, p, hy, x, d_skip.reshape(1, D_S5), w_glu, w_out, g.reshape(1, d))


def kernel(x, mem, norm_mix, norm_xattn, norm_mem, norm_mlp, xa_wq, xa_wk, xa_wv, xa_wo, mlp_w1, mlp_w2, ev_w_in, ev_pool_w, ev_pool_scale, ev_conv_w, ev_w_out, od_w_in, od_s5_lambda_re, od_s5_lambda_im, od_s5_log_dt, od_s5_b_re, od_s5_b_im, od_s5_c_re, od_s5_c_im, od_s5_d, od_s5_w_glu, od_hy_short_w, od_hy_short_b, od_hy_w1, od_hy_b1, od_hy_w2, od_hy_b2, od_hy_w3, od_hy_freq, od_hy_bias, od_w_out):
    b, l, d = x.shape
    depth = norm_mix.shape[0]
    assert b == 2, "the long convolution packs the two batch rows as one complex signal"
    tables = dft_tables(l)
    mem2d = mem.reshape(b * N_MEM, d)
    wq, wo = xa_wq.astype(BF16), xa_wo.astype(BF16)
    wkv = jnp.concatenate([xa_wk, xa_wv], axis=2).astype(BF16)
    w1, w2 = mlp_w1.astype(BF16), mlp_w2.astype(BF16)
    ev_in, ev_out, ev_pool = ev_w_in.astype(BF16), ev_w_out.astype(BF16), ev_pool_w.astype(BF16)
    od_in, od_out, od_glu = od_w_in.astype(BF16), od_w_out.astype(BF16), od_s5_w_glu.astype(BF16)
    for i in range(depth):
        j = i // 2
        if i % 2 == 0:
            x = even_mixer(x, norm_mix[i, 0], ev_in, ev_pool, ev_pool_scale[j], ev_conv_w[j], ev_out, j,
                           norm_mix[i, 1], tm=512)
        else:
            u_s5, gates = odd_in(x, norm_mix[i, 0], od_in, j, od_hy_short_w[j], od_hy_short_b[j], tm=512)
            mats = s5_matrices(od_s5_lambda_re[j], od_s5_lambda_im[j], od_s5_log_dt[j], od_s5_b_re[j],
                               od_s5_b_im[j], od_s5_c_re[j], od_s5_c_im[j])
            ys = s5_core(u_s5, mats)
            kf = hyena_filter_spectra(l, tables, od_hy_w1[j], od_hy_b1[j], od_hy_w2[j], od_hy_b2[j],
                                      od_hy_w3[j], od_hy_freq[j])
            hy = hyena_mixer(gates, kf, od_hy_bias[j], tables)
            x = odd_out(ys, u_s5, hy, x, od_s5_d[j], od_glu, od_out, j, norm_mix[i, 1], tm=512)
        kv = norm_matmul(mem2d, norm_mem[i], wkv, i, tm=b * N_MEM, out_dtype=BF16).reshape(b, N_MEM, 2 * d)
        x = xattn_block(x, kv, wq, wo, i, norm_xattn[i, 0], norm_xattn[i, 1], tm=512)
        x = mlp_block(x.reshape(b * l, d), w1, w2, i, norm_mlp[i, 0], norm_mlp[i, 1],
                      tm=1024, tf=512).reshape(b, l, d)
    return x
```

```python
import functools
import math

import numpy as np
import jax
import jax.numpy as jnp
from jax import lax
from jax.experimental import pallas as pl
from jax.experimental.pallas import tpu as pltpu

F32 = jnp.float32
BF16 = jnp.bfloat16

D_MODEL = 1024
N_MEM = 256
RMS_EPS = 1e-6
D_POOL = 512
POOL_WINDOWS = (2, 4, 8, 16)
POOL_GROUP_DIM = 128
POOL_HALO = 8
D_CONV = 512
D_S5 = 512
S5_GROUP_DIM = 16
S5_GROUPS = 32
S5_STATE = 64
S5_CHUNK = 16
D_HYENA = 512
HYENA_BANDS = 16
HYENA_FFN = 64
HYENA_TARGET = 1e-2
HYENA_SHORT_DECAY_PCT = 0.3
HYENA_LONG_DECAY_PCT = 1.5
XA_HEADS = 4
XA_HEAD_DIM = 256
D_FF = 4096

LANES = 128
SUBLANES = 8
S5_SUPER = LANES // S5_GROUP_DIM
DFT_FAST = 128
DFT_TILES = DFT_FAST // SUBLANES
DFT_PITCH = DFT_FAST + SUBLANES
DFT_UNROLL = 16
VMEM_LIMIT = 56 * 1024 * 1024
VMEM_LIMIT_CONV = 62 * 1024 * 1024


def _cparams(*sem, vmem=VMEM_LIMIT):
    return pltpu.CompilerParams(dimension_semantics=sem, vmem_limit_bytes=vmem)


def _rms(xf, g):
    ms = jnp.mean(xf * xf, axis=-1, keepdims=True)
    return xf * lax.rsqrt(ms + RMS_EPS) * g


def _dot(a, b):
    return jnp.dot(a, b, preferred_element_type=F32)


def _layer_spec(block, layer, tail_map):
    return pl.BlockSpec((None,) + block, lambda *idx: (layer,) + tail_map(*idx))


def _norm_matmul_kernel(x_ref, g_ref, w_ref, o_ref):
    xn = _rms(x_ref[...], g_ref[...]).astype(BF16)
    o_ref[...] = _dot(xn, w_ref[...]).astype(o_ref.dtype)


def norm_matmul(x2d, g, w, layer, *, tm, out_dtype=F32):
    m, d = x2d.shape
    n = w.shape[-1]
    return pl.pallas_call(
        _norm_matmul_kernel,
        grid=(m // tm,),
        in_specs=[
            pl.BlockSpec((tm, d), lambda i: (i, 0)),
            pl.BlockSpec((1, d), lambda i: (0, 0)),
            _layer_spec((d, n), layer, lambda i: (0, 0)),
        ],
        out_specs=pl.BlockSpec((tm, n), lambda i: (i, 0)),
        out_shape=jax.ShapeDtypeStruct((m, n), out_dtype),
        compiler_params=_cparams("parallel"),
        name="norm_matmul",
    )(x2d, g.reshape(1, d), w)


def _xattn_kernel(x_ref, kv_ref, wq_ref, wo_ref, g1_ref, g2_ref, o_ref):
    x = x_ref[0]
    xn = _rms(x, g1_ref[...]).astype(BF16)
    q = (_dot(xn, wq_ref[...]) * (XA_HEAD_DIM ** -0.5)).astype(BF16)
    heads = []
    for h in range(XA_HEADS):
        lo = h * XA_HEAD_DIM
        qh = q[:, lo:lo + XA_HEAD_DIM]
        kh = kv_ref[0, :, lo:lo + XA_HEAD_DIM]
        vh = kv_ref[0, :, D_MODEL + lo:D_MODEL + lo + XA_HEAD_DIM]
        s = lax.dot_general(qh, kh, (((1,), (1,)), ((), ())), preferred_element_type=F32)
        e = jnp.exp(s - jnp.max(s, axis=-1, keepdims=True))
        p = e / jnp.sum(e, axis=-1, keepdims=True)
        heads.append(_dot(p.astype(BF16), vh).astype(BF16))
    o = jnp.concatenate(heads, axis=-1)
    y = _dot(o, wo_ref[...])
    o_ref[0] = x + _rms(y, g2_ref[...])


def xattn_block(x, kv, wq, wo, layer, g1, g2, *, tm):
    b, l, d = x.shape
    return pl.pallas_call(
        _xattn_kernel,
        grid=(b, l // tm),
        in_specs=[
            pl.BlockSpec((1, tm, d), lambda bi, i: (bi, i, 0)),
            pl.BlockSpec((1, N_MEM, 2 * d), lambda bi, i: (bi, 0, 0)),
            _layer_spec((d, d), layer, lambda bi, i: (0, 0)),
            _layer_spec((d, d), layer, lambda bi, i: (0, 0)),
            pl.BlockSpec((1, d), lambda bi, i: (0, 0)),
            pl.BlockSpec((1, d), lambda bi, i: (0, 0)),
        ],
        out_specs=pl.BlockSpec((1, tm, d), lambda bi, i: (bi, i, 0)),
        out_shape=jax.ShapeDtypeStruct(x.shape, F32),
        compiler_params=_cparams("parallel", "parallel"),
        name="xattn_block",
    )(x, kv, wq, wo, g1.reshape(1, d), g2.reshape(1, d))


def _mlp_kernel(x_ref, w1_ref, w2_ref, g1_ref, g2_ref, o_ref, h_ref, *, tf):
    x = x_ref[...]
    xn = _rms(x, g1_ref[...]).astype(BF16)
    for c in range(h_ref.shape[1] // tf):
        cols = slice(c * tf, (c + 1) * tf)
        h = jnp.maximum(_dot(xn, w1_ref[:, cols]), 0.0)
        h_ref[:, cols] = (h * h).astype(BF16)
    o_ref[...] = x + _rms(_dot(h_ref[...], w2_ref[...]), g2_ref[...])


def mlp_block(x2d, w1, w2, layer, g1, g2, *, tm, tf):
    m, d = x2d.shape
    ff = w1.shape[-1]
    resident = dict(pipeline_mode=pl.Buffered(1))
    return pl.pallas_call(
        functools.partial(_mlp_kernel, tf=tf),
        grid=(m // tm,),
        in_specs=[
            pl.BlockSpec((tm, d), lambda i: (i, 0)),
            pl.BlockSpec((None, d, ff), lambda i: (layer, 0, 0), **resident),
            pl.BlockSpec((None, ff, d), lambda i: (layer, 0, 0), **resident),
            pl.BlockSpec((1, d), lambda i: (0, 0)),
            pl.BlockSpec((1, d), lambda i: (0, 0)),
        ],
        out_specs=pl.BlockSpec((tm, d), lambda i: (i, 0)),
        out_shape=jax.ShapeDtypeStruct((m, d), F32),
        scratch_shapes=[pltpu.VMEM((tm, ff), BF16)],
        compiler_params=_cparams("parallel"),
        name="mlp_block",
    )(x2d, w1, w2, g1.reshape(1, d), g2.reshape(1, d))


def _halo_specs(tm, seq_len, width, col):
    r = tm // POOL_HALO
    last = seq_len // POOL_HALO - 1

    def prev_map(bi, i, *_):
        return (bi, jnp.maximum(i * r - 1, 0), col(*_))

    def main_map(bi, i, *_):
        return (bi, i, col(*_))

    def next_map(bi, i, *_):
        return (bi, jnp.minimum((i + 1) * r, last), col(*_))

    return [
        pl.BlockSpec((1, POOL_HALO, width), prev_map),
        pl.BlockSpec((1, tm, width), main_map),
        pl.BlockSpec((1, POOL_HALO, width), next_map),
    ]


def _with_halo(prev_ref, main_ref, next_ref):
    i = pl.program_id(1)
    prev = jnp.where(i > 0, prev_ref[0], 0.0)
    nxt = jnp.where(i < pl.num_programs(1) - 1, next_ref[0], 0.0)
    return jnp.concatenate([prev, main_ref[0], nxt], axis=0)


def _shift_rows(v, k):
    return pltpu.roll(v, k % v.shape[0], 0)


def _even_mixer_kernel(xp_ref, x_ref, xn_ref, gi_ref, wi_ref, wg_ref, ps_ref, cw_ref, wo_ref, g_ref, o_ref,
                       *, tm, seq_len):
    x_ext = _with_halo(xp_ref, x_ref, xn_ref)
    ext = _dot(_rms(x_ext, gi_ref[...]).astype(BF16), wi_ref[...])
    lo, hi = POOL_HALO, POOL_HALO + tm
    t = pl.program_id(1) * tm + lax.broadcasted_iota(jnp.int32, (tm, 1), 0)
    parts = []
    for gi, win in enumerate(POOL_WINDOWS):
        half = win // 2
        u = ext[:, gi * POOL_GROUP_DIM:(gi + 1) * POOL_GROUP_DIM]
        s = u + _shift_rows(u, 1)
        step = 1
        while 2 * step < win:
            s = _shift_rows(s, step) + _shift_rows(s, -step)
            step *= 2
        cnt = (jnp.minimum(t + half, seq_len) - jnp.maximum(t - half, 0)).astype(F32)
        pooled = s[lo:hi] / cnt - u[lo:hi]
        y = _dot(pooled.astype(BF16), wg_ref[gi])
        parts.append((y * ps_ref[:, gi * POOL_GROUP_DIM:(gi + 1) * POOL_GROUP_DIM]).astype(BF16))
    b_gate = ext[lo:hi, D_POOL:D_POOL + D_CONV]
    ch = ext[:, D_POOL + D_CONV:D_POOL + 2 * D_CONV] * ext[:, D_POOL + 2 * D_CONV:D_POOL + 3 * D_CONV]
    conv = cw_ref[0:1, :] * _shift_rows(ch, 1) + cw_ref[1:2, :] * ch + cw_ref[2:3, :] * _shift_rows(ch, -1)
    parts.append((b_gate * conv[lo:hi]).astype(BF16))
    mix = _dot(jnp.concatenate(parts, axis=-1), wo_ref[...])
    o_ref[0] = x_ref[0] + _rms(mix, g_ref[...])


def even_mixer(x, g_in, w_in, w_group, pool_scale, conv_w, w_out, layer, g, *, tm):
    b, l, d = x.shape
    kern = functools.partial(_even_mixer_kernel, tm=tm, seq_len=l)
    return pl.pallas_call(
        kern,
        grid=(b, l // tm),
        in_specs=_halo_specs(tm, l, d, lambda: 0) + [
            pl.BlockSpec((1, d), lambda bi, i: (0, 0)),
            _layer_spec(w_in.shape[1:], layer, lambda bi, i: (0, 0)),
            _layer_spec(w_group.shape[1:], layer, lambda bi, i: (0, 0, 0)),
            pl.BlockSpec((1, D_POOL), lambda bi, i: (0, 0)),
            pl.BlockSpec((3, D_CONV), lambda bi, i: (0, 0)),
            _layer_spec((d, d), layer, lambda bi, i: (0, 0)),
            pl.BlockSpec((1, d), lambda bi, i: (0, 0)),
        ],
        out_specs=pl.BlockSpec((1, tm, d), lambda bi, i: (bi, i, 0)),
        out_shape=jax.ShapeDtypeStruct(x.shape, F32),
        compiler_params=_cparams("parallel", "parallel"),
        name="even_mixer",
    )(x, x, x, g_in.reshape(1, d), w_in, w_group, pool_scale.reshape(1, D_POOL), conv_w, w_out, g.reshape(1, d))


def _cmul(ar, ai, br, bi):
    return ar * br - ai * bi, ar * bi + ai * br


def s5_matrices(lam_re, lam_im, log_dt, b_re, b_im, c_re, c_im):
    tc, hd, n, sup = S5_CHUNK, S5_GROUP_DIM, S5_STATE, S5_SUPER
    n_tiles = S5_GROUPS // sup
    hp = lax.Precision.HIGHEST
    lr = jnp.minimum(lam_re, -1e-4)
    li = lam_im
    dt = jnp.exp(log_dt)[..., None]
    taus = jnp.arange(tc + 1, dtype=F32)[:, None, None, None]
    rmag = jnp.exp(lr[None] * dt[None] * (tc - taus))
    rang = li[None] * dt[None] * (tc - taus)
    pwr_re, pwr_im = rmag * jnp.cos(rang), rmag * jnp.sin(rang)
    mag = jnp.exp(lr[None] * dt[None] * taus)
    ang = li[None] * dt[None] * taus
    pw_re, pw_im = mag * jnp.cos(ang), mag * jnp.sin(ang)
    nr, ni = pw_re[1] - 1.0, pw_im[1]
    den = lr * lr + li * li
    coef_re, coef_im = (nr * lr + ni * li) / den, (ni * lr - nr * li) / den
    cb_re, cb_im = _cmul(coef_re[..., None], coef_im[..., None], b_re[None], b_im[None])
    cbt_re, cbt_im = _cmul(coef_re[:, :, None, :], coef_im[:, :, None, :],
                           jnp.swapaxes(b_re, 1, 2)[None], jnp.swapaxes(b_im, 1, 2)[None])
    q_re, q_im = _cmul(pw_re[:tc, :, :, :, None], pw_im[:tc, :, :, :, None], cb_re[None], cb_im[None])
    m = (jnp.einsum('dgkn,tdgnj->dtgjk', c_re, q_re, precision=hp)
         - jnp.einsum('dgkn,tdgnj->dtgjk', c_im, q_im, precision=hp))
    m_fwd, m_bwd = m[0], m[1]
    diag = m_fwd[0] + m_bwd[0]
    lag_table = jnp.concatenate([m_bwd[:0:-1], diag[None], m_fwd[1:]], axis=0)
    intra = jnp.stack([lag_table[tc - 1 - s:2 * tc - 1 - s] for s in range(tc)], axis=0)
    intra = jnp.transpose(intra, (2, 0, 3, 1, 4)).reshape(S5_GROUPS, tc * hd, tc * hd)
    sf_re, sf_im = _cmul(pwr_re[1:tc + 1, 0][:, :, None, :], pwr_im[1:tc + 1, 0][:, :, None, :],
                         cbt_re[0][None], cbt_im[0][None])
    sb_re, sb_im = _cmul(pw_re[:tc, 1][:, :, None, :], pw_im[:tc, 1][:, :, None, :],
                         cbt_re[1][None], cbt_im[1][None])
    summ = jnp.concatenate([sf_re, sb_re, sf_im, sb_im], axis=-1)
    summ = jnp.swapaxes(summ, 0, 1).reshape(S5_GROUPS, tc * hd, 4 * n)
    ct_re, ct_im = jnp.swapaxes(c_re, 2, 3), jnp.swapaxes(c_im, 2, 3)
    gnt = lambda z: jnp.transpose(z, (1, 2, 0))[..., None]
    qf_re, qf_im = _cmul(ct_re[0][:, :, None, :], ct_im[0][:, :, None, :],
                         gnt(pw_re[1:tc + 1, 0]), gnt(pw_im[1:tc + 1, 0]))
    qb_re, qb_im = _cmul(ct_re[1][:, :, None, :], ct_im[1][:, :, None, :], gnt(pwr_re[:tc, 1]), gnt(pwr_im[:tc, 1]))
    carry = jnp.concatenate([qf_re, qb_re, -qf_im, -qb_im], axis=1)
    carry = carry.reshape(S5_GROUPS, 4 * n, tc * hd)
    a_re = jnp.concatenate([pw_re[tc, 0], pw_re[tc, 1]], axis=-1)
    a_im = jnp.concatenate([pw_im[tc, 0], pw_im[tc, 1]], axis=-1)
    return intra, summ, carry, a_re, a_im


def s5_regroup_matrix():
    n = S5_SUPER * LANES
    src = np.arange(n)
    tb, g, h = src // LANES, (src % LANES) // S5_GROUP_DIM, src % S5_GROUP_DIM
    perm = np.zeros((n, n), np.float32)
    perm[src, g * LANES + tb * S5_GROUP_DIM + h] = 1.0
    return jnp.asarray(perm, BF16)


def _s5_regroup_kernel(p_ref, w_ref, perm_ref, u_ref, sr_ref, si_ref, *, n_chunks):
    n2 = 2 * S5_STATE
    for q in range(S5_CHUNK // S5_SUPER):
        rows_t = [p_ref[pl.ds(q * S5_SUPER + tb, n_chunks, stride=S5_CHUNK), :].astype(BF16) for tb in range(S5_SUPER)]
        grouped = _dot(jnp.concatenate(rows_t, axis=-1), perm_ref[...]).astype(BF16)
        for g in range(S5_SUPER):
            u_ref[g, :, q * LANES:(q + 1) * LANES] = grouped[:, g * LANES:(g + 1) * LANES]
    for g in range(S5_SUPER):
        s = _dot(u_ref[g], w_ref[g])
        sr_ref[pl.ds(g, n_chunks, stride=S5_SUPER), :] = s[:, :n2]
        si_ref[pl.ds(g, n_chunks, stride=S5_SUPER), :] = s[:, n2:]


def _s5_state_spec(c, n2, n_tiles):
    return pl.BlockSpec((None, c * S5_SUPER, n2), lambda a, bi: (bi * n_tiles + a, 0, 0))


def _s5_group_spec(rows, cols):
    return pl.BlockSpec((S5_SUPER, rows, cols), lambda a, bi: (a, 0, 0))


def s5_summary(p, summ):
    b, l, _ = p.shape
    c = l // S5_CHUNK
    n_tiles = S5_GROUPS // S5_SUPER
    k = S5_CHUNK * S5_GROUP_DIM
    n2 = 2 * S5_STATE
    shape = jax.ShapeDtypeStruct((b * n_tiles, c * S5_SUPER, n2), F32)
    kern = functools.partial(_s5_regroup_kernel, n_chunks=c)
    perm = s5_regroup_matrix()
    return pl.pallas_call(
        kern,
        grid=(n_tiles, b),
        in_specs=[
            pl.BlockSpec((None, l, LANES), lambda a, bi: (bi, 0, a)),
            _s5_group_spec(k, 2 * n2),
            pl.BlockSpec(perm.shape, lambda a, bi: (0, 0)),
        ],
        out_specs=[pl.BlockSpec((None, S5_SUPER, c, k), lambda a, bi: (bi * n_tiles + a, 0, 0, 0)),
                   _s5_state_spec(c, n2, n_tiles), _s5_state_spec(c, n2, n_tiles)],
        out_shape=[jax.ShapeDtypeStruct((b * n_tiles, S5_SUPER, c, k), BF16), shape, shape],
        compiler_params=_cparams("parallel", "parallel"),
        name="s5_summary",
    )(p, summ, perm)


def _s5_scan_kernel(sr_ref, si_ref, are_ref, aim_ref, or_ref, oi_ref, *, n_chunks):
    n2 = 2 * S5_STATE
    a_re = are_ref[...]
    a_im = aim_ref[...]
    fwd_lane = lax.broadcasted_iota(jnp.int32, (SUBLANES, n2), 1) < S5_STATE

    def body(i, state):
        x_re, x_im = state
        rows_i = pl.ds(pl.multiple_of(i * SUBLANES, SUBLANES), SUBLANES)
        rows_r = pl.ds(pl.multiple_of((n_chunks - 1 - i) * SUBLANES, SUBLANES), SUBLANES)
        or_ref[rows_i, 0:S5_STATE] = x_re[:, :S5_STATE]
        oi_ref[rows_i, 0:S5_STATE] = x_im[:, :S5_STATE]
        or_ref[rows_r, S5_STATE:n2] = x_re[:, S5_STATE:]
        oi_ref[rows_r, S5_STATE:n2] = x_im[:, S5_STATE:]
        s_re = jnp.where(fwd_lane, sr_ref[rows_i, :], sr_ref[rows_r, :])
        s_im = jnp.where(fwd_lane, si_ref[rows_i, :], si_ref[rows_r, :])
        return (a_re * x_re - a_im * x_im + s_re, a_re * x_im + a_im * x_re + s_im)

    zero = jnp.zeros((SUBLANES, n2), F32)
    lax.fori_loop(0, n_chunks, body, (zero, zero))


def s5_scan(s_re, s_im, a_re, a_im):
    tiles, rows, n2 = s_re.shape
    kern = functools.partial(_s5_scan_kernel, n_chunks=rows // S5_SUPER)
    st = pl.BlockSpec((None, rows, n2), lambda i: (i, 0, 0))
    av = pl.BlockSpec((SUBLANES, n2), lambda i: (i, 0))
    shape = jax.ShapeDtypeStruct(s_re.shape, F32)
    return pl.pallas_call(
        kern,
        grid=(tiles,),
        in_specs=[st, st, av, av],
        out_specs=[st, st],
        out_shape=[shape, shape],
        compiler_params=_cparams("parallel"),
        name="s5_scan",
    )(s_re, s_im, a_re, a_im)


def _s5_output_kernel(u_ref, xr_ref, xi_ref, wi_ref, wc_ref, perm_ref, o_ref, y_ref, *, n_chunks):
    for g in range(S5_SUPER):
        rows = pl.ds(g, n_chunks, stride=S5_SUPER)
        xin = jnp.concatenate([xr_ref[rows, :], xi_ref[rows, :]], axis=-1).astype(BF16)
        y_ref[g] = _dot(u_ref[g], wi_ref[g]) + _dot(xin, wc_ref[g])
    back = (((1,), (1,)), ((), ()))
    for q in range(S5_CHUNK // S5_SUPER):
        y = jnp.concatenate([y_ref[g, :, q * LANES:(q + 1) * LANES] for g in range(S5_SUPER)], axis=-1)
        hi = y.astype(BF16)
        lo = (y - hi.astype(F32)).astype(BF16)
        tok = (lax.dot_general(hi, perm_ref[...], back, preferred_element_type=F32)
               + lax.dot_general(lo, perm_ref[...], back, preferred_element_type=F32))
        for tb in range(S5_SUPER):
            o_ref[pl.ds(q * S5_SUPER + tb, n_chunks, stride=S5_CHUNK), :] = tok[:, tb * LANES:(tb + 1) * LANES]


def s5_output(u, xin_re, xin_im, intra, carry, *, batch):
    tiles, _, c, k = u.shape
    n_tiles = tiles // batch
    n2 = 2 * S5_STATE
    l = c * S5_CHUNK
    kern = functools.partial(_s5_output_kernel, n_chunks=c)
    perm = s5_regroup_matrix()
    return pl.pallas_call(
        kern,
        grid=(n_tiles, batch),
        in_specs=[
            pl.BlockSpec((None, S5_SUPER, c, k), lambda a, bi: (bi * n_tiles + a, 0, 0, 0)),
            _s5_state_spec(c, n2, n_tiles),
            _s5_state_spec(c, n2, n_tiles),
            _s5_group_spec(k, k),
            _s5_group_spec(2 * n2, k),
            pl.BlockSpec(perm.shape, lambda a, bi: (0, 0)),
        ],
        out_specs=pl.BlockSpec((None, l, LANES), lambda a, bi: (bi, 0, a)),
        out_shape=jax.ShapeDtypeStruct((batch, l, D_S5), F32),
        scratch_shapes=[pltpu.VMEM((S5_SUPER, c, k), F32)],
        compiler_params=_cparams("parallel", "parallel"),
        name="s5_output",
    )(u, xin_re, xin_im, intra, carry, perm)


def s5_core(p, mats):
    b = p.shape[0]
    intra, summ, carry, a_re, a_im = mats
    u, s_re, s_im = s5_summary(p, summ.astype(BF16))
    xin_re, xin_im = s5_scan(s_re, s_im, jnp.tile(a_re, (b, 1)), jnp.tile(a_im, (b, 1)))
    return s5_output(u, xin_re, xin_im, intra.astype(BF16), carry.astype(BF16), batch=b)


def _to_time_tiles(y):
    nf, c = DFT_FAST, y.shape[-1]
    slabs = [y[s * nf:(s + 1) * nf].reshape(DFT_TILES, SUBLANES, c) for s in range(y.shape[0] // nf)]
    return jnp.concatenate(slabs, axis=1)


def _from_time_tiles(v):
    c = v.shape[-1]
    slabs = [v[:, s * SUBLANES:(s + 1) * SUBLANES, :].reshape(DFT_FAST, c) for s in range(v.shape[1] // SUBLANES)]
    return jnp.concatenate(slabs, axis=0)


def _odd_in_kernel(xp_ref, x_ref, xn_ref, g_ref, w_ref, cw_ref, cb_ref, s5_ref, gates_ref, *, tm):
    ext = _with_halo(xp_ref, x_ref, xn_ref)
    p = _dot(_rms(ext, g_ref[...]).astype(BF16), w_ref[...])
    lo, hi = POOL_HALO, POOL_HALO + tm
    s5_ref[0] = p[lo:hi, :D_S5]
    ph = p[:, D_S5:]
    y = cw_ref[0:1, :] * _shift_rows(ph, 1) + cw_ref[1:2, :] * ph + cw_ref[2:3, :] * _shift_rows(ph, -1)
    y = y[lo:hi] + cb_ref[...]
    for k in range(3):
        for ct in range(D_HYENA // LANES):
            col = k * D_HYENA + ct * LANES
            gates_ref[k, ct] = _to_time_tiles(y[:, col:col + LANES])


def odd_in(x, g, w, layer, conv_w, conv_b, *, tm):
    b, l, d = x.shape
    n = w.shape[-1]
    rows = tm // DFT_FAST * SUBLANES
    cts = D_HYENA // LANES
    kern = functools.partial(_odd_in_kernel, tm=tm)
    return pl.pallas_call(
        kern,
        grid=(b, l // tm),
        in_specs=_halo_specs(tm, l, d, lambda: 0) + [
            pl.BlockSpec((1, d), lambda bi, i: (0, 0)),
            _layer_spec((d, n), layer, lambda bi, i: (0, 0)),
            pl.BlockSpec((3, n - D_S5), lambda bi, i: (0, 0)),
            pl.BlockSpec((1, n - D_S5), lambda bi, i: (0, 0)),
        ],
        out_specs=[pl.BlockSpec((1, tm, D_S5), lambda bi, i: (bi, i, 0)),
                   pl.BlockSpec((3, None, cts, DFT_TILES, rows, LANES), lambda bi, i: (0, bi, 0, 0, i, 0))],
        out_shape=[jax.ShapeDtypeStruct((b, l, D_S5), F32),
                   jax.ShapeDtypeStruct((3, b, cts, DFT_TILES, l // DFT_FAST * SUBLANES, LANES), F32)],
        compiler_params=_cparams("parallel", "parallel"),
        name="odd_in",
    )(x, x, x, g.reshape(1, d), w, conv_w, conv_b.reshape(1, -1))


def _filter_mlp_kernel(bands_ref, w1t_ref, w1cs_ref, b1_ref, w2_ref, b2_ref, w3h_ref, w3l_ref, fr_ref, dl_ref,
                       eo_ref, asum_ref, *, tl, seq_len):
    i = pl.program_id(0)
    hp = lax.Precision.HIGHEST
    c = D_HYENA
    t = (i * tl + lax.broadcasted_iota(jnp.int32, (tl, 1), 0)).astype(F32)
    t_norm = t / (seq_len - 1.0)
    ang = (2.0 * math.pi / seq_len) * t * bands_ref[...]
    is_cos = lax.broadcasted_iota(jnp.int32, (1, ang.shape[1]), 1) < HYENA_BANDS
    cs = jnp.sin(ang + jnp.where(is_cos, 0.5 * math.pi, math.pi))
    fr = fr_ref[...]
    z = t_norm * w1t_ref[...] + jnp.dot(cs, w1cs_ref[...], precision=hp, preferred_element_type=F32)
    h = jnp.sin(fr * (z + b1_ref[...]))
    h = jnp.sin(fr * (jnp.dot(h, w2_ref[...], precision=hp, preferred_element_type=F32) + b2_ref[...]))
    h_hi = h.astype(BF16)
    h_lo = (h - h_hi.astype(F32)).astype(BF16)
    h = _dot(h_hi, w3h_ref[...]) + (_dot(h_lo, w3h_ref[...]) + _dot(h_hi, w3l_ref[...]))
    decay = jnp.exp(-t_norm * dl_ref[...])

    @pl.when(i == 0)
    def _():
        asum_ref[...] = jnp.zeros_like(asum_ref)

    for o in range(h.shape[1] // (2 * c)):
        fwd = h[:, 2 * o * c:(2 * o + 1) * c] * decay
        bwd = h[:, (2 * o + 1) * c:(2 * o + 2) * c] * decay
        eo_ref[:, 2 * o * c:(2 * o + 1) * c] = fwd + bwd
        eo_ref[:, (2 * o + 1) * c:(2 * o + 2) * c] = fwd - bwd
        asum_ref[:, o * c:(o + 1) * c] += jnp.sum(jnp.abs(fwd) + jnp.abs(bwd), axis=0, keepdims=True)


def hyena_filter_mlp(seq_len, w1, b1, w2, b2, w3, freq, *, tl):
    n_out = w3.shape[1]
    bands = jnp.linspace(1e-4, HYENA_BANDS - 1, HYENA_BANDS, dtype=F32)
    bands2 = jnp.concatenate([bands, bands])[None, :]
    deltas = jnp.abs(jnp.linspace(math.log(HYENA_TARGET) / HYENA_LONG_DECAY_PCT,
                                  math.log(HYENA_TARGET) / HYENA_SHORT_DECAY_PCT, D_HYENA, dtype=F32))[None, :]
    kern = functools.partial(_filter_mlp_kernel, tl=tl, seq_len=seq_len)
    full = lambda a: pl.BlockSpec(a.shape, lambda i: (0,) * a.ndim)
    w3_hi = w3.astype(BF16)
    w3_lo = (w3 - w3_hi.astype(F32)).astype(BF16)
    args = [bands2, w1[0:1], w1[1:], b1[None, :], w2, b2[None, :], w3_hi, w3_lo, freq[None, :], deltas]
    return pl.pallas_call(
        kern,
        grid=(seq_len // tl,),
        in_specs=[full(a) for a in args],
        out_specs=[pl.BlockSpec((tl, n_out), lambda i: (i, 0)), pl.BlockSpec((1, n_out // 2), lambda i: (0, 0))],
        out_shape=[jax.ShapeDtypeStruct((seq_len, n_out), F32), jax.ShapeDtypeStruct((1, n_out // 2), F32)],
        compiler_params=_cparams("arbitrary"),
        name="hyena_filter_mlp",
    )(*args)


def dft_tables(seq_len):
    n = 2 * seq_len
    nf = DFT_FAST
    ns = n // nf
    k_a = np.arange(ns)[:, None]
    n_s = np.arange(ns // 2)[None, :]
    w_s = np.exp(-2j * np.pi * ((k_a * n_s) % ns) / ns)
    n_f = np.arange(nf)[:, None]
    tw = np.exp(-2j * np.pi * ((n_f * np.arange(ns)[None, :]) % n) / n)
    c32 = lambda z: (jnp.asarray(z.real, F32), jnp.asarray(z.imag, F32))
    ws_re, ws_im = c32(w_s)
    tw_re, tw_im = c32(tw)
    g_re = tw_re[:, :, None] * ws_re[None] - tw_im[:, :, None] * ws_im[None]
    g_im = tw_re[:, :, None] * ws_im[None] + tw_im[:, :, None] * ws_re[None]
    fwd_a = jnp.concatenate([jnp.concatenate([g_re, -g_im], axis=2),
                             jnp.concatenate([g_im, g_re], axis=2)], axis=1)
    gt_re = jnp.swapaxes(g_re, 1, 2) * (1.0 / n)
    gt_im = jnp.swapaxes(g_im, 1, 2) * (1.0 / n)
    inv_a = jnp.concatenate([jnp.concatenate([gt_re, gt_im], axis=2),
                             jnp.concatenate([-gt_im, gt_re], axis=2)], axis=1)
    kk = np.arange(nf)
    w_f = np.exp(-2j * np.pi * ((kk[:, None] * kk[None, :]) % nf) / nf)
    f_re, f_im = c32(w_f)
    fwd_b = jnp.concatenate([jnp.concatenate([f_re, -f_im], axis=1),
                             jnp.concatenate([f_im, f_re], axis=1)], axis=0)
    inv_b = jnp.concatenate([jnp.concatenate([f_re, f_im], axis=1),
                             jnp.concatenate([-f_im, f_re], axis=1)], axis=0)
    return dict(fwd_a=fwd_a.astype(BF16), fwd_a_real=fwd_a[:, :, :ns // 2].astype(BF16),
                inv_a=inv_a.astype(BF16), fwd_b=fwd_b.astype(BF16), inv_b=inv_b.astype(BF16),
                fwd_b_part=fwd_b.reshape(2, nf, 2 * nf).astype(BF16))


def _filter_dft_kernel(x_ref, ga_ref, fb_ref, nrm_ref, k_ref, ar_scr, ai_scr):
    nf = DFT_FAST
    ns = ga_ref.shape[1] // 2

    def stage_a(j, carry):
        x = x_ref[pl.ds(j, ns // 2, stride=nf), :].astype(BF16)
        a = _dot(ga_ref[j], x)
        ar_scr[pl.ds(j, ns, stride=DFT_PITCH), :] = a[:ns]
        ai_scr[pl.ds(j, ns, stride=DFT_PITCH), :] = a[ns:]
        return carry

    lax.fori_loop(0, nf, stage_a, 0, unroll=DFT_UNROLL)
    inv_norm = 1.0 / (nrm_ref[...] + 1e-6)

    def stage_b(k, carry):
        src = pl.multiple_of(k * DFT_PITCH, SUBLANES)
        a = jnp.concatenate([ar_scr[pl.ds(src, nf), :], ai_scr[pl.ds(src, nf), :]], axis=0).astype(BF16)
        dst = pl.multiple_of(k * nf, nf)
        k_ref[pl.ds(dst, nf), :] = _dot(fb_ref[...], a) * inv_norm
        return carry

    lax.fori_loop(0, ns, stage_b, 0, unroll=DFT_UNROLL)


def hyena_filter_dft(eo, asum, tables):
    l, cols = eo.shape
    c = D_HYENA
    orders = cols // (2 * c)
    ct = c // LANES
    ga, fb = tables['fwd_a_real'], tables['fwd_b_part']
    ns = ga.shape[1] // 2
    scr = pltpu.VMEM((ns * DFT_PITCH, LANES), F32)
    return pl.pallas_call(
        _filter_dft_kernel,
        grid=(orders, ct, 2),
        in_specs=[
            pl.BlockSpec((l, LANES), lambda o, j, part: (0, (2 * o + part) * ct + j)),
            pl.BlockSpec(ga.shape, lambda o, j, part: (0, 0, 0)),
            pl.BlockSpec((None,) + fb.shape[1:], lambda o, j, part: (part, 0, 0)),
            pl.BlockSpec((1, LANES), lambda o, j, part: (0, o * ct + j)),
        ],
        out_specs=pl.BlockSpec((None, None, None, 2 * l, LANES), lambda o, j, part: (o, part, j, 0, 0)),
        out_shape=jax.ShapeDtypeStruct((orders, 2, ct, 2 * l, LANES), F32),
        scratch_shapes=[scr, scr],
        compiler_params=_cparams("parallel", "parallel", "parallel"),
        name="hyena_filter_dft",
    )(eo, ga, fb, asum)


def _lane_cat(parts):
    return parts[0] if len(parts) == 1 else jnp.concatenate(parts, axis=-1)


def _hyena_conv_kernel(x_ref, ga_ref, k_ref, fb_ref, ib_ref, gi_ref, gate_ref, u_ref, bias_ref, o_ref, *scr,
                       n_a, n_b, slabs):
    nf = DFT_FAST
    tiles, qt = x_ref.shape[1], x_ref.shape[2]
    half = x_ref.shape[3] // SUBLANES
    ns = 2 * half
    step = pl.program_id(1)

    @pl.when(step < n_a)
    def _():
        for qj in range(qt * SUBLANES):
            q, j = divmod(qj, SUBLANES)
            rows = pl.ds(j, half, stride=SUBLANES)
            x = _lane_cat([jnp.concatenate([x_ref[0, t, q, rows, :], x_ref[1, t, q, rows, :]], axis=0)
                           for t in range(tiles)])
            a = _dot(ga_ref[qj], x.astype(BF16))
            dst = pl.ds(step * (qt * SUBLANES) + qj, ns, stride=DFT_PITCH)
            for t in range(tiles):
                scr[2 * t][dst, :] = a[:ns, t * LANES:(t + 1) * LANES]
                scr[2 * t + 1][dst, :] = a[ns:, t * LANES:(t + 1) * LANES]

    @pl.when((step >= n_a) & (step < n_a + n_b))
    def _():
        for s in range(slabs):
            k_a = (step - n_a) * slabs + s
            rows = pl.ds(pl.multiple_of(k_a * DFT_PITCH, SUBLANES), nf)
            a = jnp.concatenate([_lane_cat([scr[2 * t + part][rows, :] for t in range(tiles)]) for part in range(2)], axis=0)
            x = _dot(fb_ref[...], a.astype(BF16))
            xr, xi = x[:nf], x[nf:]
            kr = _lane_cat([k_ref[0, t, s * nf:(s + 1) * nf, :] for t in range(tiles)])
            ki = _lane_cat([k_ref[1, t, s * nf:(s + 1) * nf, :] for t in range(tiles)])
            y = jnp.concatenate([xr * kr - xi * ki, xr * ki + xi * kr], axis=0).astype(BF16)
            cc = _dot(ib_ref[...], y)
            for t in range(tiles):
                scr[2 * t][rows, :] = cc[:nf, t * LANES:(t + 1) * LANES]
                scr[2 * t + 1][rows, :] = cc[nf:, t * LANES:(t + 1) * LANES]

    @pl.when(step >= n_a + n_b)
    def _():
        first = (step - (n_a + n_b)) * (qt * SUBLANES)
        for qj in range(qt * SUBLANES):
            q, j = divmod(qj, SUBLANES)
            src = pl.ds(first + qj, ns, stride=DFT_PITCH)
            cc = jnp.concatenate([_lane_cat([scr[2 * t + part][src, :] for t in range(tiles)]) for part in range(2)], axis=0)
            y = _dot(gi_ref[qj], cc.astype(BF16))
            dst = pl.ds(j, half, stride=SUBLANES)
            for t in range(tiles):
                o_ref[0, t, q, dst, :] = y[:half, t * LANES:(t + 1) * LANES]
                o_ref[1, t, q, dst, :] = y[half:, t * LANES:(t + 1) * LANES]
        for t in range(tiles):
            bias = bias_ref[:, t * LANES:(t + 1) * LANES]
            for bi in range(2):
                o_ref[bi, t] = gate_ref[bi, t] * (o_ref[bi, t] + u_ref[bi, t] * bias)


def hyena_long_conv(u6, u_idx, gate6, gate_idx, kf, order, bias, tables, *, slabs=4, tiles=2, qt=2):
    _, _, cts, q_tiles, rows, _ = u6.shape
    ns = 2 * rows // SUBLANES
    n_a = q_tiles // qt
    n_b = ns // slabs
    ga, gi, fb, ib = tables['fwd_a'], tables['inv_a'], tables['fwd_b'], tables['inv_b']
    q_fwd = lambda s: jnp.minimum(s, n_a - 1)
    q_mid = lambda s: jnp.clip(s - n_a, 0, n_b - 1)
    q_inv = lambda s: jnp.clip(s - n_a - n_b, 0, n_a - 1)

    def seq_spec(idx, q_of):
        return pl.BlockSpec((None, 2, tiles, qt, rows, LANES), lambda c, s: (idx, 0, c, q_of(s), 0, 0))

    scr = pltpu.VMEM((ns * DFT_PITCH, LANES), F32)
    kern = functools.partial(_hyena_conv_kernel, n_a=n_a, n_b=n_b, slabs=slabs)
    return pl.pallas_call(
        kern,
        grid=(cts // tiles, 2 * n_a + n_b),
        in_specs=[
            seq_spec(u_idx, q_fwd),
            pl.BlockSpec((qt * SUBLANES,) + ga.shape[1:], lambda c, s: (q_fwd(s), 0, 0)),
            pl.BlockSpec((None, 2, tiles, slabs * DFT_FAST, LANES), lambda c, s: (order, 0, c, q_mid(s), 0)),
            pl.BlockSpec(fb.shape, lambda c, s: (0, 0)),
            pl.BlockSpec(ib.shape, lambda c, s: (0, 0)),
            pl.BlockSpec((qt * SUBLANES,) + gi.shape[1:], lambda c, s: (q_inv(s), 0, 0)),
            seq_spec(gate_idx, q_inv),
            seq_spec(u_idx, q_inv),
            pl.BlockSpec((1, tiles * LANES), lambda c, s: (0, c)),
        ],
        out_specs=pl.BlockSpec((2, tiles, qt, rows, LANES), lambda c, s: (0, c, q_inv(s), 0, 0)),
        out_shape=jax.ShapeDtypeStruct((2, cts, q_tiles, rows, LANES), F32),
        scratch_shapes=[scr] * (2 * tiles),
        compiler_params=_cparams("arbitrary", "arbitrary", vmem=VMEM_LIMIT_CONV),
        name="hyena_long_conv",
    )(u6, ga, kf, fb, ib, gi, gate6, u6, bias.reshape(1, -1))


def hyena_filter_spectra(seq_len, tables, w1, b1, w2, b2, w3, freq):
    eo, asum = hyena_filter_mlp(seq_len, w1, b1, w2, b2, w3, freq, tl=512)
    return hyena_filter_dft(eo, asum, tables)


def hyena_mixer(gates, kf, bias, tables):
    z = hyena_long_conv(gates, 2, gates, 1, kf, 0, bias[0], tables)
    return hyena_long_conv(z[None], 0, gates, 0, kf, 1, bias[1], tables)


def _odd_out_kernel(ys_ref, u_ref, hy_ref, x_ref, d_ref, wglu_ref, wo_ref, g_ref, o_ref):
    y = ys_ref[0] + d_ref[...] * u_ref[0]
    c0 = math.sqrt(2.0 / math.pi)
    gl = 0.5 * y * (1.0 + jnp.tanh(c0 * (y + 0.044715 * (y * y * y))))
    z = _dot(gl.astype(BF16), wglu_ref[...])
    s5 = gl * (1.0 / (1.0 + jnp.exp(-z)))
    hy = [_from_time_tiles(hy_ref[ct]).astype(BF16) for ct in range(hy_ref.shape[0])]
    mix = _dot(jnp.concatenate([s5.astype(BF16)] + hy, axis=-1), wo_ref[...])
    o_ref[0] = x_ref[0] + _rms(mix, g_ref[...])


def odd_out(ys, p, hy, x, d_skip, w_glu, w_out, layer, g, *, tm):
    b, l, d = x.shape
    tok = lambda w: pl.BlockSpec((1, tm, w), lambda bi, i: (bi, i, 0))
    hy_spec = pl.BlockSpec((None, D_HYENA // LANES, DFT_TILES, tm // DFT_FAST * SUBLANES, LANES),
                           lambda bi, i: (bi, 0, 0, i, 0))
    return pl.pallas_call(
        _odd_out_kernel,
        grid=(b, l // tm),
        in_specs=[tok(D_S5), tok(D_S5), hy_spec, tok(d),
                  pl.BlockSpec((1, D_S5), lambda bi, i: (0, 0)),
                  _layer_spec((D_S5, D_S5), layer, lambda bi, i: (0, 0)),
                  _layer_spec((d, d), layer, lambda bi, i: (0, 0)),
                  pl.BlockSpec((1, d), lambda bi, i: (0, 0))],
        out_specs=tok(d),
        out_shape=jax.ShapeDtypeStruct(x.shape, F32),
        compiler_params=_cparams("parallel", "parallel"),
        name="odd_out",
    )(ys, p, hy, x, d_skip.reshape(1, D_S5), w_glu, w_out, g.reshape(1, d))


def kernel(x, mem, norm_mix, norm_xattn, norm_mem, norm_mlp, xa_wq, xa_wk, xa_wv, xa_wo, mlp_w1, mlp_w2, ev_w_in, ev_pool_w, ev_pool_scale, ev_conv_w, ev_w_out, od_w_in, od_s5_lambda_re, od_s5_lambda_im, od_s5_log_dt, od_s5_b_re, od_s5_b_im, od_s5_c_re, od_s5_c_im, od_s5_d, od_s5_w_glu, od_hy_short_w, od_hy_short_b, od_hy_w1, od_hy_b1, od_hy_w2, od_hy_b2, od_hy_w3, od_hy_freq, od_hy_bias, od_w_out):
    b, l, d = x.shape
    depth = norm_mix.shape[0]
    assert b == 2, "the long convolution packs the two batch rows as one complex signal"
    tables = dft_tables(l)
    mem2d = mem.reshape(b * N_MEM, d)
    wq, wo = xa_wq.astype(BF16), xa_wo.astype(BF16)
    wkv = jnp.concatenate([xa_wk, xa_wv], axis=2).astype(BF16)
    w1, w2 = mlp_w1.astype(BF16), mlp_w2.astype(BF16)
    ev_in, ev_out, ev_pool = ev_w_in.astype(BF16), ev_w_out.astype(BF16), ev_pool_w.astype(BF16)
    od_in, od_out, od_glu = od_w_in.astype(BF16), od_w_out.astype(BF16), od_s5_w_glu.astype(BF16)
    for i in range(depth):
        j = i // 2
        if i % 2 == 0:
            x = even_mixer(x, norm_mix[i, 0], ev_in, ev_pool, ev_pool_scale[j], ev_conv_w[j], ev_out, j,
                           norm_mix[i, 1], tm=1024)
        else:
            u_s5, gates = odd_in(x, norm_mix[i, 0], od_in, j, od_hy_short_w[j], od_hy_short_b[j], tm=1024)
            mats = s5_matrices(od_s5_lambda_re[j], od_s5_lambda_im[j], od_s5_log_dt[j], od_s5_b_re[j],
                               od_s5_b_im[j], od_s5_c_re[j], od_s5_c_im[j])
            ys = s5_core(u_s5, mats)
            kf = hyena_filter_spectra(l, tables, od_hy_w1[j], od_hy_b1[j], od_hy_w2[j], od_hy_b2[j],
                                      od_hy_w3[j], od_hy_freq[j])
            hy = hyena_mixer(gates, kf, od_hy_bias[j], tables)
            x = odd_out(ys, u_s5, hy, x, od_s5_d[j], od_glu, od_out, j, norm_mix[i, 1], tm=1024)
        kv = norm_matmul(mem2d, norm_mem[i], wkv, i, tm=b * N_MEM, out_dtype=BF16).reshape(b, N_MEM, 2 * d)
        x = xattn_block(x, kv, wq, wo, i, norm_xattn[i, 0], norm_xattn[i, 1], tm=1024)
        x = mlp_block(x.reshape(b * l, d), w1, w2, i, norm_mlp[i, 0], norm_mlp[i, 1],
                      tm=1024, tf=512).reshape(b, l, d)
    return x
```

```python
import functools
import math

import numpy as np
import jax
import jax.numpy as jnp
from jax import lax
from jax.experimental import pallas as pl
from jax.experimental.pallas import tpu as pltpu

F32 = jnp.float32
BF16 = jnp.bfloat16

D_MODEL = 1024
N_MEM = 256
RMS_EPS = 1e-6
D_POOL = 512
POOL_WINDOWS = (2, 4, 8, 16)
POOL_GROUP_DIM = 128
POOL_HALO = 8
D_CONV = 512
D_S5 = 512
S5_GROUP_DIM = 16
S5_GROUPS = 32
S5_STATE = 64
S5_CHUNK = 16
D_HYENA = 512
HYENA_BANDS = 16
HYENA_FFN = 64
HYENA_TARGET = 1e-2
HYENA_SHORT_DECAY_PCT = 0.3
HYENA_LONG_DECAY_PCT = 1.5
XA_HEADS = 4
XA_HEAD_DIM = 256
D_FF = 4096

LANES = 128
SUBLANES = 8
S5_SUPER = LANES // S5_GROUP_DIM
DFT_FAST = 128
DFT_TILES = DFT_FAST // SUBLANES
DFT_PITCH = DFT_FAST + SUBLANES
DFT_UNROLL = 16
VMEM_LIMIT = 56 * 1024 * 1024
VMEM_LIMIT_CONV = 62 * 1024 * 1024


def _cparams(*sem, vmem=VMEM_LIMIT):
    return pltpu.CompilerParams(dimension_semantics=sem, vmem_limit_bytes=vmem)


def _rms(xf, g):
    ms = jnp.mean(xf * xf, axis=-1, keepdims=True)
    return xf * lax.rsqrt(ms + RMS_EPS) * g


def _dot(a, b):
    return jnp.dot(a, b, preferred_element_type=F32)


def _layer_spec(block, layer, tail_map):
    return pl.BlockSpec((None,) + block, lambda *idx: (layer,) + tail_map(*idx))


def _norm_matmul_kernel(x_ref, g_ref, w_ref, o_ref):
    xn = _rms(x_ref[...], g_ref[...]).astype(BF16)
    o_ref[...] = _dot(xn, w_ref[...]).astype(o_ref.dtype)


def norm_matmul(x2d, g, w, layer, *, tm, out_dtype=F32):
    m, d = x2d.shape
    n = w.shape[-1]
    return pl.pallas_call(
        _norm_matmul_kernel,
        grid=(m // tm,),
        in_specs=[
            pl.BlockSpec((tm, d), lambda i: (i, 0)),
            pl.BlockSpec((1, d), lambda i: (0, 0)),
            _layer_spec((d, n), layer, lambda i: (0, 0)),
        ],
        out_specs=pl.BlockSpec((tm, n), lambda i: (i, 0)),
        out_shape=jax.ShapeDtypeStruct((m, n), out_dtype),
        compiler_params=_cparams("parallel"),
        name="norm_matmul",
    )(x2d, g.reshape(1, d), w)


def _xattn_kernel(x_ref, kv_ref, wq_ref, wo_ref, g1_ref, g2_ref, o_ref):
    x = x_ref[0]
    xn = _rms(x, g1_ref[...]).astype(BF16)
    q = (_dot(xn, wq_ref[...]) * (XA_HEAD_DIM ** -0.5)).astype(BF16)
    heads = []
    for h in range(XA_HEADS):
        lo = h * XA_HEAD_DIM
        qh = q[:, lo:lo + XA_HEAD_DIM]
        kh = kv_ref[0, :, lo:lo + XA_HEAD_DIM]
        vh = kv_ref[0, :, D_MODEL + lo:D_MODEL + lo + XA_HEAD_DIM]
        s = lax.dot_general(qh, kh, (((1,), (1,)), ((), ())), preferred_element_type=F32)
        e = jnp.exp(s - jnp.max(s, axis=-1, keepdims=True))
        p = e / jnp.sum(e, axis=-1, keepdims=True)
        heads.append(_dot(p.astype(BF16), vh).astype(BF16))
    o = jnp.concatenate(heads, axis=-1)
    y = _dot(o, wo_ref[...])
    o_ref[0] = x + _rms(y, g2_ref[...])


def xattn_block(x, kv, wq, wo, layer, g1, g2, *, tm):
    b, l, d = x.shape
    return pl.pallas_call(
        _xattn_kernel,
        grid=(b, l // tm),
        in_specs=[
            pl.BlockSpec((1, tm, d), lambda bi, i: (bi, i, 0)),
            pl.BlockSpec((1, N_MEM, 2 * d), lambda bi, i: (bi, 0, 0)),
            _layer_spec((d, d), layer, lambda bi, i: (0, 0)),
            _layer_spec((d, d), layer, lambda bi, i: (0, 0)),
            pl.BlockSpec((1, d), lambda bi, i: (0, 0)),
            pl.BlockSpec((1, d), lambda bi, i: (0, 0)),
        ],
        out_specs=pl.BlockSpec((1, tm, d), lambda bi, i: (bi, i, 0)),
        out_shape=jax.ShapeDtypeStruct(x.shape, F32),
        compiler_params=_cparams("parallel", "parallel"),
        name="xattn_block",
    )(x, kv, wq, wo, g1.reshape(1, d), g2.reshape(1, d))


def _mlp_kernel(x_ref, w1_ref, w2_ref, g1_ref, g2_ref, o_ref, h_ref, *, tf):
    x = x_ref[...]
    xn = _rms(x, g1_ref[...]).astype(BF16)
    for c in range(h_ref.shape[1] // tf):
        cols = slice(c * tf, (c + 1) * tf)
        h = jnp.maximum(_dot(xn, w1_ref[:, cols]), 0.0)
        h_ref[:, cols] = (h * h).astype(BF16)
    o_ref[...] = x + _rms(_dot(h_ref[...], w2_ref[...]), g2_ref[...])


def mlp_block(x2d, w1, w2, layer, g1, g2, *, tm, tf):
    m, d = x2d.shape
    ff = w1.shape[-1]
    resident = dict(pipeline_mode=pl.Buffered(1))
    return pl.pallas_call(
        functools.partial(_mlp_kernel, tf=tf),
        grid=(m // tm,),
        in_specs=[
            pl.BlockSpec((tm, d), lambda i: (i, 0)),
            pl.BlockSpec((None, d, ff), lambda i: (layer, 0, 0), **resident),
            pl.BlockSpec((None, ff, d), lambda i: (layer, 0, 0), **resident),
            pl.BlockSpec((1, d), lambda i: (0, 0)),
            pl.BlockSpec((1, d), lambda i: (0, 0)),
        ],
        out_specs=pl.BlockSpec((tm, d), lambda i: (i, 0)),
        out_shape=jax.ShapeDtypeStruct((m, d), F32),
        scratch_shapes=[pltpu.VMEM((tm, ff), BF16)],
        compiler_params=_cparams("parallel"),
        name="mlp_block",
    )(x2d, w1, w2, g1.reshape(1, d), g2.reshape(1, d))


def _halo_specs(tm, seq_len, width, col):
    r = tm // POOL_HALO
    last = seq_len // POOL_HALO - 1

    def prev_map(bi, i, *_):
        return (bi, jnp.maximum(i * r - 1, 0), col(*_))

    def main_map(bi, i, *_):
        return (bi, i, col(*_))

    def next_map(bi, i, *_):
        return (bi, jnp.minimum((i + 1) * r, last), col(*_))

    return [
        pl.BlockSpec((1, POOL_HALO, width), prev_map),
        pl.BlockSpec((1, tm, width), main_map),
        pl.BlockSpec((1, POOL_HALO, width), next_map),
    ]


def _with_halo(prev_ref, main_ref, next_ref):
    i = pl.program_id(1)
    prev = jnp.where(i > 0, prev_ref[0], 0.0)
    nxt = jnp.where(i < pl.num_programs(1) - 1, next_ref[0], 0.0)
    return jnp.concatenate([prev, main_ref[0], nxt], axis=0)


def _shift_rows(v, k):
    return pltpu.roll(v, k % v.shape[0], 0)


def _even_mixer_kernel(xp_ref, x_ref, xn_ref, gi_ref, wi_ref, wg_ref, ps_ref, cw_ref, wo_ref, g_ref, o_ref,
                       *, tm, seq_len):
    x_ext = _with_halo(xp_ref, x_ref, xn_ref)
    ext = _dot(_rms(x_ext, gi_ref[...]).astype(BF16), wi_ref[...])
    lo, hi = POOL_HALO, POOL_HALO + tm
    t = pl.program_id(1) * tm + lax.broadcasted_iota(jnp.int32, (tm, 1), 0)
    parts = []
    for gi, win in enumerate(POOL_WINDOWS):
        half = win // 2
        u = ext[:, gi * POOL_GROUP_DIM:(gi + 1) * POOL_GROUP_DIM]
        s = u + _shift_rows(u, 1)
        step = 1
        while 2 * step < win:
            s = _shift_rows(s, step) + _shift_rows(s, -step)
            step *= 2
        cnt = (jnp.minimum(t + half, seq_len) - jnp.maximum(t - half, 0)).astype(F32)
        pooled = s[lo:hi] / cnt - u[lo:hi]
        y = _dot(pooled.astype(BF16), wg_ref[gi])
        parts.append((y * ps_ref[:, gi * POOL_GROUP_DIM:(gi + 1) * POOL_GROUP_DIM]).astype(BF16))
    b_gate = ext[lo:hi, D_POOL:D_POOL + D_CONV]
    ch = ext[:, D_POOL + D_CONV:D_POOL + 2 * D_CONV] * ext[:, D_POOL + 2 * D_CONV:D_POOL + 3 * D_CONV]
    conv = cw_ref[0:1, :] * _shift_rows(ch, 1) + cw_ref[1:2, :] * ch + cw_ref[2:3, :] * _shift_rows(ch, -1)
    parts.append((b_gate * conv[lo:hi]).astype(BF16))
    mix = _dot(jnp.concatenate(parts, axis=-1), wo_ref[...])
    o_ref[0] = x_ref[0] + _rms(mix, g_ref[...])


def even_mixer(x, g_in, w_in, w_group, pool_scale, conv_w, w_out, layer, g, *, tm):
    b, l, d = x.shape
    kern = functools.partial(_even_mixer_kernel, tm=tm, seq_len=l)
    return pl.pallas_call(
        kern,
        grid=(b, l // tm),
        in_specs=_halo_specs(tm, l, d, lambda: 0) + [
            pl.BlockSpec((1, d), lambda bi, i: (0, 0)),
            _layer_spec(w_in.shape[1:], layer, lambda bi, i: (0, 0)),
            _layer_spec(w_group.shape[1:], layer, lambda bi, i: (0, 0, 0)),
            pl.BlockSpec((1, D_POOL), lambda bi, i: (0, 0)),
            pl.BlockSpec((3, D_CONV), lambda bi, i: (0, 0)),
            _layer_spec((d, d), layer, lambda bi, i: (0, 0)),
            pl.BlockSpec((1, d), lambda bi, i: (0, 0)),
        ],
        out_specs=pl.BlockSpec((1, tm, d), lambda bi, i: (bi, i, 0)),
        out_shape=jax.ShapeDtypeStruct(x.shape, F32),
        compiler_params=_cparams("parallel", "parallel"),
        name="even_mixer",
    )(x, x, x, g_in.reshape(1, d), w_in, w_group, pool_scale.reshape(1, D_POOL), conv_w, w_out, g.reshape(1, d))


def _cmul(ar, ai, br, bi):
    return ar * br - ai * bi, ar * bi + ai * br


def s5_matrices(lam_re, lam_im, log_dt, b_re, b_im, c_re, c_im):
    tc, hd, n = S5_CHUNK, S5_GROUP_DIM, S5_STATE
    hp = lax.Precision.HIGHEST
    lr = jnp.minimum(lam_re, -1e-4)
    li = lam_im
    dt = jnp.exp(log_dt)[..., None]
    taus = jnp.arange(tc + 1, dtype=F32)[:, None, None, None]
    rmag = jnp.exp(lr[None] * dt[None] * (tc - taus))
    rang = li[None] * dt[None] * (tc - taus)
    pwr_re, pwr_im = rmag * jnp.cos(rang), rmag * jnp.sin(rang)
    mag = jnp.exp(lr[None] * dt[None] * taus)
    ang = li[None] * dt[None] * taus
    pw_re, pw_im = mag * jnp.cos(ang), mag * jnp.sin(ang)
    nr, ni = pw_re[1] - 1.0, pw_im[1]
    den = lr * lr + li * li
    coef_re, coef_im = (nr * lr + ni * li) / den, (ni * lr - nr * li) / den
    cb_re, cb_im = _cmul(coef_re[..., None], coef_im[..., None], b_re[None], b_im[None])
    cbt_re, cbt_im = _cmul(coef_re[:, :, None, :], coef_im[:, :, None, :],
                           jnp.swapaxes(b_re, 1, 2)[None], jnp.swapaxes(b_im, 1, 2)[None])
    def lag_kernels(d, p_re, p_im):
        q_re, q_im = _cmul(p_re[..., None], p_im[..., None], cb_re[d][None], cb_im[d][None])
        c_cat = jnp.concatenate([c_re[d], -c_im[d]], axis=-1)
        return jnp.einsum('gkn,tgnj->gjtk', c_cat, jnp.concatenate([q_re, q_im], axis=2), precision=hp)

    m_fwd = lag_kernels(0, pw_re[:tc, 0], pw_im[:tc, 0])
    m_bwd = lag_kernels(1, pwr_re[1:tc + 1, 1], pwr_im[1:tc + 1, 1])
    lag_table = jnp.concatenate([m_bwd[:, :, :tc - 1], m_bwd[:, :, tc - 1:] + m_fwd[:, :, :1], m_fwd[:, :, 1:]],
                                axis=2)
    intra = jnp.stack([lag_table[:, :, tc - 1 - s:2 * tc - 1 - s] for s in range(tc)], axis=1)
    intra = intra.reshape(S5_GROUPS, tc * hd, tc * hd)
    sf_re, sf_im = _cmul(pwr_re[1:tc + 1, 0][:, :, None, :], pwr_im[1:tc + 1, 0][:, :, None, :],
                         cbt_re[0][None], cbt_im[0][None])
    sb_re, sb_im = _cmul(pw_re[:tc, 1][:, :, None, :], pw_im[:tc, 1][:, :, None, :],
                         cbt_re[1][None], cbt_im[1][None])
    summ = jnp.concatenate([sf_re, sb_re, sf_im, sb_im], axis=-1)
    summ = jnp.swapaxes(summ, 0, 1).reshape(S5_GROUPS, tc * hd, 4 * n)
    ct_re, ct_im = jnp.swapaxes(c_re, 2, 3), jnp.swapaxes(c_im, 2, 3)
    gnt = lambda z: jnp.transpose(z, (1, 2, 0))[..., None]
    qf_re, qf_im = _cmul(ct_re[0][:, :, None, :], ct_im[0][:, :, None, :],
                         gnt(pw_re[1:tc + 1, 0]), gnt(pw_im[1:tc + 1, 0]))
    qb_re, qb_im = _cmul(ct_re[1][:, :, None, :], ct_im[1][:, :, None, :], gnt(pwr_re[:tc, 1]), gnt(pwr_im[:tc, 1]))
    carry = jnp.concatenate([qf_re, qb_re, -qf_im, -qb_im], axis=1)
    carry = carry.reshape(S5_GROUPS, 4 * n, tc * hd)
    a_re = jnp.concatenate([pw_re[tc, 0], pw_re[tc, 1]], axis=-1)
    a_im = jnp.concatenate([pw_im[tc, 0], pw_im[tc, 1]], axis=-1)
    return intra, summ, carry, a_re, a_im


def s5_regroup_matrix():
    n = S5_SUPER * LANES
    src = np.arange(n)
    tb, g, h = src // LANES, (src % LANES) // S5_GROUP_DIM, src % S5_GROUP_DIM
    perm = np.zeros((n, n), np.float32)
    perm[src, g * LANES + tb * S5_GROUP_DIM + h] = 1.0
    return jnp.asarray(perm, BF16)


def _s5_regroup_kernel(p_ref, w_ref, perm_ref, u_ref, sr_ref, si_ref, *, n_chunks):
    n2 = 2 * S5_STATE
    for q in range(S5_CHUNK // S5_SUPER):
        rows_t = [p_ref[pl.ds(q * S5_SUPER + tb, n_chunks, stride=S5_CHUNK), :].astype(BF16) for tb in range(S5_SUPER)]
        grouped = _dot(jnp.concatenate(rows_t, axis=-1), perm_ref[...]).astype(BF16)
        for g in range(S5_SUPER):
            u_ref[g, :, q * LANES:(q + 1) * LANES] = grouped[:, g * LANES:(g + 1) * LANES]
    for g in range(S5_SUPER):
        s = _dot(u_ref[g], w_ref[g])
        sr_ref[pl.ds(g, n_chunks, stride=S5_SUPER), :] = s[:, :n2]
        si_ref[pl.ds(g, n_chunks, stride=S5_SUPER), :] = s[:, n2:]


def _s5_state_spec(c, n2, n_tiles):
    return pl.BlockSpec((None, c * S5_SUPER, n2), lambda a, bi: (bi * n_tiles + a, 0, 0))


def _s5_group_spec(rows, cols):
    return pl.BlockSpec((S5_SUPER, rows, cols), lambda a, bi: (a, 0, 0))


def s5_summary(p, summ):
    b, l, _ = p.shape
    c = l // S5_CHUNK
    n_tiles = S5_GROUPS // S5_SUPER
    k = S5_CHUNK * S5_GROUP_DIM
    n2 = 2 * S5_STATE
    shape = jax.ShapeDtypeStruct((b * n_tiles, c * S5_SUPER, n2), F32)
    kern = functools.partial(_s5_regroup_kernel, n_chunks=c)
    perm = s5_regroup_matrix()
    return pl.pallas_call(
        kern,
        grid=(n_tiles, b),
        in_specs=[
            pl.BlockSpec((None, l, LANES), lambda a, bi: (bi, 0, a)),
            _s5_group_spec(k, 2 * n2),
            pl.BlockSpec(perm.shape, lambda a, bi: (0, 0)),
        ],
        out_specs=[pl.BlockSpec((None, S5_SUPER, c, k), lambda a, bi: (bi * n_tiles + a, 0, 0, 0)),
                   _s5_state_spec(c, n2, n_tiles), _s5_state_spec(c, n2, n_tiles)],
        out_shape=[jax.ShapeDtypeStruct((b * n_tiles, S5_SUPER, c, k), BF16), shape, shape],
        compiler_params=_cparams("parallel", "parallel"),
        name="s5_summary",
    )(p, summ, perm)


def _s5_scan(sr_ref, si_ref, are_ref, aim_ref, or_ref, oi_ref, n_chunks):
    n2 = 2 * S5_STATE
    a_re = are_ref[...]
    a_im = aim_ref[...]
    fwd_lane = lax.broadcasted_iota(jnp.int32, (SUBLANES, n2), 1) < S5_STATE

    def body(i, state):
        x_re, x_im = state
        rows_i = pl.ds(pl.multiple_of(i * SUBLANES, SUBLANES), SUBLANES)
        rows_r = pl.ds(pl.multiple_of((n_chunks - 1 - i) * SUBLANES, SUBLANES), SUBLANES)
        or_ref[rows_i, 0:S5_STATE] = x_re[:, :S5_STATE]
        oi_ref[rows_i, 0:S5_STATE] = x_im[:, :S5_STATE]
        or_ref[rows_r, S5_STATE:n2] = x_re[:, S5_STATE:]
        oi_ref[rows_r, S5_STATE:n2] = x_im[:, S5_STATE:]
        s_re = jnp.where(fwd_lane, sr_ref[rows_i, :], sr_ref[rows_r, :])
        s_im = jnp.where(fwd_lane, si_ref[rows_i, :], si_ref[rows_r, :])
        return (a_re * x_re - a_im * x_im + s_re, a_re * x_im + a_im * x_re + s_im)

    zero = jnp.zeros((SUBLANES, n2), F32)
    lax.fori_loop(0, n_chunks, body, (zero, zero))


def _s5_output_kernel(u_ref, sr_ref, si_ref, are_ref, aim_ref, wi_ref, wc_ref, perm_ref, o_ref,
                      y_ref, xr_ref, xi_ref, *, n_chunks):
    _s5_scan(sr_ref, si_ref, are_ref, aim_ref, xr_ref, xi_ref, n_chunks)
    for g in range(S5_SUPER):
        rows = pl.ds(g, n_chunks, stride=S5_SUPER)
        xin = jnp.concatenate([xr_ref[rows, :], xi_ref[rows, :]], axis=-1).astype(BF16)
        y_ref[g] = _dot(u_ref[g], wi_ref[g]) + _dot(xin, wc_ref[g])
    back = (((1,), (1,)), ((), ()))
    for q in range(S5_CHUNK // S5_SUPER):
        y = jnp.concatenate([y_ref[g, :, q * LANES:(q + 1) * LANES] for g in range(S5_SUPER)], axis=-1)
        hi = y.astype(BF16)
        lo = (y - hi.astype(F32)).astype(BF16)
        tok = (lax.dot_general(hi, perm_ref[...], back, preferred_element_type=F32)
               + lax.dot_general(lo, perm_ref[...], back, preferred_element_type=F32))
        for tb in range(S5_SUPER):
            o_ref[pl.ds(q * S5_SUPER + tb, n_chunks, stride=S5_CHUNK), :] = tok[:, tb * LANES:(tb + 1) * LANES]


def s5_output(u, s_re, s_im, a_re, a_im, intra, carry, *, batch):
    tiles, _, c, k = u.shape
    n_tiles = tiles // batch
    n2 = 2 * S5_STATE
    l = c * S5_CHUNK
    kern = functools.partial(_s5_output_kernel, n_chunks=c)
    perm = s5_regroup_matrix()
    decay_spec = pl.BlockSpec((S5_SUPER, n2), lambda a, bi: (a, 0))
    state_scr = pltpu.VMEM((c * S5_SUPER, n2), F32)
    return pl.pallas_call(
        kern,
        grid=(n_tiles, batch),
        in_specs=[
            pl.BlockSpec((None, S5_SUPER, c, k), lambda a, bi: (bi * n_tiles + a, 0, 0, 0)),
            _s5_state_spec(c, n2, n_tiles),
            _s5_state_spec(c, n2, n_tiles),
            decay_spec,
            decay_spec,
            _s5_group_spec(k, k),
            _s5_group_spec(2 * n2, k),
            pl.BlockSpec(perm.shape, lambda a, bi: (0, 0)),
        ],
        out_specs=pl.BlockSpec((None, l, LANES), lambda a, bi: (bi, 0, a)),
        out_shape=jax.ShapeDtypeStruct((batch, l, D_S5), F32),
        scratch_shapes=[pltpu.VMEM((S5_SUPER, c, k), F32), state_scr, state_scr],
        compiler_params=_cparams("parallel", "parallel"),
        name="s5_output",
    )(u, s_re, s_im, a_re, a_im, intra, carry, perm)


def s5_core(p, mats):
    intra, summ, carry, a_re, a_im = mats
    u, s_re, s_im = s5_summary(p, summ.astype(BF16))
    return s5_output(u, s_re, s_im, a_re, a_im, intra.astype(BF16), carry.astype(BF16), batch=p.shape[0])


def _to_time_tiles(y):
    nf, c = DFT_FAST, y.shape[-1]
    slabs = [y[s * nf:(s + 1) * nf].reshape(DFT_TILES, SUBLANES, c) for s in range(y.shape[0] // nf)]
    return jnp.concatenate(slabs, axis=1)


def _from_time_tiles(v):
    c = v.shape[-1]
    slabs = [v[:, s * SUBLANES:(s + 1) * SUBLANES, :].reshape(DFT_FAST, c) for s in range(v.shape[1] // SUBLANES)]
    return jnp.concatenate(slabs, axis=0)


def _odd_in_kernel(xp_ref, x_ref, xn_ref, g_ref, w_ref, cw_ref, cb_ref, s5_ref, gates_ref, *, tm):
    ext = _with_halo(xp_ref, x_ref, xn_ref)
    p = _dot(_rms(ext, g_ref[...]).astype(BF16), w_ref[...])
    lo, hi = POOL_HALO, POOL_HALO + tm
    s5_ref[0] = p[lo:hi, :D_S5]
    ph = p[:, D_S5:]
    y = cw_ref[0:1, :] * _shift_rows(ph, 1) + cw_ref[1:2, :] * ph + cw_ref[2:3, :] * _shift_rows(ph, -1)
    y = y[lo:hi] + cb_ref[...]
    for k in range(3):
        for ct in range(D_HYENA // LANES):
            col = k * D_HYENA + ct * LANES
            gates_ref[k, ct] = _to_time_tiles(y[:, col:col + LANES])


def odd_in(x, g, w, layer, conv_w, conv_b, *, tm):
    b, l, d = x.shape
    n = w.shape[-1]
    rows = tm // DFT_FAST * SUBLANES
    cts = D_HYENA // LANES
    kern = functools.partial(_odd_in_kernel, tm=tm)
    return pl.pallas_call(
        kern,
        grid=(b, l // tm),
        in_specs=_halo_specs(tm, l, d, lambda: 0) + [
            pl.BlockSpec((1, d), lambda bi, i: (0, 0)),
            _layer_spec((d, n), layer, lambda bi, i: (0, 0)),
            pl.BlockSpec((3, n - D_S5), lambda bi, i: (0, 0)),
            pl.BlockSpec((1, n - D_S5), lambda bi, i: (0, 0)),
        ],
        out_specs=[pl.BlockSpec((1, tm, D_S5), lambda bi, i: (bi, i, 0)),
                   pl.BlockSpec((3, None, cts, DFT_TILES, rows, LANES), lambda bi, i: (0, bi, 0, 0, i, 0))],
        out_shape=[jax.ShapeDtypeStruct((b, l, D_S5), F32),
                   jax.ShapeDtypeStruct((3, b, cts, DFT_TILES, l // DFT_FAST * SUBLANES, LANES), F32)],
        compiler_params=_cparams("parallel", "parallel"),
        name="odd_in",
    )(x, x, x, g.reshape(1, d), w, conv_w, conv_b.reshape(1, -1))


def _filter_mlp_kernel(bands_ref, w1t_ref, w1cs_ref, b1_ref, w2_ref, b2_ref, w3h_ref, w3l_ref, fr_ref, dl_ref,
                       eo_ref, asum_ref, *, tl, seq_len):
    i = pl.program_id(0)
    hp = lax.Precision.HIGHEST
    c = D_HYENA
    t = (i * tl + lax.broadcasted_iota(jnp.int32, (tl, 1), 0)).astype(F32)
    t_norm = t / (seq_len - 1.0)
    ang = (2.0 * math.pi / seq_len) * t * bands_ref[...]
    is_cos = lax.broadcasted_iota(jnp.int32, (1, ang.shape[1]), 1) < HYENA_BANDS
    cs = jnp.sin(ang + jnp.where(is_cos, 0.5 * math.pi, math.pi))
    fr = fr_ref[...]
    z = t_norm * w1t_ref[...] + jnp.dot(cs, w1cs_ref[...], precision=hp, preferred_element_type=F32)
    h = jnp.sin(fr * (z + b1_ref[...]))
    h = jnp.sin(fr * (jnp.dot(h, w2_ref[...], precision=hp, preferred_element_type=F32) + b2_ref[...]))
    h_hi = h.astype(BF16)
    h_lo = (h - h_hi.astype(F32)).astype(BF16)
    h = _dot(h_hi, w3h_ref[...]) + (_dot(h_lo, w3h_ref[...]) + _dot(h_hi, w3l_ref[...]))
    decay = jnp.exp(-t_norm * dl_ref[...])

    @pl.when(i == 0)
    def _():
        asum_ref[...] = jnp.zeros_like(asum_ref)

    for o in range(h.shape[1] // (2 * c)):
        fwd = h[:, 2 * o * c:(2 * o + 1) * c] * decay
        bwd = h[:, (2 * o + 1) * c:(2 * o + 2) * c] * decay
        eo_ref[:, 2 * o * c:(2 * o + 1) * c] = fwd + bwd
        eo_ref[:, (2 * o + 1) * c:(2 * o + 2) * c] = fwd - bwd
        asum_ref[:, o * c:(o + 1) * c] += jnp.sum(jnp.abs(fwd) + jnp.abs(bwd), axis=0, keepdims=True)


def hyena_filter_mlp(seq_len, w1, b1, w2, b2, w3, freq, *, tl):
    n_out = w3.shape[1]
    bands = jnp.linspace(1e-4, HYENA_BANDS - 1, HYENA_BANDS, dtype=F32)
    bands2 = jnp.concatenate([bands, bands])[None, :]
    deltas = jnp.abs(jnp.linspace(math.log(HYENA_TARGET) / HYENA_LONG_DECAY_PCT,
                                  math.log(HYENA_TARGET) / HYENA_SHORT_DECAY_PCT, D_HYENA, dtype=F32))[None, :]
    kern = functools.partial(_filter_mlp_kernel, tl=tl, seq_len=seq_len)
    full = lambda a: pl.BlockSpec(a.shape, lambda i: (0,) * a.ndim)
    w3_hi = w3.astype(BF16)
    w3_lo = (w3 - w3_hi.astype(F32)).astype(BF16)
    args = [bands2, w1[0:1], w1[1:], b1[None, :], w2, b2[None, :], w3_hi, w3_lo, freq[None, :], deltas]
    return pl.pallas_call(
        kern,
        grid=(seq_len // tl,),
        in_specs=[full(a) for a in args],
        out_specs=[pl.BlockSpec((tl, n_out), lambda i: (i, 0)), pl.BlockSpec((1, n_out // 2), lambda i: (0, 0))],
        out_shape=[jax.ShapeDtypeStruct((seq_len, n_out), F32), jax.ShapeDtypeStruct((1, n_out // 2), F32)],
        compiler_params=_cparams("arbitrary"),
        name="hyena_filter_mlp",
    )(*args)


def dft_tables(seq_len):
    n = 2 * seq_len
    nf = DFT_FAST
    ns = n // nf
    k_a = np.arange(ns)[:, None]
    n_s = np.arange(ns // 2)[None, :]
    w_s = np.exp(-2j * np.pi * ((k_a * n_s) % ns) / ns)
    n_f = np.arange(nf)[:, None]
    tw = np.exp(-2j * np.pi * ((n_f * np.arange(ns)[None, :]) % n) / n)
    c32 = lambda z: (jnp.asarray(z.real, F32), jnp.asarray(z.imag, F32))
    ws_re, ws_im = c32(w_s)
    tw_re, tw_im = c32(tw)
    g_re = tw_re[:, :, None] * ws_re[None] - tw_im[:, :, None] * ws_im[None]
    g_im = tw_re[:, :, None] * ws_im[None] + tw_im[:, :, None] * ws_re[None]
    fwd_a = jnp.concatenate([jnp.concatenate([g_re, -g_im], axis=2),
                             jnp.concatenate([g_im, g_re], axis=2)], axis=1)
    gt_re = jnp.swapaxes(g_re, 1, 2) * (1.0 / n)
    gt_im = jnp.swapaxes(g_im, 1, 2) * (1.0 / n)
    inv_a = jnp.concatenate([jnp.concatenate([gt_re, gt_im], axis=2),
                             jnp.concatenate([-gt_im, gt_re], axis=2)], axis=1)
    kk = np.arange(nf)
    w_f = np.exp(-2j * np.pi * ((kk[:, None] * kk[None, :]) % nf) / nf)
    f_re, f_im = c32(w_f)
    fwd_b = jnp.concatenate([jnp.concatenate([f_re, -f_im], axis=1),
                             jnp.concatenate([f_im, f_re], axis=1)], axis=0)
    inv_b = jnp.concatenate([jnp.concatenate([f_re, f_im], axis=1),
                             jnp.concatenate([-f_im, f_re], axis=1)], axis=0)
    return dict(fwd_a=fwd_a.astype(BF16), fwd_a_real=fwd_a[:, :, :ns // 2].astype(BF16),
                inv_a=inv_a.astype(BF16), fwd_b=fwd_b.astype(BF16), inv_b=inv_b.astype(BF16),
                fwd_b_part=fwd_b.reshape(2, nf, 2 * nf).astype(BF16))


def _filter_dft_kernel(x_ref, ga_ref, fb_ref, nrm_ref, k_ref, ar_scr, ai_scr):
    nf = DFT_FAST
    ns = ga_ref.shape[1] // 2

    def stage_a(j, carry):
        x = x_ref[pl.ds(j, ns // 2, stride=nf), :].astype(BF16)
        a = _dot(ga_ref[j], x)
        ar_scr[pl.ds(j, ns, stride=DFT_PITCH), :] = a[:ns]
        ai_scr[pl.ds(j, ns, stride=DFT_PITCH), :] = a[ns:]
        return carry

    lax.fori_loop(0, nf, stage_a, 0, unroll=DFT_UNROLL)
    inv_norm = 1.0 / (nrm_ref[...] + 1e-6)

    def stage_b(k, carry):
        src = pl.multiple_of(k * DFT_PITCH, SUBLANES)
        a = jnp.concatenate([ar_scr[pl.ds(src, nf), :], ai_scr[pl.ds(src, nf), :]], axis=0).astype(BF16)
        dst = pl.multiple_of(k * nf, nf)
        k_ref[pl.ds(dst, nf), :] = _dot(fb_ref[...], a) * inv_norm
        return carry

    lax.fori_loop(0, ns, stage_b, 0, unroll=DFT_UNROLL)


def hyena_filter_dft(eo, asum, tables):
    l, cols = eo.shape
    c = D_HYENA
    orders = cols // (2 * c)
    ct = c // LANES
    ga, fb = tables['fwd_a_real'], tables['fwd_b_part']
    ns = ga.shape[1] // 2
    scr = pltpu.VMEM((ns * DFT_PITCH, LANES), F32)
    return pl.pallas_call(
        _filter_dft_kernel,
        grid=(orders, ct, 2),
        in_specs=[
            pl.BlockSpec((l, LANES), lambda o, j, part: (0, (2 * o + part) * ct + j)),
            pl.BlockSpec(ga.shape, lambda o, j, part: (0, 0, 0)),
            pl.BlockSpec((None,) + fb.shape[1:], lambda o, j, part: (part, 0, 0)),
            pl.BlockSpec((1, LANES), lambda o, j, part: (0, o * ct + j)),
        ],
        out_specs=pl.BlockSpec((None, None, None, 2 * l, LANES), lambda o, j, part: (o, part, j, 0, 0)),
        out_shape=jax.ShapeDtypeStruct((orders, 2, ct, 2 * l, LANES), F32),
        scratch_shapes=[scr, scr],
        compiler_params=_cparams("parallel", "parallel", "parallel"),
        name="hyena_filter_dft",
    )(eo, ga, fb, asum)


def _lane_cat(parts):
    return parts[0] if len(parts) == 1 else jnp.concatenate(parts, axis=-1)


def _hyena_conv_kernel(x_ref, ga_ref, k_ref, fb_ref, ib_ref, gi_ref, gate_ref, u_ref, bias_ref, o_ref, *scr,
                       n_a, n_b, slabs):
    nf = DFT_FAST
    tiles, qt = x_ref.shape[1], x_ref.shape[2]
    half = x_ref.shape[3] // SUBLANES
    ns = 2 * half
    step = pl.program_id(1)

    @pl.when(step < n_a)
    def _():
        for qj in range(qt * SUBLANES):
            q, j = divmod(qj, SUBLANES)
            rows = pl.ds(j, half, stride=SUBLANES)
            x = _lane_cat([jnp.concatenate([x_ref[0, t, q, rows, :], x_ref[1, t, q, rows, :]], axis=0)
                           for t in range(tiles)])
            a = _dot(ga_ref[qj], x.astype(BF16))
            dst = pl.ds(step * (qt * SUBLANES) + qj, ns, stride=DFT_PITCH)
            for t in range(tiles):
                scr[2 * t][dst, :] = a[:ns, t * LANES:(t + 1) * LANES]
                scr[2 * t + 1][dst, :] = a[ns:, t * LANES:(t + 1) * LANES]

    @pl.when((step >= n_a) & (step < n_a + n_b))
    def _():
        for s in range(slabs):
            k_a = (step - n_a) * slabs + s
            rows = pl.ds(pl.multiple_of(k_a * DFT_PITCH, SUBLANES), nf)
            a = jnp.concatenate([_lane_cat([scr[2 * t + part][rows, :] for t in range(tiles)]) for part in range(2)], axis=0)
            x = _dot(fb_ref[...], a.astype(BF16))
            xr, xi = x[:nf], x[nf:]
            kr = _lane_cat([k_ref[0, t, s * nf:(s + 1) * nf, :] for t in range(tiles)])
            ki = _lane_cat([k_ref[1, t, s * nf:(s + 1) * nf, :] for t in range(tiles)])
            y = jnp.concatenate([xr * kr - xi * ki, xr * ki + xi * kr], axis=0).astype(BF16)
            cc = _dot(ib_ref[...], y)
            for t in range(tiles):
                scr[2 * t][rows, :] = cc[:nf, t * LANES:(t + 1) * LANES]
                scr[2 * t + 1][rows, :] = cc[nf:, t * LANES:(t + 1) * LANES]

    @pl.when(step >= n_a + n_b)
    def _():
        first = (step - (n_a + n_b)) * (qt * SUBLANES)
        for qj in range(qt * SUBLANES):
            q, j = divmod(qj, SUBLANES)
            src = pl.ds(first + qj, ns, stride=DFT_PITCH)
            cc = jnp.concatenate([_lane_cat([scr[2 * t + part][src, :] for t in range(tiles)]) for part in range(2)], axis=0)
            y = _dot(gi_ref[qj], cc.astype(BF16))
            dst = pl.ds(j, half, stride=SUBLANES)
            for t in range(tiles):
                o_ref[0, t, q, dst, :] = y[:half, t * LANES:(t + 1) * LANES]
                o_ref[1, t, q, dst, :] = y[half:, t * LANES:(t + 1) * LANES]
        for t in range(tiles):
            bias = bias_ref[:, t * LANES:(t + 1) * LANES]
            for bi in range(2):
                o_ref[bi, t] = gate_ref[bi, t] * (o_ref[bi, t] + u_ref[bi, t] * bias)


def hyena_long_conv(u6, u_idx, gate6, gate_idx, kf, order, bias, tables, *, slabs=8, tiles=2, qt=2):
    _, _, cts, q_tiles, rows, _ = u6.shape
    ns = 2 * rows // SUBLANES
    n_a = q_tiles // qt
    n_b = ns // slabs
    ga, gi, fb, ib = tables['fwd_a'], tables['inv_a'], tables['fwd_b'], tables['inv_b']
    q_fwd = lambda s: jnp.minimum(s, n_a - 1)
    q_mid = lambda s: jnp.clip(s - n_a, 0, n_b - 1)
    q_inv = lambda s: jnp.clip(s - n_a - n_b, 0, n_a - 1)

    def seq_spec(idx, q_of):
        return pl.BlockSpec((None, 2, tiles, qt, rows, LANES), lambda c, s: (idx, 0, c, q_of(s), 0, 0))

    scr = pltpu.VMEM((ns * DFT_PITCH, LANES), F32)
    kern = functools.partial(_hyena_conv_kernel, n_a=n_a, n_b=n_b, slabs=slabs)
    return pl.pallas_call(
        kern,
        grid=(cts // tiles, 2 * n_a + n_b),
        in_specs=[
            seq_spec(u_idx, q_fwd),
            pl.BlockSpec((qt * SUBLANES,) + ga.shape[1:], lambda c, s: (q_fwd(s), 0, 0)),
            pl.BlockSpec((None, 2, tiles, slabs * DFT_FAST, LANES), lambda c, s: (order, 0, c, q_mid(s), 0)),
            pl.BlockSpec(fb.shape, lambda c, s: (0, 0)),
            pl.BlockSpec(ib.shape, lambda c, s: (0, 0)),
            pl.BlockSpec((qt * SUBLANES,) + gi.shape[1:], lambda c, s: (q_inv(s), 0, 0)),
            seq_spec(gate_idx, q_inv),
            seq_spec(u_idx, q_inv),
            pl.BlockSpec((1, tiles * LANES), lambda c, s: (0, c)),
        ],
        out_specs=pl.BlockSpec((2, tiles, qt, rows, LANES), lambda c, s: (0, c, q_inv(s), 0, 0)),
        out_shape=jax.ShapeDtypeStruct((2, cts, q_tiles, rows, LANES), F32),
        scratch_shapes=[scr] * (2 * tiles),
        compiler_params=_cparams("arbitrary", "arbitrary", vmem=VMEM_LIMIT_CONV),
        name="hyena_long_conv",
    )(u6, ga, kf, fb, ib, gi, gate6, u6, bias.reshape(1, -1))


def hyena_filter_spectra(seq_len, tables, w1, b1, w2, b2, w3, freq):
    eo, asum = hyena_filter_mlp(seq_len, w1, b1, w2, b2, w3, freq, tl=512)
    return hyena_filter_dft(eo, asum, tables)


def hyena_mixer(gates, kf, bias, tables):
    z = hyena_long_conv(gates, 2, gates, 1, kf, 0, bias[0], tables)
    return hyena_long_conv(z[None], 0, gates, 0, kf, 1, bias[1], tables)


def _odd_out_kernel(ys_ref, u_ref, hy_ref, x_ref, d_ref, wglu_ref, wo_ref, g_ref, o_ref):
    y = ys_ref[0] + d_ref[...] * u_ref[0]
    c0 = math.sqrt(2.0 / math.pi)
    gl = 0.5 * y * (1.0 + jnp.tanh(c0 * (y + 0.044715 * (y * y * y))))
    z = _dot(gl.astype(BF16), wglu_ref[...])
    s5 = gl * (1.0 / (1.0 + jnp.exp(-z)))
    hy = [_from_time_tiles(hy_ref[ct]).astype(BF16) for ct in range(hy_ref.shape[0])]
    mix = _dot(jnp.concatenate([s5.astype(BF16)] + hy, axis=-1), wo_ref[...])
    o_ref[0] = x_ref[0] + _rms(mix, g_ref[...])


def odd_out(ys, u_s5, hy, x, d_skip, w_glu, w_out, layer, g, *, tm):
    b, l, d = x.shape
    tok = lambda w: pl.BlockSpec((1, tm, w), lambda bi, i: (bi, i, 0))
    hy_spec = pl.BlockSpec((None, D_HYENA // LANES, DFT_TILES, tm // DFT_FAST * SUBLANES, LANES),
                           lambda bi, i: (bi, 0, 0, i, 0))
    return pl.pallas_call(
        _odd_out_kernel,
        grid=(b, l // tm),
        in_specs=[tok(D_S5), tok(D_S5), hy_spec, tok(d),
                  pl.BlockSpec((1, D_S5), lambda bi, i: (0, 0)),
                  _layer_spec((D_S5, D_S5), layer, lambda bi, i: (0, 0)),
                  _layer_spec((d, d), layer, lambda bi, i: (0, 0)),
                  pl.BlockSpec((1, d), lambda bi, i: (0, 0))],
        out_specs=tok(d),
        out_shape=jax.ShapeDtypeStruct(x.shape, F32),
        compiler_params=_cparams("parallel", "parallel"),
        name="odd_out",
    )(ys, u_s5, hy, x, d_skip.reshape(1, D_S5), w_glu, w_out, g.reshape(1, d))


def kernel(x, mem, norm_mix, norm_xattn, norm_mem, norm_mlp, xa_wq, xa_wk, xa_wv, xa_wo, mlp_w1, mlp_w2, ev_w_in, ev_pool_w, ev_pool_scale, ev_conv_w, ev_w_out, od_w_in, od_s5_lambda_re, od_s5_lambda_im, od_s5_log_dt, od_s5_b_re, od_s5_b_im, od_s5_c_re, od_s5_c_im, od_s5_d, od_s5_w_glu, od_hy_short_w, od_hy_short_b, od_hy_w1, od_hy_b1, od_hy_w2, od_hy_b2, od_hy_w3, od_hy_freq, od_hy_bias, od_w_out):
    b, l, d = x.shape
    depth = norm_mix.shape[0]
    assert b == 2, "the long convolution packs the two batch rows as one complex signal"
    tables = dft_tables(l)
    mem2d = mem.reshape(b * N_MEM, d)
    wq, wo = xa_wq.astype(BF16), xa_wo.astype(BF16)
    wkv = jnp.concatenate([xa_wk, xa_wv], axis=2).astype(BF16)
    w1, w2 = mlp_w1.astype(BF16), mlp_w2.astype(BF16)
    ev_in, ev_out, ev_pool = ev_w_in.astype(BF16), ev_w_out.astype(BF16), ev_pool_w.astype(BF16)
    od_in, od_out, od_glu = od_w_in.astype(BF16), od_w_out.astype(BF16), od_s5_w_glu.astype(BF16)
    for i in range(depth):
        j = i // 2
        if i % 2 == 0:
            x = even_mixer(x, norm_mix[i, 0], ev_in, ev_pool, ev_pool_scale[j], ev_conv_w[j], ev_out, j,
                           norm_mix[i, 1], tm=1024)
        else:
            u_s5, gates = odd_in(x, norm_mix[i, 0], od_in, j, od_hy_short_w[j], od_hy_short_b[j], tm=1024)
            mats = s5_matrices(od_s5_lambda_re[j], od_s5_lambda_im[j], od_s5_log_dt[j], od_s5_b_re[j],
                               od_s5_b_im[j], od_s5_c_re[j], od_s5_c_im[j])
            ys = s5_core(u_s5, mats)
            kf = hyena_filter_spectra(l, tables, od_hy_w1[j], od_hy_b1[j], od_hy_w2[j], od_hy_b2[j],
                                      od_hy_w3[j], od_hy_freq[j])
            hy = hyena_mixer(gates, kf, od_hy_bias[j], tables)
            x = odd_out(ys, u_s5, hy, x, od_s5_d[j], od_glu, od_out, j, norm_mix[i, 1], tm=1024)
        kv = norm_matmul(mem2d, norm_mem[i], wkv, i, tm=b * N_MEM, out_dtype=BF16).reshape(b, N_MEM, 2 * d)
        x = xattn_block(x, kv, wq, wo, i, norm_xattn[i, 0], norm_xattn[i, 1], tm=1024)
        x = mlp_block(x.reshape(b * l, d), w1, w2, i, norm_mlp[i, 0], norm_mlp[i, 1],
                      tm=1024, tf=512).reshape(b, l, d)
    return x
```

```python
import functools
import math

import numpy as np
import jax
import jax.numpy as jnp
from jax import lax
from jax.experimental import pallas as pl
from jax.experimental.pallas import tpu as pltpu

F32 = jnp.float32
BF16 = jnp.bfloat16

D_MODEL = 1024
N_MEM = 256
RMS_EPS = 1e-6
D_POOL = 512
POOL_WINDOWS = (2, 4, 8, 16)
POOL_GROUP_DIM = 128
POOL_HALO = 8
D_CONV = 512
D_S5 = 512
S5_GROUP_DIM = 16
S5_GROUPS = 32
S5_STATE = 64
S5_CHUNK = 16
D_HYENA = 512
HYENA_BANDS = 16
HYENA_FFN = 64
HYENA_TARGET = 1e-2
HYENA_SHORT_DECAY_PCT = 0.3
HYENA_LONG_DECAY_PCT = 1.5
XA_HEADS = 4
XA_HEAD_DIM = 256
D_FF = 4096

LANES = 128
SUBLANES = 8
S5_SUPER = LANES // S5_GROUP_DIM
DFT_FAST = 128
DFT_TILES = DFT_FAST // SUBLANES
DFT_PITCH = DFT_FAST + SUBLANES
DFT_UNROLL = 16
VMEM_LIMIT = 56 * 1024 * 1024
VMEM_LIMIT_CONV = 62 * 1024 * 1024


def _cparams(*sem, vmem=VMEM_LIMIT):
    return pltpu.CompilerParams(dimension_semantics=sem, vmem_limit_bytes=vmem)


def _rms(xf, g):
    ms = jnp.mean(xf * xf, axis=-1, keepdims=True)
    return xf * lax.rsqrt(ms + RMS_EPS) * g


def _dot(a, b):
    return jnp.dot(a, b, preferred_element_type=F32)


def _layer_spec(block, layer, tail_map):
    return pl.BlockSpec((None,) + block, lambda *idx: (layer,) + tail_map(*idx))


def _norm_matmul_kernel(x_ref, g_ref, w_ref, o_ref):
    xn = _rms(x_ref[...], g_ref[...]).astype(BF16)
    o_ref[...] = _dot(xn, w_ref[...]).astype(o_ref.dtype)


def norm_matmul(x2d, g, w, layer, *, tm, out_dtype=F32):
    m, d = x2d.shape
    n = w.shape[-1]
    return pl.pallas_call(
        _norm_matmul_kernel,
        grid=(m // tm,),
        in_specs=[
            pl.BlockSpec((tm, d), lambda i: (i, 0)),
            pl.BlockSpec((1, d), lambda i: (0, 0)),
            _layer_spec((d, n), layer, lambda i: (0, 0)),
        ],
        out_specs=pl.BlockSpec((tm, n), lambda i: (i, 0)),
        out_shape=jax.ShapeDtypeStruct((m, n), out_dtype),
        compiler_params=_cparams("parallel"),
        name="norm_matmul",
    )(x2d, g.reshape(1, d), w)


def _xattn_kernel(x_ref, kv_ref, wq_ref, wo_ref, g1_ref, g2_ref, o_ref):
    x = x_ref[0]
    xn = _rms(x, g1_ref[...]).astype(BF16)
    q = (_dot(xn, wq_ref[...]) * (XA_HEAD_DIM ** -0.5)).astype(BF16)
    heads = []
    for h in range(XA_HEADS):
        lo = h * XA_HEAD_DIM
        qh = q[:, lo:lo + XA_HEAD_DIM]
        kh = kv_ref[0, :, lo:lo + XA_HEAD_DIM]
        vh = kv_ref[0, :, D_MODEL + lo:D_MODEL + lo + XA_HEAD_DIM]
        s = lax.dot_general(qh, kh, (((1,), (1,)), ((), ())), preferred_element_type=F32)
        e = jnp.exp(s - jnp.max(s, axis=-1, keepdims=True))
        p = e / jnp.sum(e, axis=-1, keepdims=True)
        heads.append(_dot(p.astype(BF16), vh).astype(BF16))
    o = jnp.concatenate(heads, axis=-1)
    y = _dot(o, wo_ref[...])
    o_ref[0] = x + _rms(y, g2_ref[...])


def xattn_block(x, kv, wq, wo, layer, g1, g2, *, tm):
    b, l, d = x.shape
    return pl.pallas_call(
        _xattn_kernel,
        grid=(b, l // tm),
        in_specs=[
            pl.BlockSpec((1, tm, d), lambda bi, i: (bi, i, 0)),
            pl.BlockSpec((1, N_MEM, 2 * d), lambda bi, i: (bi, 0, 0)),
            _layer_spec((d, d), layer, lambda bi, i: (0, 0)),
            _layer_spec((d, d), layer, lambda bi, i: (0, 0)),
            pl.BlockSpec((1, d), lambda bi, i: (0, 0)),
            pl.BlockSpec((1, d), lambda bi, i: (0, 0)),
        ],
        out_specs=pl.BlockSpec((1, tm, d), lambda bi, i: (bi, i, 0)),
        out_shape=jax.ShapeDtypeStruct(x.shape, F32),
        compiler_params=_cparams("parallel", "parallel"),
        name="xattn_block",
    )(x, kv, wq, wo, g1.reshape(1, d), g2.reshape(1, d))


def _mlp_kernel(x_ref, w1_ref, w2_ref, g1_ref, g2_ref, o_ref, h_ref, *, tf):
    x = x_ref[...]
    xn = _rms(x, g1_ref[...]).astype(BF16)
    for c in range(h_ref.shape[1] // tf):
        cols = slice(c * tf, (c + 1) * tf)
        h = jnp.maximum(_dot(xn, w1_ref[:, cols]), 0.0)
        h_ref[:, cols] = (h * h).astype(BF16)
    o_ref[...] = x + _rms(_dot(h_ref[...], w2_ref[...]), g2_ref[...])


def mlp_block(x2d, w1, w2, layer, g1, g2, *, tm, tf):
    m, d = x2d.shape
    ff = w1.shape[-1]
    resident = dict(pipeline_mode=pl.Buffered(1))
    return pl.pallas_call(
        functools.partial(_mlp_kernel, tf=tf),
        grid=(m // tm,),
        in_specs=[
            pl.BlockSpec((tm, d), lambda i: (i, 0)),
            pl.BlockSpec((None, d, ff), lambda i: (layer, 0, 0), **resident),
            pl.BlockSpec((None, ff, d), lambda i: (layer, 0, 0), **resident),
            pl.BlockSpec((1, d), lambda i: (0, 0)),
            pl.BlockSpec((1, d), lambda i: (0, 0)),
        ],
        out_specs=pl.BlockSpec((tm, d), lambda i: (i, 0)),
        out_shape=jax.ShapeDtypeStruct((m, d), F32),
        scratch_shapes=[pltpu.VMEM((tm, ff), BF16)],
        compiler_params=_cparams("parallel"),
        name="mlp_block",
    )(x2d, w1, w2, g1.reshape(1, d), g2.reshape(1, d))


def _halo_specs(tm, seq_len, width, col):
    r = tm // POOL_HALO
    last = seq_len // POOL_HALO - 1

    def prev_map(bi, i, *_):
        return (bi, jnp.maximum(i * r - 1, 0), col(*_))

    def main_map(bi, i, *_):
        return (bi, i, col(*_))

    def next_map(bi, i, *_):
        return (bi, jnp.minimum((i + 1) * r, last), col(*_))

    return [
        pl.BlockSpec((1, POOL_HALO, width), prev_map),
        pl.BlockSpec((1, tm, width), main_map),
        pl.BlockSpec((1, POOL_HALO, width), next_map),
    ]


def _with_halo(prev_ref, main_ref, next_ref):
    i = pl.program_id(1)
    prev = jnp.where(i > 0, prev_ref[0], 0.0)
    nxt = jnp.where(i < pl.num_programs(1) - 1, next_ref[0], 0.0)
    return jnp.concatenate([prev, main_ref[0], nxt], axis=0)


def _shift_rows(v, k):
    return pltpu.roll(v, k % v.shape[0], 0)


def _even_mixer_kernel(xp_ref, x_ref, xn_ref, gi_ref, wi_ref, wg_ref, ps_ref, cw_ref, wo_ref, g_ref, o_ref,
                       *, tm, seq_len):
    x_ext = _with_halo(xp_ref, x_ref, xn_ref)
    ext = _dot(_rms(x_ext, gi_ref[...]).astype(BF16), wi_ref[...])
    lo, hi = POOL_HALO, POOL_HALO + tm
    t = pl.program_id(1) * tm + lax.broadcasted_iota(jnp.int32, (tm, 1), 0)
    parts = []
    for gi, win in enumerate(POOL_WINDOWS):
        half = win // 2
        u = ext[:, gi * POOL_GROUP_DIM:(gi + 1) * POOL_GROUP_DIM]
        s = u + _shift_rows(u, 1)
        step = 1
        while 2 * step < win:
            s = _shift_rows(s, step) + _shift_rows(s, -step)
            step *= 2
        cnt = (jnp.minimum(t + half, seq_len) - jnp.maximum(t - half, 0)).astype(F32)
        pooled = s[lo:hi] / cnt - u[lo:hi]
        y = _dot(pooled.astype(BF16), wg_ref[gi])
        parts.append((y * ps_ref[:, gi * POOL_GROUP_DIM:(gi + 1) * POOL_GROUP_DIM]).astype(BF16))
    b_gate = ext[lo:hi, D_POOL:D_POOL + D_CONV]
    ch = ext[:, D_POOL + D_CONV:D_POOL + 2 * D_CONV] * ext[:, D_POOL + 2 * D_CONV:D_POOL + 3 * D_CONV]
    conv = cw_ref[0:1, :] * _shift_rows(ch, 1) + cw_ref[1:2, :] * ch + cw_ref[2:3, :] * _shift_rows(ch, -1)
    parts.append((b_gate * conv[lo:hi]).astype(BF16))
    mix = _dot(jnp.concatenate(parts, axis=-1), wo_ref[...])
    o_ref[0] = x_ref[0] + _rms(mix, g_ref[...])


def even_mixer(x, g_in, w_in, w_group, pool_scale, conv_w, w_out, layer, g, *, tm):
    b, l, d = x.shape
    kern = functools.partial(_even_mixer_kernel, tm=tm, seq_len=l)
    return pl.pallas_call(
        kern,
        grid=(b, l // tm),
        in_specs=_halo_specs(tm, l, d, lambda: 0) + [
            pl.BlockSpec((1, d), lambda bi, i: (0, 0)),
            _layer_spec(w_in.shape[1:], layer, lambda bi, i: (0, 0)),
            _layer_spec(w_group.shape[1:], layer, lambda bi, i: (0, 0, 0)),
            pl.BlockSpec((1, D_POOL), lambda bi, i: (0, 0)),
            pl.BlockSpec((3, D_CONV), lambda bi, i: (0, 0)),
            _layer_spec((d, d), layer, lambda bi, i: (0, 0)),
            pl.BlockSpec((1, d), lambda bi, i: (0, 0)),
        ],
        out_specs=pl.BlockSpec((1, tm, d), lambda bi, i: (bi, i, 0)),
        out_shape=jax.ShapeDtypeStruct(x.shape, F32),
        compiler_params=_cparams("parallel", "parallel"),
        name="even_mixer",
    )(x, x, x, g_in.reshape(1, d), w_in, w_group, pool_scale.reshape(1, D_POOL), conv_w, w_out, g.reshape(1, d))


def _cmul(ar, ai, br, bi):
    return ar * br - ai * bi, ar * bi + ai * br


def _s5_toeplitz_kernel(m_ref, o_ref):
    tc, hd = S5_CHUNK, S5_GROUP_DIM
    for g in range(m_ref.shape[0]):
        m = m_ref[g]
        rows = [m[:, (tc - 1 - s) * hd:(2 * tc - 1 - s) * hd] for s in range(tc)]
        o_ref[g] = jnp.concatenate(rows, axis=0).astype(o_ref.dtype)


def s5_toeplitz(lag_table):
    g, hd, w = lag_table.shape
    k = w // 2
    return pl.pallas_call(
        _s5_toeplitz_kernel,
        grid=(g // S5_SUPER,),
        in_specs=[pl.BlockSpec((S5_SUPER, hd, w), lambda i: (i, 0, 0))],
        out_specs=pl.BlockSpec((S5_SUPER, k, k), lambda i: (i, 0, 0)),
        out_shape=jax.ShapeDtypeStruct((g, k, k), BF16),
        compiler_params=_cparams("parallel"),
        name="s5_toeplitz",
    )(lag_table)


def s5_matrices(lam_re, lam_im, log_dt, b_re, b_im, c_re, c_im):
    tc, hd, n = S5_CHUNK, S5_GROUP_DIM, S5_STATE
    hp = lax.Precision.HIGHEST
    lr = jnp.minimum(lam_re, -1e-4)
    li = lam_im
    dt = jnp.exp(log_dt)[..., None]
    taus = jnp.arange(tc + 1, dtype=F32)[:, None, None, None]
    rmag = jnp.exp(lr[None] * dt[None] * (tc - taus))
    rang = li[None] * dt[None] * (tc - taus)
    pwr_re, pwr_im = rmag * jnp.cos(rang), rmag * jnp.sin(rang)
    mag = jnp.exp(lr[None] * dt[None] * taus)
    ang = li[None] * dt[None] * taus
    pw_re, pw_im = mag * jnp.cos(ang), mag * jnp.sin(ang)
    nr, ni = pw_re[1] - 1.0, pw_im[1]
    den = lr * lr + li * li
    coef_re, coef_im = (nr * lr + ni * li) / den, (ni * lr - nr * li) / den
    cb_re, cb_im = _cmul(coef_re[..., None], coef_im[..., None], b_re[None], b_im[None])
    cbt_re, cbt_im = _cmul(coef_re[:, :, None, :], coef_im[:, :, None, :],
                           jnp.swapaxes(b_re, 1, 2)[None], jnp.swapaxes(b_im, 1, 2)[None])
    def lag_kernels(d, p_re, p_im):
        q_re, q_im = _cmul(p_re[..., None], p_im[..., None], cb_re[d][None], cb_im[d][None])
        c_cat = jnp.concatenate([c_re[d], -c_im[d]], axis=-1)
        return jnp.einsum('gkn,tgnj->gjtk', c_cat, jnp.concatenate([q_re, q_im], axis=2), precision=hp)

    m_fwd = lag_kernels(0, pw_re[:tc, 0], pw_im[:tc, 0])
    m_bwd = lag_kernels(1, pwr_re[1:tc + 1, 1], pwr_im[1:tc + 1, 1])
    lag_table = jnp.concatenate([m_bwd[:, :, :tc - 1], m_bwd[:, :, tc - 1:] + m_fwd[:, :, :1], m_fwd[:, :, 1:],
                                 jnp.zeros_like(m_fwd[:, :, :1])], axis=2)
    intra = s5_toeplitz(lag_table.reshape(S5_GROUPS, hd, 2 * tc * hd))
    sf_re, sf_im = _cmul(pwr_re[1:tc + 1, 0][:, :, None, :], pwr_im[1:tc + 1, 0][:, :, None, :],
                         cbt_re[0][None], cbt_im[0][None])
    sb_re, sb_im = _cmul(pw_re[:tc, 1][:, :, None, :], pw_im[:tc, 1][:, :, None, :],
                         cbt_re[1][None], cbt_im[1][None])
    summ = jnp.concatenate([sf_re, sb_re, sf_im, sb_im], axis=-1)
    summ = jnp.swapaxes(summ, 0, 1).reshape(S5_GROUPS, tc * hd, 4 * n)
    gtn = lambda z: jnp.swapaxes(z, 0, 1)[:, :, None, :]
    qf_re, qf_im = _cmul(c_re[0][:, None], c_im[0][:, None], gtn(pw_re[1:tc + 1, 0]), gtn(pw_im[1:tc + 1, 0]))
    qb_re, qb_im = _cmul(c_re[1][:, None], c_im[1][:, None], gtn(pwr_re[:tc, 1]), gtn(pwr_im[:tc, 1]))
    carry = jnp.concatenate([qf_re, qb_re, -qf_im, -qb_im], axis=-1)
    carry = carry.reshape(S5_GROUPS, tc * hd, 4 * n)
    a_re = jnp.concatenate([pw_re[tc, 0], pw_re[tc, 1]], axis=-1)
    a_im = jnp.concatenate([pw_im[tc, 0], pw_im[tc, 1]], axis=-1)
    return intra, summ, carry, a_re, a_im


def s5_regroup_matrix():
    n = S5_SUPER * LANES
    src = np.arange(n)
    tb, g, h = src // LANES, (src % LANES) // S5_GROUP_DIM, src % S5_GROUP_DIM
    perm = np.zeros((n, n), np.float32)
    perm[src, g * LANES + tb * S5_GROUP_DIM + h] = 1.0
    return jnp.asarray(perm, BF16)


def _s5_regroup_kernel(p_ref, w_ref, perm_ref, u_ref, sr_ref, si_ref, *, n_chunks):
    n2 = 2 * S5_STATE
    for q in range(S5_CHUNK // S5_SUPER):
        rows_t = [p_ref[pl.ds(q * S5_SUPER + tb, n_chunks, stride=S5_CHUNK), :].astype(BF16) for tb in range(S5_SUPER)]
        grouped = _dot(jnp.concatenate(rows_t, axis=-1), perm_ref[...]).astype(BF16)
        for g in range(S5_SUPER):
            u_ref[g, :, q * LANES:(q + 1) * LANES] = grouped[:, g * LANES:(g + 1) * LANES]
    for g in range(S5_SUPER):
        s = _dot(u_ref[g], w_ref[g])
        sr_ref[pl.ds(g, n_chunks, stride=S5_SUPER), :] = s[:, :n2]
        si_ref[pl.ds(g, n_chunks, stride=S5_SUPER), :] = s[:, n2:]


def _s5_state_spec(c, n2, n_tiles):
    return pl.BlockSpec((None, c * S5_SUPER, n2), lambda a, bi: (bi * n_tiles + a, 0, 0))


def _s5_group_spec(rows, cols):
    return pl.BlockSpec((S5_SUPER, rows, cols), lambda a, bi: (a, 0, 0))


def s5_summary(p, summ):
    b, l, _ = p.shape
    c = l // S5_CHUNK
    n_tiles = S5_GROUPS // S5_SUPER
    k = S5_CHUNK * S5_GROUP_DIM
    n2 = 2 * S5_STATE
    shape = jax.ShapeDtypeStruct((b * n_tiles, c * S5_SUPER, n2), F32)
    kern = functools.partial(_s5_regroup_kernel, n_chunks=c)
    perm = s5_regroup_matrix()
    return pl.pallas_call(
        kern,
        grid=(n_tiles, b),
        in_specs=[
            pl.BlockSpec((None, l, LANES), lambda a, bi: (bi, 0, a)),
            _s5_group_spec(k, 2 * n2),
            pl.BlockSpec(perm.shape, lambda a, bi: (0, 0)),
        ],
        out_specs=[pl.BlockSpec((None, S5_SUPER, c, k), lambda a, bi: (bi * n_tiles + a, 0, 0, 0)),
                   _s5_state_spec(c, n2, n_tiles), _s5_state_spec(c, n2, n_tiles)],
        out_shape=[jax.ShapeDtypeStruct((b * n_tiles, S5_SUPER, c, k), BF16), shape, shape],
        compiler_params=_cparams("parallel", "parallel"),
        name="s5_summary",
    )(p, summ, perm)


def _s5_scan(sr_ref, si_ref, are_ref, aim_ref, or_ref, oi_ref, n_chunks):
    n2 = 2 * S5_STATE
    a_re = are_ref[...]
    a_im = aim_ref[...]
    fwd_lane = lax.broadcasted_iota(jnp.int32, (SUBLANES, n2), 1) < S5_STATE

    def body(i, state):
        x_re, x_im = state
        rows_i = pl.ds(pl.multiple_of(i * SUBLANES, SUBLANES), SUBLANES)
        rows_r = pl.ds(pl.multiple_of((n_chunks - 1 - i) * SUBLANES, SUBLANES), SUBLANES)
        or_ref[rows_i, 0:S5_STATE] = x_re[:, :S5_STATE]
        oi_ref[rows_i, 0:S5_STATE] = x_im[:, :S5_STATE]
        or_ref[rows_r, S5_STATE:n2] = x_re[:, S5_STATE:]
        oi_ref[rows_r, S5_STATE:n2] = x_im[:, S5_STATE:]
        s_re = jnp.where(fwd_lane, sr_ref[rows_i, :], sr_ref[rows_r, :])
        s_im = jnp.where(fwd_lane, si_ref[rows_i, :], si_ref[rows_r, :])
        return (a_re * x_re - a_im * x_im + s_re, a_re * x_im + a_im * x_re + s_im)

    zero = jnp.zeros((SUBLANES, n2), F32)
    lax.fori_loop(0, n_chunks, body, (zero, zero))


def _s5_output_kernel(u_ref, sr_ref, si_ref, are_ref, aim_ref, wi_ref, wc_ref, perm_ref, o_ref,
                      y_ref, xr_ref, xi_ref, *, n_chunks):
    _s5_scan(sr_ref, si_ref, are_ref, aim_ref, xr_ref, xi_ref, n_chunks)
    back = (((1,), (1,)), ((), ()))
    for g in range(S5_SUPER):
        rows = pl.ds(g, n_chunks, stride=S5_SUPER)
        xin = jnp.concatenate([xr_ref[rows, :], xi_ref[rows, :]], axis=-1).astype(BF16)
        y_ref[g] = _dot(u_ref[g], wi_ref[g]) + lax.dot_general(xin, wc_ref[g], back, preferred_element_type=F32)
    for q in range(S5_CHUNK // S5_SUPER):
        y = jnp.concatenate([y_ref[g, :, q * LANES:(q + 1) * LANES] for g in range(S5_SUPER)], axis=-1)
        hi = y.astype(BF16)
        lo = (y - hi.astype(F32)).astype(BF16)
        tok = (lax.dot_general(hi, perm_ref[...], back, preferred_element_type=F32)
               + lax.dot_general(lo, perm_ref[...], back, preferred_element_type=F32))
        for tb in range(S5_SUPER):
            o_ref[pl.ds(q * S5_SUPER + tb, n_chunks, stride=S5_CHUNK), :] = tok[:, tb * LANES:(tb + 1) * LANES]


def s5_output(u, s_re, s_im, a_re, a_im, intra, carry, *, batch):
    tiles, _, c, k = u.shape
    n_tiles = tiles // batch
    n2 = 2 * S5_STATE
    l = c * S5_CHUNK
    kern = functools.partial(_s5_output_kernel, n_chunks=c)
    perm = s5_regroup_matrix()
    decay_spec = pl.BlockSpec((S5_SUPER, n2), lambda a, bi: (a, 0))
    state_scr = pltpu.VMEM((c * S5_SUPER, n2), F32)
    return pl.pallas_call(
        kern,
        grid=(n_tiles, batch),
        in_specs=[
            pl.BlockSpec((None, S5_SUPER, c, k), lambda a, bi: (bi * n_tiles + a, 0, 0, 0)),
            _s5_state_spec(c, n2, n_tiles),
            _s5_state_spec(c, n2, n_tiles),
            decay_spec,
            decay_spec,
            _s5_group_spec(k, k),
            _s5_group_spec(k, 2 * n2),
            pl.BlockSpec(perm.shape, lambda a, bi: (0, 0)),
        ],
        out_specs=pl.BlockSpec((None, l, LANES), lambda a, bi: (bi, 0, a)),
        out_shape=jax.ShapeDtypeStruct((batch, l, D_S5), F32),
        scratch_shapes=[pltpu.VMEM((S5_SUPER, c, k), F32), state_scr, state_scr],
        compiler_params=_cparams("parallel", "parallel"),
        name="s5_output",
    )(u, s_re, s_im, a_re, a_im, intra, carry, perm)


def s5_core(p, mats):
    intra, summ, carry, a_re, a_im = mats
    u, s_re, s_im = s5_summary(p, summ.astype(BF16))
    return s5_output(u, s_re, s_im, a_re, a_im, intra.astype(BF16), carry.astype(BF16), batch=p.shape[0])


def _to_time_tiles(y):
    nf, c = DFT_FAST, y.shape[-1]
    slabs = [y[s * nf:(s + 1) * nf].reshape(DFT_TILES, SUBLANES, c) for s in range(y.shape[0] // nf)]
    return jnp.concatenate(slabs, axis=1)


def _from_time_tiles(v):
    c = v.shape[-1]
    slabs = [v[:, s * SUBLANES:(s + 1) * SUBLANES, :].reshape(DFT_FAST, c) for s in range(v.shape[1] // SUBLANES)]
    return jnp.concatenate(slabs, axis=0)


def _odd_in_kernel(xp_ref, x_ref, xn_ref, g_ref, w_ref, cw_ref, cb_ref, s5_ref, gates_ref, *, tm):
    ext = _with_halo(xp_ref, x_ref, xn_ref)
    p = _dot(_rms(ext, g_ref[...]).astype(BF16), w_ref[...])
    lo, hi = POOL_HALO, POOL_HALO + tm
    s5_ref[0] = p[lo:hi, :D_S5]
    ph = p[:, D_S5:]
    y = cw_ref[0:1, :] * _shift_rows(ph, 1) + cw_ref[1:2, :] * ph + cw_ref[2:3, :] * _shift_rows(ph, -1)
    y = y[lo:hi] + cb_ref[...]
    for k in range(3):
        for ct in range(D_HYENA // LANES):
            col = k * D_HYENA + ct * LANES
            gates_ref[k, ct] = _to_time_tiles(y[:, col:col + LANES])


def odd_in(x, g, w, layer, conv_w, conv_b, *, tm):
    b, l, d = x.shape
    n = w.shape[-1]
    rows = tm // DFT_FAST * SUBLANES
    cts = D_HYENA // LANES
    kern = functools.partial(_odd_in_kernel, tm=tm)
    return pl.pallas_call(
        kern,
        grid=(b, l // tm),
        in_specs=_halo_specs(tm, l, d, lambda: 0) + [
            pl.BlockSpec((1, d), lambda bi, i: (0, 0)),
            _layer_spec((d, n), layer, lambda bi, i: (0, 0)),
            pl.BlockSpec((3, n - D_S5), lambda bi, i: (0, 0)),
            pl.BlockSpec((1, n - D_S5), lambda bi, i: (0, 0)),
        ],
        out_specs=[pl.BlockSpec((1, tm, D_S5), lambda bi, i: (bi, i, 0)),
                   pl.BlockSpec((3, None, cts, DFT_TILES, rows, LANES), lambda bi, i: (0, bi, 0, 0, i, 0))],
        out_shape=[jax.ShapeDtypeStruct((b, l, D_S5), F32),
                   jax.ShapeDtypeStruct((3, b, cts, DFT_TILES, l // DFT_FAST * SUBLANES, LANES), F32)],
        compiler_params=_cparams("parallel", "parallel"),
        name="odd_in",
    )(x, x, x, g.reshape(1, d), w, conv_w, conv_b.reshape(1, -1))


def _filter_mlp_kernel(bands_ref, w1t_ref, w1cs_ref, b1_ref, w2_ref, b2_ref, w3h_ref, w3l_ref, fr_ref, dl_ref,
                       eo_ref, asum_ref, *, tl, seq_len):
    i = pl.program_id(0)
    hp = lax.Precision.HIGHEST
    c = D_HYENA
    t = (i * tl + lax.broadcasted_iota(jnp.int32, (tl, 1), 0)).astype(F32)
    t_norm = t / (seq_len - 1.0)
    ang = (2.0 * math.pi / seq_len) * t * bands_ref[...]
    is_cos = lax.broadcasted_iota(jnp.int32, (1, ang.shape[1]), 1) < HYENA_BANDS
    cs = jnp.sin(ang + jnp.where(is_cos, 0.5 * math.pi, math.pi))
    fr = fr_ref[...]
    z = t_norm * w1t_ref[...] + jnp.dot(cs, w1cs_ref[...], precision=hp, preferred_element_type=F32)
    h = jnp.sin(fr * (z + b1_ref[...]))
    h = jnp.sin(fr * (jnp.dot(h, w2_ref[...], precision=hp, preferred_element_type=F32) + b2_ref[...]))
    h_hi = h.astype(BF16)
    h_lo = (h - h_hi.astype(F32)).astype(BF16)
    h = _dot(h_hi, w3h_ref[...]) + (_dot(h_lo, w3h_ref[...]) + _dot(h_hi, w3l_ref[...]))
    decay = jnp.exp(-t_norm * dl_ref[...])

    @pl.when(i == 0)
    def _():
        asum_ref[...] = jnp.zeros_like(asum_ref)

    for o in range(h.shape[1] // (2 * c)):
        fwd = h[:, 2 * o * c:(2 * o + 1) * c] * decay
        bwd = h[:, (2 * o + 1) * c:(2 * o + 2) * c] * decay
        for part, val in enumerate((fwd + bwd, fwd - bwd)):
            for ct in range(c // LANES):
                eo_ref[((2 * o + part) * c) // LANES + ct] = _to_time_tiles(val[:, ct * LANES:(ct + 1) * LANES])
        asum_ref[:, o * c:(o + 1) * c] += jnp.sum(jnp.abs(fwd) + jnp.abs(bwd), axis=0, keepdims=True)


def hyena_filter_mlp(seq_len, w1, b1, w2, b2, w3, freq, *, tl):
    n_out = w3.shape[1]
    bands = jnp.linspace(1e-4, HYENA_BANDS - 1, HYENA_BANDS, dtype=F32)
    bands2 = jnp.concatenate([bands, bands])[None, :]
    deltas = jnp.abs(jnp.linspace(math.log(HYENA_TARGET) / HYENA_LONG_DECAY_PCT,
                                  math.log(HYENA_TARGET) / HYENA_SHORT_DECAY_PCT, D_HYENA, dtype=F32))[None, :]
    kern = functools.partial(_filter_mlp_kernel, tl=tl, seq_len=seq_len)
    full = lambda a: pl.BlockSpec(a.shape, lambda i: (0,) * a.ndim)
    w3_hi = w3.astype(BF16)
    w3_lo = (w3 - w3_hi.astype(F32)).astype(BF16)
    args = [bands2, w1[0:1], w1[1:], b1[None, :], w2, b2[None, :], w3_hi, w3_lo, freq[None, :], deltas]
    return pl.pallas_call(
        kern,
        grid=(seq_len // tl,),
        in_specs=[full(a) for a in args],
        out_specs=[pl.BlockSpec((n_out // LANES, DFT_TILES, tl // DFT_FAST * SUBLANES, LANES), lambda i: (0, 0, i, 0)),
                   pl.BlockSpec((1, n_out // 2), lambda i: (0, 0))],
        out_shape=[jax.ShapeDtypeStruct((n_out // LANES, DFT_TILES, seq_len // DFT_FAST * SUBLANES, LANES), F32),
                   jax.ShapeDtypeStruct((1, n_out // 2), F32)],
        compiler_params=_cparams("arbitrary"),
        name="hyena_filter_mlp",
    )(*args)


def dft_tables(seq_len):
    n = 2 * seq_len
    nf = DFT_FAST
    ns = n // nf
    k_a = np.arange(ns)[:, None]
    n_s = np.arange(ns // 2)[None, :]
    w_s = np.exp(-2j * np.pi * ((k_a * n_s) % ns) / ns)
    n_f = np.arange(nf)[:, None]
    tw = np.exp(-2j * np.pi * ((n_f * np.arange(ns)[None, :]) % n) / n)
    c32 = lambda z: (jnp.asarray(z.real, F32), jnp.asarray(z.imag, F32))
    ws_re, ws_im = c32(w_s)
    tw_re, tw_im = c32(tw)
    g_re = tw_re[:, :, None] * ws_re[None] - tw_im[:, :, None] * ws_im[None]
    g_im = tw_re[:, :, None] * ws_im[None] + tw_im[:, :, None] * ws_re[None]
    fwd_a = jnp.concatenate([jnp.concatenate([g_re, -g_im], axis=2),
                             jnp.concatenate([g_im, g_re], axis=2)], axis=1)
    gt_re = jnp.swapaxes(g_re, 1, 2) * (1.0 / n)
    gt_im = jnp.swapaxes(g_im, 1, 2) * (1.0 / n)
    inv_a = jnp.concatenate([jnp.concatenate([gt_re, gt_im], axis=2),
                             jnp.concatenate([-gt_im, gt_re], axis=2)], axis=1)
    kk = np.arange(nf)
    w_f = np.exp(-2j * np.pi * ((kk[:, None] * kk[None, :]) % nf) / nf)
    f_re, f_im = c32(w_f)
    fwd_b = jnp.concatenate([jnp.concatenate([f_re, -f_im], axis=1),
                             jnp.concatenate([f_im, f_re], axis=1)], axis=0)
    inv_b = jnp.concatenate([jnp.concatenate([f_re, f_im], axis=1),
                             jnp.concatenate([-f_im, f_re], axis=1)], axis=0)
    kept = min(ns, -(-(ns // 2 + 1) // SUBLANES) * SUBLANES)
    fwd_a_real = jnp.concatenate([fwd_a[:, :kept, :ns // 2], fwd_a[:, ns:ns + kept, :ns // 2]], axis=1)
    m_re, m_im = jnp.roll(f_re, -1, axis=0), jnp.roll(f_im, -1, axis=0)
    mirror_part = jnp.stack([jnp.concatenate([m_re, m_im], axis=1), jnp.concatenate([m_im, -m_re], axis=1)])
    return dict(fwd_a=fwd_a.astype(BF16), fwd_a_real=fwd_a_real.astype(BF16),
                inv_a=inv_a.astype(BF16), fwd_b=fwd_b.astype(BF16), inv_b=inv_b.astype(BF16),
                fwd_b_part=fwd_b.reshape(2, nf, 2 * nf).astype(BF16), mirror_b_part=mirror_part.astype(BF16))


def _filter_dft_kernel(x_ref, ga_ref, fb_ref, fbm_ref, nrm_ref, k_ref, ar_scr, ai_scr):
    nf = DFT_FAST
    kept = ga_ref.shape[1] // 2
    n_in = ga_ref.shape[2]
    ns = 2 * n_in

    def stage_a(q, carry):
        for j in range(SUBLANES):
            x = x_ref[q, pl.ds(j, n_in, stride=SUBLANES), :].astype(BF16)
            n_f = q * SUBLANES + j
            a = _dot(ga_ref[n_f], x)
            ar_scr[pl.ds(n_f, kept, stride=DFT_PITCH), :] = a[:kept]
            ai_scr[pl.ds(n_f, kept, stride=DFT_PITCH), :] = a[kept:]
        return carry

    lax.fori_loop(0, DFT_TILES, stage_a, 0, unroll=DFT_UNROLL // SUBLANES)
    inv_norm = 1.0 / (nrm_ref[...] + 1e-6)

    def slab(start):
        return jnp.concatenate([ar_scr[pl.ds(start, nf), :], ai_scr[pl.ds(start, nf), :]], axis=0).astype(BF16)

    for k in (0, n_in):
        k_ref[k * nf:(k + 1) * nf, :] = _dot(fb_ref[...], slab(k * DFT_PITCH)) * inv_norm

    def stage_b(k, carry):
        a = slab(pl.multiple_of(k * DFT_PITCH, SUBLANES))
        k_ref[pl.ds(pl.multiple_of(k * nf, nf), nf), :] = _dot(fb_ref[...], a) * inv_norm
        k_ref[pl.ds(pl.multiple_of((ns - k) * nf, nf), nf), :] = _dot(fbm_ref[...], a) * inv_norm
        return carry

    pairs = n_in - 1
    lax.fori_loop(1, n_in, stage_b, 0, unroll=max(u for u in (9, 8, 7, 4, 3, 2, 1) if pairs % u == 0))


def hyena_filter_dft(eo, asum, tables):
    col_tiles, q_tiles, rows, _ = eo.shape
    l = q_tiles * rows
    c = D_HYENA
    ct = c // LANES
    orders = col_tiles // (2 * ct)
    ga, fb, fbm = tables['fwd_a_real'], tables['fwd_b_part'], tables['mirror_b_part']
    kept = ga.shape[1] // 2
    scr = pltpu.VMEM((kept * DFT_PITCH, LANES), F32)
    part_spec = pl.BlockSpec((None,) + fb.shape[1:], lambda o, j, part: (part, 0, 0))
    return pl.pallas_call(
        _filter_dft_kernel,
        grid=(orders, ct, 2),
        in_specs=[
            pl.BlockSpec((None,) + eo.shape[1:], lambda o, j, part: ((2 * o + part) * ct + j, 0, 0, 0)),
            pl.BlockSpec(ga.shape, lambda o, j, part: (0, 0, 0)),
            part_spec,
            part_spec,
            pl.BlockSpec((1, LANES), lambda o, j, part: (0, o * ct + j)),
        ],
        out_specs=pl.BlockSpec((None, None, None, 2 * l, LANES), lambda o, j, part: (o, part, j, 0, 0)),
        out_shape=jax.ShapeDtypeStruct((orders, 2, ct, 2 * l, LANES), F32),
        scratch_shapes=[scr, scr],
        compiler_params=_cparams("parallel", "parallel", "parallel"),
        name="hyena_filter_dft",
    )(eo, ga, fb, fbm, asum)


def _lane_cat(parts):
    return parts[0] if len(parts) == 1 else jnp.concatenate(parts, axis=-1)


def _hyena_conv_kernel(x_ref, ga_ref, k_ref, fb_ref, ib_ref, gi_ref, gate_ref, u_ref, bias_ref, o_ref, *scr,
                       n_a, n_b, slabs):
    nf = DFT_FAST
    tiles, qt = x_ref.shape[1], x_ref.shape[2]
    half = x_ref.shape[3] // SUBLANES
    ns = 2 * half
    step = pl.program_id(1)

    @pl.when(step < n_a)
    def _():
        for qj in range(qt * SUBLANES):
            q, j = divmod(qj, SUBLANES)
            rows = pl.ds(j, half, stride=SUBLANES)
            x = _lane_cat([jnp.concatenate([x_ref[0, t, q, rows, :], x_ref[1, t, q, rows, :]], axis=0)
                           for t in range(tiles)])
            a = _dot(ga_ref[qj], x.astype(BF16))
            dst = pl.ds(step * (qt * SUBLANES) + qj, ns, stride=DFT_PITCH)
            for t in range(tiles):
                scr[2 * t][dst, :] = a[:ns, t * LANES:(t + 1) * LANES]
                scr[2 * t + 1][dst, :] = a[ns:, t * LANES:(t + 1) * LANES]

    @pl.when((step >= n_a) & (step < n_a + n_b))
    def _():
        for s in range(slabs):
            k_a = (step - n_a) * slabs + s
            rows = pl.ds(pl.multiple_of(k_a * DFT_PITCH, SUBLANES), nf)
            a = jnp.concatenate([_lane_cat([scr[2 * t + part][rows, :] for t in range(tiles)]) for part in range(2)], axis=0)
            x = _dot(fb_ref[...], a.astype(BF16))
            xr, xi = x[:nf], x[nf:]
            kr = _lane_cat([k_ref[0, t, s * nf:(s + 1) * nf, :] for t in range(tiles)])
            ki = _lane_cat([k_ref[1, t, s * nf:(s + 1) * nf, :] for t in range(tiles)])
            y = jnp.concatenate([xr * kr - xi * ki, xr * ki + xi * kr], axis=0).astype(BF16)
            cc = _dot(ib_ref[...], y)
            for t in range(tiles):
                scr[2 * t][rows, :] = cc[:nf, t * LANES:(t + 1) * LANES]
                scr[2 * t + 1][rows, :] = cc[nf:, t * LANES:(t + 1) * LANES]

    @pl.when(step >= n_a + n_b)
    def _():
        first = (step - (n_a + n_b)) * (qt * SUBLANES)
        for qj in range(qt * SUBLANES):
            q, j = divmod(qj, SUBLANES)
            src = pl.ds(first + qj, ns, stride=DFT_PITCH)
            cc = jnp.concatenate([_lane_cat([scr[2 * t + part][src, :] for t in range(tiles)]) for part in range(2)], axis=0)
            y = _dot(gi_ref[qj], cc.astype(BF16))
            dst = pl.ds(j, half, stride=SUBLANES)
            for t in range(tiles):
                o_ref[0, t, q, dst, :] = y[:half, t * LANES:(t + 1) * LANES]
                o_ref[1, t, q, dst, :] = y[half:, t * LANES:(t + 1) * LANES]
        for t in range(tiles):
            bias = bias_ref[:, t * LANES:(t + 1) * LANES]
            for bi in range(2):
                o_ref[bi, t] = gate_ref[bi, t] * (o_ref[bi, t] + u_ref[bi, t] * bias)


def hyena_long_conv(u6, u_idx, gate6, gate_idx, kf, order, bias, tables, *, slabs=8, tiles=2, qt=2):
    _, _, cts, q_tiles, rows, _ = u6.shape
    ns = 2 * rows // SUBLANES
    n_a = q_tiles // qt
    n_b = ns // slabs
    ga, gi, fb, ib = tables['fwd_a'], tables['inv_a'], tables['fwd_b'], tables['inv_b']
    q_fwd = lambda s: jnp.minimum(s, n_a - 1)
    q_mid = lambda s: jnp.clip(s - n_a, 0, n_b - 1)
    q_inv = lambda s: jnp.clip(s - n_a - n_b, 0, n_a - 1)

    def seq_spec(idx, q_of):
        return pl.BlockSpec((None, 2, tiles, qt, rows, LANES), lambda c, s: (idx, 0, c, q_of(s), 0, 0))

    scr = pltpu.VMEM((ns * DFT_PITCH, LANES), F32)
    kern = functools.partial(_hyena_conv_kernel, n_a=n_a, n_b=n_b, slabs=slabs)
    return pl.pallas_call(
        kern,
        grid=(cts // tiles, 2 * n_a + n_b),
        in_specs=[
            seq_spec(u_idx, q_fwd),
            pl.BlockSpec((qt * SUBLANES,) + ga.shape[1:], lambda c, s: (q_fwd(s), 0, 0)),
            pl.BlockSpec((None, 2, tiles, slabs * DFT_FAST, LANES), lambda c, s: (order, 0, c, q_mid(s), 0)),
            pl.BlockSpec(fb.shape, lambda c, s: (0, 0)),
            pl.BlockSpec(ib.shape, lambda c, s: (0, 0)),
            pl.BlockSpec((qt * SUBLANES,) + gi.shape[1:], lambda c, s: (q_inv(s), 0, 0)),
            seq_spec(gate_idx, q_inv),
            seq_spec(u_idx, q_inv),
            pl.BlockSpec((1, tiles * LANES), lambda c, s: (0, c)),
        ],
        out_specs=pl.BlockSpec((2, tiles, qt, rows, LANES), lambda c, s: (0, c, q_inv(s), 0, 0)),
        out_shape=jax.ShapeDtypeStruct((2, cts, q_tiles, rows, LANES), F32),
        scratch_shapes=[scr] * (2 * tiles),
        compiler_params=_cparams("arbitrary", "arbitrary", vmem=VMEM_LIMIT_CONV),
        name="hyena_long_conv",
    )(u6, ga, kf, fb, ib, gi, gate6, u6, bias.reshape(1, -1))


def hyena_filter_spectra(seq_len, tables, w1, b1, w2, b2, w3, freq):
    eo, asum = hyena_filter_mlp(seq_len, w1, b1, w2, b2, w3, freq, tl=512)
    return hyena_filter_dft(eo, asum, tables)


def hyena_mixer(gates, kf, bias, tables):
    z = hyena_long_conv(gates, 2, gates, 1, kf, 0, bias[0], tables)
    return hyena_long_conv(z[None], 0, gates, 0, kf, 1, bias[1], tables)


def _odd_out_kernel(ys_ref, u_ref, hy_ref, x_ref, d_ref, wglu_ref, wo_ref, g_ref, o_ref):
    y = ys_ref[0] + d_ref[...] * u_ref[0]
    c0 = math.sqrt(2.0 / math.pi)
    gl = 0.5 * y * (1.0 + jnp.tanh(c0 * (y + 0.044715 * (y * y * y))))
    z = _dot(gl.astype(BF16), wglu_ref[...])
    s5 = gl * (1.0 / (1.0 + jnp.exp(-z)))
    hy = [_from_time_tiles(hy_ref[ct]).astype(BF16) for ct in range(hy_ref.shape[0])]
    mix = _dot(jnp.concatenate([s5.astype(BF16)] + hy, axis=-1), wo_ref[...])
    o_ref[0] = x_ref[0] + _rms(mix, g_ref[...])


def odd_out(ys, u_s5, hy, x, d_skip, w_glu, w_out, layer, g, *, tm):
    b, l, d = x.shape
    tok = lambda w: pl.BlockSpec((1, tm, w), lambda bi, i: (bi, i, 0))
    hy_spec = pl.BlockSpec((None, D_HYENA // LANES, DFT_TILES, tm // DFT_FAST * SUBLANES, LANES),
                           lambda bi, i: (bi, 0, 0, i, 0))
    return pl.pallas_call(
        _odd_out_kernel,
        grid=(b, l // tm),
        in_specs=[tok(D_S5), tok(D_S5), hy_spec, tok(d),
                  pl.BlockSpec((1, D_S5), lambda bi, i: (0, 0)),
                  _layer_spec((D_S5, D_S5), layer, lambda bi, i: (0, 0)),
                  _layer_spec((d, d), layer, lambda bi, i: (0, 0)),
                  pl.BlockSpec((1, d), lambda bi, i: (0, 0))],
        out_specs=tok(d),
        out_shape=jax.ShapeDtypeStruct(x.shape, F32),
        compiler_params=_cparams("parallel", "parallel"),
        name="odd_out",
    )(ys, u_s5, hy, x, d_skip.reshape(1, D_S5), w_glu, w_out, g.reshape(1, d))


def kernel(x, mem, norm_mix, norm_xattn, norm_mem, norm_mlp, xa_wq, xa_wk, xa_wv, xa_wo, mlp_w1, mlp_w2, ev_w_in, ev_pool_w, ev_pool_scale, ev_conv_w, ev_w_out, od_w_in, od_s5_lambda_re, od_s5_lambda_im, od_s5_log_dt, od_s5_b_re, od_s5_b_im, od_s5_c_re, od_s5_c_im, od_s5_d, od_s5_w_glu, od_hy_short_w, od_hy_short_b, od_hy_w1, od_hy_b1, od_hy_w2, od_hy_b2, od_hy_w3, od_hy_freq, od_hy_bias, od_w_out):
    b, l, d = x.shape
    depth = norm_mix.shape[0]
    assert b == 2, "the long convolution packs the two batch rows as one complex signal"
    tables = dft_tables(l)
    mem2d = mem.reshape(b * N_MEM, d)
    wq, wo = xa_wq.astype(BF16), xa_wo.astype(BF16)
    wkv = jnp.concatenate([xa_wk, xa_wv], axis=2).astype(BF16)
    w1, w2 = mlp_w1.astype(BF16), mlp_w2.astype(BF16)
    ev_in, ev_out, ev_pool = ev_w_in.astype(BF16), ev_w_out.astype(BF16), ev_pool_w.astype(BF16)
    od_in, od_out, od_glu = od_w_in.astype(BF16), od_w_out.astype(BF16), od_s5_w_glu.astype(BF16)
    for i in range(depth):
        j = i // 2
        if i % 2 == 0:
            x = even_mixer(x, norm_mix[i, 0], ev_in, ev_pool, ev_pool_scale[j], ev_conv_w[j], ev_out, j,
                           norm_mix[i, 1], tm=1024)
        else:
            u_s5, gates = odd_in(x, norm_mix[i, 0], od_in, j, od_hy_short_w[j], od_hy_short_b[j], tm=1024)
            mats = s5_matrices(od_s5_lambda_re[j], od_s5_lambda_im[j], od_s5_log_dt[j], od_s5_b_re[j],
                               od_s5_b_im[j], od_s5_c_re[j], od_s5_c_im[j])
            ys = s5_core(u_s5, mats)
            kf = hyena_filter_spectra(l, tables, od_hy_w1[j], od_hy_b1[j], od_hy_w2[j], od_hy_b2[j],
                                      od_hy_w3[j], od_hy_freq[j])
            hy = hyena_mixer(gates, kf, od_hy_bias[j], tables)
            x = odd_out(ys, u_s5, hy, x, od_s5_d[j], od_glu, od_out, j, norm_mix[i, 1], tm=1024)
        kv = norm_matmul(mem2d, norm_mem[i], wkv, i, tm=b * N_MEM, out_dtype=BF16).reshape(b, N_MEM, 2 * d)
        x = xattn_block(x, kv, wq, wo, i, norm_xattn[i, 0], norm_xattn[i, 1], tm=1024)
        x = mlp_block(x.reshape(b * l, d), w1, w2, i, norm_mlp[i, 0], norm_mlp[i, 1],
                      tm=1024, tf=512).reshape(b, l, d)
    return x
```

```python
import functools
import math

import numpy as np
import jax
import jax.numpy as jnp
from jax import lax
from jax.experimental import pallas as pl
from jax.experimental.pallas import tpu as pltpu

F32 = jnp.float32
BF16 = jnp.bfloat16

D_MODEL = 1024
N_MEM = 256
RMS_EPS = 1e-6
D_POOL = 512
POOL_WINDOWS = (2, 4, 8, 16)
POOL_GROUP_DIM = 128
POOL_HALO = 8
D_CONV = 512
D_S5 = 512
S5_GROUP_DIM = 16
S5_GROUPS = 32
S5_STATE = 64
S5_CHUNK = 16
D_HYENA = 512
HYENA_BANDS = 16
HYENA_FFN = 64
HYENA_TARGET = 1e-2
HYENA_SHORT_DECAY_PCT = 0.3
HYENA_LONG_DECAY_PCT = 1.5
XA_HEADS = 4
XA_HEAD_DIM = 256
D_FF = 4096

LANES = 128
SUBLANES = 8
S5_SUPER = LANES // S5_GROUP_DIM
DFT_FAST = 128
DFT_TILES = DFT_FAST // SUBLANES
DFT_PITCH = DFT_FAST + SUBLANES
DFT_UNROLL = 16
VMEM_LIMIT = 56 * 1024 * 1024
VMEM_LIMIT_CONV = 62 * 1024 * 1024


def _cparams(*sem, vmem=VMEM_LIMIT):
    return pltpu.CompilerParams(dimension_semantics=sem, vmem_limit_bytes=vmem)


def _rms(xf, g):
    ms = jnp.mean(xf * xf, axis=-1, keepdims=True)
    return xf * lax.rsqrt(ms + RMS_EPS) * g


def _dot(a, b):
    return jnp.dot(a, b, preferred_element_type=F32)


def _layer_spec(block, layer, tail_map):
    return pl.BlockSpec((None,) + block, lambda *idx: (layer,) + tail_map(*idx))


def _norm_matmul_kernel(x_ref, g_ref, w_ref, o_ref):
    xn = _rms(x_ref[...], g_ref[...]).astype(BF16)
    o_ref[...] = _dot(xn, w_ref[...]).astype(o_ref.dtype)


def memory_kv(mem2d, g, w):
    layers, d, n = w.shape
    m = mem2d.shape[0]
    return pl.pallas_call(
        _norm_matmul_kernel,
        grid=(layers,),
        in_specs=[
            pl.BlockSpec((m, d), lambda i: (0, 0)),
            pl.BlockSpec((None, 1, d), lambda i: (i, 0, 0)),
            pl.BlockSpec((None, d, n), lambda i: (i, 0, 0)),
        ],
        out_specs=pl.BlockSpec((None, m, n), lambda i: (i, 0, 0)),
        out_shape=jax.ShapeDtypeStruct((layers, m, n), BF16),
        compiler_params=_cparams("parallel"),
        name="memory_kv",
    )(mem2d, g.reshape(layers, 1, d), w)


def _xattn_kernel(x_ref, kv_ref, wq_ref, wo_ref, g1_ref, g2_ref, o_ref):
    x = x_ref[0]
    xn = _rms(x, g1_ref[...]).astype(BF16)
    q = (_dot(xn, wq_ref[...]) * (XA_HEAD_DIM ** -0.5)).astype(BF16)
    heads = []
    for h in range(XA_HEADS):
        lo = h * XA_HEAD_DIM
        qh = q[:, lo:lo + XA_HEAD_DIM]
        kh = kv_ref[0, :, lo:lo + XA_HEAD_DIM]
        vh = kv_ref[0, :, D_MODEL + lo:D_MODEL + lo + XA_HEAD_DIM]
        s = lax.dot_general(qh, kh, (((1,), (1,)), ((), ())), preferred_element_type=F32)
        e = jnp.exp(s - jnp.max(s, axis=-1, keepdims=True))
        p = e / jnp.sum(e, axis=-1, keepdims=True)
        heads.append(_dot(p.astype(BF16), vh).astype(BF16))
    o = jnp.concatenate(heads, axis=-1)
    y = _dot(o, wo_ref[...])
    o_ref[0] = x + _rms(y, g2_ref[...])


def xattn_block(x, kv, wq, wo, layer, g1, g2, *, tm):
    b, l, d = x.shape
    return pl.pallas_call(
        _xattn_kernel,
        grid=(b, l // tm),
        in_specs=[
            pl.BlockSpec((1, tm, d), lambda bi, i: (bi, i, 0)),
            pl.BlockSpec((None, 1, N_MEM, 2 * d), lambda bi, i: (layer, bi, 0, 0)),
            _layer_spec((d, d), layer, lambda bi, i: (0, 0)),
            _layer_spec((d, d), layer, lambda bi, i: (0, 0)),
            pl.BlockSpec((1, d), lambda bi, i: (0, 0)),
            pl.BlockSpec((1, d), lambda bi, i: (0, 0)),
        ],
        out_specs=pl.BlockSpec((1, tm, d), lambda bi, i: (bi, i, 0)),
        out_shape=jax.ShapeDtypeStruct(x.shape, F32),
        compiler_params=_cparams("parallel", "parallel"),
        name="xattn_block",
    )(x, kv, wq, wo, g1.reshape(1, d), g2.reshape(1, d))


def _mlp_kernel(x_ref, w1_ref, w2_ref, g1_ref, g2_ref, o_ref, h_ref, *, tf):
    x = x_ref[...]
    xn = _rms(x, g1_ref[...]).astype(BF16)
    for c in range(h_ref.shape[1] // tf):
        cols = slice(c * tf, (c + 1) * tf)
        h = jnp.maximum(_dot(xn, w1_ref[:, cols]), 0.0)
        h_ref[:, cols] = (h * h).astype(BF16)
    o_ref[...] = x + _rms(_dot(h_ref[...], w2_ref[...]), g2_ref[...])


def mlp_block(x2d, w1, w2, layer, g1, g2, *, tm, tf):
    m, d = x2d.shape
    ff = w1.shape[-1]
    resident = dict(pipeline_mode=pl.Buffered(1))
    return pl.pallas_call(
        functools.partial(_mlp_kernel, tf=tf),
        grid=(m // tm,),
        in_specs=[
            pl.BlockSpec((tm, d), lambda i: (i, 0)),
            pl.BlockSpec((None, d, ff), lambda i: (layer, 0, 0), **resident),
            pl.BlockSpec((None, ff, d), lambda i: (layer, 0, 0), **resident),
            pl.BlockSpec((1, d), lambda i: (0, 0)),
            pl.BlockSpec((1, d), lambda i: (0, 0)),
        ],
        out_specs=pl.BlockSpec((tm, d), lambda i: (i, 0)),
        out_shape=jax.ShapeDtypeStruct((m, d), F32),
        scratch_shapes=[pltpu.VMEM((tm, ff), BF16)],
        compiler_params=_cparams("parallel"),
        name="mlp_block",
    )(x2d, w1, w2, g1.reshape(1, d), g2.reshape(1, d))


def _halo_specs(tm, seq_len, width, col):
    r = tm // POOL_HALO
    last = seq_len // POOL_HALO - 1

    def prev_map(bi, i, *_):
        return (bi, jnp.maximum(i * r - 1, 0), col(*_))

    def main_map(bi, i, *_):
        return (bi, i, col(*_))

    def next_map(bi, i, *_):
        return (bi, jnp.minimum((i + 1) * r, last), col(*_))

    return [
        pl.BlockSpec((1, POOL_HALO, width), prev_map),
        pl.BlockSpec((1, tm, width), main_map),
        pl.BlockSpec((1, POOL_HALO, width), next_map),
    ]


def _with_halo(prev_ref, main_ref, next_ref):
    i = pl.program_id(1)
    prev = jnp.where(i > 0, prev_ref[0], 0.0)
    nxt = jnp.where(i < pl.num_programs(1) - 1, next_ref[0], 0.0)
    return jnp.concatenate([prev, main_ref[0], nxt], axis=0)


def _shift_rows(v, k):
    return pltpu.roll(v, k % v.shape[0], 0)


def _even_mixer_kernel(xp_ref, x_ref, xn_ref, gi_ref, wi_ref, wg_ref, ps_ref, cw_ref, wo_ref, g_ref, o_ref,
                       *, tm, seq_len):
    x_ext = _with_halo(xp_ref, x_ref, xn_ref)
    ext = _dot(_rms(x_ext, gi_ref[...]).astype(BF16), wi_ref[...])
    lo, hi = POOL_HALO, POOL_HALO + tm
    t = pl.program_id(1) * tm + lax.broadcasted_iota(jnp.int32, (tm, 1), 0)
    parts = []
    for gi, win in enumerate(POOL_WINDOWS):
        half = win // 2
        u = ext[:, gi * POOL_GROUP_DIM:(gi + 1) * POOL_GROUP_DIM]
        s = u + _shift_rows(u, 1)
        step = 1
        while 2 * step < win:
            s = _shift_rows(s, step) + _shift_rows(s, -step)
            step *= 2
        cnt = (jnp.minimum(t + half, seq_len) - jnp.maximum(t - half, 0)).astype(F32)
        pooled = s[lo:hi] / cnt - u[lo:hi]
        y = _dot(pooled.astype(BF16), wg_ref[gi])
        parts.append((y * ps_ref[:, gi * POOL_GROUP_DIM:(gi + 1) * POOL_GROUP_DIM]).astype(BF16))
    b_gate = ext[lo:hi, D_POOL:D_POOL + D_CONV]
    ch = ext[:, D_POOL + D_CONV:D_POOL + 2 * D_CONV] * ext[:, D_POOL + 2 * D_CONV:D_POOL + 3 * D_CONV]
    conv = cw_ref[0:1, :] * _shift_rows(ch, 1) + cw_ref[1:2, :] * ch + cw_ref[2:3, :] * _shift_rows(ch, -1)
    parts.append((b_gate * conv[lo:hi]).astype(BF16))
    mix = _dot(jnp.concatenate(parts, axis=-1), wo_ref[...])
    o_ref[0] = x_ref[0] + _rms(mix, g_ref[...])


def even_mixer(x, g_in, w_in, w_group, pool_scale, conv_w, w_out, layer, g, *, tm):
    b, l, d = x.shape
    kern = functools.partial(_even_mixer_kernel, tm=tm, seq_len=l)
    return pl.pallas_call(
        kern,
        grid=(b, l // tm),
        in_specs=_halo_specs(tm, l, d, lambda: 0) + [
            pl.BlockSpec((1, d), lambda bi, i: (0, 0)),
            _layer_spec(w_in.shape[1:], layer, lambda bi, i: (0, 0)),
            _layer_spec(w_group.shape[1:], layer, lambda bi, i: (0, 0, 0)),
            pl.BlockSpec((1, D_POOL), lambda bi, i: (0, 0)),
            pl.BlockSpec((3, D_CONV), lambda bi, i: (0, 0)),
            _layer_spec((d, d), layer, lambda bi, i: (0, 0)),
            pl.BlockSpec((1, d), lambda bi, i: (0, 0)),
        ],
        out_specs=pl.BlockSpec((1, tm, d), lambda bi, i: (bi, i, 0)),
        out_shape=jax.ShapeDtypeStruct(x.shape, F32),
        compiler_params=_cparams("parallel", "parallel"),
        name="even_mixer",
    )(x, x, x, g_in.reshape(1, d), w_in, w_group, pool_scale.reshape(1, D_POOL), conv_w, w_out, g.reshape(1, d))


def _cmul(ar, ai, br, bi):
    return ar * br - ai * bi, ar * bi + ai * br


def _s5_toeplitz_kernel(m_ref, o_ref):
    tc, hd = S5_CHUNK, S5_GROUP_DIM
    for g in range(m_ref.shape[0]):
        m = m_ref[g]
        rows = [m[:, (tc - 1 - s) * hd:(2 * tc - 1 - s) * hd] for s in range(tc)]
        o_ref[g] = jnp.concatenate(rows, axis=0).astype(o_ref.dtype)


def s5_toeplitz(lag_table):
    g, hd, w = lag_table.shape
    k = w // 2
    return pl.pallas_call(
        _s5_toeplitz_kernel,
        grid=(g // S5_SUPER,),
        in_specs=[pl.BlockSpec((S5_SUPER, hd, w), lambda i: (i, 0, 0))],
        out_specs=pl.BlockSpec((S5_SUPER, k, k), lambda i: (i, 0, 0)),
        out_shape=jax.ShapeDtypeStruct((g, k, k), BF16),
        compiler_params=_cparams("parallel"),
        name="s5_toeplitz",
    )(lag_table)


def s5_matrices(lam_re, lam_im, log_dt, b_re, b_im, c_re, c_im):
    tc, hd, n = S5_CHUNK, S5_GROUP_DIM, S5_STATE
    hp = lax.Precision.HIGHEST
    lr = jnp.minimum(lam_re, -1e-4)
    li = lam_im
    dt = jnp.exp(log_dt)[..., None]
    taus = jnp.arange(tc + 1, dtype=F32)[:, None, None, None]
    rmag = jnp.exp(lr[None] * dt[None] * (tc - taus))
    rang = li[None] * dt[None] * (tc - taus)
    pwr_re, pwr_im = rmag * jnp.cos(rang), rmag * jnp.sin(rang)
    mag = jnp.exp(lr[None] * dt[None] * taus)
    ang = li[None] * dt[None] * taus
    pw_re, pw_im = mag * jnp.cos(ang), mag * jnp.sin(ang)
    nr, ni = pw_re[1] - 1.0, pw_im[1]
    den = lr * lr + li * li
    coef_re, coef_im = (nr * lr + ni * li) / den, (ni * lr - nr * li) / den
    cb_re, cb_im = _cmul(coef_re[..., None], coef_im[..., None], b_re[None], b_im[None])
    cbt_re, cbt_im = _cmul(coef_re[:, :, None, :], coef_im[:, :, None, :],
                           jnp.swapaxes(b_re, 1, 2)[None], jnp.swapaxes(b_im, 1, 2)[None])
    def lag_kernels(d, p_re, p_im):
        q_re, q_im = _cmul(p_re[..., None], p_im[..., None], cb_re[d][None], cb_im[d][None])
        c_cat = jnp.concatenate([c_re[d], -c_im[d]], axis=-1)
        return jnp.einsum('gkn,tgnj->gjtk', c_cat, jnp.concatenate([q_re, q_im], axis=2), precision=hp)

    m_fwd = lag_kernels(0, pw_re[:tc, 0], pw_im[:tc, 0])
    m_bwd = lag_kernels(1, pwr_re[1:tc + 1, 1], pwr_im[1:tc + 1, 1])
    lag_table = jnp.concatenate([m_bwd[:, :, :tc - 1], m_bwd[:, :, tc - 1:] + m_fwd[:, :, :1], m_fwd[:, :, 1:],
                                 jnp.zeros_like(m_fwd[:, :, :1])], axis=2)
    intra = s5_toeplitz(lag_table.reshape(S5_GROUPS, hd, 2 * tc * hd))
    sf_re, sf_im = _cmul(pwr_re[1:tc + 1, 0][:, :, None, :], pwr_im[1:tc + 1, 0][:, :, None, :],
                         cbt_re[0][None], cbt_im[0][None])
    sb_re, sb_im = _cmul(pw_re[:tc, 1][:, :, None, :], pw_im[:tc, 1][:, :, None, :],
                         cbt_re[1][None], cbt_im[1][None])
    summ = jnp.concatenate([sf_re, sb_re, sf_im, sb_im], axis=-1)
    summ = jnp.swapaxes(summ, 0, 1).reshape(S5_GROUPS, tc * hd, 4 * n)
    gtn = lambda z: jnp.swapaxes(z, 0, 1)[:, :, None, :]
    qf_re, qf_im = _cmul(c_re[0][:, None], c_im[0][:, None], gtn(pw_re[1:tc + 1, 0]), gtn(pw_im[1:tc + 1, 0]))
    qb_re, qb_im = _cmul(c_re[1][:, None], c_im[1][:, None], gtn(pwr_re[:tc, 1]), gtn(pwr_im[:tc, 1]))
    carry = jnp.concatenate([qf_re, qb_re, -qf_im, -qb_im], axis=-1)
    carry = carry.reshape(S5_GROUPS, tc * hd, 4 * n)
    a_re = jnp.concatenate([pw_re[tc, 0], pw_re[tc, 1]], axis=-1)
    a_im = jnp.concatenate([pw_im[tc, 0], pw_im[tc, 1]], axis=-1)
    return intra, summ, carry, a_re, a_im


def s5_regroup_matrix():
    n = S5_SUPER * LANES
    src = np.arange(n)
    tb, g, h = src // LANES, (src % LANES) // S5_GROUP_DIM, src % S5_GROUP_DIM
    perm = np.zeros((n, n), np.float32)
    perm[src, g * LANES + tb * S5_GROUP_DIM + h] = 1.0
    return jnp.asarray(perm, BF16)


def _s5_regroup_kernel(p_ref, w_ref, perm_ref, u_ref, sr_ref, si_ref, *, n_chunks):
    n2 = 2 * S5_STATE
    for q in range(S5_CHUNK // S5_SUPER):
        rows_t = [p_ref[pl.ds(q * S5_SUPER + tb, n_chunks, stride=S5_CHUNK), :].astype(BF16) for tb in range(S5_SUPER)]
        grouped = _dot(jnp.concatenate(rows_t, axis=-1), perm_ref[...]).astype(BF16)
        for g in range(S5_SUPER):
            u_ref[g, :, q * LANES:(q + 1) * LANES] = grouped[:, g * LANES:(g + 1) * LANES]
    for g in range(S5_SUPER):
        s = _dot(u_ref[g], w_ref[g])
        sr_ref[pl.ds(g, n_chunks, stride=S5_SUPER), :] = s[:, :n2]
        si_ref[pl.ds(g, n_chunks, stride=S5_SUPER), :] = s[:, n2:]


def _s5_state_spec(c, n2, n_tiles):
    return pl.BlockSpec((None, c * S5_SUPER, n2), lambda a, bi: (bi * n_tiles + a, 0, 0))


def _s5_group_spec(rows, cols):
    return pl.BlockSpec((S5_SUPER, rows, cols), lambda a, bi: (a, 0, 0))


def s5_summary(p, summ):
    b, l, _ = p.shape
    c = l // S5_CHUNK
    n_tiles = S5_GROUPS // S5_SUPER
    k = S5_CHUNK * S5_GROUP_DIM
    n2 = 2 * S5_STATE
    shape = jax.ShapeDtypeStruct((b * n_tiles, c * S5_SUPER, n2), F32)
    kern = functools.partial(_s5_regroup_kernel, n_chunks=c)
    perm = s5_regroup_matrix()
    return pl.pallas_call(
        kern,
        grid=(n_tiles, b),
        in_specs=[
            pl.BlockSpec((None, l, LANES), lambda a, bi: (bi, 0, a)),
            _s5_group_spec(k, 2 * n2),
            pl.BlockSpec(perm.shape, lambda a, bi: (0, 0)),
        ],
        out_specs=[pl.BlockSpec((None, S5_SUPER, c, k), lambda a, bi: (bi * n_tiles + a, 0, 0, 0)),
                   _s5_state_spec(c, n2, n_tiles), _s5_state_spec(c, n2, n_tiles)],
        out_shape=[jax.ShapeDtypeStruct((b * n_tiles, S5_SUPER, c, k), BF16), shape, shape],
        compiler_params=_cparams("parallel", "parallel"),
        name="s5_summary",
    )(p, summ, perm)


def _s5_scan(sr_ref, si_ref, are_ref, aim_ref, or_ref, oi_ref, n_chunks):
    n2 = 2 * S5_STATE
    a_re = are_ref[...]
    a_im = aim_ref[...]
    fwd_lane = lax.broadcasted_iota(jnp.int32, (SUBLANES, n2), 1) < S5_STATE

    def body(i, state):
        x_re, x_im = state
        rows_i = pl.ds(pl.multiple_of(i * SUBLANES, SUBLANES), SUBLANES)
        rows_r = pl.ds(pl.multiple_of((n_chunks - 1 - i) * SUBLANES, SUBLANES), SUBLANES)
        or_ref[rows_i, 0:S5_STATE] = x_re[:, :S5_STATE]
        oi_ref[rows_i, 0:S5_STATE] = x_im[:, :S5_STATE]
        or_ref[rows_r, S5_STATE:n2] = x_re[:, S5_STATE:]
        oi_ref[rows_r, S5_STATE:n2] = x_im[:, S5_STATE:]
        s_re = jnp.where(fwd_lane, sr_ref[rows_i, :], sr_ref[rows_r, :])
        s_im = jnp.where(fwd_lane, si_ref[rows_i, :], si_ref[rows_r, :])
        return (a_re * x_re - a_im * x_im + s_re, a_re * x_im + a_im * x_re + s_im)

    zero = jnp.zeros((SUBLANES, n2), F32)
    lax.fori_loop(0, n_chunks, body, (zero, zero))


def _s5_output_kernel(u_ref, sr_ref, si_ref, are_ref, aim_ref, wi_ref, wc_ref, perm_ref, o_ref,
                      y_ref, xr_ref, xi_ref, *, n_chunks):
    _s5_scan(sr_ref, si_ref, are_ref, aim_ref, xr_ref, xi_ref, n_chunks)
    back = (((1,), (1,)), ((), ()))
    for g in range(S5_SUPER):
        rows = pl.ds(g, n_chunks, stride=S5_SUPER)
        xin = jnp.concatenate([xr_ref[rows, :], xi_ref[rows, :]], axis=-1).astype(BF16)
        y_ref[g] = _dot(u_ref[g], wi_ref[g]) + lax.dot_general(xin, wc_ref[g], back, preferred_element_type=F32)
    for q in range(S5_CHUNK // S5_SUPER):
        y = jnp.concatenate([y_ref[g, :, q * LANES:(q + 1) * LANES] for g in range(S5_SUPER)], axis=-1)
        hi = y.astype(BF16)
        lo = (y - hi.astype(F32)).astype(BF16)
        tok = (lax.dot_general(hi, perm_ref[...], back, preferred_element_type=F32)
               + lax.dot_general(lo, perm_ref[...], back, preferred_element_type=F32))
        for tb in range(S5_SUPER):
            o_ref[pl.ds(q * S5_SUPER + tb, n_chunks, stride=S5_CHUNK), :] = tok[:, tb * LANES:(tb + 1) * LANES]


def s5_output(u, s_re, s_im, a_re, a_im, intra, carry, *, batch):
    tiles, _, c, k = u.shape
    n_tiles = tiles // batch
    n2 = 2 * S5_STATE
    l = c * S5_CHUNK
    kern = functools.partial(_s5_output_kernel, n_chunks=c)
    perm = s5_regroup_matrix()
    decay_spec = pl.BlockSpec((S5_SUPER, n2), lambda a, bi: (a, 0))
    state_scr = pltpu.VMEM((c * S5_SUPER, n2), F32)
    return pl.pallas_call(
        kern,
        grid=(n_tiles, batch),
        in_specs=[
            pl.BlockSpec((None, S5_SUPER, c, k), lambda a, bi: (bi * n_tiles + a, 0, 0, 0)),
            _s5_state_spec(c, n2, n_tiles),
            _s5_state_spec(c, n2, n_tiles),
            decay_spec,
            decay_spec,
            _s5_group_spec(k, k),
            _s5_group_spec(k, 2 * n2),
            pl.BlockSpec(perm.shape, lambda a, bi: (0, 0)),
        ],
        out_specs=pl.BlockSpec((None, l, LANES), lambda a, bi: (bi, 0, a)),
        out_shape=jax.ShapeDtypeStruct((batch, l, D_S5), F32),
        scratch_shapes=[pltpu.VMEM((S5_SUPER, c, k), F32), state_scr, state_scr],
        compiler_params=_cparams("parallel", "parallel"),
        name="s5_output",
    )(u, s_re, s_im, a_re, a_im, intra, carry, perm)


def s5_core(p, mats):
    intra, summ, carry, a_re, a_im = mats
    u, s_re, s_im = s5_summary(p, summ.astype(BF16))
    return s5_output(u, s_re, s_im, a_re, a_im, intra.astype(BF16), carry.astype(BF16), batch=p.shape[0])


def _to_time_tiles(y):
    nf, c = DFT_FAST, y.shape[-1]
    slabs = [y[s * nf:(s + 1) * nf].reshape(DFT_TILES, SUBLANES, c) for s in range(y.shape[0] // nf)]
    return jnp.concatenate(slabs, axis=1)


def _from_time_tiles(v):
    c = v.shape[-1]
    slabs = [v[:, s * SUBLANES:(s + 1) * SUBLANES, :].reshape(DFT_FAST, c) for s in range(v.shape[1] // SUBLANES)]
    return jnp.concatenate(slabs, axis=0)


def _odd_in_kernel(xp_ref, x_ref, xn_ref, g_ref, w_ref, cw_ref, cb_ref, s5_ref, gates_ref, *, tm):
    ext = _with_halo(xp_ref, x_ref, xn_ref)
    p = _dot(_rms(ext, g_ref[...]).astype(BF16), w_ref[...])
    lo, hi = POOL_HALO, POOL_HALO + tm
    s5_ref[0] = p[lo:hi, :D_S5]
    ph = p[:, D_S5:]
    y = cw_ref[0:1, :] * _shift_rows(ph, 1) + cw_ref[1:2, :] * ph + cw_ref[2:3, :] * _shift_rows(ph, -1)
    y = y[lo:hi] + cb_ref[...]
    for k in range(3):
        for ct in range(D_HYENA // LANES):
            col = k * D_HYENA + ct * LANES
            gates_ref[k, ct] = _to_time_tiles(y[:, col:col + LANES])


def odd_in(x, g, w, layer, conv_w, conv_b, *, tm):
    b, l, d = x.shape
    n = w.shape[-1]
    rows = tm // DFT_FAST * SUBLANES
    cts = D_HYENA // LANES
    kern = functools.partial(_odd_in_kernel, tm=tm)
    return pl.pallas_call(
        kern,
        grid=(b, l // tm),
        in_specs=_halo_specs(tm, l, d, lambda: 0) + [
            pl.BlockSpec((1, d), lambda bi, i: (0, 0)),
            _layer_spec((d, n), layer, lambda bi, i: (0, 0)),
            pl.BlockSpec((3, n - D_S5), lambda bi, i: (0, 0)),
            pl.BlockSpec((1, n - D_S5), lambda bi, i: (0, 0)),
        ],
        out_specs=[pl.BlockSpec((1, tm, D_S5), lambda bi, i: (bi, i, 0)),
                   pl.BlockSpec((3, None, cts, DFT_TILES, rows, LANES), lambda bi, i: (0, bi, 0, 0, i, 0))],
        out_shape=[jax.ShapeDtypeStruct((b, l, D_S5), F32),
                   jax.ShapeDtypeStruct((3, b, cts, DFT_TILES, l // DFT_FAST * SUBLANES, LANES), F32)],
        compiler_params=_cparams("parallel", "parallel"),
        name="odd_in",
    )(x, x, x, g.reshape(1, d), w, conv_w, conv_b.reshape(1, -1))


def _filter_mlp_kernel(bands_ref, w1t_ref, w1cs_ref, b1_ref, w2_ref, b2_ref, w3h_ref, w3l_ref, fr_ref, dl_ref,
                       eo_ref, asum_ref, *, tl, seq_len):
    i = pl.program_id(0)
    hp = lax.Precision.HIGHEST
    c = D_HYENA
    half = tl // 2
    first = i * tl + lax.broadcasted_iota(jnp.int32, (half, 1), 0)

    def positions(width):
        right = lax.broadcasted_iota(jnp.int32, (1, width), 1) >= width // 2
        return (first + jnp.where(right, half, 0)).astype(F32)

    feat = bands_ref.shape[1]
    ang = (2.0 * math.pi / seq_len) * positions(feat) * bands_ref[...]
    is_cos = lax.broadcasted_iota(jnp.int32, (1, feat), 1) % (2 * HYENA_BANDS) < HYENA_BANDS
    cs = jnp.sin(ang + jnp.where(is_cos, 0.5 * math.pi, math.pi))
    fr = fr_ref[...]
    z = (positions(w1t_ref.shape[1]) / (seq_len - 1.0)) * w1t_ref[...] + jnp.dot(
        cs, w1cs_ref[...], precision=hp, preferred_element_type=F32)
    h = jnp.sin(fr * (z + b1_ref[...]))
    h = jnp.sin(fr * (jnp.dot(h, w2_ref[...], precision=hp, preferred_element_type=F32) + b2_ref[...]))
    h_hi = h.astype(BF16)
    h_lo = (h - h_hi.astype(F32)).astype(BF16)
    h = jnp.concatenate([_dot(h_hi, w3h_ref[r]) + (_dot(h_lo, w3h_ref[r]) + _dot(h_hi, w3l_ref[r])) for r in range(2)],
                        axis=0)
    t_norm = (i * tl + lax.broadcasted_iota(jnp.int32, (tl, 1), 0)).astype(F32) / (seq_len - 1.0)
    decay = jnp.exp(-t_norm * dl_ref[...])

    @pl.when(i == 0)
    def _():
        asum_ref[...] = jnp.zeros_like(asum_ref)

    for o in range(h.shape[1] // (2 * c)):
        fwd = h[:, 2 * o * c:(2 * o + 1) * c] * decay
        bwd = h[:, (2 * o + 1) * c:(2 * o + 2) * c] * decay
        for part, val in enumerate((fwd + bwd, fwd - bwd)):
            for ct in range(c // LANES):
                eo_ref[((2 * o + part) * c) // LANES + ct] = _to_time_tiles(val[:, ct * LANES:(ct + 1) * LANES])
        asum_ref[:, o * c:(o + 1) * c] += jnp.sum(jnp.abs(fwd) + jnp.abs(bwd), axis=0, keepdims=True)


def hyena_filter_mlp(seq_len, w1, b1, w2, b2, w3, freq, *, tl):
    n_out = w3.shape[1]
    bands = jnp.linspace(1e-4, HYENA_BANDS - 1, HYENA_BANDS, dtype=F32)
    bands2 = jnp.concatenate([bands, bands])[None, :]
    deltas = jnp.abs(jnp.linspace(math.log(HYENA_TARGET) / HYENA_LONG_DECAY_PCT,
                                  math.log(HYENA_TARGET) / HYENA_SHORT_DECAY_PCT, D_HYENA, dtype=F32))[None, :]
    kern = functools.partial(_filter_mlp_kernel, tl=tl, seq_len=seq_len)
    full = lambda a: pl.BlockSpec(a.shape, lambda i: (0,) * a.ndim)
    w3_hi = w3.astype(BF16)
    w3_lo = (w3 - w3_hi.astype(F32)).astype(BF16)
    twice = lambda v: jnp.concatenate([v, v])[None, :]
    blockdiag = lambda m: jnp.concatenate([jnp.pad(m, ((0, 0), (0, m.shape[1]))),
                                           jnp.pad(m, ((0, 0), (m.shape[1], 0)))], axis=0)
    per_half = lambda m: jnp.stack([jnp.pad(m, ((0, m.shape[0]), (0, 0))), jnp.pad(m, ((m.shape[0], 0), (0, 0)))])
    args = [twice(bands2[0]), twice(w1[0]), blockdiag(w1[1:]), twice(b1), blockdiag(w2), twice(b2),
            per_half(w3_hi), per_half(w3_lo), twice(freq), deltas]
    return pl.pallas_call(
        kern,
        grid=(seq_len // tl,),
        in_specs=[full(a) for a in args],
        out_specs=[pl.BlockSpec((n_out // LANES, DFT_TILES, tl // DFT_FAST * SUBLANES, LANES), lambda i: (0, 0, i, 0)),
                   pl.BlockSpec((1, n_out // 2), lambda i: (0, 0))],
        out_shape=[jax.ShapeDtypeStruct((n_out // LANES, DFT_TILES, seq_len // DFT_FAST * SUBLANES, LANES), F32),
                   jax.ShapeDtypeStruct((1, n_out // 2), F32)],
        compiler_params=_cparams("arbitrary"),
        name="hyena_filter_mlp",
    )(*args)


def dft_tables(seq_len):
    n = 2 * seq_len
    nf = DFT_FAST
    ns = n // nf
    k_a = np.arange(ns)[:, None]
    n_s = np.arange(ns // 2)[None, :]
    w_s = np.exp(-2j * np.pi * ((k_a * n_s) % ns) / ns)
    n_f = np.arange(nf)[:, None]
    tw = np.exp(-2j * np.pi * ((n_f * np.arange(ns)[None, :]) % n) / n)
    c32 = lambda z: (jnp.asarray(z.real, F32), jnp.asarray(z.imag, F32))
    ws_re, ws_im = c32(w_s)
    tw_re, tw_im = c32(tw)
    g_re = tw_re[:, :, None] * ws_re[None] - tw_im[:, :, None] * ws_im[None]
    g_im = tw_re[:, :, None] * ws_im[None] + tw_im[:, :, None] * ws_re[None]
    fwd_a = jnp.concatenate([jnp.concatenate([g_re, -g_im], axis=2),
                             jnp.concatenate([g_im, g_re], axis=2)], axis=1)
    gt_re = jnp.swapaxes(g_re, 1, 2) * (1.0 / n)
    gt_im = jnp.swapaxes(g_im, 1, 2) * (1.0 / n)
    inv_a = jnp.concatenate([jnp.concatenate([gt_re, gt_im], axis=2),
                             jnp.concatenate([-gt_im, gt_re], axis=2)], axis=1)
    kk = np.arange(nf)
    w_f = np.exp(-2j * np.pi * ((kk[:, None] * kk[None, :]) % nf) / nf)
    f_re, f_im = c32(w_f)
    fwd_b = jnp.concatenate([jnp.concatenate([f_re, -f_im], axis=1),
                             jnp.concatenate([f_im, f_re], axis=1)], axis=0)
    inv_b = jnp.concatenate([jnp.concatenate([f_re, f_im], axis=1),
                             jnp.concatenate([-f_im, f_re], axis=1)], axis=0)
    kept = min(ns, -(-(ns // 2 + 1) // SUBLANES) * SUBLANES)
    fwd_a_real = jnp.concatenate([fwd_a[:, :kept, :ns // 2], fwd_a[:, ns:ns + kept, :ns // 2]], axis=1)
    m_re, m_im = jnp.roll(f_re, -1, axis=0), jnp.roll(f_im, -1, axis=0)
    mirror_part = jnp.stack([jnp.concatenate([m_re, m_im], axis=1), jnp.concatenate([m_im, -m_re], axis=1)])
    return dict(fwd_a=fwd_a.astype(BF16), fwd_a_real=fwd_a_real.astype(BF16),
                inv_a=inv_a.astype(BF16), fwd_b=fwd_b.astype(BF16), inv_b=inv_b.astype(BF16),
                fwd_b_part=fwd_b.reshape(2, nf, 2 * nf).astype(BF16), mirror_b_part=mirror_part.astype(BF16))


def _filter_dft_kernel(x_ref, ga_ref, fb_ref, fbm_ref, nrm_ref, k_ref, ar_scr, ai_scr):
    nf = DFT_FAST
    kept = ga_ref.shape[1] // 2
    n_in = ga_ref.shape[2]
    ns = 2 * n_in

    def stage_a(q, carry):
        for j in range(SUBLANES):
            x = x_ref[q, pl.ds(j, n_in, stride=SUBLANES), :].astype(BF16)
            n_f = q * SUBLANES + j
            a = _dot(ga_ref[n_f], x)
            ar_scr[pl.ds(n_f, kept, stride=DFT_PITCH), :] = a[:kept]
            ai_scr[pl.ds(n_f, kept, stride=DFT_PITCH), :] = a[kept:]
        return carry

    lax.fori_loop(0, DFT_TILES, stage_a, 0, unroll=DFT_UNROLL // SUBLANES)
    inv_norm = 1.0 / (nrm_ref[...] + 1e-6)

    def slab(start):
        return jnp.concatenate([ar_scr[pl.ds(start, nf), :], ai_scr[pl.ds(start, nf), :]], axis=0).astype(BF16)

    for k in (0, n_in):
        k_ref[k * nf:(k + 1) * nf, :] = _dot(fb_ref[...], slab(k * DFT_PITCH)) * inv_norm

    def stage_b(k, carry):
        a = slab(pl.multiple_of(k * DFT_PITCH, SUBLANES))
        k_ref[pl.ds(pl.multiple_of(k * nf, nf), nf), :] = _dot(fb_ref[...], a) * inv_norm
        k_ref[pl.ds(pl.multiple_of((ns - k) * nf, nf), nf), :] = _dot(fbm_ref[...], a) * inv_norm
        return carry

    pairs = n_in - 1
    lax.fori_loop(1, n_in, stage_b, 0, unroll=max(u for u in (9, 8, 7, 4, 3, 2, 1) if pairs % u == 0))


def hyena_filter_dft(eo, asum, tables):
    col_tiles, q_tiles, rows, _ = eo.shape
    l = q_tiles * rows
    c = D_HYENA
    ct = c // LANES
    orders = col_tiles // (2 * ct)
    ga, fb, fbm = tables['fwd_a_real'], tables['fwd_b_part'], tables['mirror_b_part']
    kept = ga.shape[1] // 2
    scr = pltpu.VMEM((kept * DFT_PITCH, LANES), F32)
    part_spec = pl.BlockSpec((None,) + fb.shape[1:], lambda o, j, part: (part, 0, 0))
    return pl.pallas_call(
        _filter_dft_kernel,
        grid=(orders, ct, 2),
        in_specs=[
            pl.BlockSpec((None,) + eo.shape[1:], lambda o, j, part: ((2 * o + part) * ct + j, 0, 0, 0)),
            pl.BlockSpec(ga.shape, lambda o, j, part: (0, 0, 0)),
            part_spec,
            part_spec,
            pl.BlockSpec((1, LANES), lambda o, j, part: (0, o * ct + j)),
        ],
        out_specs=pl.BlockSpec((None, None, None, 2 * l, LANES), lambda o, j, part: (o, part, j, 0, 0)),
        out_shape=jax.ShapeDtypeStruct((orders, 2, ct, 2 * l, LANES), F32),
        scratch_shapes=[scr, scr],
        compiler_params=_cparams("parallel", "parallel", "parallel"),
        name="hyena_filter_dft",
    )(eo, ga, fb, fbm, asum)


def _lane_cat(parts):
    return parts[0] if len(parts) == 1 else jnp.concatenate(parts, axis=-1)


def _hyena_conv_kernel(x_ref, ga_ref, k_ref, fb_ref, ib_ref, gi_ref, gate_ref, u_ref, bias_ref, o_ref, *scr,
                       n_a, n_b, slabs):
    nf = DFT_FAST
    tiles, qt = x_ref.shape[1], x_ref.shape[2]
    half = x_ref.shape[3] // SUBLANES
    ns = 2 * half
    step = pl.program_id(1)

    @pl.when(step < n_a)
    def _():
        for qj in range(qt * SUBLANES):
            q, j = divmod(qj, SUBLANES)
            rows = pl.ds(j, half, stride=SUBLANES)
            x = _lane_cat([jnp.concatenate([x_ref[0, t, q, rows, :], x_ref[1, t, q, rows, :]], axis=0)
                           for t in range(tiles)])
            a = _dot(ga_ref[qj], x.astype(BF16))
            dst = pl.ds(step * (qt * SUBLANES) + qj, ns, stride=DFT_PITCH)
            for t in range(tiles):
                scr[2 * t][dst, :] = a[:ns, t * LANES:(t + 1) * LANES]
                scr[2 * t + 1][dst, :] = a[ns:, t * LANES:(t + 1) * LANES]

    @pl.when((step >= n_a) & (step < n_a + n_b))
    def _():
        for s in range(slabs):
            k_a = (step - n_a) * slabs + s
            rows = pl.ds(pl.multiple_of(k_a * DFT_PITCH, SUBLANES), nf)
            a = jnp.concatenate([_lane_cat([scr[2 * t + part][rows, :] for t in range(tiles)]) for part in range(2)], axis=0)
            x = _dot(fb_ref[...], a.astype(BF16))
            xr, xi = x[:nf], x[nf:]
            kr = _lane_cat([k_ref[0, t, s * nf:(s + 1) * nf, :] for t in range(tiles)])
            ki = _lane_cat([k_ref[1, t, s * nf:(s + 1) * nf, :] for t in range(tiles)])
            y = jnp.concatenate([xr * kr - xi * ki, xr * ki + xi * kr], axis=0).astype(BF16)
            cc = _dot(ib_ref[...], y)
            for t in range(tiles):
                scr[2 * t][rows, :] = cc[:nf, t * LANES:(t + 1) * LANES]
                scr[2 * t + 1][rows, :] = cc[nf:, t * LANES:(t + 1) * LANES]

    @pl.when(step >= n_a + n_b)
    def _():
        first = (step - (n_a + n_b)) * (qt * SUBLANES)
        for qj in range(qt * SUBLANES):
            q, j = divmod(qj, SUBLANES)
            src = pl.ds(first + qj, ns, stride=DFT_PITCH)
            cc = jnp.concatenate([_lane_cat([scr[2 * t + part][src, :] for t in range(tiles)]) for part in range(2)], axis=0)
            y = _dot(gi_ref[qj], cc.astype(BF16))
            dst = pl.ds(j, half, stride=SUBLANES)
            for t in range(tiles):
                o_ref[0, t, q, dst, :] = y[:half, t * LANES:(t + 1) * LANES]
                o_ref[1, t, q, dst, :] = y[half:, t * LANES:(t + 1) * LANES]
        for t in range(tiles):
            bias = bias_ref[:, t * LANES:(t + 1) * LANES]
            for bi in range(2):
                o_ref[bi, t] = gate_ref[bi, t] * (o_ref[bi, t] + u_ref[bi, t] * bias)


def hyena_long_conv(u6, u_idx, gate6, gate_idx, kf, order, bias, tables, *, slabs=8, tiles=2, qt=2):
    _, _, cts, q_tiles, rows, _ = u6.shape
    ns = 2 * rows // SUBLANES
    n_a = q_tiles // qt
    n_b = ns // slabs
    ga, gi, fb, ib = tables['fwd_a'], tables['inv_a'], tables['fwd_b'], tables['inv_b']
    q_fwd = lambda s: jnp.minimum(s, n_a - 1)
    q_mid = lambda s: jnp.clip(s - n_a, 0, n_b - 1)
    q_inv = lambda s: jnp.clip(s - n_a - n_b, 0, n_a - 1)

    def seq_spec(idx, q_of):
        return pl.BlockSpec((None, 2, tiles, qt, rows, LANES), lambda c, s: (idx, 0, c, q_of(s), 0, 0))

    scr = pltpu.VMEM((ns * DFT_PITCH, LANES), F32)
    kern = functools.partial(_hyena_conv_kernel, n_a=n_a, n_b=n_b, slabs=slabs)
    return pl.pallas_call(
        kern,
        grid=(cts // tiles, 2 * n_a + n_b),
        in_specs=[
            seq_spec(u_idx, q_fwd),
            pl.BlockSpec((qt * SUBLANES,) + ga.shape[1:], lambda c, s: (q_fwd(s), 0, 0)),
            pl.BlockSpec((None, 2, tiles, slabs * DFT_FAST, LANES), lambda c, s: (order, 0, c, q_mid(s), 0)),
            pl.BlockSpec(fb.shape, lambda c, s: (0, 0)),
            pl.BlockSpec(ib.shape, lambda c, s: (0, 0)),
            pl.BlockSpec((qt * SUBLANES,) + gi.shape[1:], lambda c, s: (q_inv(s), 0, 0)),
            seq_spec(gate_idx, q_inv),
            seq_spec(u_idx, q_inv),
            pl.BlockSpec((1, tiles * LANES), lambda c, s: (0, c)),
        ],
        out_specs=pl.BlockSpec((2, tiles, qt, rows, LANES), lambda c, s: (0, c, q_inv(s), 0, 0)),
        out_shape=jax.ShapeDtypeStruct((2, cts, q_tiles, rows, LANES), F32),
        scratch_shapes=[scr] * (2 * tiles),
        compiler_params=_cparams("arbitrary", "arbitrary", vmem=VMEM_LIMIT_CONV),
        name="hyena_long_conv",
    )(u6, ga, kf, fb, ib, gi, gate6, u6, bias.reshape(1, -1))


def hyena_filter_spectra(seq_len, tables, w1, b1, w2, b2, w3, freq):
    eo, asum = hyena_filter_mlp(seq_len, w1, b1, w2, b2, w3, freq, tl=512)
    return hyena_filter_dft(eo, asum, tables)


def hyena_mixer(gates, kf, bias, tables):
    z = hyena_long_conv(gates, 2, gates, 1, kf, 0, bias[0], tables)
    return hyena_long_conv(z[None], 0, gates, 0, kf, 1, bias[1], tables)


def _odd_out_kernel(ys_ref, u_ref, hy_ref, x_ref, d_ref, wglu_ref, wo_ref, g_ref, o_ref):
    y = ys_ref[0] + d_ref[...] * u_ref[0]
    c0 = math.sqrt(2.0 / math.pi)
    gl = 0.5 * y * (1.0 + jnp.tanh(c0 * (y + 0.044715 * (y * y * y))))
    z = _dot(gl.astype(BF16), wglu_ref[...])
    s5 = gl * (1.0 / (1.0 + jnp.exp(-z)))
    hy = [_from_time_tiles(hy_ref[ct]).astype(BF16) for ct in range(hy_ref.shape[0])]
    mix = _dot(jnp.concatenate([s5.astype(BF16)] + hy, axis=-1), wo_ref[...])
    o_ref[0] = x_ref[0] + _rms(mix, g_ref[...])


def odd_out(ys, u_s5, hy, x, d_skip, w_glu, w_out, layer, g, *, tm):
    b, l, d = x.shape
    tok = lambda w: pl.BlockSpec((1, tm, w), lambda bi, i: (bi, i, 0))
    hy_spec = pl.BlockSpec((None, D_HYENA // LANES, DFT_TILES, tm // DFT_FAST * SUBLANES, LANES),
                           lambda bi, i: (bi, 0, 0, i, 0))
    return pl.pallas_call(
        _odd_out_kernel,
        grid=(b, l // tm),
        in_specs=[tok(D_S5), tok(D_S5), hy_spec, tok(d),
                  pl.BlockSpec((1, D_S5), lambda bi, i: (0, 0)),
                  _layer_spec((D_S5, D_S5), layer, lambda bi, i: (0, 0)),
                  _layer_spec((d, d), layer, lambda bi, i: (0, 0)),
                  pl.BlockSpec((1, d), lambda bi, i: (0, 0))],
        out_specs=tok(d),
        out_shape=jax.ShapeDtypeStruct(x.shape, F32),
        compiler_params=_cparams("parallel", "parallel"),
        name="odd_out",
    )(ys, u_s5, hy, x, d_skip.reshape(1, D_S5), w_glu, w_out, g.reshape(1, d))


def kernel(x, mem, norm_mix, norm_xattn, norm_mem, norm_mlp, xa_wq, xa_wk, xa_wv, xa_wo, mlp_w1, mlp_w2, ev_w_in, ev_pool_w, ev_pool_scale, ev_conv_w, ev_w_out, od_w_in, od_s5_lambda_re, od_s5_lambda_im, od_s5_log_dt, od_s5_b_re, od_s5_b_im, od_s5_c_re, od_s5_c_im, od_s5_d, od_s5_w_glu, od_hy_short_w, od_hy_short_b, od_hy_w1, od_hy_b1, od_hy_w2, od_hy_b2, od_hy_w3, od_hy_freq, od_hy_bias, od_w_out):
    b, l, d = x.shape
    depth = norm_mix.shape[0]
    assert b == 2, "the long convolution packs the two batch rows as one complex signal"
    tables = dft_tables(l)
    mem2d = mem.reshape(b * N_MEM, d)
    wq, wo = xa_wq.astype(BF16), xa_wo.astype(BF16)
    wkv = jnp.concatenate([xa_wk, xa_wv], axis=2).astype(BF16)
    kv = memory_kv(mem2d, norm_mem, wkv).reshape(depth, b, N_MEM, 2 * d)
    w1, w2 = mlp_w1.astype(BF16), mlp_w2.astype(BF16)
    ev_in, ev_out, ev_pool = ev_w_in.astype(BF16), ev_w_out.astype(BF16), ev_pool_w.astype(BF16)
    od_in, od_out, od_glu = od_w_in.astype(BF16), od_w_out.astype(BF16), od_s5_w_glu.astype(BF16)
    for i in range(depth):
        j = i // 2
        if i % 2 == 0:
            x = even_mixer(x, norm_mix[i, 0], ev_in, ev_pool, ev_pool_scale[j], ev_conv_w[j], ev_out, j,
                           norm_mix[i, 1], tm=1024)
        else:
            u_s5, gates = odd_in(x, norm_mix[i, 0], od_in, j, od_hy_short_w[j], od_hy_short_b[j], tm=1024)
            mats = s5_matrices(od_s5_lambda_re[j], od_s5_lambda_im[j], od_s5_log_dt[j], od_s5_b_re[j],
                               od_s5_b_im[j], od_s5_c_re[j], od_s5_c_im[j])
            ys = s5_core(u_s5, mats)
            kf = hyena_filter_spectra(l, tables, od_hy_w1[j], od_hy_b1[j], od_hy_w2[j], od_hy_b2[j],
                                      od_hy_w3[j], od_hy_freq[j])
            hy = hyena_mixer(gates, kf, od_hy_bias[j], tables)
            x = odd_out(ys, u_s5, hy, x, od_s5_d[j], od_glu, od_out, j, norm_mix[i, 1], tm=1024)
        x = xattn_block(x, kv, wq, wo, i, norm_xattn[i, 0], norm_xattn[i, 1], tm=1024)
        x = mlp_block(x.reshape(b * l, d), w1, w2, i, norm_mlp[i, 0], norm_mlp[i, 1],
                      tm=1024, tf=512).reshape(b, l, d)
    return x
```

```python
import functools
import math

import numpy as np
import jax
import jax.numpy as jnp
from jax import lax
from jax.experimental import pallas as pl
from jax.experimental.pallas import tpu as pltpu

F32 = jnp.float32
BF16 = jnp.bfloat16

D_MODEL = 1024
N_MEM = 256
RMS_EPS = 1e-6
D_POOL = 512
POOL_WINDOWS = (2, 4, 8, 16)
POOL_GROUP_DIM = 128
POOL_HALO = 8
D_CONV = 512
D_S5 = 512
S5_GROUP_DIM = 16
S5_GROUPS = 32
S5_STATE = 64
S5_CHUNK = 16
D_HYENA = 512
HYENA_BANDS = 16
HYENA_FFN = 64
HYENA_TARGET = 1e-2
HYENA_SHORT_DECAY_PCT = 0.3
HYENA_LONG_DECAY_PCT = 1.5
XA_HEADS = 4
XA_HEAD_DIM = 256
D_FF = 4096

LANES = 128
SUBLANES = 8
S5_SUPER = LANES // S5_GROUP_DIM
DFT_FAST = 128
DFT_TILES = DFT_FAST // SUBLANES
DFT_PITCH = DFT_FAST + SUBLANES
DFT_UNROLL = 16
ROW_TILE = 1024
MLP_HIDDEN_TILE = 512
FILTER_ROW_TILE = 512
VMEM_LIMIT = 56 * 1024 * 1024
VMEM_LIMIT_CONV = 62 * 1024 * 1024


def _cparams(*sem, vmem=VMEM_LIMIT):
    return pltpu.CompilerParams(dimension_semantics=sem, vmem_limit_bytes=vmem)


def _rms(xf, g):
    ms = jnp.mean(xf * xf, axis=-1, keepdims=True)
    return xf * lax.rsqrt(ms + RMS_EPS) * g


def _dot(a, b):
    return jnp.dot(a, b, preferred_element_type=F32)


def _layer_spec(block, layer, tail_map):
    return pl.BlockSpec((None,) + block, lambda *idx: (layer,) + tail_map(*idx))


def _norm_matmul_kernel(x_ref, g_ref, w_ref, o_ref):
    xn = _rms(x_ref[...], g_ref[...]).astype(BF16)
    o_ref[...] = _dot(xn, w_ref[...]).astype(o_ref.dtype)


def memory_kv(mem2d, g, w):
    layers, d, n = w.shape
    m = mem2d.shape[0]
    return pl.pallas_call(
        _norm_matmul_kernel,
        grid=(layers,),
        in_specs=[
            pl.BlockSpec((m, d), lambda i: (0, 0)),
            pl.BlockSpec((None, 1, d), lambda i: (i, 0, 0)),
            pl.BlockSpec((None, d, n), lambda i: (i, 0, 0)),
        ],
        out_specs=pl.BlockSpec((None, m, n), lambda i: (i, 0, 0)),
        out_shape=jax.ShapeDtypeStruct((layers, m, n), BF16),
        compiler_params=_cparams("parallel"),
        name="memory_kv",
    )(mem2d, g.reshape(layers, 1, d), w)


def _xattn_kernel(x_ref, kv_ref, wq_ref, wo_ref, g1_ref, g2_ref, o_ref):
    x = x_ref[0]
    xn = _rms(x, g1_ref[...]).astype(BF16)
    q = (_dot(xn, wq_ref[...]) * (XA_HEAD_DIM ** -0.5)).astype(BF16)
    heads = []
    for h in range(XA_HEADS):
        lo = h * XA_HEAD_DIM
        qh = q[:, lo:lo + XA_HEAD_DIM]
        kh = kv_ref[0, :, lo:lo + XA_HEAD_DIM]
        vh = kv_ref[0, :, D_MODEL + lo:D_MODEL + lo + XA_HEAD_DIM]
        s = lax.dot_general(qh, kh, (((1,), (1,)), ((), ())), preferred_element_type=F32)
        e = jnp.exp(s - jnp.max(s, axis=-1, keepdims=True))
        p = e / jnp.sum(e, axis=-1, keepdims=True)
        heads.append(_dot(p.astype(BF16), vh).astype(BF16))
    o = jnp.concatenate(heads, axis=-1)
    y = _dot(o, wo_ref[...])
    o_ref[0] = x + _rms(y, g2_ref[...])


def xattn_block(x, kv, wq, wo, layer, g1, g2, *, tm):
    b, l, d = x.shape
    return pl.pallas_call(
        _xattn_kernel,
        grid=(b, l // tm),
        in_specs=[
            pl.BlockSpec((1, tm, d), lambda bi, i: (bi, i, 0)),
            pl.BlockSpec((None, 1, N_MEM, 2 * d), lambda bi, i: (layer, bi, 0, 0)),
            _layer_spec((d, d), layer, lambda bi, i: (0, 0)),
            _layer_spec((d, d), layer, lambda bi, i: (0, 0)),
            pl.BlockSpec((1, d), lambda bi, i: (0, 0)),
            pl.BlockSpec((1, d), lambda bi, i: (0, 0)),
        ],
        out_specs=pl.BlockSpec((1, tm, d), lambda bi, i: (bi, i, 0)),
        out_shape=jax.ShapeDtypeStruct(x.shape, F32),
        compiler_params=_cparams("parallel", "parallel"),
        name="xattn_block",
    )(x, kv, wq, wo, g1.reshape(1, d), g2.reshape(1, d))


def _mlp_kernel(x_ref, w1_ref, w2_ref, g1_ref, g2_ref, o_ref, h_ref, *, tf):
    x = x_ref[...]
    xn = _rms(x, g1_ref[...]).astype(BF16)
    for c in range(h_ref.shape[1] // tf):
        cols = slice(c * tf, (c + 1) * tf)
        h = jnp.maximum(_dot(xn, w1_ref[:, cols]), 0.0)
        h_ref[:, cols] = (h * h).astype(BF16)
    o_ref[...] = x + _rms(_dot(h_ref[...], w2_ref[...]), g2_ref[...])


def mlp_block(x2d, w1, w2, layer, g1, g2, *, tm, tf):
    m, d = x2d.shape
    ff = w1.shape[-1]
    resident = dict(pipeline_mode=pl.Buffered(1))
    return pl.pallas_call(
        functools.partial(_mlp_kernel, tf=tf),
        grid=(m // tm,),
        in_specs=[
            pl.BlockSpec((tm, d), lambda i: (i, 0)),
            pl.BlockSpec((None, d, ff), lambda i: (layer, 0, 0), **resident),
            pl.BlockSpec((None, ff, d), lambda i: (layer, 0, 0), **resident),
            pl.BlockSpec((1, d), lambda i: (0, 0)),
            pl.BlockSpec((1, d), lambda i: (0, 0)),
        ],
        out_specs=pl.BlockSpec((tm, d), lambda i: (i, 0)),
        out_shape=jax.ShapeDtypeStruct((m, d), F32),
        scratch_shapes=[pltpu.VMEM((tm, ff), BF16)],
        compiler_params=_cparams("parallel"),
        name="mlp_block",
    )(x2d, w1, w2, g1.reshape(1, d), g2.reshape(1, d))


def _halo_specs(tm, seq_len, width, col):
    r = tm // POOL_HALO
    last = seq_len // POOL_HALO - 1

    def prev_map(bi, i, *_):
        return (bi, jnp.maximum(i * r - 1, 0), col(*_))

    def main_map(bi, i, *_):
        return (bi, i, col(*_))

    def next_map(bi, i, *_):
        return (bi, jnp.minimum((i + 1) * r, last), col(*_))

    return [
        pl.BlockSpec((1, POOL_HALO, width), prev_map),
        pl.BlockSpec((1, tm, width), main_map),
        pl.BlockSpec((1, POOL_HALO, width), next_map),
    ]


def _with_halo(prev_ref, main_ref, next_ref):
    i = pl.program_id(1)
    prev = jnp.where(i > 0, prev_ref[0], 0.0)
    nxt = jnp.where(i < pl.num_programs(1) - 1, next_ref[0], 0.0)
    return jnp.concatenate([prev, main_ref[0], nxt], axis=0)


def _shift_rows(v, k):
    return pltpu.roll(v, k % v.shape[0], 0)


def _even_mixer_kernel(xp_ref, x_ref, xn_ref, gi_ref, wi_ref, wg_ref, ps_ref, cw_ref, wo_ref, g_ref, o_ref,
                       *, tm, seq_len):
    x_ext = _with_halo(xp_ref, x_ref, xn_ref)
    ext = _dot(_rms(x_ext, gi_ref[...]).astype(BF16), wi_ref[...])
    lo, hi = POOL_HALO, POOL_HALO + tm
    t = pl.program_id(1) * tm + lax.broadcasted_iota(jnp.int32, (tm, 1), 0)
    parts = []
    for gi, win in enumerate(POOL_WINDOWS):
        half = win // 2
        u = ext[:, gi * POOL_GROUP_DIM:(gi + 1) * POOL_GROUP_DIM]
        s = u + _shift_rows(u, 1)
        step = 1
        while 2 * step < win:
            s = _shift_rows(s, step) + _shift_rows(s, -step)
            step *= 2
        cnt = (jnp.minimum(t + half, seq_len) - jnp.maximum(t - half, 0)).astype(F32)
        pooled = s[lo:hi] / cnt - u[lo:hi]
        y = _dot(pooled.astype(BF16), wg_ref[gi])
        parts.append((y * ps_ref[:, gi * POOL_GROUP_DIM:(gi + 1) * POOL_GROUP_DIM]).astype(BF16))
    b_gate = ext[lo:hi, D_POOL:D_POOL + D_CONV]
    ch = ext[:, D_POOL + D_CONV:D_POOL + 2 * D_CONV] * ext[:, D_POOL + 2 * D_CONV:D_POOL + 3 * D_CONV]
    conv = cw_ref[0:1, :] * _shift_rows(ch, 1) + cw_ref[1:2, :] * ch + cw_ref[2:3, :] * _shift_rows(ch, -1)
    parts.append((b_gate * conv[lo:hi]).astype(BF16))
    mix = _dot(jnp.concatenate(parts, axis=-1), wo_ref[...])
    o_ref[0] = x_ref[0] + _rms(mix, g_ref[...])


def even_mixer(x, g_in, w_in, w_group, pool_scale, conv_w, w_out, layer, g, *, tm):
    b, l, d = x.shape
    kern = functools.partial(_even_mixer_kernel, tm=tm, seq_len=l)
    return pl.pallas_call(
        kern,
        grid=(b, l // tm),
        in_specs=_halo_specs(tm, l, d, lambda: 0) + [
            pl.BlockSpec((1, d), lambda bi, i: (0, 0)),
            _layer_spec(w_in.shape[1:], layer, lambda bi, i: (0, 0)),
            _layer_spec(w_group.shape[1:], layer, lambda bi, i: (0, 0, 0)),
            pl.BlockSpec((1, D_POOL), lambda bi, i: (0, 0)),
            pl.BlockSpec((3, D_CONV), lambda bi, i: (0, 0)),
            _layer_spec((d, d), layer, lambda bi, i: (0, 0)),
            pl.BlockSpec((1, d), lambda bi, i: (0, 0)),
        ],
        out_specs=pl.BlockSpec((1, tm, d), lambda bi, i: (bi, i, 0)),
        out_shape=jax.ShapeDtypeStruct(x.shape, F32),
        compiler_params=_cparams("parallel", "parallel"),
        name="even_mixer",
    )(x, x, x, g_in.reshape(1, d), w_in, w_group, pool_scale.reshape(1, D_POOL), conv_w, w_out, g.reshape(1, d))


def _cmul(ar, ai, br, bi):
    return ar * br - ai * bi, ar * bi + ai * br


def _s5_toeplitz_kernel(m_ref, o_ref):
    tc, hd = S5_CHUNK, S5_GROUP_DIM
    for g in range(m_ref.shape[0]):
        m = m_ref[g]
        rows = [m[:, (tc - 1 - s) * hd:(2 * tc - 1 - s) * hd] for s in range(tc)]
        o_ref[g] = jnp.concatenate(rows, axis=0).astype(o_ref.dtype)


def s5_toeplitz(lag_table):
    g, hd, w = lag_table.shape
    k = w // 2
    return pl.pallas_call(
        _s5_toeplitz_kernel,
        grid=(g // S5_SUPER,),
        in_specs=[pl.BlockSpec((S5_SUPER, hd, w), lambda i: (i, 0, 0))],
        out_specs=pl.BlockSpec((S5_SUPER, k, k), lambda i: (i, 0, 0)),
        out_shape=jax.ShapeDtypeStruct((g, k, k), BF16),
        compiler_params=_cparams("parallel"),
        name="s5_toeplitz",
    )(lag_table)


def s5_matrices(lam_re, lam_im, log_dt, b_re, b_im, c_re, c_im):
    tc, hd, n = S5_CHUNK, S5_GROUP_DIM, S5_STATE
    hp = lax.Precision.HIGHEST
    lr = jnp.minimum(lam_re, -1e-4)
    li = lam_im
    dt = jnp.exp(log_dt)[..., None]
    taus = jnp.arange(tc + 1, dtype=F32)[:, None, None, None]
    rmag = jnp.exp(lr[None] * dt[None] * (tc - taus))
    rang = li[None] * dt[None] * (tc - taus)
    pwr_re, pwr_im = rmag * jnp.cos(rang), rmag * jnp.sin(rang)
    mag = jnp.exp(lr[None] * dt[None] * taus)
    ang = li[None] * dt[None] * taus
    pw_re, pw_im = mag * jnp.cos(ang), mag * jnp.sin(ang)
    nr, ni = pw_re[1] - 1.0, pw_im[1]
    den = lr * lr + li * li
    coef_re, coef_im = (nr * lr + ni * li) / den, (ni * lr - nr * li) / den
    cb_re, cb_im = _cmul(coef_re[..., None], coef_im[..., None], b_re[None], b_im[None])
    cbt_re, cbt_im = _cmul(coef_re[:, :, None, :], coef_im[:, :, None, :],
                           jnp.swapaxes(b_re, 1, 2)[None], jnp.swapaxes(b_im, 1, 2)[None])
    def lag_kernels(d, p_re, p_im):
        q_re, q_im = _cmul(p_re[..., None], p_im[..., None], cb_re[d][None], cb_im[d][None])
        c_cat = jnp.concatenate([c_re[d], -c_im[d]], axis=-1)
        return jnp.einsum('gkn,tgnj->gjtk', c_cat, jnp.concatenate([q_re, q_im], axis=2), precision=hp)

    m_fwd = lag_kernels(0, pw_re[:tc, 0], pw_im[:tc, 0])
    m_bwd = lag_kernels(1, pwr_re[1:tc + 1, 1], pwr_im[1:tc + 1, 1])
    lag_table = jnp.concatenate([m_bwd[:, :, :tc - 1], m_bwd[:, :, tc - 1:] + m_fwd[:, :, :1], m_fwd[:, :, 1:],
                                 jnp.zeros_like(m_fwd[:, :, :1])], axis=2)
    intra = s5_toeplitz(lag_table.reshape(S5_GROUPS, hd, 2 * tc * hd))
    sf_re, sf_im = _cmul(pwr_re[1:tc + 1, 0][:, :, None, :], pwr_im[1:tc + 1, 0][:, :, None, :],
                         cbt_re[0][None], cbt_im[0][None])
    sb_re, sb_im = _cmul(pw_re[:tc, 1][:, :, None, :], pw_im[:tc, 1][:, :, None, :],
                         cbt_re[1][None], cbt_im[1][None])
    summ = jnp.concatenate([sf_re, sb_re, sf_im, sb_im], axis=-1)
    summ = jnp.swapaxes(summ, 0, 1).reshape(S5_GROUPS, tc * hd, 4 * n)
    gtn = lambda z: jnp.swapaxes(z, 0, 1)[:, :, None, :]
    qf_re, qf_im = _cmul(c_re[0][:, None], c_im[0][:, None], gtn(pw_re[1:tc + 1, 0]), gtn(pw_im[1:tc + 1, 0]))
    qb_re, qb_im = _cmul(c_re[1][:, None], c_im[1][:, None], gtn(pwr_re[:tc, 1]), gtn(pwr_im[:tc, 1]))
    carry = jnp.concatenate([qf_re, qb_re, -qf_im, -qb_im], axis=-1)
    carry = carry.reshape(S5_GROUPS, tc * hd, 4 * n)
    a_re = jnp.concatenate([pw_re[tc, 0], pw_re[tc, 1]], axis=-1)
    a_im = jnp.concatenate([pw_im[tc, 0], pw_im[tc, 1]], axis=-1)
    return intra, summ, carry, a_re, a_im


def s5_regroup_matrix():
    n = S5_SUPER * LANES
    src = np.arange(n)
    tb, g, h = src // LANES, (src % LANES) // S5_GROUP_DIM, src % S5_GROUP_DIM
    perm = np.zeros((n, n), np.float32)
    perm[src, g * LANES + tb * S5_GROUP_DIM + h] = 1.0
    return jnp.asarray(perm, BF16)


def _s5_regroup_kernel(p_ref, w_ref, perm_ref, u_ref, sr_ref, si_ref, *, n_chunks):
    n2 = 2 * S5_STATE
    for q in range(S5_CHUNK // S5_SUPER):
        rows_t = [p_ref[pl.ds(q * S5_SUPER + tb, n_chunks, stride=S5_CHUNK), :].astype(BF16) for tb in range(S5_SUPER)]
        grouped = _dot(jnp.concatenate(rows_t, axis=-1), perm_ref[...]).astype(BF16)
        for g in range(S5_SUPER):
            u_ref[g, :, q * LANES:(q + 1) * LANES] = grouped[:, g * LANES:(g + 1) * LANES]
    for g in range(S5_SUPER):
        s = _dot(u_ref[g], w_ref[g])
        sr_ref[pl.ds(g, n_chunks, stride=S5_SUPER), :] = s[:, :n2]
        si_ref[pl.ds(g, n_chunks, stride=S5_SUPER), :] = s[:, n2:]


def _s5_state_spec(c, n2, n_tiles):
    return pl.BlockSpec((None, c * S5_SUPER, n2), lambda a, bi: (bi * n_tiles + a, 0, 0))


def _s5_group_spec(rows, cols):
    return pl.BlockSpec((S5_SUPER, rows, cols), lambda a, bi: (a, 0, 0))


def s5_summary(p, summ):
    b, l, _ = p.shape
    c = l // S5_CHUNK
    n_tiles = S5_GROUPS // S5_SUPER
    k = S5_CHUNK * S5_GROUP_DIM
    n2 = 2 * S5_STATE
    shape = jax.ShapeDtypeStruct((b * n_tiles, c * S5_SUPER, n2), F32)
    kern = functools.partial(_s5_regroup_kernel, n_chunks=c)
    perm = s5_regroup_matrix()
    return pl.pallas_call(
        kern,
        grid=(n_tiles, b),
        in_specs=[
            pl.BlockSpec((None, l, LANES), lambda a, bi: (bi, 0, a)),
            _s5_group_spec(k, 2 * n2),
            pl.BlockSpec(perm.shape, lambda a, bi: (0, 0)),
        ],
        out_specs=[pl.BlockSpec((None, S5_SUPER, c, k), lambda a, bi: (bi * n_tiles + a, 0, 0, 0)),
                   _s5_state_spec(c, n2, n_tiles), _s5_state_spec(c, n2, n_tiles)],
        out_shape=[jax.ShapeDtypeStruct((b * n_tiles, S5_SUPER, c, k), BF16), shape, shape],
        compiler_params=_cparams("parallel", "parallel"),
        name="s5_summary",
    )(p, summ, perm)


def _s5_scan(sr_ref, si_ref, are_ref, aim_ref, or_ref, oi_ref, n_chunks):
    n2 = 2 * S5_STATE
    a_re = are_ref[...]
    a_im = aim_ref[...]
    fwd_lane = lax.broadcasted_iota(jnp.int32, (SUBLANES, n2), 1) < S5_STATE

    def body(i, state):
        x_re, x_im = state
        rows_i = pl.ds(pl.multiple_of(i * SUBLANES, SUBLANES), SUBLANES)
        rows_r = pl.ds(pl.multiple_of((n_chunks - 1 - i) * SUBLANES, SUBLANES), SUBLANES)
        or_ref[rows_i, 0:S5_STATE] = x_re[:, :S5_STATE]
        oi_ref[rows_i, 0:S5_STATE] = x_im[:, :S5_STATE]
        or_ref[rows_r, S5_STATE:n2] = x_re[:, S5_STATE:]
        oi_ref[rows_r, S5_STATE:n2] = x_im[:, S5_STATE:]
        s_re = jnp.where(fwd_lane, sr_ref[rows_i, :], sr_ref[rows_r, :])
        s_im = jnp.where(fwd_lane, si_ref[rows_i, :], si_ref[rows_r, :])
        return (a_re * x_re - a_im * x_im + s_re, a_re * x_im + a_im * x_re + s_im)

    zero = jnp.zeros((SUBLANES, n2), F32)
    lax.fori_loop(0, n_chunks, body, (zero, zero))


def _s5_output_kernel(u_ref, sr_ref, si_ref, are_ref, aim_ref, wi_ref, wc_ref, perm_ref, o_ref,
                      y_ref, xr_ref, xi_ref, *, n_chunks):
    _s5_scan(sr_ref, si_ref, are_ref, aim_ref, xr_ref, xi_ref, n_chunks)
    back = (((1,), (1,)), ((), ()))
    for g in range(S5_SUPER):
        rows = pl.ds(g, n_chunks, stride=S5_SUPER)
        xin = jnp.concatenate([xr_ref[rows, :], xi_ref[rows, :]], axis=-1).astype(BF16)
        y_ref[g] = _dot(u_ref[g], wi_ref[g]) + lax.dot_general(xin, wc_ref[g], back, preferred_element_type=F32)
    for q in range(S5_CHUNK // S5_SUPER):
        y = jnp.concatenate([y_ref[g, :, q * LANES:(q + 1) * LANES] for g in range(S5_SUPER)], axis=-1)
        hi = y.astype(BF16)
        lo = (y - hi.astype(F32)).astype(BF16)
        tok = (lax.dot_general(hi, perm_ref[...], back, preferred_element_type=F32)
               + lax.dot_general(lo, perm_ref[...], back, preferred_element_type=F32))
        for tb in range(S5_SUPER):
            o_ref[pl.ds(q * S5_SUPER + tb, n_chunks, stride=S5_CHUNK), :] = tok[:, tb * LANES:(tb + 1) * LANES]


def s5_output(u, s_re, s_im, a_re, a_im, intra, carry, *, batch):
    tiles, _, c, k = u.shape
    n_tiles = tiles // batch
    n2 = 2 * S5_STATE
    l = c * S5_CHUNK
    kern = functools.partial(_s5_output_kernel, n_chunks=c)
    perm = s5_regroup_matrix()
    decay_spec = pl.BlockSpec((S5_SUPER, n2), lambda a, bi: (a, 0))
    state_scr = pltpu.VMEM((c * S5_SUPER, n2), F32)
    return pl.pallas_call(
        kern,
        grid=(n_tiles, batch),
        in_specs=[
            pl.BlockSpec((None, S5_SUPER, c, k), lambda a, bi: (bi * n_tiles + a, 0, 0, 0)),
            _s5_state_spec(c, n2, n_tiles),
            _s5_state_spec(c, n2, n_tiles),
            decay_spec,
            decay_spec,
            _s5_group_spec(k, k),
            _s5_group_spec(k, 2 * n2),
            pl.BlockSpec(perm.shape, lambda a, bi: (0, 0)),
        ],
        out_specs=pl.BlockSpec((None, l, LANES), lambda a, bi: (bi, 0, a)),
        out_shape=jax.ShapeDtypeStruct((batch, l, D_S5), F32),
        scratch_shapes=[pltpu.VMEM((S5_SUPER, c, k), F32), state_scr, state_scr],
        compiler_params=_cparams("parallel", "parallel"),
        name="s5_output",
    )(u, s_re, s_im, a_re, a_im, intra, carry, perm)


def s5_core(p, mats):
    intra, summ, carry, a_re, a_im = mats
    u, s_re, s_im = s5_summary(p, summ.astype(BF16))
    return s5_output(u, s_re, s_im, a_re, a_im, intra.astype(BF16), carry.astype(BF16), batch=p.shape[0])


def _to_time_tiles(y):
    nf, c = DFT_FAST, y.shape[-1]
    slabs = [y[s * nf:(s + 1) * nf].reshape(DFT_TILES, SUBLANES, c) for s in range(y.shape[0] // nf)]
    return jnp.concatenate(slabs, axis=1)


def _from_time_tiles(v):
    c = v.shape[-1]
    slabs = [v[:, s * SUBLANES:(s + 1) * SUBLANES, :].reshape(DFT_FAST, c) for s in range(v.shape[1] // SUBLANES)]
    return jnp.concatenate(slabs, axis=0)


def _odd_in_kernel(xp_ref, x_ref, xn_ref, g_ref, w_ref, cw_ref, cb_ref, s5_ref, gates_ref, *, tm):
    ext = _with_halo(xp_ref, x_ref, xn_ref)
    p = _dot(_rms(ext, g_ref[...]).astype(BF16), w_ref[...])
    lo, hi = POOL_HALO, POOL_HALO + tm
    s5_ref[0] = p[lo:hi, :D_S5]
    ph = p[:, D_S5:]
    y = cw_ref[0:1, :] * _shift_rows(ph, 1) + cw_ref[1:2, :] * ph + cw_ref[2:3, :] * _shift_rows(ph, -1)
    y = y[lo:hi] + cb_ref[...]
    for k in range(3):
        for ct in range(D_HYENA // LANES):
            col = k * D_HYENA + ct * LANES
            gates_ref[k, ct] = _to_time_tiles(y[:, col:col + LANES])


def odd_in(x, g, w, layer, conv_w, conv_b, *, tm):
    b, l, d = x.shape
    n = w.shape[-1]
    rows = tm // DFT_FAST * SUBLANES
    cts = D_HYENA // LANES
    kern = functools.partial(_odd_in_kernel, tm=tm)
    return pl.pallas_call(
        kern,
        grid=(b, l // tm),
        in_specs=_halo_specs(tm, l, d, lambda: 0) + [
            pl.BlockSpec((1, d), lambda bi, i: (0, 0)),
            _layer_spec((d, n), layer, lambda bi, i: (0, 0)),
            pl.BlockSpec((3, n - D_S5), lambda bi, i: (0, 0)),
            pl.BlockSpec((1, n - D_S5), lambda bi, i: (0, 0)),
        ],
        out_specs=[pl.BlockSpec((1, tm, D_S5), lambda bi, i: (bi, i, 0)),
                   pl.BlockSpec((3, None, cts, DFT_TILES, rows, LANES), lambda bi, i: (0, bi, 0, 0, i, 0))],
        out_shape=[jax.ShapeDtypeStruct((b, l, D_S5), F32),
                   jax.ShapeDtypeStruct((3, b, cts, DFT_TILES, l // DFT_FAST * SUBLANES, LANES), F32)],
        compiler_params=_cparams("parallel", "parallel"),
        name="odd_in",
    )(x, x, x, g.reshape(1, d), w, conv_w, conv_b.reshape(1, -1))


def _filter_mlp_kernel(bands_ref, w1t_ref, w1cs_ref, b1_ref, w2_ref, b2_ref, w3h_ref, w3l_ref, fr_ref, dl_ref,
                       eo_ref, asum_ref, *, tl, seq_len):
    i = pl.program_id(0)
    hp = lax.Precision.HIGHEST
    c = D_HYENA
    half = tl // 2
    first = i * tl + lax.broadcasted_iota(jnp.int32, (half, 1), 0)

    def positions(width):
        right = lax.broadcasted_iota(jnp.int32, (1, width), 1) >= width // 2
        return (first + jnp.where(right, half, 0)).astype(F32)

    feat = bands_ref.shape[1]
    ang = (2.0 * math.pi / seq_len) * positions(feat) * bands_ref[...]
    is_cos = lax.broadcasted_iota(jnp.int32, (1, feat), 1) % (2 * HYENA_BANDS) < HYENA_BANDS
    cs = jnp.sin(ang + jnp.where(is_cos, 0.5 * math.pi, math.pi))
    fr = fr_ref[...]
    z = (positions(w1t_ref.shape[1]) / (seq_len - 1.0)) * w1t_ref[...] + jnp.dot(
        cs, w1cs_ref[...], precision=hp, preferred_element_type=F32)
    h = jnp.sin(fr * (z + b1_ref[...]))
    h = jnp.sin(fr * (jnp.dot(h, w2_ref[...], precision=hp, preferred_element_type=F32) + b2_ref[...]))
    h_hi = h.astype(BF16)
    h_lo = (h - h_hi.astype(F32)).astype(BF16)
    h = jnp.concatenate([_dot(h_hi, w3h_ref[r]) + (_dot(h_lo, w3h_ref[r]) + _dot(h_hi, w3l_ref[r])) for r in range(2)],
                        axis=0)
    t_norm = (i * tl + lax.broadcasted_iota(jnp.int32, (tl, 1), 0)).astype(F32) / (seq_len - 1.0)
    decay = jnp.exp(-t_norm * dl_ref[...])

    @pl.when(i == 0)
    def _():
        asum_ref[...] = jnp.zeros_like(asum_ref)

    for o in range(h.shape[1] // (2 * c)):
        fwd = h[:, 2 * o * c:(2 * o + 1) * c] * decay
        bwd = h[:, (2 * o + 1) * c:(2 * o + 2) * c] * decay
        for part, val in enumerate((fwd + bwd, fwd - bwd)):
            for ct in range(c // LANES):
                eo_ref[((2 * o + part) * c) // LANES + ct] = _to_time_tiles(val[:, ct * LANES:(ct + 1) * LANES])
        asum_ref[:, o * c:(o + 1) * c] += jnp.sum(jnp.abs(fwd) + jnp.abs(bwd), axis=0, keepdims=True)


def hyena_filter_mlp(seq_len, w1, b1, w2, b2, w3, freq, *, tl):
    n_out = w3.shape[1]
    bands = jnp.linspace(1e-4, HYENA_BANDS - 1, HYENA_BANDS, dtype=F32)
    bands2 = jnp.concatenate([bands, bands])[None, :]
    deltas = jnp.abs(jnp.linspace(math.log(HYENA_TARGET) / HYENA_LONG_DECAY_PCT,
                                  math.log(HYENA_TARGET) / HYENA_SHORT_DECAY_PCT, D_HYENA, dtype=F32))[None, :]
    kern = functools.partial(_filter_mlp_kernel, tl=tl, seq_len=seq_len)
    full = lambda a: pl.BlockSpec(a.shape, lambda i: (0,) * a.ndim)
    w3_hi = w3.astype(BF16)
    w3_lo = (w3 - w3_hi.astype(F32)).astype(BF16)
    twice = lambda v: jnp.concatenate([v, v])[None, :]
    blockdiag = lambda m: jnp.concatenate([jnp.pad(m, ((0, 0), (0, m.shape[1]))),
                                           jnp.pad(m, ((0, 0), (m.shape[1], 0)))], axis=0)
    per_half = lambda m: jnp.stack([jnp.pad(m, ((0, m.shape[0]), (0, 0))), jnp.pad(m, ((m.shape[0], 0), (0, 0)))])
    args = [twice(bands2[0]), twice(w1[0]), blockdiag(w1[1:]), twice(b1), blockdiag(w2), twice(b2),
            per_half(w3_hi), per_half(w3_lo), twice(freq), deltas]
    return pl.pallas_call(
        kern,
        grid=(seq_len // tl,),
        in_specs=[full(a) for a in args],
        out_specs=[pl.BlockSpec((n_out // LANES, DFT_TILES, tl // DFT_FAST * SUBLANES, LANES), lambda i: (0, 0, i, 0)),
                   pl.BlockSpec((1, n_out // 2), lambda i: (0, 0))],
        out_shape=[jax.ShapeDtypeStruct((n_out // LANES, DFT_TILES, seq_len // DFT_FAST * SUBLANES, LANES), F32),
                   jax.ShapeDtypeStruct((1, n_out // 2), F32)],
        compiler_params=_cparams("arbitrary"),
        name="hyena_filter_mlp",
    )(*args)


def dft_tables(seq_len):
    n = 2 * seq_len
    nf = DFT_FAST
    ns = n // nf
    k_a = np.arange(ns)[:, None]
    n_s = np.arange(ns // 2)[None, :]
    w_s = np.exp(-2j * np.pi * ((k_a * n_s) % ns) / ns)
    n_f = np.arange(nf)[:, None]
    tw = np.exp(-2j * np.pi * ((n_f * np.arange(ns)[None, :]) % n) / n)
    c32 = lambda z: (jnp.asarray(z.real, F32), jnp.asarray(z.imag, F32))
    ws_re, ws_im = c32(w_s)
    tw_re, tw_im = c32(tw)
    g_re = tw_re[:, :, None] * ws_re[None] - tw_im[:, :, None] * ws_im[None]
    g_im = tw_re[:, :, None] * ws_im[None] + tw_im[:, :, None] * ws_re[None]
    fwd_a = jnp.concatenate([jnp.concatenate([g_re, -g_im], axis=2),
                             jnp.concatenate([g_im, g_re], axis=2)], axis=1)
    gt_re = jnp.swapaxes(g_re, 1, 2) * (1.0 / n)
    gt_im = jnp.swapaxes(g_im, 1, 2) * (1.0 / n)
    inv_a = jnp.concatenate([jnp.concatenate([gt_re, gt_im], axis=2),
                             jnp.concatenate([-gt_im, gt_re], axis=2)], axis=1)
    kk = np.arange(nf)
    w_f = np.exp(-2j * np.pi * ((kk[:, None] * kk[None, :]) % nf) / nf)
    f_re, f_im = c32(w_f)
    fwd_b = jnp.concatenate([jnp.concatenate([f_re, -f_im], axis=1),
                             jnp.concatenate([f_im, f_re], axis=1)], axis=0)
    inv_b = jnp.concatenate([jnp.concatenate([f_re, f_im], axis=1),
                             jnp.concatenate([-f_im, f_re], axis=1)], axis=0)
    kept = min(ns, -(-(ns // 2 + 1) // SUBLANES) * SUBLANES)
    fwd_a_real = jnp.concatenate([fwd_a[:, :kept, :ns // 2], fwd_a[:, ns:ns + kept, :ns // 2]], axis=1)
    m_re, m_im = jnp.roll(f_re, -1, axis=0), jnp.roll(f_im, -1, axis=0)
    mirror_part = jnp.stack([jnp.concatenate([m_re, m_im], axis=1), jnp.concatenate([m_im, -m_re], axis=1)])
    return dict(fwd_a=fwd_a.astype(BF16), fwd_a_real=fwd_a_real.astype(BF16),
                inv_a=inv_a.astype(BF16), fwd_b=fwd_b.astype(BF16), inv_b=inv_b.astype(BF16),
                fwd_b_part=fwd_b.reshape(2, nf, 2 * nf).astype(BF16), mirror_b_part=mirror_part.astype(BF16))


def _filter_dft_kernel(x_ref, ga_ref, fb_ref, fbm_ref, nrm_ref, k_ref, ar_scr, ai_scr):
    nf = DFT_FAST
    kept = ga_ref.shape[1] // 2
    n_in = ga_ref.shape[2]
    ns = 2 * n_in

    def stage_a(q, carry):
        for j in range(SUBLANES):
            x = x_ref[q, pl.ds(j, n_in, stride=SUBLANES), :].astype(BF16)
            n_f = q * SUBLANES + j
            a = _dot(ga_ref[n_f], x)
            ar_scr[pl.ds(n_f, kept, stride=DFT_PITCH), :] = a[:kept]
            ai_scr[pl.ds(n_f, kept, stride=DFT_PITCH), :] = a[kept:]
        return carry

    lax.fori_loop(0, DFT_TILES, stage_a, 0, unroll=DFT_UNROLL // SUBLANES)
    inv_norm = 1.0 / (nrm_ref[...] + 1e-6)

    def slab(start):
        return jnp.concatenate([ar_scr[pl.ds(start, nf), :], ai_scr[pl.ds(start, nf), :]], axis=0).astype(BF16)

    for k in (0, n_in):
        k_ref[k * nf:(k + 1) * nf, :] = _dot(fb_ref[...], slab(k * DFT_PITCH)) * inv_norm

    def stage_b(k, carry):
        a = slab(pl.multiple_of(k * DFT_PITCH, SUBLANES))
        k_ref[pl.ds(pl.multiple_of(k * nf, nf), nf), :] = _dot(fb_ref[...], a) * inv_norm
        k_ref[pl.ds(pl.multiple_of((ns - k) * nf, nf), nf), :] = _dot(fbm_ref[...], a) * inv_norm
        return carry

    pairs = n_in - 1
    lax.fori_loop(1, n_in, stage_b, 0, unroll=max(u for u in (9, 8, 7, 4, 3, 2, 1) if pairs % u == 0))


def hyena_filter_dft(eo, asum, tables):
    col_tiles, q_tiles, rows, _ = eo.shape
    l = q_tiles * rows
    c = D_HYENA
    ct = c // LANES
    orders = col_tiles // (2 * ct)
    ga, fb, fbm = tables['fwd_a_real'], tables['fwd_b_part'], tables['mirror_b_part']
    kept = ga.shape[1] // 2
    scr = pltpu.VMEM((kept * DFT_PITCH, LANES), F32)
    part_spec = pl.BlockSpec((None,) + fb.shape[1:], lambda o, j, part: (part, 0, 0))
    return pl.pallas_call(
        _filter_dft_kernel,
        grid=(orders, ct, 2),
        in_specs=[
            pl.BlockSpec((None,) + eo.shape[1:], lambda o, j, part: ((2 * o + part) * ct + j, 0, 0, 0)),
            pl.BlockSpec(ga.shape, lambda o, j, part: (0, 0, 0)),
            part_spec,
            part_spec,
            pl.BlockSpec((1, LANES), lambda o, j, part: (0, o * ct + j)),
        ],
        out_specs=pl.BlockSpec((None, None, None, 2 * l, LANES), lambda o, j, part: (o, part, j, 0, 0)),
        out_shape=jax.ShapeDtypeStruct((orders, 2, ct, 2 * l, LANES), F32),
        scratch_shapes=[scr, scr],
        compiler_params=_cparams("parallel", "parallel", "parallel"),
        name="hyena_filter_dft",
    )(eo, ga, fb, fbm, asum)


def _lane_cat(parts):
    return parts[0] if len(parts) == 1 else jnp.concatenate(parts, axis=-1)


def _hyena_conv_kernel(x_ref, ga_ref, k_ref, fb_ref, ib_ref, gi_ref, gate_ref, u_ref, bias_ref, o_ref, *scr,
                       n_a, n_b, slabs):
    nf = DFT_FAST
    tiles, qt = x_ref.shape[1], x_ref.shape[2]
    half = x_ref.shape[3] // SUBLANES
    ns = 2 * half
    step = pl.program_id(1)

    @pl.when(step < n_a)
    def _():
        for qj in range(qt * SUBLANES):
            q, j = divmod(qj, SUBLANES)
            rows = pl.ds(j, half, stride=SUBLANES)
            x = _lane_cat([jnp.concatenate([x_ref[0, t, q, rows, :], x_ref[1, t, q, rows, :]], axis=0)
                           for t in range(tiles)])
            a = _dot(ga_ref[qj], x.astype(BF16))
            dst = pl.ds(step * (qt * SUBLANES) + qj, ns, stride=DFT_PITCH)
            for t in range(tiles):
                scr[2 * t][dst, :] = a[:ns, t * LANES:(t + 1) * LANES]
                scr[2 * t + 1][dst, :] = a[ns:, t * LANES:(t + 1) * LANES]

    @pl.when((step >= n_a) & (step < n_a + n_b))
    def _():
        for s in range(slabs):
            k_a = (step - n_a) * slabs + s
            rows = pl.ds(pl.multiple_of(k_a * DFT_PITCH, SUBLANES), nf)
            a = jnp.concatenate([_lane_cat([scr[2 * t + part][rows, :] for t in range(tiles)]) for part in range(2)], axis=0)
            x = _dot(fb_ref[...], a.astype(BF16))
            xr, xi = x[:nf], x[nf:]
            kr = _lane_cat([k_ref[0, t, s * nf:(s + 1) * nf, :] for t in range(tiles)])
            ki = _lane_cat([k_ref[1, t, s * nf:(s + 1) * nf, :] for t in range(tiles)])
            y = jnp.concatenate([xr * kr - xi * ki, xr * ki + xi * kr], axis=0).astype(BF16)
            cc = _dot(ib_ref[...], y)
            for t in range(tiles):
                scr[2 * t][rows, :] = cc[:nf, t * LANES:(t + 1) * LANES]
                scr[2 * t + 1][rows, :] = cc[nf:, t * LANES:(t + 1) * LANES]

    @pl.when(step >= n_a + n_b)
    def _():
        first = (step - (n_a + n_b)) * (qt * SUBLANES)
        for qj in range(qt * SUBLANES):
            q, j = divmod(qj, SUBLANES)
            src = pl.ds(first + qj, ns, stride=DFT_PITCH)
            cc = jnp.concatenate([_lane_cat([scr[2 * t + part][src, :] for t in range(tiles)]) for part in range(2)], axis=0)
            y = _dot(gi_ref[qj], cc.astype(BF16))
            dst = pl.ds(j, half, stride=SUBLANES)
            for t in range(tiles):
                o_ref[0, t, q, dst, :] = y[:half, t * LANES:(t + 1) * LANES]
                o_ref[1, t, q, dst, :] = y[half:, t * LANES:(t + 1) * LANES]
        for t in range(tiles):
            bias = bias_ref[:, t * LANES:(t + 1) * LANES]
            for bi in range(2):
                o_ref[bi, t] = gate_ref[bi, t] * (o_ref[bi, t] + u_ref[bi, t] * bias)


def hyena_long_conv(u6, u_idx, gate6, gate_idx, kf, order, bias, tables, *, slabs=8, tiles=2, qt=2):
    _, _, cts, q_tiles, rows, _ = u6.shape
    ns = 2 * rows // SUBLANES
    n_a = q_tiles // qt
    n_b = ns // slabs
    ga, gi, fb, ib = tables['fwd_a'], tables['inv_a'], tables['fwd_b'], tables['inv_b']
    q_fwd = lambda s: jnp.minimum(s, n_a - 1)
    q_mid = lambda s: jnp.clip(s - n_a, 0, n_b - 1)
    q_inv = lambda s: jnp.clip(s - n_a - n_b, 0, n_a - 1)

    def seq_spec(idx, q_of):
        return pl.BlockSpec((None, 2, tiles, qt, rows, LANES), lambda c, s: (idx, 0, c, q_of(s), 0, 0))

    scr = pltpu.VMEM((ns * DFT_PITCH, LANES), F32)
    kern = functools.partial(_hyena_conv_kernel, n_a=n_a, n_b=n_b, slabs=slabs)
    return pl.pallas_call(
        kern,
        grid=(cts // tiles, 2 * n_a + n_b),
        in_specs=[
            seq_spec(u_idx, q_fwd),
            pl.BlockSpec((qt * SUBLANES,) + ga.shape[1:], lambda c, s: (q_fwd(s), 0, 0)),
            pl.BlockSpec((None, 2, tiles, slabs * DFT_FAST, LANES), lambda c, s: (order, 0, c, q_mid(s), 0)),
            pl.BlockSpec(fb.shape, lambda c, s: (0, 0)),
            pl.BlockSpec(ib.shape, lambda c, s: (0, 0)),
            pl.BlockSpec((qt * SUBLANES,) + gi.shape[1:], lambda c, s: (q_inv(s), 0, 0)),
            seq_spec(gate_idx, q_inv),
            seq_spec(u_idx, q_inv),
            pl.BlockSpec((1, tiles * LANES), lambda c, s: (0, c)),
        ],
        out_specs=pl.BlockSpec((2, tiles, qt, rows, LANES), lambda c, s: (0, c, q_inv(s), 0, 0)),
        out_shape=jax.ShapeDtypeStruct((2, cts, q_tiles, rows, LANES), F32),
        scratch_shapes=[scr] * (2 * tiles),
        compiler_params=_cparams("arbitrary", "arbitrary", vmem=VMEM_LIMIT_CONV),
        name="hyena_long_conv",
    )(u6, ga, kf, fb, ib, gi, gate6, u6, bias.reshape(1, -1))


def hyena_filter_spectra(seq_len, tables, w1, b1, w2, b2, w3, freq):
    eo, asum = hyena_filter_mlp(seq_len, w1, b1, w2, b2, w3, freq, tl=FILTER_ROW_TILE)
    return hyena_filter_dft(eo, asum, tables)


def hyena_mixer(gates, kf, bias, tables):
    z = hyena_long_conv(gates, 2, gates, 1, kf, 0, bias[0], tables)
    return hyena_long_conv(z[None], 0, gates, 0, kf, 1, bias[1], tables)


def _odd_out_kernel(ys_ref, u_ref, hy_ref, x_ref, d_ref, wglu_ref, wo_ref, g_ref, o_ref):
    y = ys_ref[0] + d_ref[...] * u_ref[0]
    c0 = math.sqrt(2.0 / math.pi)
    gl = 0.5 * y * (1.0 + jnp.tanh(c0 * (y + 0.044715 * (y * y * y))))
    z = _dot(gl.astype(BF16), wglu_ref[...])
    s5 = gl * (1.0 / (1.0 + jnp.exp(-z)))
    hy = [_from_time_tiles(hy_ref[ct]).astype(BF16) for ct in range(hy_ref.shape[0])]
    mix = _dot(jnp.concatenate([s5.astype(BF16)] + hy, axis=-1), wo_ref[...])
    o_ref[0] = x_ref[0] + _rms(mix, g_ref[...])


def odd_out(ys, u_s5, hy, x, d_skip, w_glu, w_out, layer, g, *, tm):
    b, l, d = x.shape
    tok = lambda w: pl.BlockSpec((1, tm, w), lambda bi, i: (bi, i, 0))
    hy_spec = pl.BlockSpec((None, D_HYENA // LANES, DFT_TILES, tm // DFT_FAST * SUBLANES, LANES),
                           lambda bi, i: (bi, 0, 0, i, 0))
    return pl.pallas_call(
        _odd_out_kernel,
        grid=(b, l // tm),
        in_specs=[tok(D_S5), tok(D_S5), hy_spec, tok(d),
                  pl.BlockSpec((1, D_S5), lambda bi, i: (0, 0)),
                  _layer_spec((D_S5, D_S5), layer, lambda bi, i: (0, 0)),
                  _layer_spec((d, d), layer, lambda bi, i: (0, 0)),
                  pl.BlockSpec((1, d), lambda bi, i: (0, 0))],
        out_specs=tok(d),
        out_shape=jax.ShapeDtypeStruct(x.shape, F32),
        compiler_params=_cparams("parallel", "parallel"),
        name="odd_out",
    )(ys, u_s5, hy, x, d_skip.reshape(1, D_S5), w_glu, w_out, g.reshape(1, d))


def kernel(x, mem, norm_mix, norm_xattn, norm_mem, norm_mlp, xa_wq, xa_wk, xa_wv, xa_wo, mlp_w1, mlp_w2, ev_w_in, ev_pool_w, ev_pool_scale, ev_conv_w, ev_w_out, od_w_in, od_s5_lambda_re, od_s5_lambda_im, od_s5_log_dt, od_s5_b_re, od_s5_b_im, od_s5_c_re, od_s5_c_im, od_s5_d, od_s5_w_glu, od_hy_short_w, od_hy_short_b, od_hy_w1, od_hy_b1, od_hy_w2, od_hy_b2, od_hy_w3, od_hy_freq, od_hy_bias, od_w_out):
    b, l, d = x.shape
    depth = norm_mix.shape[0]
    assert b == 2, "the long convolution packs the two batch rows as one complex signal"
    tables = dft_tables(l)
    mem2d = mem.reshape(b * N_MEM, d)
    wq, wo = xa_wq.astype(BF16), xa_wo.astype(BF16)
    wkv = jnp.concatenate([xa_wk, xa_wv], axis=2).astype(BF16)
    kv = memory_kv(mem2d, norm_mem, wkv).reshape(depth, b, N_MEM, 2 * d)
    w1, w2 = mlp_w1.astype(BF16), mlp_w2.astype(BF16)
    ev_in, ev_out, ev_pool = ev_w_in.astype(BF16), ev_w_out.astype(BF16), ev_pool_w.astype(BF16)
    od_in, od_out, od_glu = od_w_in.astype(BF16), od_w_out.astype(BF16), od_s5_w_glu.astype(BF16)
    for i in range(depth):
        j = i // 2
        if i % 2 == 0:
            x = even_mixer(x, norm_mix[i, 0], ev_in, ev_pool, ev_pool_scale[j], ev_conv_w[j], ev_out, j,
                           norm_mix[i, 1], tm=ROW_TILE)
        else:
            u_s5, gates = odd_in(x, norm_mix[i, 0], od_in, j, od_hy_short_w[j], od_hy_short_b[j], tm=ROW_TILE)
            mats = s5_matrices(od_s5_lambda_re[j], od_s5_lambda_im[j], od_s5_log_dt[j], od_s5_b_re[j],
                               od_s5_b_im[j], od_s5_c_re[j], od_s5_c_im[j])
            ys = s5_core(u_s5, mats)
            kf = hyena_filter_spectra(l, tables, od_hy_w1[j], od_hy_b1[j], od_hy_w2[j], od_hy_b2[j],
                                      od_hy_w3[j], od_hy_freq[j])
            hy = hyena_mixer(gates, kf, od_hy_bias[j], tables)
            x = odd_out(ys, u_s5, hy, x, od_s5_d[j], od_glu, od_out, j, norm_mix[i, 1], tm=ROW_TILE)
        x = xattn_block(x, kv, wq, wo, i, norm_xattn[i, 0], norm_xattn[i, 1], tm=ROW_TILE)
        x = mlp_block(x.reshape(b * l, d), w1, w2, i, norm_mlp[i, 0], norm_mlp[i, 1],
                      tm=ROW_TILE, tf=MLP_HIDDEN_TILE).reshape(b, l, d)
    return x
```

```python
import functools
import math

import numpy as np
import jax
import jax.numpy as jnp
from jax import lax
from jax.experimental import pallas as pl
from jax.experimental.pallas import tpu as pltpu

F32 = jnp.float32
BF16 = jnp.bfloat16

D_MODEL = 1024
N_MEM = 256
RMS_EPS = 1e-6
D_POOL = 512
POOL_WINDOWS = (2, 4, 8, 16)
POOL_GROUP_DIM = 128
POOL_HALO = 8
D_CONV = 512
D_S5 = 512
S5_GROUP_DIM = 16
S5_GROUPS = 32
S5_STATE = 64
S5_CHUNK = 16
D_HYENA = 512
HYENA_BANDS = 16
HYENA_FFN = 64
HYENA_TARGET = 1e-2
HYENA_SHORT_DECAY_PCT = 0.3
HYENA_LONG_DECAY_PCT = 1.5
XA_HEADS = 4
XA_HEAD_DIM = 256
D_FF = 4096

LANES = 128
SUBLANES = 8
S5_SUPER = LANES // S5_GROUP_DIM
DFT_FAST = 128
DFT_TILES = DFT_FAST // SUBLANES
DFT_PITCH = DFT_FAST + SUBLANES
DFT_UNROLL = 16
ROW_TILE = 1024
MLP_HIDDEN_TILE = 512
FILTER_ROW_TILE = 512
VMEM_LIMIT = 56 * 1024 * 1024
VMEM_LIMIT_CONV = 62 * 1024 * 1024


def _cparams(*sem, vmem=VMEM_LIMIT):
    return pltpu.CompilerParams(dimension_semantics=sem, vmem_limit_bytes=vmem)


def _rms(xf, g):
    ms = jnp.mean(xf * xf, axis=-1, keepdims=True)
    return xf * lax.rsqrt(ms + RMS_EPS) * g


def _dot(a, b):
    return jnp.dot(a, b, preferred_element_type=F32)


def _layer_spec(block, layer, tail_map):
    return pl.BlockSpec((None,) + block, lambda *idx: (layer,) + tail_map(*idx))


def _norm_matmul_kernel(x_ref, g_ref, w_ref, o_ref):
    xn = _rms(x_ref[...], g_ref[...]).astype(BF16)
    o_ref[...] = _dot(xn, w_ref[...]).astype(o_ref.dtype)


def memory_kv(mem2d, g, w):
    layers, d, n = w.shape
    m = mem2d.shape[0]
    return pl.pallas_call(
        _norm_matmul_kernel,
        grid=(layers,),
        in_specs=[
            pl.BlockSpec((m, d), lambda i: (0, 0)),
            pl.BlockSpec((None, 1, d), lambda i: (i, 0, 0)),
            pl.BlockSpec((None, d, n), lambda i: (i, 0, 0)),
        ],
        out_specs=pl.BlockSpec((None, m, n), lambda i: (i, 0, 0)),
        out_shape=jax.ShapeDtypeStruct((layers, m, n), BF16),
        compiler_params=_cparams("parallel"),
        name="memory_kv",
    )(mem2d, g.reshape(layers, 1, d), w)


def _xattn(x, kv_ref, wq_ref, wo_ref, g1_ref, g2_ref):
    xn = _rms(x, g1_ref[...]).astype(BF16)
    q = (_dot(xn, wq_ref[...]) * (XA_HEAD_DIM ** -0.5)).astype(BF16)
    heads = []
    for h in range(XA_HEADS):
        lo = h * XA_HEAD_DIM
        qh = q[:, lo:lo + XA_HEAD_DIM]
        kh = kv_ref[0, :, lo:lo + XA_HEAD_DIM]
        vh = kv_ref[0, :, D_MODEL + lo:D_MODEL + lo + XA_HEAD_DIM]
        s = lax.dot_general(qh, kh, (((1,), (1,)), ((), ())), preferred_element_type=F32)
        e = jnp.exp(s - jnp.max(s, axis=-1, keepdims=True))
        p = e / jnp.sum(e, axis=-1, keepdims=True)
        heads.append(_dot(p.astype(BF16), vh).astype(BF16))
    o = jnp.concatenate(heads, axis=-1)
    y = _dot(o, wo_ref[...])
    return x + _rms(y, g2_ref[...])


def _xattn_kernel(x_ref, kv_ref, wq_ref, wo_ref, g1_ref, g2_ref, o_ref):
    o_ref[0] = _xattn(x_ref[0], kv_ref, wq_ref, wo_ref, g1_ref, g2_ref)


def _xattn_specs(d, layer):
    return [
        pl.BlockSpec((None, 1, N_MEM, 2 * d), lambda bi, i: (layer, bi, 0, 0)),
        _layer_spec((d, d), layer, lambda bi, i: (0, 0)),
        _layer_spec((d, d), layer, lambda bi, i: (0, 0)),
        pl.BlockSpec((1, d), lambda bi, i: (0, 0)),
        pl.BlockSpec((1, d), lambda bi, i: (0, 0)),
    ]


def xattn_block(x, kv, wq, wo, layer, g1, g2, *, tm):
    b, l, d = x.shape
    return pl.pallas_call(
        _xattn_kernel,
        grid=(b, l // tm),
        in_specs=[pl.BlockSpec((1, tm, d), lambda bi, i: (bi, i, 0))] + _xattn_specs(d, layer),
        out_specs=pl.BlockSpec((1, tm, d), lambda bi, i: (bi, i, 0)),
        out_shape=jax.ShapeDtypeStruct(x.shape, F32),
        compiler_params=_cparams("parallel", "parallel"),
        name="xattn_block",
    )(x, kv, wq, wo, g1.reshape(1, d), g2.reshape(1, d))


def _mlp_kernel(x_ref, w1_ref, w2_ref, g1_ref, g2_ref, o_ref, h_ref, *, tf):
    x = x_ref[...]
    xn = _rms(x, g1_ref[...]).astype(BF16)
    for c in range(h_ref.shape[1] // tf):
        cols = slice(c * tf, (c + 1) * tf)
        h = jnp.maximum(_dot(xn, w1_ref[:, cols]), 0.0)
        h_ref[:, cols] = (h * h).astype(BF16)
    o_ref[...] = x + _rms(_dot(h_ref[...], w2_ref[...]), g2_ref[...])


def mlp_block(x2d, w1, w2, layer, g1, g2, *, tm, tf):
    m, d = x2d.shape
    ff = w1.shape[-1]
    resident = dict(pipeline_mode=pl.Buffered(1))
    return pl.pallas_call(
        functools.partial(_mlp_kernel, tf=tf),
        grid=(m // tm,),
        in_specs=[
            pl.BlockSpec((tm, d), lambda i: (i, 0)),
            pl.BlockSpec((None, d, ff), lambda i: (layer, 0, 0), **resident),
            pl.BlockSpec((None, ff, d), lambda i: (layer, 0, 0), **resident),
            pl.BlockSpec((1, d), lambda i: (0, 0)),
            pl.BlockSpec((1, d), lambda i: (0, 0)),
        ],
        out_specs=pl.BlockSpec((tm, d), lambda i: (i, 0)),
        out_shape=jax.ShapeDtypeStruct((m, d), F32),
        scratch_shapes=[pltpu.VMEM((tm, ff), BF16)],
        compiler_params=_cparams("parallel"),
        name="mlp_block",
    )(x2d, w1, w2, g1.reshape(1, d), g2.reshape(1, d))


def _halo_specs(tm, seq_len, width, col):
    r = tm // POOL_HALO
    last = seq_len // POOL_HALO - 1

    def prev_map(bi, i, *_):
        return (bi, jnp.maximum(i * r - 1, 0), col(*_))

    def main_map(bi, i, *_):
        return (bi, i, col(*_))

    def next_map(bi, i, *_):
        return (bi, jnp.minimum((i + 1) * r, last), col(*_))

    return [
        pl.BlockSpec((1, POOL_HALO, width), prev_map),
        pl.BlockSpec((1, tm, width), main_map),
        pl.BlockSpec((1, POOL_HALO, width), next_map),
    ]


def _with_halo(prev_ref, main_ref, next_ref):
    i = pl.program_id(1)
    prev = jnp.where(i > 0, prev_ref[0], 0.0)
    nxt = jnp.where(i < pl.num_programs(1) - 1, next_ref[0], 0.0)
    return jnp.concatenate([prev, main_ref[0], nxt], axis=0)


def _shift_rows(v, k):
    return pltpu.roll(v, k % v.shape[0], 0)


def _even_mixer_kernel(xp_ref, x_ref, xn_ref, gi_ref, wi_ref, wg_ref, ps_ref, cw_ref, wo_ref, g_ref, o_ref,
                       *, tm, seq_len):
    x_ext = _with_halo(xp_ref, x_ref, xn_ref)
    ext = _dot(_rms(x_ext, gi_ref[...]).astype(BF16), wi_ref[...])
    lo, hi = POOL_HALO, POOL_HALO + tm
    t = pl.program_id(1) * tm + lax.broadcasted_iota(jnp.int32, (tm, 1), 0)
    parts = []
    for gi, win in enumerate(POOL_WINDOWS):
        half = win // 2
        u = ext[:, gi * POOL_GROUP_DIM:(gi + 1) * POOL_GROUP_DIM]
        s = u + _shift_rows(u, 1)
        step = 1
        while 2 * step < win:
            s = _shift_rows(s, step) + _shift_rows(s, -step)
            step *= 2
        cnt = (jnp.minimum(t + half, seq_len) - jnp.maximum(t - half, 0)).astype(F32)
        pooled = s[lo:hi] / cnt - u[lo:hi]
        y = _dot(pooled.astype(BF16), wg_ref[gi])
        parts.append((y * ps_ref[:, gi * POOL_GROUP_DIM:(gi + 1) * POOL_GROUP_DIM]).astype(BF16))
    b_gate = ext[lo:hi, D_POOL:D_POOL + D_CONV]
    ch = ext[:, D_POOL + D_CONV:D_POOL + 2 * D_CONV] * ext[:, D_POOL + 2 * D_CONV:D_POOL + 3 * D_CONV]
    conv = cw_ref[0:1, :] * _shift_rows(ch, 1) + cw_ref[1:2, :] * ch + cw_ref[2:3, :] * _shift_rows(ch, -1)
    parts.append((b_gate * conv[lo:hi]).astype(BF16))
    mix = _dot(jnp.concatenate(parts, axis=-1), wo_ref[...])
    o_ref[0] = x_ref[0] + _rms(mix, g_ref[...])


def even_mixer(x, g_in, w_in, w_group, pool_scale, conv_w, w_out, layer, g, *, tm):
    b, l, d = x.shape
    kern = functools.partial(_even_mixer_kernel, tm=tm, seq_len=l)
    return pl.pallas_call(
        kern,
        grid=(b, l // tm),
        in_specs=_halo_specs(tm, l, d, lambda: 0) + [
            pl.BlockSpec((1, d), lambda bi, i: (0, 0)),
            _layer_spec(w_in.shape[1:], layer, lambda bi, i: (0, 0)),
            _layer_spec(w_group.shape[1:], layer, lambda bi, i: (0, 0, 0)),
            pl.BlockSpec((1, D_POOL), lambda bi, i: (0, 0)),
            pl.BlockSpec((3, D_CONV), lambda bi, i: (0, 0)),
            _layer_spec((d, d), layer, lambda bi, i: (0, 0)),
            pl.BlockSpec((1, d), lambda bi, i: (0, 0)),
        ],
        out_specs=pl.BlockSpec((1, tm, d), lambda bi, i: (bi, i, 0)),
        out_shape=jax.ShapeDtypeStruct(x.shape, F32),
        compiler_params=_cparams("parallel", "parallel"),
        name="even_mixer",
    )(x, x, x, g_in.reshape(1, d), w_in, w_group, pool_scale.reshape(1, D_POOL), conv_w, w_out, g.reshape(1, d))


def _cmul(ar, ai, br, bi):
    return ar * br - ai * bi, ar * bi + ai * br


def _s5_toeplitz_kernel(m_ref, o_ref):
    tc, hd = S5_CHUNK, S5_GROUP_DIM
    for g in range(m_ref.shape[0]):
        m = m_ref[g]
        rows = [m[:, (tc - 1 - s) * hd:(2 * tc - 1 - s) * hd] for s in range(tc)]
        o_ref[g] = jnp.concatenate(rows, axis=0).astype(o_ref.dtype)


def s5_toeplitz(lag_table):
    g, hd, w = lag_table.shape
    k = w // 2
    return pl.pallas_call(
        _s5_toeplitz_kernel,
        grid=(g // S5_SUPER,),
        in_specs=[pl.BlockSpec((S5_SUPER, hd, w), lambda i: (i, 0, 0))],
        out_specs=pl.BlockSpec((S5_SUPER, k, k), lambda i: (i, 0, 0)),
        out_shape=jax.ShapeDtypeStruct((g, k, k), BF16),
        compiler_params=_cparams("parallel"),
        name="s5_toeplitz",
    )(lag_table)


def s5_matrices(lam_re, lam_im, log_dt, b_re, b_im, c_re, c_im):
    tc, hd, n = S5_CHUNK, S5_GROUP_DIM, S5_STATE
    hp = lax.Precision.HIGHEST
    lr = jnp.minimum(lam_re, -1e-4)
    li = lam_im
    dt = jnp.exp(log_dt)[..., None]
    taus = jnp.arange(tc + 1, dtype=F32)[:, None, None, None]
    rmag = jnp.exp(lr[None] * dt[None] * (tc - taus))
    rang = li[None] * dt[None] * (tc - taus)
    pwr_re, pwr_im = rmag * jnp.cos(rang), rmag * jnp.sin(rang)
    mag = jnp.exp(lr[None] * dt[None] * taus)
    ang = li[None] * dt[None] * taus
    pw_re, pw_im = mag * jnp.cos(ang), mag * jnp.sin(ang)
    nr, ni = pw_re[1] - 1.0, pw_im[1]
    den = lr * lr + li * li
    coef_re, coef_im = (nr * lr + ni * li) / den, (ni * lr - nr * li) / den
    cb_re, cb_im = _cmul(coef_re[..., None], coef_im[..., None], b_re[None], b_im[None])
    cbt_re, cbt_im = _cmul(coef_re[:, :, None, :], coef_im[:, :, None, :],
                           jnp.swapaxes(b_re, 1, 2)[None], jnp.swapaxes(b_im, 1, 2)[None])
    def lag_kernels(d, p_re, p_im):
        q_re, q_im = _cmul(p_re[..., None], p_im[..., None], cb_re[d][None], cb_im[d][None])
        c_cat = jnp.concatenate([c_re[d], -c_im[d]], axis=-1)
        return jnp.einsum('gkn,tgnj->gjtk', c_cat, jnp.concatenate([q_re, q_im], axis=2), precision=hp)

    m_fwd = lag_kernels(0, pw_re[:tc, 0], pw_im[:tc, 0])
    m_bwd = lag_kernels(1, pwr_re[1:tc + 1, 1], pwr_im[1:tc + 1, 1])
    lag_table = jnp.concatenate([m_bwd[:, :, :tc - 1], m_bwd[:, :, tc - 1:] + m_fwd[:, :, :1], m_fwd[:, :, 1:],
                                 jnp.zeros_like(m_fwd[:, :, :1])], axis=2)
    intra = s5_toeplitz(lag_table.reshape(S5_GROUPS, hd, 2 * tc * hd))
    sf_re, sf_im = _cmul(pwr_re[1:tc + 1, 0][:, :, None, :], pwr_im[1:tc + 1, 0][:, :, None, :],
                         cbt_re[0][None], cbt_im[0][None])
    sb_re, sb_im = _cmul(pw_re[:tc, 1][:, :, None, :], pw_im[:tc, 1][:, :, None, :],
                         cbt_re[1][None], cbt_im[1][None])
    summ = jnp.concatenate([sf_re, sb_re, sf_im, sb_im], axis=-1)
    summ = jnp.swapaxes(summ, 0, 1).reshape(S5_GROUPS, tc * hd, 4 * n)
    gtn = lambda z: jnp.swapaxes(z, 0, 1)[:, :, None, :]
    qf_re, qf_im = _cmul(c_re[0][:, None], c_im[0][:, None], gtn(pw_re[1:tc + 1, 0]), gtn(pw_im[1:tc + 1, 0]))
    qb_re, qb_im = _cmul(c_re[1][:, None], c_im[1][:, None], gtn(pwr_re[:tc, 1]), gtn(pwr_im[:tc, 1]))
    carry = jnp.concatenate([qf_re, qb_re, -qf_im, -qb_im], axis=-1)
    carry = carry.reshape(S5_GROUPS, tc * hd, 4 * n)
    a_re = jnp.concatenate([pw_re[tc, 0], pw_re[tc, 1]], axis=-1)
    a_im = jnp.concatenate([pw_im[tc, 0], pw_im[tc, 1]], axis=-1)
    return intra, summ, carry, a_re, a_im


def s5_regroup_matrix():
    n = S5_SUPER * LANES
    src = np.arange(n)
    tb, g, h = src // LANES, (src % LANES) // S5_GROUP_DIM, src % S5_GROUP_DIM
    perm = np.zeros((n, n), np.float32)
    perm[src, g * LANES + tb * S5_GROUP_DIM + h] = 1.0
    return jnp.asarray(perm, BF16)


def _s5_regroup_kernel(p_ref, w_ref, perm_ref, u_ref, sr_ref, si_ref, *, n_chunks):
    n2 = 2 * S5_STATE
    for q in range(S5_CHUNK // S5_SUPER):
        rows_t = [p_ref[pl.ds(q * S5_SUPER + tb, n_chunks, stride=S5_CHUNK), :].astype(BF16) for tb in range(S5_SUPER)]
        grouped = _dot(jnp.concatenate(rows_t, axis=-1), perm_ref[...]).astype(BF16)
        for g in range(S5_SUPER):
            u_ref[g, :, q * LANES:(q + 1) * LANES] = grouped[:, g * LANES:(g + 1) * LANES]
    for g in range(S5_SUPER):
        s = _dot(u_ref[g], w_ref[g])
        sr_ref[pl.ds(g, n_chunks, stride=S5_SUPER), :] = s[:, :n2]
        si_ref[pl.ds(g, n_chunks, stride=S5_SUPER), :] = s[:, n2:]


def _s5_state_spec(c, n2, n_tiles):
    return pl.BlockSpec((None, c * S5_SUPER, n2), lambda a, bi: (bi * n_tiles + a, 0, 0))


def _s5_group_spec(rows, cols):
    return pl.BlockSpec((S5_SUPER, rows, cols), lambda a, bi: (a, 0, 0))


def s5_summary(p, summ):
    b, l, _ = p.shape
    c = l // S5_CHUNK
    n_tiles = S5_GROUPS // S5_SUPER
    k = S5_CHUNK * S5_GROUP_DIM
    n2 = 2 * S5_STATE
    shape = jax.ShapeDtypeStruct((b * n_tiles, c * S5_SUPER, n2), F32)
    kern = functools.partial(_s5_regroup_kernel, n_chunks=c)
    perm = s5_regroup_matrix()
    return pl.pallas_call(
        kern,
        grid=(n_tiles, b),
        in_specs=[
            pl.BlockSpec((None, l, LANES), lambda a, bi: (bi, 0, a)),
            _s5_group_spec(k, 2 * n2),
            pl.BlockSpec(perm.shape, lambda a, bi: (0, 0)),
        ],
        out_specs=[pl.BlockSpec((None, S5_SUPER, c, k), lambda a, bi: (bi * n_tiles + a, 0, 0, 0)),
                   _s5_state_spec(c, n2, n_tiles), _s5_state_spec(c, n2, n_tiles)],
        out_shape=[jax.ShapeDtypeStruct((b * n_tiles, S5_SUPER, c, k), BF16), shape, shape],
        compiler_params=_cparams("parallel", "parallel"),
        name="s5_summary",
    )(p, summ, perm)


def _s5_scan(sr_ref, si_ref, are_ref, aim_ref, or_ref, oi_ref, n_chunks):
    n2 = 2 * S5_STATE
    a_re = are_ref[...]
    a_im = aim_ref[...]
    fwd_lane = lax.broadcasted_iota(jnp.int32, (SUBLANES, n2), 1) < S5_STATE

    def body(i, state):
        x_re, x_im = state
        rows_i = pl.ds(pl.multiple_of(i * SUBLANES, SUBLANES), SUBLANES)
        rows_r = pl.ds(pl.multiple_of((n_chunks - 1 - i) * SUBLANES, SUBLANES), SUBLANES)
        or_ref[rows_i, 0:S5_STATE] = x_re[:, :S5_STATE]
        oi_ref[rows_i, 0:S5_STATE] = x_im[:, :S5_STATE]
        or_ref[rows_r, S5_STATE:n2] = x_re[:, S5_STATE:]
        oi_ref[rows_r, S5_STATE:n2] = x_im[:, S5_STATE:]
        s_re = jnp.where(fwd_lane, sr_ref[rows_i, :], sr_ref[rows_r, :])
        s_im = jnp.where(fwd_lane, si_ref[rows_i, :], si_ref[rows_r, :])
        return (a_re * x_re - a_im * x_im + s_re, a_re * x_im + a_im * x_re + s_im)

    zero = jnp.zeros((SUBLANES, n2), F32)
    lax.fori_loop(0, n_chunks, body, (zero, zero))


def _s5_output_kernel(u_ref, sr_ref, si_ref, are_ref, aim_ref, wi_ref, wc_ref, perm_ref, o_ref,
                      y_ref, xr_ref, xi_ref, *, n_chunks):
    _s5_scan(sr_ref, si_ref, are_ref, aim_ref, xr_ref, xi_ref, n_chunks)
    back = (((1,), (1,)), ((), ()))
    for g in range(S5_SUPER):
        rows = pl.ds(g, n_chunks, stride=S5_SUPER)
        xin = jnp.concatenate([xr_ref[rows, :], xi_ref[rows, :]], axis=-1).astype(BF16)
        y_ref[g] = _dot(u_ref[g], wi_ref[g]) + lax.dot_general(xin, wc_ref[g], back, preferred_element_type=F32)
    for q in range(S5_CHUNK // S5_SUPER):
        y = jnp.concatenate([y_ref[g, :, q * LANES:(q + 1) * LANES] for g in range(S5_SUPER)], axis=-1)
        hi = y.astype(BF16)
        lo = (y - hi.astype(F32)).astype(BF16)
        tok = (lax.dot_general(hi, perm_ref[...], back, preferred_element_type=F32)
               + lax.dot_general(lo, perm_ref[...], back, preferred_element_type=F32))
        for tb in range(S5_SUPER):
            o_ref[pl.ds(q * S5_SUPER + tb, n_chunks, stride=S5_CHUNK), :] = tok[:, tb * LANES:(tb + 1) * LANES]


def s5_output(u, s_re, s_im, a_re, a_im, intra, carry, *, batch):
    tiles, _, c, k = u.shape
    n_tiles = tiles // batch
    n2 = 2 * S5_STATE
    l = c * S5_CHUNK
    kern = functools.partial(_s5_output_kernel, n_chunks=c)
    perm = s5_regroup_matrix()
    decay_spec = pl.BlockSpec((S5_SUPER, n2), lambda a, bi: (a, 0))
    state_scr = pltpu.VMEM((c * S5_SUPER, n2), F32)
    return pl.pallas_call(
        kern,
        grid=(n_tiles, batch),
        in_specs=[
            pl.BlockSpec((None, S5_SUPER, c, k), lambda a, bi: (bi * n_tiles + a, 0, 0, 0)),
            _s5_state_spec(c, n2, n_tiles),
            _s5_state_spec(c, n2, n_tiles),
            decay_spec,
            decay_spec,
            _s5_group_spec(k, k),
            _s5_group_spec(k, 2 * n2),
            pl.BlockSpec(perm.shape, lambda a, bi: (0, 0)),
        ],
        out_specs=pl.BlockSpec((None, l, LANES), lambda a, bi: (bi, 0, a)),
        out_shape=jax.ShapeDtypeStruct((batch, l, D_S5), F32),
        scratch_shapes=[pltpu.VMEM((S5_SUPER, c, k), F32), state_scr, state_scr],
        compiler_params=_cparams("parallel", "parallel"),
        name="s5_output",
    )(u, s_re, s_im, a_re, a_im, intra, carry, perm)


def s5_core(p, mats):
    intra, summ, carry, a_re, a_im = mats
    u, s_re, s_im = s5_summary(p, summ.astype(BF16))
    return s5_output(u, s_re, s_im, a_re, a_im, intra.astype(BF16), carry.astype(BF16), batch=p.shape[0])


def _to_time_tiles(y):
    nf, c = DFT_FAST, y.shape[-1]
    slabs = [y[s * nf:(s + 1) * nf].reshape(DFT_TILES, SUBLANES, c) for s in range(y.shape[0] // nf)]
    return jnp.concatenate(slabs, axis=1)


def _from_time_tiles(v):
    c = v.shape[-1]
    slabs = [v[:, s * SUBLANES:(s + 1) * SUBLANES, :].reshape(DFT_FAST, c) for s in range(v.shape[1] // SUBLANES)]
    return jnp.concatenate(slabs, axis=0)


def _odd_in_kernel(xp_ref, x_ref, xn_ref, g_ref, w_ref, cw_ref, cb_ref, s5_ref, gates_ref, *, tm):
    ext = _with_halo(xp_ref, x_ref, xn_ref)
    p = _dot(_rms(ext, g_ref[...]).astype(BF16), w_ref[...])
    lo, hi = POOL_HALO, POOL_HALO + tm
    s5_ref[0] = p[lo:hi, :D_S5]
    ph = p[:, D_S5:]
    y = cw_ref[0:1, :] * _shift_rows(ph, 1) + cw_ref[1:2, :] * ph + cw_ref[2:3, :] * _shift_rows(ph, -1)
    y = y[lo:hi] + cb_ref[...]
    for k in range(3):
        for ct in range(D_HYENA // LANES):
            col = k * D_HYENA + ct * LANES
            gates_ref[k, ct] = _to_time_tiles(y[:, col:col + LANES])


def odd_in(x, g, w, layer, conv_w, conv_b, *, tm):
    b, l, d = x.shape
    n = w.shape[-1]
    rows = tm // DFT_FAST * SUBLANES
    cts = D_HYENA // LANES
    kern = functools.partial(_odd_in_kernel, tm=tm)
    return pl.pallas_call(
        kern,
        grid=(b, l // tm),
        in_specs=_halo_specs(tm, l, d, lambda: 0) + [
            pl.BlockSpec((1, d), lambda bi, i: (0, 0)),
            _layer_spec((d, n), layer, lambda bi, i: (0, 0)),
            pl.BlockSpec((3, n - D_S5), lambda bi, i: (0, 0)),
            pl.BlockSpec((1, n - D_S5), lambda bi, i: (0, 0)),
        ],
        out_specs=[pl.BlockSpec((1, tm, D_S5), lambda bi, i: (bi, i, 0)),
                   pl.BlockSpec((3, None, cts, DFT_TILES, rows, LANES), lambda bi, i: (0, bi, 0, 0, i, 0))],
        out_shape=[jax.ShapeDtypeStruct((b, l, D_S5), F32),
                   jax.ShapeDtypeStruct((3, b, cts, DFT_TILES, l // DFT_FAST * SUBLANES, LANES), F32)],
        compiler_params=_cparams("parallel", "parallel"),
        name="odd_in",
    )(x, x, x, g.reshape(1, d), w, conv_w, conv_b.reshape(1, -1))


def _filter_mlp_kernel(bands_ref, w1t_ref, w1cs_ref, b1_ref, w2_ref, b2_ref, w3h_ref, w3l_ref, fr_ref, dl_ref,
                       eo_ref, asum_ref, *, tl, seq_len):
    i = pl.program_id(0)
    hp = lax.Precision.HIGHEST
    c = D_HYENA
    half = tl // 2
    first = i * tl + lax.broadcasted_iota(jnp.int32, (half, 1), 0)

    def positions(width):
        right = lax.broadcasted_iota(jnp.int32, (1, width), 1) >= width // 2
        return (first + jnp.where(right, half, 0)).astype(F32)

    feat = bands_ref.shape[1]
    ang = (2.0 * math.pi / seq_len) * positions(feat) * bands_ref[...]
    is_cos = lax.broadcasted_iota(jnp.int32, (1, feat), 1) % (2 * HYENA_BANDS) < HYENA_BANDS
    cs = jnp.sin(ang + jnp.where(is_cos, 0.5 * math.pi, math.pi))
    fr = fr_ref[...]
    z = (positions(w1t_ref.shape[1]) / (seq_len - 1.0)) * w1t_ref[...] + jnp.dot(
        cs, w1cs_ref[...], precision=hp, preferred_element_type=F32)
    h = jnp.sin(fr * (z + b1_ref[...]))
    h = jnp.sin(fr * (jnp.dot(h, w2_ref[...], precision=hp, preferred_element_type=F32) + b2_ref[...]))
    h_hi = h.astype(BF16)
    h_lo = (h - h_hi.astype(F32)).astype(BF16)
    h = jnp.concatenate([_dot(h_hi, w3h_ref[r]) + (_dot(h_lo, w3h_ref[r]) + _dot(h_hi, w3l_ref[r])) for r in range(2)],
                        axis=0)
    t_norm = (i * tl + lax.broadcasted_iota(jnp.int32, (tl, 1), 0)).astype(F32) / (seq_len - 1.0)
    decay = jnp.exp(-t_norm * dl_ref[...])

    @pl.when(i == 0)
    def _():
        asum_ref[...] = jnp.zeros_like(asum_ref)

    for o in range(h.shape[1] // (2 * c)):
        fwd = h[:, 2 * o * c:(2 * o + 1) * c] * decay
        bwd = h[:, (2 * o + 1) * c:(2 * o + 2) * c] * decay
        for part, val in enumerate((fwd + bwd, fwd - bwd)):
            for ct in range(c // LANES):
                eo_ref[((2 * o + part) * c) // LANES + ct] = _to_time_tiles(val[:, ct * LANES:(ct + 1) * LANES])
        asum_ref[:, o * c:(o + 1) * c] += jnp.sum(jnp.abs(fwd) + jnp.abs(bwd), axis=0, keepdims=True)


def hyena_filter_mlp(seq_len, w1, b1, w2, b2, w3, freq, *, tl):
    n_out = w3.shape[1]
    bands = jnp.linspace(1e-4, HYENA_BANDS - 1, HYENA_BANDS, dtype=F32)
    bands2 = jnp.concatenate([bands, bands])[None, :]
    deltas = jnp.abs(jnp.linspace(math.log(HYENA_TARGET) / HYENA_LONG_DECAY_PCT,
                                  math.log(HYENA_TARGET) / HYENA_SHORT_DECAY_PCT, D_HYENA, dtype=F32))[None, :]
    kern = functools.partial(_filter_mlp_kernel, tl=tl, seq_len=seq_len)
    full = lambda a: pl.BlockSpec(a.shape, lambda i: (0,) * a.ndim)
    w3_hi = w3.astype(BF16)
    w3_lo = (w3 - w3_hi.astype(F32)).astype(BF16)
    twice = lambda v: jnp.concatenate([v, v])[None, :]
    blockdiag = lambda m: jnp.concatenate([jnp.pad(m, ((0, 0), (0, m.shape[1]))),
                                           jnp.pad(m, ((0, 0), (m.shape[1], 0)))], axis=0)
    per_half = lambda m: jnp.stack([jnp.pad(m, ((0, m.shape[0]), (0, 0))), jnp.pad(m, ((m.shape[0], 0), (0, 0)))])
    args = [twice(bands2[0]), twice(w1[0]), blockdiag(w1[1:]), twice(b1), blockdiag(w2), twice(b2),
            per_half(w3_hi), per_half(w3_lo), twice(freq), deltas]
    return pl.pallas_call(
        kern,
        grid=(seq_len // tl,),
        in_specs=[full(a) for a in args],
        out_specs=[pl.BlockSpec((n_out // LANES, DFT_TILES, tl // DFT_FAST * SUBLANES, LANES), lambda i: (0, 0, i, 0)),
                   pl.BlockSpec((1, n_out // 2), lambda i: (0, 0))],
        out_shape=[jax.ShapeDtypeStruct((n_out // LANES, DFT_TILES, seq_len // DFT_FAST * SUBLANES, LANES), F32),
                   jax.ShapeDtypeStruct((1, n_out // 2), F32)],
        compiler_params=_cparams("arbitrary"),
        name="hyena_filter_mlp",
    )(*args)


def dft_tables(seq_len):
    n = 2 * seq_len
    nf = DFT_FAST
    ns = n // nf
    k_a = np.arange(ns)[:, None]
    n_s = np.arange(ns // 2)[None, :]
    w_s = np.exp(-2j * np.pi * ((k_a * n_s) % ns) / ns)
    n_f = np.arange(nf)[:, None]
    tw = np.exp(-2j * np.pi * ((n_f * np.arange(ns)[None, :]) % n) / n)
    c32 = lambda z: (jnp.asarray(z.real, F32), jnp.asarray(z.imag, F32))
    ws_re, ws_im = c32(w_s)
    tw_re, tw_im = c32(tw)
    g_re = tw_re[:, :, None] * ws_re[None] - tw_im[:, :, None] * ws_im[None]
    g_im = tw_re[:, :, None] * ws_im[None] + tw_im[:, :, None] * ws_re[None]
    fwd_a = jnp.concatenate([jnp.concatenate([g_re, -g_im], axis=2),
                             jnp.concatenate([g_im, g_re], axis=2)], axis=1)
    gt_re = jnp.swapaxes(g_re, 1, 2) * (1.0 / n)
    gt_im = jnp.swapaxes(g_im, 1, 2) * (1.0 / n)
    inv_a = jnp.concatenate([jnp.concatenate([gt_re, gt_im], axis=2),
                             jnp.concatenate([-gt_im, gt_re], axis=2)], axis=1)
    kk = np.arange(nf)
    w_f = np.exp(-2j * np.pi * ((kk[:, None] * kk[None, :]) % nf) / nf)
    f_re, f_im = c32(w_f)
    fwd_b = jnp.concatenate([jnp.concatenate([f_re, -f_im], axis=1),
                             jnp.concatenate([f_im, f_re], axis=1)], axis=0)
    inv_b = jnp.concatenate([jnp.concatenate([f_re, f_im], axis=1),
                             jnp.concatenate([-f_im, f_re], axis=1)], axis=0)
    kept = min(ns, -(-(ns // 2 + 1) // SUBLANES) * SUBLANES)
    fwd_a_real = jnp.concatenate([fwd_a[:, :kept, :ns // 2], fwd_a[:, ns:ns + kept, :ns // 2]], axis=1)
    m_re, m_im = jnp.roll(f_re, -1, axis=0), jnp.roll(f_im, -1, axis=0)
    mirror_part = jnp.stack([jnp.concatenate([m_re, m_im], axis=1), jnp.concatenate([m_im, -m_re], axis=1)])
    return dict(fwd_a=fwd_a.astype(BF16), fwd_a_real=fwd_a_real.astype(BF16),
                inv_a=inv_a.astype(BF16), fwd_b=fwd_b.astype(BF16), inv_b=inv_b.astype(BF16),
                fwd_b_part=fwd_b.reshape(2, nf, 2 * nf).astype(BF16), mirror_b_part=mirror_part.astype(BF16))


def _filter_dft_kernel(x_ref, ga_ref, fb_ref, fbm_ref, nrm_ref, k_ref, ar_scr, ai_scr):
    nf = DFT_FAST
    kept = ga_ref.shape[1] // 2
    n_in = ga_ref.shape[2]
    ns = 2 * n_in

    def stage_a(q, carry):
        for j in range(SUBLANES):
            x = x_ref[q, pl.ds(j, n_in, stride=SUBLANES), :].astype(BF16)
            n_f = q * SUBLANES + j
            a = _dot(ga_ref[n_f], x)
            ar_scr[pl.ds(n_f, kept, stride=DFT_PITCH), :] = a[:kept]
            ai_scr[pl.ds(n_f, kept, stride=DFT_PITCH), :] = a[kept:]
        return carry

    lax.fori_loop(0, DFT_TILES, stage_a, 0, unroll=DFT_UNROLL // SUBLANES)
    inv_norm = 1.0 / (nrm_ref[...] + 1e-6)

    def slab(start):
        return jnp.concatenate([ar_scr[pl.ds(start, nf), :], ai_scr[pl.ds(start, nf), :]], axis=0).astype(BF16)

    for k in (0, n_in):
        k_ref[k * nf:(k + 1) * nf, :] = _dot(fb_ref[...], slab(k * DFT_PITCH)) * inv_norm

    def stage_b(k, carry):
        a = slab(pl.multiple_of(k * DFT_PITCH, SUBLANES))
        k_ref[pl.ds(pl.multiple_of(k * nf, nf), nf), :] = _dot(fb_ref[...], a) * inv_norm
        k_ref[pl.ds(pl.multiple_of((ns - k) * nf, nf), nf), :] = _dot(fbm_ref[...], a) * inv_norm
        return carry

    pairs = n_in - 1
    lax.fori_loop(1, n_in, stage_b, 0, unroll=max(u for u in (9, 8, 7, 4, 3, 2, 1) if pairs % u == 0))


def hyena_filter_dft(eo, asum, tables):
    col_tiles, q_tiles, rows, _ = eo.shape
    l = q_tiles * rows
    c = D_HYENA
    ct = c // LANES
    orders = col_tiles // (2 * ct)
    ga, fb, fbm = tables['fwd_a_real'], tables['fwd_b_part'], tables['mirror_b_part']
    kept = ga.shape[1] // 2
    scr = pltpu.VMEM((kept * DFT_PITCH, LANES), F32)
    part_spec = pl.BlockSpec((None,) + fb.shape[1:], lambda o, j, part: (part, 0, 0))
    return pl.pallas_call(
        _filter_dft_kernel,
        grid=(orders, ct, 2),
        in_specs=[
            pl.BlockSpec((None,) + eo.shape[1:], lambda o, j, part: ((2 * o + part) * ct + j, 0, 0, 0)),
            pl.BlockSpec(ga.shape, lambda o, j, part: (0, 0, 0)),
            part_spec,
            part_spec,
            pl.BlockSpec((1, LANES), lambda o, j, part: (0, o * ct + j)),
        ],
        out_specs=pl.BlockSpec((None, None, None, 2 * l, LANES), lambda o, j, part: (o, part, j, 0, 0)),
        out_shape=jax.ShapeDtypeStruct((orders, 2, ct, 2 * l, LANES), F32),
        scratch_shapes=[scr, scr],
        compiler_params=_cparams("parallel", "parallel", "parallel"),
        name="hyena_filter_dft",
    )(eo, ga, fb, fbm, asum)


def _lane_cat(parts):
    return parts[0] if len(parts) == 1 else jnp.concatenate(parts, axis=-1)


def _hyena_conv_kernel(x_ref, ga_ref, k_ref, fb_ref, ib_ref, gi_ref, gate_ref, u_ref, bias_ref, o_ref, *scr,
                       n_a, n_b, slabs):
    nf = DFT_FAST
    tiles, qt = x_ref.shape[1], x_ref.shape[2]
    half = x_ref.shape[3] // SUBLANES
    ns = 2 * half
    step = pl.program_id(1)

    @pl.when(step < n_a)
    def _():
        for qj in range(qt * SUBLANES):
            q, j = divmod(qj, SUBLANES)
            rows = pl.ds(j, half, stride=SUBLANES)
            x = _lane_cat([jnp.concatenate([x_ref[0, t, q, rows, :], x_ref[1, t, q, rows, :]], axis=0)
                           for t in range(tiles)])
            a = _dot(ga_ref[qj], x.astype(BF16))
            dst = pl.ds(step * (qt * SUBLANES) + qj, ns, stride=DFT_PITCH)
            for t in range(tiles):
                scr[2 * t][dst, :] = a[:ns, t * LANES:(t + 1) * LANES]
                scr[2 * t + 1][dst, :] = a[ns:, t * LANES:(t + 1) * LANES]

    @pl.when((step >= n_a) & (step < n_a + n_b))
    def _():
        for s in range(slabs):
            k_a = (step - n_a) * slabs + s
            rows = pl.ds(pl.multiple_of(k_a * DFT_PITCH, SUBLANES), nf)
            a = jnp.concatenate([_lane_cat([scr[2 * t + part][rows, :] for t in range(tiles)]) for part in range(2)], axis=0)
            x = _dot(fb_ref[...], a.astype(BF16))
            xr, xi = x[:nf], x[nf:]
            kr = _lane_cat([k_ref[0, t, s * nf:(s + 1) * nf, :] for t in range(tiles)])
            ki = _lane_cat([k_ref[1, t, s * nf:(s + 1) * nf, :] for t in range(tiles)])
            y = jnp.concatenate([xr * kr - xi * ki, xr * ki + xi * kr], axis=0).astype(BF16)
            cc = _dot(ib_ref[...], y)
            for t in range(tiles):
                scr[2 * t][rows, :] = cc[:nf, t * LANES:(t + 1) * LANES]
                scr[2 * t + 1][rows, :] = cc[nf:, t * LANES:(t + 1) * LANES]

    @pl.when(step >= n_a + n_b)
    def _():
        first = (step - (n_a + n_b)) * (qt * SUBLANES)
        for qj in range(qt * SUBLANES):
            q, j = divmod(qj, SUBLANES)
            src = pl.ds(first + qj, ns, stride=DFT_PITCH)
            cc = jnp.concatenate([_lane_cat([scr[2 * t + part][src, :] for t in range(tiles)]) for part in range(2)], axis=0)
            y = _dot(gi_ref[qj], cc.astype(BF16))
            dst = pl.ds(j, half, stride=SUBLANES)
            for t in range(tiles):
                o_ref[0, t, q, dst, :] = y[:half, t * LANES:(t + 1) * LANES]
                o_ref[1, t, q, dst, :] = y[half:, t * LANES:(t + 1) * LANES]
        for t in range(tiles):
            bias = bias_ref[:, t * LANES:(t + 1) * LANES]
            for bi in range(2):
                o_ref[bi, t] = gate_ref[bi, t] * (o_ref[bi, t] + u_ref[bi, t] * bias)


def hyena_long_conv(u6, u_idx, gate6, gate_idx, kf, order, bias, tables, *, slabs=8, tiles=2, qt=2):
    _, _, cts, q_tiles, rows, _ = u6.shape
    ns = 2 * rows // SUBLANES
    n_a = q_tiles // qt
    n_b = ns // slabs
    ga, gi, fb, ib = tables['fwd_a'], tables['inv_a'], tables['fwd_b'], tables['inv_b']
    q_fwd = lambda s: jnp.minimum(s, n_a - 1)
    q_mid = lambda s: jnp.clip(s - n_a, 0, n_b - 1)
    q_inv = lambda s: jnp.clip(s - n_a - n_b, 0, n_a - 1)

    def seq_spec(idx, q_of):
        return pl.BlockSpec((None, 2, tiles, qt, rows, LANES), lambda c, s: (idx, 0, c, q_of(s), 0, 0))

    scr = pltpu.VMEM((ns * DFT_PITCH, LANES), F32)
    kern = functools.partial(_hyena_conv_kernel, n_a=n_a, n_b=n_b, slabs=slabs)
    return pl.pallas_call(
        kern,
        grid=(cts // tiles, 2 * n_a + n_b),
        in_specs=[
            seq_spec(u_idx, q_fwd),
            pl.BlockSpec((qt * SUBLANES,) + ga.shape[1:], lambda c, s: (q_fwd(s), 0, 0)),
            pl.BlockSpec((None, 2, tiles, slabs * DFT_FAST, LANES), lambda c, s: (order, 0, c, q_mid(s), 0)),
            pl.BlockSpec(fb.shape, lambda c, s: (0, 0)),
            pl.BlockSpec(ib.shape, lambda c, s: (0, 0)),
            pl.BlockSpec((qt * SUBLANES,) + gi.shape[1:], lambda c, s: (q_inv(s), 0, 0)),
            seq_spec(gate_idx, q_inv),
            seq_spec(u_idx, q_inv),
            pl.BlockSpec((1, tiles * LANES), lambda c, s: (0, c)),
        ],
        out_specs=pl.BlockSpec((2, tiles, qt, rows, LANES), lambda c, s: (0, c, q_inv(s), 0, 0)),
        out_shape=jax.ShapeDtypeStruct((2, cts, q_tiles, rows, LANES), F32),
        scratch_shapes=[scr] * (2 * tiles),
        compiler_params=_cparams("arbitrary", "arbitrary", vmem=VMEM_LIMIT_CONV),
        name="hyena_long_conv",
    )(u6, ga, kf, fb, ib, gi, gate6, u6, bias.reshape(1, -1))


def hyena_filter_spectra(seq_len, tables, w1, b1, w2, b2, w3, freq):
    eo, asum = hyena_filter_mlp(seq_len, w1, b1, w2, b2, w3, freq, tl=FILTER_ROW_TILE)
    return hyena_filter_dft(eo, asum, tables)


def hyena_mixer(gates, kf, bias, tables):
    z = hyena_long_conv(gates, 2, gates, 1, kf, 0, bias[0], tables)
    return hyena_long_conv(z[None], 0, gates, 0, kf, 1, bias[1], tables)


def _odd_out_xattn_kernel(ys_ref, u_ref, hy_ref, x_ref, d_ref, wglu_ref, wo_ref, g_ref,
                          kv_ref, wq_ref, xwo_ref, g1_ref, g2_ref, o_ref):
    y = ys_ref[0] + d_ref[...] * u_ref[0]
    c0 = math.sqrt(2.0 / math.pi)
    gl = 0.5 * y * (1.0 + jnp.tanh(c0 * (y + 0.044715 * (y * y * y))))
    z = _dot(gl.astype(BF16), wglu_ref[...])
    s5 = gl * (1.0 / (1.0 + jnp.exp(-z)))
    hy = [_from_time_tiles(hy_ref[ct]).astype(BF16) for ct in range(hy_ref.shape[0])]
    mix = _dot(jnp.concatenate([s5.astype(BF16)] + hy, axis=-1), wo_ref[...])
    x = x_ref[0] + _rms(mix, g_ref[...])
    o_ref[0] = _xattn(x, kv_ref, wq_ref, xwo_ref, g1_ref, g2_ref)


def odd_out_xattn(ys, u_s5, hy, x, d_skip, w_glu, w_out, odd_layer, g, kv, wq, xwo, layer, g1, g2, *, tm):
    b, l, d = x.shape
    tok = lambda w: pl.BlockSpec((1, tm, w), lambda bi, i: (bi, i, 0))
    hy_spec = pl.BlockSpec((None, D_HYENA // LANES, DFT_TILES, tm // DFT_FAST * SUBLANES, LANES),
                           lambda bi, i: (bi, 0, 0, i, 0))
    return pl.pallas_call(
        _odd_out_xattn_kernel,
        grid=(b, l // tm),
        in_specs=[tok(D_S5), tok(D_S5), hy_spec, tok(d),
                  pl.BlockSpec((1, D_S5), lambda bi, i: (0, 0)),
                  _layer_spec((D_S5, D_S5), odd_layer, lambda bi, i: (0, 0)),
                  _layer_spec((d, d), odd_layer, lambda bi, i: (0, 0)),
                  pl.BlockSpec((1, d), lambda bi, i: (0, 0))] + _xattn_specs(d, layer),
        out_specs=tok(d),
        out_shape=jax.ShapeDtypeStruct(x.shape, F32),
        compiler_params=_cparams("parallel", "parallel"),
        name="odd_out_xattn",
    )(ys, u_s5, hy, x, d_skip.reshape(1, D_S5), w_glu, w_out, g.reshape(1, d),
      kv, wq, xwo, g1.reshape(1, d), g2.reshape(1, d))


def kernel(x, mem, norm_mix, norm_xattn, norm_mem, norm_mlp, xa_wq, xa_wk, xa_wv, xa_wo, mlp_w1, mlp_w2, ev_w_in, ev_pool_w, ev_pool_scale, ev_conv_w, ev_w_out, od_w_in, od_s5_lambda_re, od_s5_lambda_im, od_s5_log_dt, od_s5_b_re, od_s5_b_im, od_s5_c_re, od_s5_c_im, od_s5_d, od_s5_w_glu, od_hy_short_w, od_hy_short_b, od_hy_w1, od_hy_b1, od_hy_w2, od_hy_b2, od_hy_w3, od_hy_freq, od_hy_bias, od_w_out):
    b, l, d = x.shape
    depth = norm_mix.shape[0]
    assert b == 2, "the long convolution packs the two batch rows as one complex signal"
    tables = dft_tables(l)
    mem2d = mem.reshape(b * N_MEM, d)
    wq, wo = xa_wq.astype(BF16), xa_wo.astype(BF16)
    wkv = jnp.concatenate([xa_wk, xa_wv], axis=2).astype(BF16)
    kv = memory_kv(mem2d, norm_mem, wkv).reshape(depth, b, N_MEM, 2 * d)
    w1, w2 = mlp_w1.astype(BF16), mlp_w2.astype(BF16)
    ev_in, ev_out, ev_pool = ev_w_in.astype(BF16), ev_w_out.astype(BF16), ev_pool_w.astype(BF16)
    od_in, od_out, od_glu = od_w_in.astype(BF16), od_w_out.astype(BF16), od_s5_w_glu.astype(BF16)
    for i in range(depth):
        j = i // 2
        if i % 2 == 0:
            x = even_mixer(x, norm_mix[i, 0], ev_in, ev_pool, ev_pool_scale[j], ev_conv_w[j], ev_out, j,
                           norm_mix[i, 1], tm=ROW_TILE)
            x = xattn_block(x, kv, wq, wo, i, norm_xattn[i, 0], norm_xattn[i, 1], tm=ROW_TILE)
        else:
            u_s5, gates = odd_in(x, norm_mix[i, 0], od_in, j, od_hy_short_w[j], od_hy_short_b[j], tm=ROW_TILE)
            mats = s5_matrices(od_s5_lambda_re[j], od_s5_lambda_im[j], od_s5_log_dt[j], od_s5_b_re[j],
                               od_s5_b_im[j], od_s5_c_re[j], od_s5_c_im[j])
            ys = s5_core(u_s5, mats)
            kf = hyena_filter_spectra(l, tables, od_hy_w1[j], od_hy_b1[j], od_hy_w2[j], od_hy_b2[j],
                                      od_hy_w3[j], od_hy_freq[j])
            hy = hyena_mixer(gates, kf, od_hy_bias[j], tables)
            x = odd_out_xattn(ys, u_s5, hy, x, od_s5_d[j], od_glu, od_out, j, norm_mix[i, 1],
                              kv, wq, wo, i, norm_xattn[i, 0], norm_xattn[i, 1], tm=ROW_TILE)
        x = mlp_block(x.reshape(b * l, d), w1, w2, i, norm_mlp[i, 0], norm_mlp[i, 1],
                      tm=ROW_TILE, tf=MLP_HIDDEN_TILE).reshape(b, l, d)
    return x
```

```python
import functools
import math

import numpy as np
import jax
import jax.numpy as jnp
from jax import lax
from jax.experimental import pallas as pl
from jax.experimental.pallas import tpu as pltpu

F32 = jnp.float32
BF16 = jnp.bfloat16

D_MODEL = 1024
N_MEM = 256
RMS_EPS = 1e-6
D_POOL = 512
POOL_WINDOWS = (2, 4, 8, 16)
POOL_GROUP_DIM = 128
POOL_HALO = 8
D_CONV = 512
D_S5 = 512
S5_GROUP_DIM = 16
S5_GROUPS = 32
S5_STATE = 64
S5_CHUNK = 16
D_HYENA = 512
HYENA_BANDS = 16
HYENA_FFN = 64
HYENA_TARGET = 1e-2
HYENA_SHORT_DECAY_PCT = 0.3
HYENA_LONG_DECAY_PCT = 1.5
XA_HEADS = 4
XA_HEAD_DIM = 256
D_FF = 4096

LANES = 128
SUBLANES = 8
S5_SUPER = LANES // S5_GROUP_DIM
DFT_FAST = 128
DFT_TILES = DFT_FAST // SUBLANES
DFT_PITCH = DFT_FAST + SUBLANES
DFT_UNROLL = 16
ROW_TILE = 1024
MLP_HIDDEN_TILE = 512
FILTER_ROW_TILE = 512
VMEM_LIMIT = 56 * 1024 * 1024
VMEM_LIMIT_CONV = 62 * 1024 * 1024


def _cparams(*sem, vmem=VMEM_LIMIT):
    return pltpu.CompilerParams(dimension_semantics=sem, vmem_limit_bytes=vmem)


def _rms(xf, g):
    ms = jnp.mean(xf * xf, axis=-1, keepdims=True)
    return xf * lax.rsqrt(ms + RMS_EPS) * g


def _dot(a, b):
    return jnp.dot(a, b, preferred_element_type=F32)


def _layer_spec(block, layer, tail_map):
    return pl.BlockSpec((None,) + block, lambda *idx: (layer,) + tail_map(*idx))


def _norm_matmul_kernel(x_ref, g_ref, w_ref, o_ref):
    xn = _rms(x_ref[...], g_ref[...]).astype(BF16)
    o_ref[...] = _dot(xn, w_ref[...]).astype(o_ref.dtype)


def memory_kv(mem2d, g, w):
    layers, d, n = w.shape
    m = mem2d.shape[0]
    return pl.pallas_call(
        _norm_matmul_kernel,
        grid=(layers,),
        in_specs=[
            pl.BlockSpec((m, d), lambda i: (0, 0)),
            pl.BlockSpec((None, 1, d), lambda i: (i, 0, 0)),
            pl.BlockSpec((None, d, n), lambda i: (i, 0, 0)),
        ],
        out_specs=pl.BlockSpec((None, m, n), lambda i: (i, 0, 0)),
        out_shape=jax.ShapeDtypeStruct((layers, m, n), BF16),
        compiler_params=_cparams("parallel"),
        name="memory_kv",
    )(mem2d, g.reshape(layers, 1, d), w)


def _xattn(x, kv_ref, wq_ref, wo_ref, g1_ref, g2_ref):
    xn = _rms(x, g1_ref[...]).astype(BF16)
    q = (_dot(xn, wq_ref[...]) * (XA_HEAD_DIM ** -0.5)).astype(BF16)
    heads = []
    for h in range(XA_HEADS):
        lo = h * XA_HEAD_DIM
        qh = q[:, lo:lo + XA_HEAD_DIM]
        kh = kv_ref[0, :, lo:lo + XA_HEAD_DIM]
        vh = kv_ref[0, :, D_MODEL + lo:D_MODEL + lo + XA_HEAD_DIM]
        s = lax.dot_general(qh, kh, (((1,), (1,)), ((), ())), preferred_element_type=F32)
        e = jnp.exp(s - jnp.max(s, axis=-1, keepdims=True))
        p = e / jnp.sum(e, axis=-1, keepdims=True)
        heads.append(_dot(p.astype(BF16), vh).astype(BF16))
    o = jnp.concatenate(heads, axis=-1)
    y = _dot(o, wo_ref[...])
    return x + _rms(y, g2_ref[...])


def _xattn_specs(d, layer):
    return [
        pl.BlockSpec((None, 1, N_MEM, 2 * d), lambda bi, i: (layer, bi, 0, 0)),
        _layer_spec((d, d), layer, lambda bi, i: (0, 0)),
        _layer_spec((d, d), layer, lambda bi, i: (0, 0)),
        pl.BlockSpec((1, d), lambda bi, i: (0, 0)),
        pl.BlockSpec((1, d), lambda bi, i: (0, 0)),
    ]


def _mlp_kernel(x_ref, w1_ref, w2_ref, g1_ref, g2_ref, o_ref, h_ref, *, tf):
    x = x_ref[...]
    xn = _rms(x, g1_ref[...]).astype(BF16)
    for c in range(h_ref.shape[1] // tf):
        cols = slice(c * tf, (c + 1) * tf)
        h = jnp.maximum(_dot(xn, w1_ref[:, cols]), 0.0)
        h_ref[:, cols] = (h * h).astype(BF16)
    o_ref[...] = x + _rms(_dot(h_ref[...], w2_ref[...]), g2_ref[...])


def mlp_block(x2d, w1, w2, layer, g1, g2, *, tm, tf):
    m, d = x2d.shape
    ff = w1.shape[-1]
    resident = dict(pipeline_mode=pl.Buffered(1))
    return pl.pallas_call(
        functools.partial(_mlp_kernel, tf=tf),
        grid=(m // tm,),
        in_specs=[
            pl.BlockSpec((tm, d), lambda i: (i, 0)),
            pl.BlockSpec((None, d, ff), lambda i: (layer, 0, 0), **resident),
            pl.BlockSpec((None, ff, d), lambda i: (layer, 0, 0), **resident),
            pl.BlockSpec((1, d), lambda i: (0, 0)),
            pl.BlockSpec((1, d), lambda i: (0, 0)),
        ],
        out_specs=pl.BlockSpec((tm, d), lambda i: (i, 0)),
        out_shape=jax.ShapeDtypeStruct((m, d), F32),
        scratch_shapes=[pltpu.VMEM((tm, ff), BF16)],
        compiler_params=_cparams("parallel"),
        name="mlp_block",
    )(x2d, w1, w2, g1.reshape(1, d), g2.reshape(1, d))


def _halo_specs(tm, seq_len, width, col):
    r = tm // POOL_HALO
    last = seq_len // POOL_HALO - 1

    def prev_map(bi, i, *_):
        return (bi, jnp.maximum(i * r - 1, 0), col(*_))

    def main_map(bi, i, *_):
        return (bi, i, col(*_))

    def next_map(bi, i, *_):
        return (bi, jnp.minimum((i + 1) * r, last), col(*_))

    return [
        pl.BlockSpec((1, POOL_HALO, width), prev_map),
        pl.BlockSpec((1, tm, width), main_map),
        pl.BlockSpec((1, POOL_HALO, width), next_map),
    ]


def _with_halo(prev_ref, main_ref, next_ref):
    i = pl.program_id(1)
    prev = jnp.where(i > 0, prev_ref[0], 0.0)
    nxt = jnp.where(i < pl.num_programs(1) - 1, next_ref[0], 0.0)
    return jnp.concatenate([prev, main_ref[0], nxt], axis=0)


def _shift_rows(v, k):
    return pltpu.roll(v, k % v.shape[0], 0)


def _even_mixer_kernel(xp_ref, x_ref, xn_ref, gi_ref, wi_ref, wg_ref, ps_ref, cw_ref, wo_ref, g_ref,
                       kv_ref, wq_ref, xwo_ref, g1_ref, g2_ref, o_ref, *, tm, seq_len):
    x_ext = _with_halo(xp_ref, x_ref, xn_ref)
    ext = _dot(_rms(x_ext, gi_ref[...]).astype(BF16), wi_ref[...])
    lo, hi = POOL_HALO, POOL_HALO + tm
    t = pl.program_id(1) * tm + lax.broadcasted_iota(jnp.int32, (tm, 1), 0)
    parts = []
    for gi, win in enumerate(POOL_WINDOWS):
        half = win // 2
        u = ext[:, gi * POOL_GROUP_DIM:(gi + 1) * POOL_GROUP_DIM]
        s = u + _shift_rows(u, 1)
        step = 1
        while 2 * step < win:
            s = _shift_rows(s, step) + _shift_rows(s, -step)
            step *= 2
        cnt = (jnp.minimum(t + half, seq_len) - jnp.maximum(t - half, 0)).astype(F32)
        pooled = s[lo:hi] / cnt - u[lo:hi]
        y = _dot(pooled.astype(BF16), wg_ref[gi])
        parts.append((y * ps_ref[:, gi * POOL_GROUP_DIM:(gi + 1) * POOL_GROUP_DIM]).astype(BF16))
    b_gate = ext[lo:hi, D_POOL:D_POOL + D_CONV]
    ch = ext[:, D_POOL + D_CONV:D_POOL + 2 * D_CONV] * ext[:, D_POOL + 2 * D_CONV:D_POOL + 3 * D_CONV]
    conv = cw_ref[0:1, :] * _shift_rows(ch, 1) + cw_ref[1:2, :] * ch + cw_ref[2:3, :] * _shift_rows(ch, -1)
    parts.append((b_gate * conv[lo:hi]).astype(BF16))
    mix = _dot(jnp.concatenate(parts, axis=-1), wo_ref[...])
    x = x_ref[0] + _rms(mix, g_ref[...])
    o_ref[0] = _xattn(x, kv_ref, wq_ref, xwo_ref, g1_ref, g2_ref)


def even_mixer_xattn(x, g_in, w_in, w_group, pool_scale, conv_w, w_out, even_layer, g,
                     kv, wq, xwo, layer, g1, g2, *, tm):
    b, l, d = x.shape
    kern = functools.partial(_even_mixer_kernel, tm=tm, seq_len=l)
    return pl.pallas_call(
        kern,
        grid=(b, l // tm),
        in_specs=_halo_specs(tm, l, d, lambda: 0) + [
            pl.BlockSpec((1, d), lambda bi, i: (0, 0)),
            _layer_spec(w_in.shape[1:], even_layer, lambda bi, i: (0, 0)),
            _layer_spec(w_group.shape[1:], even_layer, lambda bi, i: (0, 0, 0)),
            pl.BlockSpec((1, D_POOL), lambda bi, i: (0, 0)),
            pl.BlockSpec((3, D_CONV), lambda bi, i: (0, 0)),
            _layer_spec((d, d), even_layer, lambda bi, i: (0, 0)),
            pl.BlockSpec((1, d), lambda bi, i: (0, 0)),
        ] + _xattn_specs(d, layer),
        out_specs=pl.BlockSpec((1, tm, d), lambda bi, i: (bi, i, 0)),
        out_shape=jax.ShapeDtypeStruct(x.shape, F32),
        compiler_params=_cparams("parallel", "parallel"),
        name="even_mixer_xattn",
    )(x, x, x, g_in.reshape(1, d), w_in, w_group, pool_scale.reshape(1, D_POOL), conv_w, w_out, g.reshape(1, d),
      kv, wq, xwo, g1.reshape(1, d), g2.reshape(1, d))


def _cmul(ar, ai, br, bi):
    return ar * br - ai * bi, ar * bi + ai * br


def _s5_toeplitz_kernel(m_ref, o_ref):
    tc, hd = S5_CHUNK, S5_GROUP_DIM
    for g in range(m_ref.shape[0]):
        m = m_ref[g]
        rows = [m[:, (tc - 1 - s) * hd:(2 * tc - 1 - s) * hd] for s in range(tc)]
        o_ref[g] = jnp.concatenate(rows, axis=0).astype(o_ref.dtype)


def s5_toeplitz(lag_table):
    g, hd, w = lag_table.shape
    k = w // 2
    return pl.pallas_call(
        _s5_toeplitz_kernel,
        grid=(g // S5_SUPER,),
        in_specs=[pl.BlockSpec((S5_SUPER, hd, w), lambda i: (i, 0, 0))],
        out_specs=pl.BlockSpec((S5_SUPER, k, k), lambda i: (i, 0, 0)),
        out_shape=jax.ShapeDtypeStruct((g, k, k), BF16),
        compiler_params=_cparams("parallel"),
        name="s5_toeplitz",
    )(lag_table)


def s5_matrices(lam_re, lam_im, log_dt, b_re, b_im, c_re, c_im):
    tc, hd, n = S5_CHUNK, S5_GROUP_DIM, S5_STATE
    hp = lax.Precision.HIGHEST
    lr = jnp.minimum(lam_re, -1e-4)
    li = lam_im
    dt = jnp.exp(log_dt)[..., None]
    taus = jnp.arange(tc + 1, dtype=F32)[:, None, None, None]
    rmag = jnp.exp(lr[None] * dt[None] * (tc - taus))
    rang = li[None] * dt[None] * (tc - taus)
    pwr_re, pwr_im = rmag * jnp.cos(rang), rmag * jnp.sin(rang)
    mag = jnp.exp(lr[None] * dt[None] * taus)
    ang = li[None] * dt[None] * taus
    pw_re, pw_im = mag * jnp.cos(ang), mag * jnp.sin(ang)
    nr, ni = pw_re[1] - 1.0, pw_im[1]
    den = lr * lr + li * li
    coef_re, coef_im = (nr * lr + ni * li) / den, (ni * lr - nr * li) / den
    cb_re, cb_im = _cmul(coef_re[..., None], coef_im[..., None], b_re[None], b_im[None])
    cbt_re, cbt_im = _cmul(coef_re[:, :, None, :], coef_im[:, :, None, :],
                           jnp.swapaxes(b_re, 1, 2)[None], jnp.swapaxes(b_im, 1, 2)[None])
    def lag_kernels(d, p_re, p_im):
        q_re, q_im = _cmul(p_re[..., None], p_im[..., None], cb_re[d][None], cb_im[d][None])
        c_cat = jnp.concatenate([c_re[d], -c_im[d]], axis=-1)
        return jnp.einsum('gkn,tgnj->gjtk', c_cat, jnp.concatenate([q_re, q_im], axis=2), precision=hp)

    m_fwd = lag_kernels(0, pw_re[:tc, 0], pw_im[:tc, 0])
    m_bwd = lag_kernels(1, pwr_re[1:tc + 1, 1], pwr_im[1:tc + 1, 1])
    lag_table = jnp.concatenate([m_bwd[:, :, :tc - 1], m_bwd[:, :, tc - 1:] + m_fwd[:, :, :1], m_fwd[:, :, 1:],
                                 jnp.zeros_like(m_fwd[:, :, :1])], axis=2)
    intra = s5_toeplitz(lag_table.reshape(S5_GROUPS, hd, 2 * tc * hd))
    sf_re, sf_im = _cmul(pwr_re[1:tc + 1, 0][:, :, None, :], pwr_im[1:tc + 1, 0][:, :, None, :],
                         cbt_re[0][None], cbt_im[0][None])
    sb_re, sb_im = _cmul(pw_re[:tc, 1][:, :, None, :], pw_im[:tc, 1][:, :, None, :],
                         cbt_re[1][None], cbt_im[1][None])
    summ = jnp.concatenate([sf_re, sb_re, sf_im, sb_im], axis=-1)
    summ = jnp.swapaxes(summ, 0, 1).reshape(S5_GROUPS, tc * hd, 4 * n)
    gtn = lambda z: jnp.swapaxes(z, 0, 1)[:, :, None, :]
    qf_re, qf_im = _cmul(c_re[0][:, None], c_im[0][:, None], gtn(pw_re[1:tc + 1, 0]), gtn(pw_im[1:tc + 1, 0]))
    qb_re, qb_im = _cmul(c_re[1][:, None], c_im[1][:, None], gtn(pwr_re[:tc, 1]), gtn(pwr_im[:tc, 1]))
    carry = jnp.concatenate([qf_re, qb_re, -qf_im, -qb_im], axis=-1)
    carry = carry.reshape(S5_GROUPS, tc * hd, 4 * n)
    a_re = jnp.concatenate([pw_re[tc, 0], pw_re[tc, 1]], axis=-1)
    a_im = jnp.concatenate([pw_im[tc, 0], pw_im[tc, 1]], axis=-1)
    return intra, summ, carry, a_re, a_im


def s5_regroup_matrix():
    n = S5_SUPER * LANES
    src = np.arange(n)
    tb, g, h = src // LANES, (src % LANES) // S5_GROUP_DIM, src % S5_GROUP_DIM
    perm = np.zeros((n, n), np.float32)
    perm[src, g * LANES + tb * S5_GROUP_DIM + h] = 1.0
    return jnp.asarray(perm, BF16)


def _s5_regroup_kernel(p_ref, w_ref, perm_ref, u_ref, sr_ref, si_ref, *, n_chunks):
    n2 = 2 * S5_STATE
    for q in range(S5_CHUNK // S5_SUPER):
        rows_t = [p_ref[pl.ds(q * S5_SUPER + tb, n_chunks, stride=S5_CHUNK), :].astype(BF16) for tb in range(S5_SUPER)]
        grouped = _dot(jnp.concatenate(rows_t, axis=-1), perm_ref[...]).astype(BF16)
        for g in range(S5_SUPER):
            u_ref[g, :, q * LANES:(q + 1) * LANES] = grouped[:, g * LANES:(g + 1) * LANES]
    for g in range(S5_SUPER):
        s = _dot(u_ref[g], w_ref[g])
        sr_ref[pl.ds(g, n_chunks, stride=S5_SUPER), :] = s[:, :n2]
        si_ref[pl.ds(g, n_chunks, stride=S5_SUPER), :] = s[:, n2:]


def _s5_state_spec(c, n2, n_tiles):
    return pl.BlockSpec((None, c * S5_SUPER, n2), lambda a, bi: (bi * n_tiles + a, 0, 0))


def _s5_group_spec(rows, cols):
    return pl.BlockSpec((S5_SUPER, rows, cols), lambda a, bi: (a, 0, 0))


def s5_summary(p, summ):
    b, l, _ = p.shape
    c = l // S5_CHUNK
    n_tiles = S5_GROUPS // S5_SUPER
    k = S5_CHUNK * S5_GROUP_DIM
    n2 = 2 * S5_STATE
    shape = jax.ShapeDtypeStruct((b * n_tiles, c * S5_SUPER, n2), F32)
    kern = functools.partial(_s5_regroup_kernel, n_chunks=c)
    perm = s5_regroup_matrix()
    return pl.pallas_call(
        kern,
        grid=(n_tiles, b),
        in_specs=[
            pl.BlockSpec((None, l, LANES), lambda a, bi: (bi, 0, a)),
            _s5_group_spec(k, 2 * n2),
            pl.BlockSpec(perm.shape, lambda a, bi: (0, 0)),
        ],
        out_specs=[pl.BlockSpec((None, S5_SUPER, c, k), lambda a, bi: (bi * n_tiles + a, 0, 0, 0)),
                   _s5_state_spec(c, n2, n_tiles), _s5_state_spec(c, n2, n_tiles)],
        out_shape=[jax.ShapeDtypeStruct((b * n_tiles, S5_SUPER, c, k), BF16), shape, shape],
        compiler_params=_cparams("parallel", "parallel"),
        name="s5_summary",
    )(p, summ, perm)


def _s5_scan(sr_ref, si_ref, are_ref, aim_ref, or_ref, oi_ref, n_chunks):
    n2 = 2 * S5_STATE
    a_re = are_ref[...]
    a_im = aim_ref[...]
    fwd_lane = lax.broadcasted_iota(jnp.int32, (SUBLANES, n2), 1) < S5_STATE

    def body(i, state):
        x_re, x_im = state
        rows_i = pl.ds(pl.multiple_of(i * SUBLANES, SUBLANES), SUBLANES)
        rows_r = pl.ds(pl.multiple_of((n_chunks - 1 - i) * SUBLANES, SUBLANES), SUBLANES)
        or_ref[rows_i, 0:S5_STATE] = x_re[:, :S5_STATE]
        oi_ref[rows_i, 0:S5_STATE] = x_im[:, :S5_STATE]
        or_ref[rows_r, S5_STATE:n2] = x_re[:, S5_STATE:]
        oi_ref[rows_r, S5_STATE:n2] = x_im[:, S5_STATE:]
        s_re = jnp.where(fwd_lane, sr_ref[rows_i, :], sr_ref[rows_r, :])
        s_im = jnp.where(fwd_lane, si_ref[rows_i, :], si_ref[rows_r, :])
        return (a_re * x_re - a_im * x_im + s_re, a_re * x_im + a_im * x_re + s_im)

    zero = jnp.zeros((SUBLANES, n2), F32)
    lax.fori_loop(0, n_chunks, body, (zero, zero))


def _s5_output_kernel(u_ref, sr_ref, si_ref, are_ref, aim_ref, wi_ref, wc_ref, perm_ref, o_ref,
                      y_ref, xr_ref, xi_ref, *, n_chunks):
    _s5_scan(sr_ref, si_ref, are_ref, aim_ref, xr_ref, xi_ref, n_chunks)
    back = (((1,), (1,)), ((), ()))
    for g in range(S5_SUPER):
        rows = pl.ds(g, n_chunks, stride=S5_SUPER)
        xin = jnp.concatenate([xr_ref[rows, :], xi_ref[rows, :]], axis=-1).astype(BF16)
        y_ref[g] = _dot(u_ref[g], wi_ref[g]) + lax.dot_general(xin, wc_ref[g], back, preferred_element_type=F32)
    for q in range(S5_CHUNK // S5_SUPER):
        y = jnp.concatenate([y_ref[g, :, q * LANES:(q + 1) * LANES] for g in range(S5_SUPER)], axis=-1)
        hi = y.astype(BF16)
        lo = (y - hi.astype(F32)).astype(BF16)
        tok = (lax.dot_general(hi, perm_ref[...], back, preferred_element_type=F32)
               + lax.dot_general(lo, perm_ref[...], back, preferred_element_type=F32))
        for tb in range(S5_SUPER):
            o_ref[pl.ds(q * S5_SUPER + tb, n_chunks, stride=S5_CHUNK), :] = tok[:, tb * LANES:(tb + 1) * LANES]


def s5_output(u, s_re, s_im, a_re, a_im, intra, carry, *, batch):
    tiles, _, c, k = u.shape
    n_tiles = tiles // batch
    n2 = 2 * S5_STATE
    l = c * S5_CHUNK
    kern = functools.partial(_s5_output_kernel, n_chunks=c)
    perm = s5_regroup_matrix()
    decay_spec = pl.BlockSpec((S5_SUPER, n2), lambda a, bi: (a, 0))
    state_scr = pltpu.VMEM((c * S5_SUPER, n2), F32)
    return pl.pallas_call(
        kern,
        grid=(n_tiles, batch),
        in_specs=[
            pl.BlockSpec((None, S5_SUPER, c, k), lambda a, bi: (bi * n_tiles + a, 0, 0, 0)),
            _s5_state_spec(c, n2, n_tiles),
            _s5_state_spec(c, n2, n_tiles),
            decay_spec,
            decay_spec,
            _s5_group_spec(k, k),
            _s5_group_spec(k, 2 * n2),
            pl.BlockSpec(perm.shape, lambda a, bi: (0, 0)),
        ],
        out_specs=pl.BlockSpec((None, l, LANES), lambda a, bi: (bi, 0, a)),
        out_shape=jax.ShapeDtypeStruct((batch, l, D_S5), F32),
        scratch_shapes=[pltpu.VMEM((S5_SUPER, c, k), F32), state_scr, state_scr],
        compiler_params=_cparams("parallel", "parallel"),
        name="s5_output",
    )(u, s_re, s_im, a_re, a_im, intra, carry, perm)


def s5_core(p, mats):
    intra, summ, carry, a_re, a_im = mats
    u, s_re, s_im = s5_summary(p, summ.astype(BF16))
    return s5_output(u, s_re, s_im, a_re, a_im, intra.astype(BF16), carry.astype(BF16), batch=p.shape[0])


def _to_time_tiles(y):
    nf, c = DFT_FAST, y.shape[-1]
    slabs = [y[s * nf:(s + 1) * nf].reshape(DFT_TILES, SUBLANES, c) for s in range(y.shape[0] // nf)]
    return jnp.concatenate(slabs, axis=1)


def _from_time_tiles(v):
    c = v.shape[-1]
    slabs = [v[:, s * SUBLANES:(s + 1) * SUBLANES, :].reshape(DFT_FAST, c) for s in range(v.shape[1] // SUBLANES)]
    return jnp.concatenate(slabs, axis=0)


def _odd_in_kernel(xp_ref, x_ref, xn_ref, g_ref, w_ref, cw_ref, cb_ref, s5_ref, gates_ref, *, tm):
    ext = _with_halo(xp_ref, x_ref, xn_ref)
    p = _dot(_rms(ext, g_ref[...]).astype(BF16), w_ref[...])
    lo, hi = POOL_HALO, POOL_HALO + tm
    s5_ref[0] = p[lo:hi, :D_S5]
    ph = p[:, D_S5:]
    y = cw_ref[0:1, :] * _shift_rows(ph, 1) + cw_ref[1:2, :] * ph + cw_ref[2:3, :] * _shift_rows(ph, -1)
    y = y[lo:hi] + cb_ref[...]
    for k in range(3):
        for ct in range(D_HYENA // LANES):
            col = k * D_HYENA + ct * LANES
            gates_ref[k, ct] = _to_time_tiles(y[:, col:col + LANES])


def odd_in(x, g, w, layer, conv_w, conv_b, *, tm):
    b, l, d = x.shape
    n = w.shape[-1]
    rows = tm // DFT_FAST * SUBLANES
    cts = D_HYENA // LANES
    kern = functools.partial(_odd_in_kernel, tm=tm)
    return pl.pallas_call(
        kern,
        grid=(b, l // tm),
        in_specs=_halo_specs(tm, l, d, lambda: 0) + [
            pl.BlockSpec((1, d), lambda bi, i: (0, 0)),
            _layer_spec((d, n), layer, lambda bi, i: (0, 0)),
            pl.BlockSpec((3, n - D_S5), lambda bi, i: (0, 0)),
            pl.BlockSpec((1, n - D_S5), lambda bi, i: (0, 0)),
        ],
        out_specs=[pl.BlockSpec((1, tm, D_S5), lambda bi, i: (bi, i, 0)),
                   pl.BlockSpec((3, None, cts, DFT_TILES, rows, LANES), lambda bi, i: (0, bi, 0, 0, i, 0))],
        out_shape=[jax.ShapeDtypeStruct((b, l, D_S5), F32),
                   jax.ShapeDtypeStruct((3, b, cts, DFT_TILES, l // DFT_FAST * SUBLANES, LANES), F32)],
        compiler_params=_cparams("parallel", "parallel"),
        name="odd_in",
    )(x, x, x, g.reshape(1, d), w, conv_w, conv_b.reshape(1, -1))


def _filter_mlp_kernel(bands_ref, w1t_ref, w1cs_ref, b1_ref, w2_ref, b2_ref, w3h_ref, w3l_ref, fr_ref, dl_ref,
                       eo_ref, asum_ref, *, tl, seq_len):
    i = pl.program_id(0)
    hp = lax.Precision.HIGHEST
    c = D_HYENA
    half = tl // 2
    first = i * tl + lax.broadcasted_iota(jnp.int32, (half, 1), 0)

    def positions(width):
        right = lax.broadcasted_iota(jnp.int32, (1, width), 1) >= width // 2
        return (first + jnp.where(right, half, 0)).astype(F32)

    feat = bands_ref.shape[1]
    ang = (2.0 * math.pi / seq_len) * positions(feat) * bands_ref[...]
    is_cos = lax.broadcasted_iota(jnp.int32, (1, feat), 1) % (2 * HYENA_BANDS) < HYENA_BANDS
    cs = jnp.sin(ang + jnp.where(is_cos, 0.5 * math.pi, math.pi))
    fr = fr_ref[...]
    z = (positions(w1t_ref.shape[1]) / (seq_len - 1.0)) * w1t_ref[...] + jnp.dot(
        cs, w1cs_ref[...], precision=hp, preferred_element_type=F32)
    h = jnp.sin(fr * (z + b1_ref[...]))
    h = jnp.sin(fr * (jnp.dot(h, w2_ref[...], precision=hp, preferred_element_type=F32) + b2_ref[...]))
    h_hi = h.astype(BF16)
    h_lo = (h - h_hi.astype(F32)).astype(BF16)
    h = jnp.concatenate([_dot(h_hi, w3h_ref[r]) + (_dot(h_lo, w3h_ref[r]) + _dot(h_hi, w3l_ref[r])) for r in range(2)],
                        axis=0)
    t_norm = (i * tl + lax.broadcasted_iota(jnp.int32, (tl, 1), 0)).astype(F32) / (seq_len - 1.0)
    decay = jnp.exp(-t_norm * dl_ref[...])

    @pl.when(i == 0)
    def _():
        asum_ref[...] = jnp.zeros_like(asum_ref)

    for o in range(h.shape[1] // (2 * c)):
        fwd = h[:, 2 * o * c:(2 * o + 1) * c] * decay
        bwd = h[:, (2 * o + 1) * c:(2 * o + 2) * c] * decay
        for part, val in enumerate((fwd + bwd, fwd - bwd)):
            for ct in range(c // LANES):
                eo_ref[((2 * o + part) * c) // LANES + ct] = _to_time_tiles(val[:, ct * LANES:(ct + 1) * LANES])
        asum_ref[:, o * c:(o + 1) * c] += jnp.sum(jnp.abs(fwd) + jnp.abs(bwd), axis=0, keepdims=True)


def hyena_filter_mlp(seq_len, w1, b1, w2, b2, w3, freq, *, tl):
    n_out = w3.shape[1]
    bands = jnp.linspace(1e-4, HYENA_BANDS - 1, HYENA_BANDS, dtype=F32)
    bands2 = jnp.concatenate([bands, bands])[None, :]
    deltas = jnp.abs(jnp.linspace(math.log(HYENA_TARGET) / HYENA_LONG_DECAY_PCT,
                                  math.log(HYENA_TARGET) / HYENA_SHORT_DECAY_PCT, D_HYENA, dtype=F32))[None, :]
    kern = functools.partial(_filter_mlp_kernel, tl=tl, seq_len=seq_len)
    full = lambda a: pl.BlockSpec(a.shape, lambda i: (0,) * a.ndim)
    w3_hi = w3.astype(BF16)
    w3_lo = (w3 - w3_hi.astype(F32)).astype(BF16)
    twice = lambda v: jnp.concatenate([v, v])[None, :]
    blockdiag = lambda m: jnp.concatenate([jnp.pad(m, ((0, 0), (0, m.shape[1]))),
                                           jnp.pad(m, ((0, 0), (m.shape[1], 0)))], axis=0)
    per_half = lambda m: jnp.stack([jnp.pad(m, ((0, m.shape[0]), (0, 0))), jnp.pad(m, ((m.shape[0], 0), (0, 0)))])
    args = [twice(bands2[0]), twice(w1[0]), blockdiag(w1[1:]), twice(b1), blockdiag(w2), twice(b2),
            per_half(w3_hi), per_half(w3_lo), twice(freq), deltas]
    return pl.pallas_call(
        kern,
        grid=(seq_len // tl,),
        in_specs=[full(a) for a in args],
        out_specs=[pl.BlockSpec((n_out // LANES, DFT_TILES, tl // DFT_FAST * SUBLANES, LANES), lambda i: (0, 0, i, 0)),
                   pl.BlockSpec((1, n_out // 2), lambda i: (0, 0))],
        out_shape=[jax.ShapeDtypeStruct((n_out // LANES, DFT_TILES, seq_len // DFT_FAST * SUBLANES, LANES), F32),
                   jax.ShapeDtypeStruct((1, n_out // 2), F32)],
        compiler_params=_cparams("arbitrary"),
        name="hyena_filter_mlp",
    )(*args)


def dft_tables(seq_len):
    n = 2 * seq_len
    nf = DFT_FAST
    ns = n // nf
    k_a = np.arange(ns)[:, None]
    n_s = np.arange(ns // 2)[None, :]
    w_s = np.exp(-2j * np.pi * ((k_a * n_s) % ns) / ns)
    n_f = np.arange(nf)[:, None]
    tw = np.exp(-2j * np.pi * ((n_f * np.arange(ns)[None, :]) % n) / n)
    c32 = lambda z: (jnp.asarray(z.real, F32), jnp.asarray(z.imag, F32))
    ws_re, ws_im = c32(w_s)
    tw_re, tw_im = c32(tw)
    g_re = tw_re[:, :, None] * ws_re[None] - tw_im[:, :, None] * ws_im[None]
    g_im = tw_re[:, :, None] * ws_im[None] + tw_im[:, :, None] * ws_re[None]
    fwd_a = jnp.concatenate([jnp.concatenate([g_re, -g_im], axis=2),
                             jnp.concatenate([g_im, g_re], axis=2)], axis=1)
    gt_re = jnp.swapaxes(g_re, 1, 2) * (1.0 / n)
    gt_im = jnp.swapaxes(g_im, 1, 2) * (1.0 / n)
    inv_a = jnp.concatenate([jnp.concatenate([gt_re, gt_im], axis=2),
                             jnp.concatenate([-gt_im, gt_re], axis=2)], axis=1)
    kk = np.arange(nf)
    w_f = np.exp(-2j * np.pi * ((kk[:, None] * kk[None, :]) % nf) / nf)
    f_re, f_im = c32(w_f)
    fwd_b = jnp.concatenate([jnp.concatenate([f_re, -f_im], axis=1),
                             jnp.concatenate([f_im, f_re], axis=1)], axis=0)
    inv_b = jnp.concatenate([jnp.concatenate([f_re, f_im], axis=1),
                             jnp.concatenate([-f_im, f_re], axis=1)], axis=0)
    kept = min(ns, -(-(ns // 2 + 1) // SUBLANES) * SUBLANES)
    fwd_a_real = jnp.concatenate([fwd_a[:, :kept, :ns // 2], fwd_a[:, ns:ns + kept, :ns // 2]], axis=1)
    m_re, m_im = jnp.roll(f_re, -1, axis=0), jnp.roll(f_im, -1, axis=0)
    mirror_part = jnp.stack([jnp.concatenate([m_re, m_im], axis=1), jnp.concatenate([m_im, -m_re], axis=1)])
    return dict(fwd_a=fwd_a.astype(BF16), fwd_a_real=fwd_a_real.astype(BF16),
                inv_a=inv_a.astype(BF16), fwd_b=fwd_b.astype(BF16), inv_b=inv_b.astype(BF16),
                fwd_b_part=fwd_b.reshape(2, nf, 2 * nf).astype(BF16), mirror_b_part=mirror_part.astype(BF16))


def _filter_dft_kernel(x_ref, ga_ref, fb_ref, fbm_ref, nrm_ref, k_ref, ar_scr, ai_scr):
    nf = DFT_FAST
    kept = ga_ref.shape[1] // 2
    n_in = ga_ref.shape[2]
    ns = 2 * n_in

    def stage_a(q, carry):
        for j in range(SUBLANES):
            x = x_ref[q, pl.ds(j, n_in, stride=SUBLANES), :].astype(BF16)
            n_f = q * SUBLANES + j
            a = _dot(ga_ref[n_f], x)
            ar_scr[pl.ds(n_f, kept, stride=DFT_PITCH), :] = a[:kept]
            ai_scr[pl.ds(n_f, kept, stride=DFT_PITCH), :] = a[kept:]
        return carry

    lax.fori_loop(0, DFT_TILES, stage_a, 0, unroll=DFT_UNROLL // SUBLANES)
    inv_norm = 1.0 / (nrm_ref[...] + 1e-6)

    def slab(start):
        return jnp.concatenate([ar_scr[pl.ds(start, nf), :], ai_scr[pl.ds(start, nf), :]], axis=0).astype(BF16)

    for k in (0, n_in):
        k_ref[k * nf:(k + 1) * nf, :] = _dot(fb_ref[...], slab(k * DFT_PITCH)) * inv_norm

    def stage_b(k, carry):
        a = slab(pl.multiple_of(k * DFT_PITCH, SUBLANES))
        k_ref[pl.ds(pl.multiple_of(k * nf, nf), nf), :] = _dot(fb_ref[...], a) * inv_norm
        k_ref[pl.ds(pl.multiple_of((ns - k) * nf, nf), nf), :] = _dot(fbm_ref[...], a) * inv_norm
        return carry

    pairs = n_in - 1
    lax.fori_loop(1, n_in, stage_b, 0, unroll=max(u for u in (9, 8, 7, 4, 3, 2, 1) if pairs % u == 0))


def hyena_filter_dft(eo, asum, tables):
    col_tiles, q_tiles, rows, _ = eo.shape
    l = q_tiles * rows
    c = D_HYENA
    ct = c // LANES
    orders = col_tiles // (2 * ct)
    ga, fb, fbm = tables['fwd_a_real'], tables['fwd_b_part'], tables['mirror_b_part']
    kept = ga.shape[1] // 2
    scr = pltpu.VMEM((kept * DFT_PITCH, LANES), F32)
    part_spec = pl.BlockSpec((None,) + fb.shape[1:], lambda o, j, part: (part, 0, 0))
    return pl.pallas_call(
        _filter_dft_kernel,
        grid=(orders, ct, 2),
        in_specs=[
            pl.BlockSpec((None,) + eo.shape[1:], lambda o, j, part: ((2 * o + part) * ct + j, 0, 0, 0)),
            pl.BlockSpec(ga.shape, lambda o, j, part: (0, 0, 0)),
            part_spec,
            part_spec,
            pl.BlockSpec((1, LANES), lambda o, j, part: (0, o * ct + j)),
        ],
        out_specs=pl.BlockSpec((None, None, None, 2 * l, LANES), lambda o, j, part: (o, part, j, 0, 0)),
        out_shape=jax.ShapeDtypeStruct((orders, 2, ct, 2 * l, LANES), F32),
        scratch_shapes=[scr, scr],
        compiler_params=_cparams("parallel", "parallel", "parallel"),
        name="hyena_filter_dft",
    )(eo, ga, fb, fbm, asum)


def _lane_cat(parts):
    return parts[0] if len(parts) == 1 else jnp.concatenate(parts, axis=-1)


def _hyena_conv_kernel(x_ref, ga_ref, k_ref, fb_ref, ib_ref, gi_ref, gate_ref, u_ref, bias_ref, o_ref, *scr,
                       n_a, n_b, slabs):
    nf = DFT_FAST
    tiles, qt = x_ref.shape[1], x_ref.shape[2]
    half = x_ref.shape[3] // SUBLANES
    ns = 2 * half
    step = pl.program_id(1)

    @pl.when(step < n_a)
    def _():
        for qj in range(qt * SUBLANES):
            q, j = divmod(qj, SUBLANES)
            rows = pl.ds(j, half, stride=SUBLANES)
            x = _lane_cat([jnp.concatenate([x_ref[0, t, q, rows, :], x_ref[1, t, q, rows, :]], axis=0)
                           for t in range(tiles)])
            a = _dot(ga_ref[qj], x.astype(BF16))
            dst = pl.ds(step * (qt * SUBLANES) + qj, ns, stride=DFT_PITCH)
            for t in range(tiles):
                scr[2 * t][dst, :] = a[:ns, t * LANES:(t + 1) * LANES]
                scr[2 * t + 1][dst, :] = a[ns:, t * LANES:(t + 1) * LANES]

    @pl.when((step >= n_a) & (step < n_a + n_b))
    def _():
        for s in range(slabs):
            k_a = (step - n_a) * slabs + s
            rows = pl.ds(pl.multiple_of(k_a * DFT_PITCH, SUBLANES), nf)
            a = jnp.concatenate([_lane_cat([scr[2 * t + part][rows, :] for t in range(tiles)]) for part in range(2)], axis=0)
            x = _dot(fb_ref[...], a.astype(BF16))
            xr, xi = x[:nf], x[nf:]
            kr = _lane_cat([k_ref[0, t, s * nf:(s + 1) * nf, :] for t in range(tiles)])
            ki = _lane_cat([k_ref[1, t, s * nf:(s + 1) * nf, :] for t in range(tiles)])
            y = jnp.concatenate([xr * kr - xi * ki, xr * ki + xi * kr], axis=0).astype(BF16)
            cc = _dot(ib_ref[...], y)
            for t in range(tiles):
                scr[2 * t][rows, :] = cc[:nf, t * LANES:(t + 1) * LANES]
                scr[2 * t + 1][rows, :] = cc[nf:, t * LANES:(t + 1) * LANES]

    @pl.when(step >= n_a + n_b)
    def _():
        first = (step - (n_a + n_b)) * (qt * SUBLANES)
        for qj in range(qt * SUBLANES):
            q, j = divmod(qj, SUBLANES)
            src = pl.ds(first + qj, ns, stride=DFT_PITCH)
            cc = jnp.concatenate([_lane_cat([scr[2 * t + part][src, :] for t in range(tiles)]) for part in range(2)], axis=0)
            y = _dot(gi_ref[qj], cc.astype(BF16))
            dst = pl.ds(j, half, stride=SUBLANES)
            for t in range(tiles):
                o_ref[0, t, q, dst, :] = y[:half, t * LANES:(t + 1) * LANES]
                o_ref[1, t, q, dst, :] = y[half:, t * LANES:(t + 1) * LANES]
        for t in range(tiles):
            bias = bias_ref[:, t * LANES:(t + 1) * LANES]
            for bi in range(2):
                o_ref[bi, t] = gate_ref[bi, t] * (o_ref[bi, t] + u_ref[bi, t] * bias)


def hyena_long_conv(u6, u_idx, gate6, gate_idx, kf, order, bias, tables, *, slabs=8, tiles=2, qt=2):
    _, _, cts, q_tiles, rows, _ = u6.shape
    ns = 2 * rows // SUBLANES
    n_a = q_tiles // qt
    n_b = ns // slabs
    ga, gi, fb, ib = tables['fwd_a'], tables['inv_a'], tables['fwd_b'], tables['inv_b']
    q_fwd = lambda s: jnp.minimum(s, n_a - 1)
    q_mid = lambda s: jnp.clip(s - n_a, 0, n_b - 1)
    q_inv = lambda s: jnp.clip(s - n_a - n_b, 0, n_a - 1)

    def seq_spec(idx, q_of):
        return pl.BlockSpec((None, 2, tiles, qt, rows, LANES), lambda c, s: (idx, 0, c, q_of(s), 0, 0))

    scr = pltpu.VMEM((ns * DFT_PITCH, LANES), F32)
    kern = functools.partial(_hyena_conv_kernel, n_a=n_a, n_b=n_b, slabs=slabs)
    return pl.pallas_call(
        kern,
        grid=(cts // tiles, 2 * n_a + n_b),
        in_specs=[
            seq_spec(u_idx, q_fwd),
            pl.BlockSpec((qt * SUBLANES,) + ga.shape[1:], lambda c, s: (q_fwd(s), 0, 0)),
            pl.BlockSpec((None, 2, tiles, slabs * DFT_FAST, LANES), lambda c, s: (order, 0, c, q_mid(s), 0)),
            pl.BlockSpec(fb.shape, lambda c, s: (0, 0)),
            pl.BlockSpec(ib.shape, lambda c, s: (0, 0)),
            pl.BlockSpec((qt * SUBLANES,) + gi.shape[1:], lambda c, s: (q_inv(s), 0, 0)),
            seq_spec(gate_idx, q_inv),
            seq_spec(u_idx, q_inv),
            pl.BlockSpec((1, tiles * LANES), lambda c, s: (0, c)),
        ],
        out_specs=pl.BlockSpec((2, tiles, qt, rows, LANES), lambda c, s: (0, c, q_inv(s), 0, 0)),
        out_shape=jax.ShapeDtypeStruct((2, cts, q_tiles, rows, LANES), F32),
        scratch_shapes=[scr] * (2 * tiles),
        compiler_params=_cparams("arbitrary", "arbitrary", vmem=VMEM_LIMIT_CONV),
        name="hyena_long_conv",
    )(u6, ga, kf, fb, ib, gi, gate6, u6, bias.reshape(1, -1))


def hyena_filter_spectra(seq_len, tables, w1, b1, w2, b2, w3, freq):
    eo, asum = hyena_filter_mlp(seq_len, w1, b1, w2, b2, w3, freq, tl=FILTER_ROW_TILE)
    return hyena_filter_dft(eo, asum, tables)


def hyena_mixer(gates, kf, bias, tables):
    z = hyena_long_conv(gates, 2, gates, 1, kf, 0, bias[0], tables)
    return hyena_long_conv(z[None], 0, gates, 0, kf, 1, bias[1], tables)


def _odd_out_xattn_kernel(ys_ref, u_ref, hy_ref, x_ref, d_ref, wglu_ref, wo_ref, g_ref,
                          kv_ref, wq_ref, xwo_ref, g1_ref, g2_ref, o_ref):
    y = ys_ref[0] + d_ref[...] * u_ref[0]
    c0 = math.sqrt(2.0 / math.pi)
    gl = 0.5 * y * (1.0 + jnp.tanh(c0 * (y + 0.044715 * (y * y * y))))
    z = _dot(gl.astype(BF16), wglu_ref[...])
    s5 = gl * (1.0 / (1.0 + jnp.exp(-z)))
    hy = [_from_time_tiles(hy_ref[ct]).astype(BF16) for ct in range(hy_ref.shape[0])]
    mix = _dot(jnp.concatenate([s5.astype(BF16)] + hy, axis=-1), wo_ref[...])
    x = x_ref[0] + _rms(mix, g_ref[...])
    o_ref[0] = _xattn(x, kv_ref, wq_ref, xwo_ref, g1_ref, g2_ref)


def odd_out_xattn(ys, u_s5, hy, x, d_skip, w_glu, w_out, odd_layer, g, kv, wq, xwo, layer, g1, g2, *, tm):
    b, l, d = x.shape
    tok = lambda w: pl.BlockSpec((1, tm, w), lambda bi, i: (bi, i, 0))
    hy_spec = pl.BlockSpec((None, D_HYENA // LANES, DFT_TILES, tm // DFT_FAST * SUBLANES, LANES),
                           lambda bi, i: (bi, 0, 0, i, 0))
    return pl.pallas_call(
        _odd_out_xattn_kernel,
        grid=(b, l // tm),
        in_specs=[tok(D_S5), tok(D_S5), hy_spec, tok(d),
                  pl.BlockSpec((1, D_S5), lambda bi, i: (0, 0)),
                  _layer_spec((D_S5, D_S5), odd_layer, lambda bi, i: (0, 0)),
                  _layer_spec((d, d), odd_layer, lambda bi, i: (0, 0)),
                  pl.BlockSpec((1, d), lambda bi, i: (0, 0))] + _xattn_specs(d, layer),
        out_specs=tok(d),
        out_shape=jax.ShapeDtypeStruct(x.shape, F32),
        compiler_params=_cparams("parallel", "parallel"),
        name="odd_out_xattn",
    )(ys, u_s5, hy, x, d_skip.reshape(1, D_S5), w_glu, w_out, g.reshape(1, d),
      kv, wq, xwo, g1.reshape(1, d), g2.reshape(1, d))


def kernel(x, mem, norm_mix, norm_xattn, norm_mem, norm_mlp, xa_wq, xa_wk, xa_wv, xa_wo, mlp_w1, mlp_w2, ev_w_in, ev_pool_w, ev_pool_scale, ev_conv_w, ev_w_out, od_w_in, od_s5_lambda_re, od_s5_lambda_im, od_s5_log_dt, od_s5_b_re, od_s5_b_im, od_s5_c_re, od_s5_c_im, od_s5_d, od_s5_w_glu, od_hy_short_w, od_hy_short_b, od_hy_w1, od_hy_b1, od_hy_w2, od_hy_b2, od_hy_w3, od_hy_freq, od_hy_bias, od_w_out):
    b, l, d = x.shape
    depth = norm_mix.shape[0]
    assert b == 2, "the long convolution packs the two batch rows as one complex signal"
    tables = dft_tables(l)
    mem2d = mem.reshape(b * N_MEM, d)
    wq, wo = xa_wq.astype(BF16), xa_wo.astype(BF16)
    wkv = jnp.concatenate([xa_wk, xa_wv], axis=2).astype(BF16)
    kv = memory_kv(mem2d, norm_mem, wkv).reshape(depth, b, N_MEM, 2 * d)
    w1, w2 = mlp_w1.astype(BF16), mlp_w2.astype(BF16)
    ev_in, ev_out, ev_pool = ev_w_in.astype(BF16), ev_w_out.astype(BF16), ev_pool_w.astype(BF16)
    od_in, od_out, od_glu = od_w_in.astype(BF16), od_w_out.astype(BF16), od_s5_w_glu.astype(BF16)
    for i in range(depth):
        j = i // 2
        if i % 2 == 0:
            x = even_mixer_xattn(x, norm_mix[i, 0], ev_in, ev_pool, ev_pool_scale[j], ev_conv_w[j], ev_out, j,
                                 norm_mix[i, 1], kv, wq, wo, i, norm_xattn[i, 0], norm_xattn[i, 1], tm=ROW_TILE)
        else:
            u_s5, gates = odd_in(x, norm_mix[i, 0], od_in, j, od_hy_short_w[j], od_hy_short_b[j], tm=ROW_TILE)
            mats = s5_matrices(od_s5_lambda_re[j], od_s5_lambda_im[j], od_s5_log_dt[j], od_s5_b_re[j],
                               od_s5_b_im[j], od_s5_c_re[j], od_s5_c_im[j])
            ys = s5_core(u_s5, mats)
            kf = hyena_filter_spectra(l, tables, od_hy_w1[j], od_hy_b1[j], od_hy_w2[j], od_hy_b2[j],
                                      od_hy_w3[j], od_hy_freq[j])
            hy = hyena_mixer(gates, kf, od_hy_bias[j], tables)
            x = odd_out_xattn(ys, u_s5, hy, x, od_s5_d[j], od_glu, od_out, j, norm_mix[i, 1],
                              kv, wq, wo, i, norm_xattn[i, 0], norm_xattn[i, 1], tm=ROW_TILE)
        x = mlp_block(x.reshape(b * l, d), w1, w2, i, norm_mlp[i, 0], norm_mlp[i, 1],
                      tm=ROW_TILE, tf=MLP_HIDDEN_TILE).reshape(b, l, d)
    return x
```

```python
import functools
import math

import numpy as np
import jax
import jax.numpy as jnp
from jax import lax
from jax.experimental import pallas as pl
from jax.experimental.pallas import tpu as pltpu

F32 = jnp.float32
BF16 = jnp.bfloat16

D_MODEL = 1024
N_MEM = 256
RMS_EPS = 1e-6
D_POOL = 512
POOL_WINDOWS = (2, 4, 8, 16)
POOL_GROUP_DIM = 128
POOL_HALO = 8
D_CONV = 512
D_S5 = 512
S5_GROUP_DIM = 16
S5_GROUPS = 32
S5_STATE = 64
S5_CHUNK = 16
D_HYENA = 512
HYENA_BANDS = 16
HYENA_FFN = 64
HYENA_TARGET = 1e-2
HYENA_SHORT_DECAY_PCT = 0.3
HYENA_LONG_DECAY_PCT = 1.5
XA_HEADS = 4
XA_HEAD_DIM = 256
D_FF = 4096

LANES = 128
SUBLANES = 8
S5_SUPER = LANES // S5_GROUP_DIM
DFT_FAST = 128
DFT_TILES = DFT_FAST // SUBLANES
DFT_PITCH = DFT_FAST + SUBLANES
DFT_UNROLL = 16
ROW_TILE = 1024
MLP_HIDDEN_TILE = 512
FILTER_ROW_TILE = 512
VMEM_LIMIT = 56 * 1024 * 1024
VMEM_LIMIT_CONV = 62 * 1024 * 1024


def _cparams(*sem, vmem=VMEM_LIMIT):
    return pltpu.CompilerParams(dimension_semantics=sem, vmem_limit_bytes=vmem)


def _rms(xf, g):
    ms = jnp.mean(xf * xf, axis=-1, keepdims=True)
    return xf * lax.rsqrt(ms + RMS_EPS) * g


def _dot(a, b):
    return jnp.dot(a, b, preferred_element_type=F32)


def _layer_spec(block, layer, tail_map):
    return pl.BlockSpec((None,) + block, lambda *idx: (layer,) + tail_map(*idx))


def _norm_matmul_kernel(x_ref, g_ref, w_ref, o_ref):
    xn = _rms(x_ref[...], g_ref[...]).astype(BF16)
    o_ref[...] = _dot(xn, w_ref[...]).astype(o_ref.dtype)


def memory_kv(mem2d, g, w):
    layers, d, n = w.shape
    m = mem2d.shape[0]
    return pl.pallas_call(
        _norm_matmul_kernel,
        grid=(layers,),
        in_specs=[
            pl.BlockSpec((m, d), lambda i: (0, 0)),
            pl.BlockSpec((None, 1, d), lambda i: (i, 0, 0)),
            pl.BlockSpec((None, d, n), lambda i: (i, 0, 0)),
        ],
        out_specs=pl.BlockSpec((None, m, n), lambda i: (i, 0, 0)),
        out_shape=jax.ShapeDtypeStruct((layers, m, n), BF16),
        compiler_params=_cparams("parallel"),
        name="memory_kv",
    )(mem2d, g.reshape(layers, 1, d), w)


def _xattn(x, kv_ref, wq_ref, wo_ref, g1_ref, g2_ref):
    xn = _rms(x, g1_ref[...]).astype(BF16)
    q = (_dot(xn, wq_ref[...]) * (XA_HEAD_DIM ** -0.5)).astype(BF16)
    heads = []
    for h in range(XA_HEADS):
        lo = h * XA_HEAD_DIM
        qh = q[:, lo:lo + XA_HEAD_DIM]
        kh = kv_ref[0, :, lo:lo + XA_HEAD_DIM]
        vh = kv_ref[0, :, D_MODEL + lo:D_MODEL + lo + XA_HEAD_DIM]
        s = lax.dot_general(qh, kh, (((1,), (1,)), ((), ())), preferred_element_type=F32)
        e = jnp.exp(s - jnp.max(s, axis=-1, keepdims=True))
        p = e / jnp.sum(e, axis=-1, keepdims=True)
        heads.append(_dot(p.astype(BF16), vh).astype(BF16))
    o = jnp.concatenate(heads, axis=-1)
    y = _dot(o, wo_ref[...])
    return x + _rms(y, g2_ref[...])


def _xattn_kernel(x_ref, kv_ref, wq_ref, wo_ref, g1_ref, g2_ref, o_ref):
    o_ref[0] = _xattn(x_ref[0], kv_ref, wq_ref, wo_ref, g1_ref, g2_ref)


def _xattn_specs(d, layer):
    return [
        pl.BlockSpec((None, 1, N_MEM, 2 * d), lambda bi, i: (layer, bi, 0, 0)),
        _layer_spec((d, d), layer, lambda bi, i: (0, 0)),
        _layer_spec((d, d), layer, lambda bi, i: (0, 0)),
        pl.BlockSpec((1, d), lambda bi, i: (0, 0)),
        pl.BlockSpec((1, d), lambda bi, i: (0, 0)),
    ]


def xattn_block(x, kv, wq, wo, layer, g1, g2, *, tm):
    b, l, d = x.shape
    return pl.pallas_call(
        _xattn_kernel,
        grid=(b, l // tm),
        in_specs=[pl.BlockSpec((1, tm, d), lambda bi, i: (bi, i, 0))] + _xattn_specs(d, layer),
        out_specs=pl.BlockSpec((1, tm, d), lambda bi, i: (bi, i, 0)),
        out_shape=jax.ShapeDtypeStruct(x.shape, F32),
        compiler_params=_cparams("parallel", "parallel"),
        name="xattn_block",
    )(x, kv, wq, wo, g1.reshape(1, d), g2.reshape(1, d))


def _mlp_kernel(x_ref, w1_ref, w2_ref, g1_ref, g2_ref, o_ref, h_ref, *, tf):
    x = x_ref[...]
    xn = _rms(x, g1_ref[...]).astype(BF16)
    for c in range(h_ref.shape[1] // tf):
        cols = slice(c * tf, (c + 1) * tf)
        h = jnp.maximum(_dot(xn, w1_ref[:, cols]), 0.0)
        h_ref[:, cols] = (h * h).astype(BF16)
    o_ref[...] = x + _rms(_dot(h_ref[...], w2_ref[...]), g2_ref[...])


def mlp_block(x2d, w1, w2, layer, g1, g2, *, tm, tf):
    m, d = x2d.shape
    ff = w1.shape[-1]
    resident = dict(pipeline_mode=pl.Buffered(1))
    return pl.pallas_call(
        functools.partial(_mlp_kernel, tf=tf),
        grid=(m // tm,),
        in_specs=[
            pl.BlockSpec((tm, d), lambda i: (i, 0)),
            pl.BlockSpec((None, d, ff), lambda i: (layer, 0, 0), **resident),
            pl.BlockSpec((None, ff, d), lambda i: (layer, 0, 0), **resident),
            pl.BlockSpec((1, d), lambda i: (0, 0)),
            pl.BlockSpec((1, d), lambda i: (0, 0)),
        ],
        out_specs=pl.BlockSpec((tm, d), lambda i: (i, 0)),
        out_shape=jax.ShapeDtypeStruct((m, d), F32),
        scratch_shapes=[pltpu.VMEM((tm, ff), BF16)],
        compiler_params=_cparams("parallel"),
        name="mlp_block",
    )(x2d, w1, w2, g1.reshape(1, d), g2.reshape(1, d))


def _halo_specs(tm, seq_len, width, col):
    r = tm // POOL_HALO
    last = seq_len // POOL_HALO - 1

    def prev_map(bi, i, *_):
        return (bi, jnp.maximum(i * r - 1, 0), col(*_))

    def main_map(bi, i, *_):
        return (bi, i, col(*_))

    def next_map(bi, i, *_):
        return (bi, jnp.minimum((i + 1) * r, last), col(*_))

    return [
        pl.BlockSpec((1, POOL_HALO, width), prev_map),
        pl.BlockSpec((1, tm, width), main_map),
        pl.BlockSpec((1, POOL_HALO, width), next_map),
    ]


def _with_halo(prev_ref, main_ref, next_ref):
    i = pl.program_id(1)
    prev = jnp.where(i > 0, prev_ref[0], 0.0)
    nxt = jnp.where(i < pl.num_programs(1) - 1, next_ref[0], 0.0)
    return jnp.concatenate([prev, main_ref[0], nxt], axis=0)


def _shift_rows(v, k):
    return pltpu.roll(v, k % v.shape[0], 0)


def _even_mixer_kernel(xp_ref, x_ref, xn_ref, gi_ref, wi_ref, wg_ref, ps_ref, cw_ref, wo_ref, g_ref, o_ref,
                       *, tm, seq_len):
    x_ext = _with_halo(xp_ref, x_ref, xn_ref)
    ext = _dot(_rms(x_ext, gi_ref[...]).astype(BF16), wi_ref[...])
    lo, hi = POOL_HALO, POOL_HALO + tm
    t = pl.program_id(1) * tm + lax.broadcasted_iota(jnp.int32, (tm, 1), 0)
    parts = []
    for gi, win in enumerate(POOL_WINDOWS):
        half = win // 2
        u = ext[:, gi * POOL_GROUP_DIM:(gi + 1) * POOL_GROUP_DIM]
        s = u + _shift_rows(u, 1)
        step = 1
        while 2 * step < win:
            s = _shift_rows(s, step) + _shift_rows(s, -step)
            step *= 2
        cnt = (jnp.minimum(t + half, seq_len) - jnp.maximum(t - half, 0)).astype(F32)
        pooled = s[lo:hi] / cnt - u[lo:hi]
        y = _dot(pooled.astype(BF16), wg_ref[gi])
        parts.append((y * ps_ref[:, gi * POOL_GROUP_DIM:(gi + 1) * POOL_GROUP_DIM]).astype(BF16))
    b_gate = ext[lo:hi, D_POOL:D_POOL + D_CONV]
    ch = ext[:, D_POOL + D_CONV:D_POOL + 2 * D_CONV] * ext[:, D_POOL + 2 * D_CONV:D_POOL + 3 * D_CONV]
    conv = cw_ref[0:1, :] * _shift_rows(ch, 1) + cw_ref[1:2, :] * ch + cw_ref[2:3, :] * _shift_rows(ch, -1)
    parts.append((b_gate * conv[lo:hi]).astype(BF16))
    mix = _dot(jnp.concatenate(parts, axis=-1), wo_ref[...])
    o_ref[0] = x_ref[0] + _rms(mix, g_ref[...])


def even_mixer(x, g_in, w_in, w_group, pool_scale, conv_w, w_out, layer, g, *, tm):
    b, l, d = x.shape
    kern = functools.partial(_even_mixer_kernel, tm=tm, seq_len=l)
    return pl.pallas_call(
        kern,
        grid=(b, l // tm),
        in_specs=_halo_specs(tm, l, d, lambda: 0) + [
            pl.BlockSpec((1, d), lambda bi, i: (0, 0)),
            _layer_spec(w_in.shape[1:], layer, lambda bi, i: (0, 0)),
            _layer_spec(w_group.shape[1:], layer, lambda bi, i: (0, 0, 0)),
            pl.BlockSpec((1, D_POOL), lambda bi, i: (0, 0)),
            pl.BlockSpec((3, D_CONV), lambda bi, i: (0, 0)),
            _layer_spec((d, d), layer, lambda bi, i: (0, 0)),
            pl.BlockSpec((1, d), lambda bi, i: (0, 0)),
        ],
        out_specs=pl.BlockSpec((1, tm, d), lambda bi, i: (bi, i, 0)),
        out_shape=jax.ShapeDtypeStruct(x.shape, F32),
        compiler_params=_cparams("parallel", "parallel"),
        name="even_mixer",
    )(x, x, x, g_in.reshape(1, d), w_in, w_group, pool_scale.reshape(1, D_POOL), conv_w, w_out, g.reshape(1, d))


def _cmul(ar, ai, br, bi):
    return ar * br - ai * bi, ar * bi + ai * br


def _s5_toeplitz_kernel(m_ref, o_ref):
    tc, hd = S5_CHUNK, S5_GROUP_DIM
    for g in range(m_ref.shape[0]):
        m = m_ref[g]
        rows = [m[:, (tc - 1 - s) * hd:(2 * tc - 1 - s) * hd] for s in range(tc)]
        o_ref[g] = jnp.concatenate(rows, axis=0).astype(o_ref.dtype)


def s5_toeplitz(lag_table):
    g, hd, w = lag_table.shape
    k = w // 2
    return pl.pallas_call(
        _s5_toeplitz_kernel,
        grid=(g // S5_SUPER,),
        in_specs=[pl.BlockSpec((S5_SUPER, hd, w), lambda i: (i, 0, 0))],
        out_specs=pl.BlockSpec((S5_SUPER, k, k), lambda i: (i, 0, 0)),
        out_shape=jax.ShapeDtypeStruct((g, k, k), BF16),
        compiler_params=_cparams("parallel"),
        name="s5_toeplitz",
    )(lag_table)


def s5_matrices(lam_re, lam_im, log_dt, b_re, b_im, c_re, c_im):
    tc, hd, n = S5_CHUNK, S5_GROUP_DIM, S5_STATE
    hp = lax.Precision.HIGHEST
    lr = jnp.minimum(lam_re, -1e-4)
    li = lam_im
    dt = jnp.exp(log_dt)[..., None]
    taus = jnp.arange(tc + 1, dtype=F32)[:, None, None, None]
    rmag = jnp.exp(lr[None] * dt[None] * (tc - taus))
    rang = li[None] * dt[None] * (tc - taus)
    pwr_re, pwr_im = rmag * jnp.cos(rang), rmag * jnp.sin(rang)
    mag = jnp.exp(lr[None] * dt[None] * taus)
    ang = li[None] * dt[None] * taus
    pw_re, pw_im = mag * jnp.cos(ang), mag * jnp.sin(ang)
    nr, ni = pw_re[1] - 1.0, pw_im[1]
    den = lr * lr + li * li
    coef_re, coef_im = (nr * lr + ni * li) / den, (ni * lr - nr * li) / den
    cb_re, cb_im = _cmul(coef_re[..., None], coef_im[..., None], b_re[None], b_im[None])
    cbt_re, cbt_im = _cmul(coef_re[:, :, None, :], coef_im[:, :, None, :],
                           jnp.swapaxes(b_re, 1, 2)[None], jnp.swapaxes(b_im, 1, 2)[None])
    def lag_kernels(d, p_re, p_im):
        q_re, q_im = _cmul(p_re[..., None], p_im[..., None], cb_re[d][None], cb_im[d][None])
        c_cat = jnp.concatenate([c_re[d], -c_im[d]], axis=-1)
        return jnp.einsum('gkn,tgnj->gjtk', c_cat, jnp.concatenate([q_re, q_im], axis=2), precision=hp)

    m_fwd = lag_kernels(0, pw_re[:tc, 0], pw_im[:tc, 0])
    m_bwd = lag_kernels(1, pwr_re[1:tc + 1, 1], pwr_im[1:tc + 1, 1])
    lag_table = jnp.concatenate([m_bwd[:, :, :tc - 1], m_bwd[:, :, tc - 1:] + m_fwd[:, :, :1], m_fwd[:, :, 1:],
                                 jnp.zeros_like(m_fwd[:, :, :1])], axis=2)
    intra = s5_toeplitz(lag_table.reshape(S5_GROUPS, hd, 2 * tc * hd))
    sf_re, sf_im = _cmul(pwr_re[1:tc + 1, 0][:, :, None, :], pwr_im[1:tc + 1, 0][:, :, None, :],
                         cbt_re[0][None], cbt_im[0][None])
    sb_re, sb_im = _cmul(pw_re[:tc, 1][:, :, None, :], pw_im[:tc, 1][:, :, None, :],
                         cbt_re[1][None], cbt_im[1][None])
    summ = jnp.concatenate([sf_re, sb_re, sf_im, sb_im], axis=-1)
    summ = jnp.swapaxes(summ, 0, 1).reshape(S5_GROUPS, tc * hd, 4 * n)
    gtn = lambda z: jnp.swapaxes(z, 0, 1)[:, :, None, :]
    qf_re, qf_im = _cmul(c_re[0][:, None], c_im[0][:, None], gtn(pw_re[1:tc + 1, 0]), gtn(pw_im[1:tc + 1, 0]))
    qb_re, qb_im = _cmul(c_re[1][:, None], c_im[1][:, None], gtn(pwr_re[:tc, 1]), gtn(pwr_im[:tc, 1]))
    carry = jnp.concatenate([qf_re, qb_re, -qf_im, -qb_im], axis=-1)
    carry = carry.reshape(S5_GROUPS, tc * hd, 4 * n)
    a_re = jnp.concatenate([pw_re[tc, 0], pw_re[tc, 1]], axis=-1)
    a_im = jnp.concatenate([pw_im[tc, 0], pw_im[tc, 1]], axis=-1)
    return intra, summ, carry, a_re, a_im


def s5_regroup_matrix():
    n = S5_SUPER * LANES
    src = np.arange(n)
    tb, g, h = src // LANES, (src % LANES) // S5_GROUP_DIM, src % S5_GROUP_DIM
    perm = np.zeros((n, n), np.float32)
    perm[src, g * LANES + tb * S5_GROUP_DIM + h] = 1.0
    return jnp.asarray(perm, BF16)


def _s5_regroup_kernel(p_ref, w_ref, perm_ref, u_ref, sr_ref, si_ref, *, n_chunks):
    n2 = 2 * S5_STATE
    for q in range(S5_CHUNK // S5_SUPER):
        rows_t = [p_ref[pl.ds(q * S5_SUPER + tb, n_chunks, stride=S5_CHUNK), :].astype(BF16) for tb in range(S5_SUPER)]
        grouped = _dot(jnp.concatenate(rows_t, axis=-1), perm_ref[...]).astype(BF16)
        for g in range(S5_SUPER):
            u_ref[g, :, q * LANES:(q + 1) * LANES] = grouped[:, g * LANES:(g + 1) * LANES]
    for g in range(S5_SUPER):
        s = _dot(u_ref[g], w_ref[g])
        sr_ref[pl.ds(g, n_chunks, stride=S5_SUPER), :] = s[:, :n2]
        si_ref[pl.ds(g, n_chunks, stride=S5_SUPER), :] = s[:, n2:]


def _s5_state_spec(c, n2, n_tiles):
    return pl.BlockSpec((None, c * S5_SUPER, n2), lambda a, bi: (bi * n_tiles + a, 0, 0))


def _s5_group_spec(rows, cols):
    return pl.BlockSpec((S5_SUPER, rows, cols), lambda a, bi: (a, 0, 0))


def s5_summary(p, summ):
    b, l, _ = p.shape
    c = l // S5_CHUNK
    n_tiles = S5_GROUPS // S5_SUPER
    k = S5_CHUNK * S5_GROUP_DIM
    n2 = 2 * S5_STATE
    shape = jax.ShapeDtypeStruct((b * n_tiles, c * S5_SUPER, n2), F32)
    kern = functools.partial(_s5_regroup_kernel, n_chunks=c)
    perm = s5_regroup_matrix()
    return pl.pallas_call(
        kern,
        grid=(n_tiles, b),
        in_specs=[
            pl.BlockSpec((None, l, LANES), lambda a, bi: (bi, 0, a)),
            _s5_group_spec(k, 2 * n2),
            pl.BlockSpec(perm.shape, lambda a, bi: (0, 0)),
        ],
        out_specs=[pl.BlockSpec((None, S5_SUPER, c, k), lambda a, bi: (bi * n_tiles + a, 0, 0, 0)),
                   _s5_state_spec(c, n2, n_tiles), _s5_state_spec(c, n2, n_tiles)],
        out_shape=[jax.ShapeDtypeStruct((b * n_tiles, S5_SUPER, c, k), BF16), shape, shape],
        compiler_params=_cparams("parallel", "parallel"),
        name="s5_summary",
    )(p, summ, perm)


def _s5_scan(sr_ref, si_ref, are_ref, aim_ref, or_ref, oi_ref, n_chunks):
    n2 = 2 * S5_STATE
    a_re = are_ref[...]
    a_im = aim_ref[...]
    fwd_lane = lax.broadcasted_iota(jnp.int32, (SUBLANES, n2), 1) < S5_STATE

    def body(i, state):
        x_re, x_im = state
        rows_i = pl.ds(pl.multiple_of(i * SUBLANES, SUBLANES), SUBLANES)
        rows_r = pl.ds(pl.multiple_of((n_chunks - 1 - i) * SUBLANES, SUBLANES), SUBLANES)
        or_ref[rows_i, 0:S5_STATE] = x_re[:, :S5_STATE]
        oi_ref[rows_i, 0:S5_STATE] = x_im[:, :S5_STATE]
        or_ref[rows_r, S5_STATE:n2] = x_re[:, S5_STATE:]
        oi_ref[rows_r, S5_STATE:n2] = x_im[:, S5_STATE:]
        s_re = jnp.where(fwd_lane, sr_ref[rows_i, :], sr_ref[rows_r, :])
        s_im = jnp.where(fwd_lane, si_ref[rows_i, :], si_ref[rows_r, :])
        return (a_re * x_re - a_im * x_im + s_re, a_re * x_im + a_im * x_re + s_im)

    zero = jnp.zeros((SUBLANES, n2), F32)
    lax.fori_loop(0, n_chunks, body, (zero, zero))


def _s5_output_kernel(u_ref, sr_ref, si_ref, are_ref, aim_ref, wi_ref, wc_ref, perm_ref, o_ref,
                      y_ref, xr_ref, xi_ref, *, n_chunks):
    _s5_scan(sr_ref, si_ref, are_ref, aim_ref, xr_ref, xi_ref, n_chunks)
    back = (((1,), (1,)), ((), ()))
    for g in range(S5_SUPER):
        rows = pl.ds(g, n_chunks, stride=S5_SUPER)
        xin = jnp.concatenate([xr_ref[rows, :], xi_ref[rows, :]], axis=-1).astype(BF16)
        y_ref[g] = _dot(u_ref[g], wi_ref[g]) + lax.dot_general(xin, wc_ref[g], back, preferred_element_type=F32)
    for q in range(S5_CHUNK // S5_SUPER):
        y = jnp.concatenate([y_ref[g, :, q * LANES:(q + 1) * LANES] for g in range(S5_SUPER)], axis=-1)
        hi = y.astype(BF16)
        lo = (y - hi.astype(F32)).astype(BF16)
        tok = (lax.dot_general(hi, perm_ref[...], back, preferred_element_type=F32)
               + lax.dot_general(lo, perm_ref[...], back, preferred_element_type=F32))
        for tb in range(S5_SUPER):
            o_ref[pl.ds(q * S5_SUPER + tb, n_chunks, stride=S5_CHUNK), :] = tok[:, tb * LANES:(tb + 1) * LANES]


def s5_output(u, s_re, s_im, a_re, a_im, intra, carry, *, batch):
    tiles, _, c, k = u.shape
    n_tiles = tiles // batch
    n2 = 2 * S5_STATE
    l = c * S5_CHUNK
    kern = functools.partial(_s5_output_kernel, n_chunks=c)
    perm = s5_regroup_matrix()
    decay_spec = pl.BlockSpec((S5_SUPER, n2), lambda a, bi: (a, 0))
    state_scr = pltpu.VMEM((c * S5_SUPER, n2), F32)
    return pl.pallas_call(
        kern,
        grid=(n_tiles, batch),
        in_specs=[
            pl.BlockSpec((None, S5_SUPER, c, k), lambda a, bi: (bi * n_tiles + a, 0, 0, 0)),
            _s5_state_spec(c, n2, n_tiles),
            _s5_state_spec(c, n2, n_tiles),
            decay_spec,
            decay_spec,
            _s5_group_spec(k, k),
            _s5_group_spec(k, 2 * n2),
            pl.BlockSpec(perm.shape, lambda a, bi: (0, 0)),
        ],
        out_specs=pl.BlockSpec((None, l, LANES), lambda a, bi: (bi, 0, a)),
        out_shape=jax.ShapeDtypeStruct((batch, l, D_S5), F32),
        scratch_shapes=[pltpu.VMEM((S5_SUPER, c, k), F32), state_scr, state_scr],
        compiler_params=_cparams("parallel", "parallel"),
        name="s5_output",
    )(u, s_re, s_im, a_re, a_im, intra, carry, perm)


def s5_core(p, mats):
    intra, summ, carry, a_re, a_im = mats
    u, s_re, s_im = s5_summary(p, summ.astype(BF16))
    return s5_output(u, s_re, s_im, a_re, a_im, intra.astype(BF16), carry.astype(BF16), batch=p.shape[0])


def _to_time_tiles(y):
    nf, c = DFT_FAST, y.shape[-1]
    slabs = [y[s * nf:(s + 1) * nf].reshape(DFT_TILES, SUBLANES, c) for s in range(y.shape[0] // nf)]
    return jnp.concatenate(slabs, axis=1)


def _from_time_tiles(v):
    c = v.shape[-1]
    slabs = [v[:, s * SUBLANES:(s + 1) * SUBLANES, :].reshape(DFT_FAST, c) for s in range(v.shape[1] // SUBLANES)]
    return jnp.concatenate(slabs, axis=0)


def _odd_in_kernel(xp_ref, x_ref, xn_ref, g_ref, w_ref, cw_ref, cb_ref, s5_ref, gates_ref, *, tm):
    ext = _with_halo(xp_ref, x_ref, xn_ref)
    p = _dot(_rms(ext, g_ref[...]).astype(BF16), w_ref[...])
    lo, hi = POOL_HALO, POOL_HALO + tm
    s5_ref[0] = p[lo:hi, :D_S5]
    ph = p[:, D_S5:]
    y = cw_ref[0:1, :] * _shift_rows(ph, 1) + cw_ref[1:2, :] * ph + cw_ref[2:3, :] * _shift_rows(ph, -1)
    y = y[lo:hi] + cb_ref[...]
    for k in range(3):
        for ct in range(D_HYENA // LANES):
            col = k * D_HYENA + ct * LANES
            gates_ref[k, ct] = _to_time_tiles(y[:, col:col + LANES])


def odd_in(x, g, w, layer, conv_w, conv_b, *, tm):
    b, l, d = x.shape
    n = w.shape[-1]
    rows = tm // DFT_FAST * SUBLANES
    cts = D_HYENA // LANES
    kern = functools.partial(_odd_in_kernel, tm=tm)
    return pl.pallas_call(
        kern,
        grid=(b, l // tm),
        in_specs=_halo_specs(tm, l, d, lambda: 0) + [
            pl.BlockSpec((1, d), lambda bi, i: (0, 0)),
            _layer_spec((d, n), layer, lambda bi, i: (0, 0)),
            pl.BlockSpec((3, n - D_S5), lambda bi, i: (0, 0)),
            pl.BlockSpec((1, n - D_S5), lambda bi, i: (0, 0)),
        ],
        out_specs=[pl.BlockSpec((1, tm, D_S5), lambda bi, i: (bi, i, 0)),
                   pl.BlockSpec((3, None, cts, DFT_TILES, rows, LANES), lambda bi, i: (0, bi, 0, 0, i, 0))],
        out_shape=[jax.ShapeDtypeStruct((b, l, D_S5), F32),
                   jax.ShapeDtypeStruct((3, b, cts, DFT_TILES, l // DFT_FAST * SUBLANES, LANES), F32)],
        compiler_params=_cparams("parallel", "parallel"),
        name="odd_in",
    )(x, x, x, g.reshape(1, d), w, conv_w, conv_b.reshape(1, -1))


def _filter_mlp_kernel(bands_ref, w1t_ref, w1cs_ref, b1_ref, w2_ref, b2_ref, w3h_ref, w3l_ref, fr_ref, dl_ref,
                       eo_ref, asum_ref, *, tl, seq_len):
    i = pl.program_id(0)
    hp = lax.Precision.HIGHEST
    c = D_HYENA
    half = tl // 2
    first = i * tl + lax.broadcasted_iota(jnp.int32, (half, 1), 0)

    def positions(width):
        right = lax.broadcasted_iota(jnp.int32, (1, width), 1) >= width // 2
        return (first + jnp.where(right, half, 0)).astype(F32)

    feat = bands_ref.shape[1]
    ang = (2.0 * math.pi / seq_len) * positions(feat) * bands_ref[...]
    is_cos = lax.broadcasted_iota(jnp.int32, (1, feat), 1) % (2 * HYENA_BANDS) < HYENA_BANDS
    cs = jnp.sin(ang + jnp.where(is_cos, 0.5 * math.pi, math.pi))
    fr = fr_ref[...]
    z = (positions(w1t_ref.shape[1]) / (seq_len - 1.0)) * w1t_ref[...] + jnp.dot(
        cs, w1cs_ref[...], precision=hp, preferred_element_type=F32)
    h = jnp.sin(fr * (z + b1_ref[...]))
    h = jnp.sin(fr * (jnp.dot(h, w2_ref[...], precision=hp, preferred_element_type=F32) + b2_ref[...]))
    h_hi = h.astype(BF16)
    h_lo = (h - h_hi.astype(F32)).astype(BF16)
    h = jnp.concatenate([_dot(h_hi, w3h_ref[r]) + (_dot(h_lo, w3h_ref[r]) + _dot(h_hi, w3l_ref[r])) for r in range(2)],
                        axis=0)
    t_norm = (i * tl + lax.broadcasted_iota(jnp.int32, (tl, 1), 0)).astype(F32) / (seq_len - 1.0)
    decay = jnp.exp(-t_norm * dl_ref[...])

    @pl.when(i == 0)
    def _():
        asum_ref[...] = jnp.zeros_like(asum_ref)

    for o in range(h.shape[1] // (2 * c)):
        fwd = h[:, 2 * o * c:(2 * o + 1) * c] * decay
        bwd = h[:, (2 * o + 1) * c:(2 * o + 2) * c] * decay
        for part, val in enumerate((fwd + bwd, fwd - bwd)):
            for ct in range(c // LANES):
                eo_ref[((2 * o + part) * c) // LANES + ct] = _to_time_tiles(val[:, ct * LANES:(ct + 1) * LANES])
        asum_ref[:, o * c:(o + 1) * c] += jnp.sum(jnp.abs(fwd) + jnp.abs(bwd), axis=0, keepdims=True)


def hyena_filter_mlp(seq_len, w1, b1, w2, b2, w3, freq, *, tl):
    n_out = w3.shape[1]
    bands = jnp.linspace(1e-4, HYENA_BANDS - 1, HYENA_BANDS, dtype=F32)
    bands2 = jnp.concatenate([bands, bands])[None, :]
    deltas = jnp.abs(jnp.linspace(math.log(HYENA_TARGET) / HYENA_LONG_DECAY_PCT,
                                  math.log(HYENA_TARGET) / HYENA_SHORT_DECAY_PCT, D_HYENA, dtype=F32))[None, :]
    kern = functools.partial(_filter_mlp_kernel, tl=tl, seq_len=seq_len)
    full = lambda a: pl.BlockSpec(a.shape, lambda i: (0,) * a.ndim)
    w3_hi = w3.astype(BF16)
    w3_lo = (w3 - w3_hi.astype(F32)).astype(BF16)
    twice = lambda v: jnp.concatenate([v, v])[None, :]
    blockdiag = lambda m: jnp.concatenate([jnp.pad(m, ((0, 0), (0, m.shape[1]))),
                                           jnp.pad(m, ((0, 0), (m.shape[1], 0)))], axis=0)
    per_half = lambda m: jnp.stack([jnp.pad(m, ((0, m.shape[0]), (0, 0))), jnp.pad(m, ((m.shape[0], 0), (0, 0)))])
    args = [twice(bands2[0]), twice(w1[0]), blockdiag(w1[1:]), twice(b1), blockdiag(w2), twice(b2),
            per_half(w3_hi), per_half(w3_lo), twice(freq), deltas]
    return pl.pallas_call(
        kern,
        grid=(seq_len // tl,),
        in_specs=[full(a) for a in args],
        out_specs=[pl.BlockSpec((n_out // LANES, DFT_TILES, tl // DFT_FAST * SUBLANES, LANES), lambda i: (0, 0, i, 0)),
                   pl.BlockSpec((1, n_out // 2), lambda i: (0, 0))],
        out_shape=[jax.ShapeDtypeStruct((n_out // LANES, DFT_TILES, seq_len // DFT_FAST * SUBLANES, LANES), F32),
                   jax.ShapeDtypeStruct((1, n_out // 2), F32)],
        compiler_params=_cparams("arbitrary"),
        name="hyena_filter_mlp",
    )(*args)


def dft_tables(seq_len):
    n = 2 * seq_len
    nf = DFT_FAST
    ns = n // nf
    k_a = np.arange(ns)[:, None]
    n_s = np.arange(ns // 2)[None, :]
    w_s = np.exp(-2j * np.pi * ((k_a * n_s) % ns) / ns)
    n_f = np.arange(nf)[:, None]
    tw = np.exp(-2j * np.pi * ((n_f * np.arange(ns)[None, :]) % n) / n)
    c32 = lambda z: (jnp.asarray(z.real, F32), jnp.asarray(z.imag, F32))
    ws_re, ws_im = c32(w_s)
    tw_re, tw_im = c32(tw)
    g_re = tw_re[:, :, None] * ws_re[None] - tw_im[:, :, None] * ws_im[None]
    g_im = tw_re[:, :, None] * ws_im[None] + tw_im[:, :, None] * ws_re[None]
    fwd_a = jnp.concatenate([jnp.concatenate([g_re, -g_im], axis=2),
                             jnp.concatenate([g_im, g_re], axis=2)], axis=1)
    gt_re = jnp.swapaxes(g_re, 1, 2) * (1.0 / n)
    gt_im = jnp.swapaxes(g_im, 1, 2) * (1.0 / n)
    inv_a = jnp.concatenate([jnp.concatenate([gt_re, gt_im], axis=2),
                             jnp.concatenate([-gt_im, gt_re], axis=2)], axis=1)
    kk = np.arange(nf)
    w_f = np.exp(-2j * np.pi * ((kk[:, None] * kk[None, :]) % nf) / nf)
    f_re, f_im = c32(w_f)
    fwd_b = jnp.concatenate([jnp.concatenate([f_re, -f_im], axis=1),
                             jnp.concatenate([f_im, f_re], axis=1)], axis=0)
    inv_b = jnp.concatenate([jnp.concatenate([f_re, f_im], axis=1),
                             jnp.concatenate([-f_im, f_re], axis=1)], axis=0)
    kept = min(ns, -(-(ns // 2 + 1) // SUBLANES) * SUBLANES)
    fwd_a_real = jnp.concatenate([fwd_a[:, :kept, :ns // 2], fwd_a[:, ns:ns + kept, :ns // 2]], axis=1)
    m_re, m_im = jnp.roll(f_re, -1, axis=0), jnp.roll(f_im, -1, axis=0)
    mirror_part = jnp.stack([jnp.concatenate([m_re, m_im], axis=1), jnp.concatenate([m_im, -m_re], axis=1)])
    return dict(fwd_a=fwd_a.astype(BF16), fwd_a_real=fwd_a_real.astype(BF16),
                inv_a=inv_a.astype(BF16), fwd_b=fwd_b.astype(BF16), inv_b=inv_b.astype(BF16),
                fwd_b_part=fwd_b.reshape(2, nf, 2 * nf).astype(BF16), mirror_b_part=mirror_part.astype(BF16))


def _filter_dft_kernel(x_ref, ga_ref, fb_ref, fbm_ref, nrm_ref, k_ref, *scr):
    nf = DFT_FAST
    tiles = x_ref.shape[0]
    kept = ga_ref.shape[1] // 2
    n_in = ga_ref.shape[2]
    upper = pl.program_id(3)
    inv_norm = 1.0 / (nrm_ref[...] + 1e-6)

    def slab(start):
        return jnp.concatenate([_lane_cat([scr[2 * t + part][pl.ds(start, nf), :] for t in range(tiles)])
                                for part in range(2)], axis=0).astype(BF16)

    def put(row, mat_ref, a):
        y = _dot(mat_ref[...], a) * inv_norm
        for t in range(tiles):
            k_ref[t, pl.ds(row, nf), :] = y[:, t * LANES:(t + 1) * LANES]

    def divisor(trips):
        return max(u for u in (9, 8, 7, 5, 4, 3, 2, 1) if trips % u == 0)

    @pl.when(upper == 0)
    def _():
        def stage_a(q, carry):
            for j in range(SUBLANES):
                rows = pl.ds(j, n_in, stride=SUBLANES)
                x = _lane_cat([x_ref[t, q, rows, :] for t in range(tiles)]).astype(BF16)
                n_f = q * SUBLANES + j
                a = _dot(ga_ref[n_f], x)
                dst = pl.ds(n_f, kept, stride=DFT_PITCH)
                for t in range(tiles):
                    scr[2 * t][dst, :] = a[:kept, t * LANES:(t + 1) * LANES]
                    scr[2 * t + 1][dst, :] = a[kept:, t * LANES:(t + 1) * LANES]
            return carry

        lax.fori_loop(0, DFT_TILES, stage_a, 0, unroll=DFT_UNROLL // SUBLANES)

        def direct(k, carry):
            put(pl.multiple_of(k * nf, nf), fb_ref, slab(pl.multiple_of(k * DFT_PITCH, SUBLANES)))
            return carry

        lax.fori_loop(0, n_in, direct, 0, unroll=divisor(n_in))

    @pl.when(upper == 1)
    def _():
        put(0, fb_ref, slab(n_in * DFT_PITCH))

        def mirrored(k, carry):
            put(pl.multiple_of((n_in - k) * nf, nf), fbm_ref, slab(pl.multiple_of(k * DFT_PITCH, SUBLANES)))
            return carry

        lax.fori_loop(1, n_in, mirrored, 0, unroll=divisor(n_in - 1))


def hyena_filter_dft(eo, asum, tables, *, tiles=2):
    col_tiles, q_tiles, rows, _ = eo.shape
    l = q_tiles * rows
    c = D_HYENA
    groups = c // LANES // tiles
    orders = col_tiles // (2 * groups * tiles)
    ga, fb, fbm = tables['fwd_a_real'], tables['fwd_b_part'], tables['mirror_b_part']
    kept = ga.shape[1] // 2
    scr = pltpu.VMEM((kept * DFT_PITCH, LANES), F32)
    part_spec = pl.BlockSpec((None,) + fb.shape[1:], lambda o, part, j, h: (part, 0, 0))
    return pl.pallas_call(
        _filter_dft_kernel,
        grid=(orders, 2, groups, 2),
        in_specs=[
            pl.BlockSpec((tiles,) + eo.shape[1:], lambda o, part, j, h: ((2 * o + part) * groups + j, 0, 0, 0)),
            pl.BlockSpec(ga.shape, lambda o, part, j, h: (0, 0, 0)),
            part_spec,
            part_spec,
            pl.BlockSpec((1, tiles * LANES), lambda o, part, j, h: (0, o * groups + j)),
        ],
        out_specs=pl.BlockSpec((None, None, tiles, l, LANES), lambda o, part, j, h: (o, part, j, h, 0)),
        out_shape=jax.ShapeDtypeStruct((orders, 2, groups * tiles, 2 * l, LANES), F32),
        scratch_shapes=[scr] * (2 * tiles),
        compiler_params=_cparams("parallel", "parallel", "parallel", "arbitrary", vmem=VMEM_LIMIT_CONV),
        name="hyena_filter_dft",
    )(eo, ga, fb, fbm, asum)


def _lane_cat(parts):
    return parts[0] if len(parts) == 1 else jnp.concatenate(parts, axis=-1)


def _hyena_conv_kernel(x_ref, ga_ref, k_ref, fb_ref, ib_ref, gi_ref, gate_ref, u_ref, bias_ref, o_ref, *scr,
                       n_a, n_b, slabs):
    nf = DFT_FAST
    tiles, qt = x_ref.shape[1], x_ref.shape[2]
    half = x_ref.shape[3] // SUBLANES
    ns = 2 * half
    step = pl.program_id(1)

    @pl.when(step < n_a)
    def _():
        for qj in range(qt * SUBLANES):
            q, j = divmod(qj, SUBLANES)
            rows = pl.ds(j, half, stride=SUBLANES)
            x = _lane_cat([jnp.concatenate([x_ref[0, t, q, rows, :], x_ref[1, t, q, rows, :]], axis=0)
                           for t in range(tiles)])
            a = _dot(ga_ref[qj], x.astype(BF16))
            dst = pl.ds(step * (qt * SUBLANES) + qj, ns, stride=DFT_PITCH)
            for t in range(tiles):
                scr[2 * t][dst, :] = a[:ns, t * LANES:(t + 1) * LANES]
                scr[2 * t + 1][dst, :] = a[ns:, t * LANES:(t + 1) * LANES]

    @pl.when((step >= n_a) & (step < n_a + n_b))
    def _():
        for s in range(slabs):
            k_a = (step - n_a) * slabs + s
            rows = pl.ds(pl.multiple_of(k_a * DFT_PITCH, SUBLANES), nf)
            a = jnp.concatenate([_lane_cat([scr[2 * t + part][rows, :] for t in range(tiles)]) for part in range(2)], axis=0)
            x = _dot(fb_ref[...], a.astype(BF16))
            xr, xi = x[:nf], x[nf:]
            kr = _lane_cat([k_ref[0, t, s * nf:(s + 1) * nf, :] for t in range(tiles)])
            ki = _lane_cat([k_ref[1, t, s * nf:(s + 1) * nf, :] for t in range(tiles)])
            y = jnp.concatenate([xr * kr - xi * ki, xr * ki + xi * kr], axis=0).astype(BF16)
            cc = _dot(ib_ref[...], y)
            for t in range(tiles):
                scr[2 * t][rows, :] = cc[:nf, t * LANES:(t + 1) * LANES]
                scr[2 * t + 1][rows, :] = cc[nf:, t * LANES:(t + 1) * LANES]

    @pl.when(step >= n_a + n_b)
    def _():
        first = (step - (n_a + n_b)) * (qt * SUBLANES)
        for qj in range(qt * SUBLANES):
            q, j = divmod(qj, SUBLANES)
            src = pl.ds(first + qj, ns, stride=DFT_PITCH)
            cc = jnp.concatenate([_lane_cat([scr[2 * t + part][src, :] for t in range(tiles)]) for part in range(2)], axis=0)
            y = _dot(gi_ref[qj], cc.astype(BF16))
            dst = pl.ds(j, half, stride=SUBLANES)
            for t in range(tiles):
                o_ref[0, t, q, dst, :] = y[:half, t * LANES:(t + 1) * LANES]
                o_ref[1, t, q, dst, :] = y[half:, t * LANES:(t + 1) * LANES]
        for t in range(tiles):
            bias = bias_ref[:, t * LANES:(t + 1) * LANES]
            for bi in range(2):
                o_ref[bi, t] = gate_ref[bi, t] * (o_ref[bi, t] + u_ref[bi, t] * bias)


def hyena_long_conv(u6, u_idx, gate6, gate_idx, kf, order, bias, tables, *, slabs=8, tiles=2, qt=2):
    _, _, cts, q_tiles, rows, _ = u6.shape
    ns = 2 * rows // SUBLANES
    n_a = q_tiles // qt
    n_b = ns // slabs
    ga, gi, fb, ib = tables['fwd_a'], tables['inv_a'], tables['fwd_b'], tables['inv_b']
    q_fwd = lambda s: jnp.minimum(s, n_a - 1)
    q_mid = lambda s: jnp.clip(s - n_a, 0, n_b - 1)
    q_inv = lambda s: jnp.clip(s - n_a - n_b, 0, n_a - 1)

    def seq_spec(idx, q_of):
        return pl.BlockSpec((None, 2, tiles, qt, rows, LANES), lambda c, s: (idx, 0, c, q_of(s), 0, 0))

    scr = pltpu.VMEM((ns * DFT_PITCH, LANES), F32)
    kern = functools.partial(_hyena_conv_kernel, n_a=n_a, n_b=n_b, slabs=slabs)
    return pl.pallas_call(
        kern,
        grid=(cts // tiles, 2 * n_a + n_b),
        in_specs=[
            seq_spec(u_idx, q_fwd),
            pl.BlockSpec((qt * SUBLANES,) + ga.shape[1:], lambda c, s: (q_fwd(s), 0, 0)),
            pl.BlockSpec((None, 2, tiles, slabs * DFT_FAST, LANES), lambda c, s: (order, 0, c, q_mid(s), 0)),
            pl.BlockSpec(fb.shape, lambda c, s: (0, 0)),
            pl.BlockSpec(ib.shape, lambda c, s: (0, 0)),
            pl.BlockSpec((qt * SUBLANES,) + gi.shape[1:], lambda c, s: (q_inv(s), 0, 0)),
            seq_spec(gate_idx, q_inv),
            seq_spec(u_idx, q_inv),
            pl.BlockSpec((1, tiles * LANES), lambda c, s: (0, c)),
        ],
        out_specs=pl.BlockSpec((2, tiles, qt, rows, LANES), lambda c, s: (0, c, q_inv(s), 0, 0)),
        out_shape=jax.ShapeDtypeStruct((2, cts, q_tiles, rows, LANES), F32),
        scratch_shapes=[scr] * (2 * tiles),
        compiler_params=_cparams("arbitrary", "arbitrary", vmem=VMEM_LIMIT_CONV),
        name="hyena_long_conv",
    )(u6, ga, kf, fb, ib, gi, gate6, u6, bias.reshape(1, -1))


def hyena_filter_spectra(seq_len, tables, w1, b1, w2, b2, w3, freq):
    eo, asum = hyena_filter_mlp(seq_len, w1, b1, w2, b2, w3, freq, tl=FILTER_ROW_TILE)
    return hyena_filter_dft(eo, asum, tables)


def hyena_mixer(gates, kf, bias, tables):
    z = hyena_long_conv(gates, 2, gates, 1, kf, 0, bias[0], tables)
    return hyena_long_conv(z[None], 0, gates, 0, kf, 1, bias[1], tables)


def _odd_out_xattn_kernel(ys_ref, u_ref, hy_ref, x_ref, d_ref, wglu_ref, wo_ref, g_ref,
                          kv_ref, wq_ref, xwo_ref, g1_ref, g2_ref, o_ref):
    y = ys_ref[0] + d_ref[...] * u_ref[0]
    c0 = math.sqrt(2.0 / math.pi)
    gl = 0.5 * y * (1.0 + jnp.tanh(c0 * (y + 0.044715 * (y * y * y))))
    z = _dot(gl.astype(BF16), wglu_ref[...])
    s5 = gl * (1.0 / (1.0 + jnp.exp(-z)))
    hy = [_from_time_tiles(hy_ref[ct]).astype(BF16) for ct in range(hy_ref.shape[0])]
    mix = _dot(jnp.concatenate([s5.astype(BF16)] + hy, axis=-1), wo_ref[...])
    x = x_ref[0] + _rms(mix, g_ref[...])
    o_ref[0] = _xattn(x, kv_ref, wq_ref, xwo_ref, g1_ref, g2_ref)


def odd_out_xattn(ys, u_s5, hy, x, d_skip, w_glu, w_out, odd_layer, g, kv, wq, xwo, layer, g1, g2, *, tm):
    b, l, d = x.shape
    tok = lambda w: pl.BlockSpec((1, tm, w), lambda bi, i: (bi, i, 0))
    hy_spec = pl.BlockSpec((None, D_HYENA // LANES, DFT_TILES, tm // DFT_FAST * SUBLANES, LANES),
                           lambda bi, i: (bi, 0, 0, i, 0))
    return pl.pallas_call(
        _odd_out_xattn_kernel,
        grid=(b, l // tm),
        in_specs=[tok(D_S5), tok(D_S5), hy_spec, tok(d),
                  pl.BlockSpec((1, D_S5), lambda bi, i: (0, 0)),
                  _layer_spec((D_S5, D_S5), odd_layer, lambda bi, i: (0, 0)),
                  _layer_spec((d, d), odd_layer, lambda bi, i: (0, 0)),
                  pl.BlockSpec((1, d), lambda bi, i: (0, 0))] + _xattn_specs(d, layer),
        out_specs=tok(d),
        out_shape=jax.ShapeDtypeStruct(x.shape, F32),
        compiler_params=_cparams("parallel", "parallel"),
        name="odd_out_xattn",
    )(ys, u_s5, hy, x, d_skip.reshape(1, D_S5), w_glu, w_out, g.reshape(1, d),
      kv, wq, xwo, g1.reshape(1, d), g2.reshape(1, d))


def kernel(x, mem, norm_mix, norm_xattn, norm_mem, norm_mlp, xa_wq, xa_wk, xa_wv, xa_wo, mlp_w1, mlp_w2, ev_w_in, ev_pool_w, ev_pool_scale, ev_conv_w, ev_w_out, od_w_in, od_s5_lambda_re, od_s5_lambda_im, od_s5_log_dt, od_s5_b_re, od_s5_b_im, od_s5_c_re, od_s5_c_im, od_s5_d, od_s5_w_glu, od_hy_short_w, od_hy_short_b, od_hy_w1, od_hy_b1, od_hy_w2, od_hy_b2, od_hy_w3, od_hy_freq, od_hy_bias, od_w_out):
    b, l, d = x.shape
    depth = norm_mix.shape[0]
    assert b == 2, "the long convolution packs the two batch rows as one complex signal"
    tables = dft_tables(l)
    mem2d = mem.reshape(b * N_MEM, d)
    wq, wo = xa_wq.astype(BF16), xa_wo.astype(BF16)
    wkv = jnp.concatenate([xa_wk, xa_wv], axis=2).astype(BF16)
    kv = memory_kv(mem2d, norm_mem, wkv).reshape(depth, b, N_MEM, 2 * d)
    w1, w2 = mlp_w1.astype(BF16), mlp_w2.astype(BF16)
    ev_in, ev_out, ev_pool = ev_w_in.astype(BF16), ev_w_out.astype(BF16), ev_pool_w.astype(BF16)
    od_in, od_out, od_glu = od_w_in.astype(BF16), od_w_out.astype(BF16), od_s5_w_glu.astype(BF16)
    for i in range(depth):
        j = i // 2
        if i % 2 == 0:
            x = even_mixer(x, norm_mix[i, 0], ev_in, ev_pool, ev_pool_scale[j], ev_conv_w[j], ev_out, j,
                           norm_mix[i, 1], tm=ROW_TILE)
            x = xattn_block(x, kv, wq, wo, i, norm_xattn[i, 0], norm_xattn[i, 1], tm=ROW_TILE)
        else:
            u_s5, gates = odd_in(x, norm_mix[i, 0], od_in, j, od_hy_short_w[j], od_hy_short_b[j], tm=ROW_TILE)
            mats = s5_matrices(od_s5_lambda_re[j], od_s5_lambda_im[j], od_s5_log_dt[j], od_s5_b_re[j],
                               od_s5_b_im[j], od_s5_c_re[j], od_s5_c_im[j])
            ys = s5_core(u_s5, mats)
            kf = hyena_filter_spectra(l, tables, od_hy_w1[j], od_hy_b1[j], od_hy_w2[j], od_hy_b2[j],
                                      od_hy_w3[j], od_hy_freq[j])
            hy = hyena_mixer(gates, kf, od_hy_bias[j], tables)
            x = odd_out_xattn(ys, u_s5, hy, x, od_s5_d[j], od_glu, od_out, j, norm_mix[i, 1],
                              kv, wq, wo, i, norm_xattn[i, 0], norm_xattn[i, 1], tm=ROW_TILE)
        x = mlp_block(x.reshape(b * l, d), w1, w2, i, norm_mlp[i, 0], norm_mlp[i, 1],
                      tm=ROW_TILE, tf=MLP_HIDDEN_TILE).reshape(b, l, d)
    return x
```

```python
import functools
import math

import numpy as np
import jax
import jax.numpy as jnp
from jax import lax
from jax.experimental import pallas as pl
from jax.experimental.pallas import tpu as pltpu

F32 = jnp.float32
BF16 = jnp.bfloat16

D_MODEL = 1024
N_MEM = 256
RMS_EPS = 1e-6
D_POOL = 512
POOL_WINDOWS = (2, 4, 8, 16)
POOL_GROUP_DIM = 128
POOL_HALO = 8
D_CONV = 512
D_S5 = 512
S5_GROUP_DIM = 16
S5_GROUPS = 32
S5_STATE = 64
S5_CHUNK = 16
D_HYENA = 512
HYENA_BANDS = 16
HYENA_FFN = 64
HYENA_TARGET = 1e-2
HYENA_SHORT_DECAY_PCT = 0.3
HYENA_LONG_DECAY_PCT = 1.5
XA_HEADS = 4
XA_HEAD_DIM = 256
D_FF = 4096

LANES = 128
SUBLANES = 8
S5_SUPER = LANES // S5_GROUP_DIM
DFT_FAST = 128
DFT_TILES = DFT_FAST // SUBLANES
DFT_PITCH = DFT_FAST + SUBLANES
DFT_UNROLL = 16
ROW_TILE = 1024
MLP_HIDDEN_TILE = 512
FILTER_ROW_TILE = 512
VMEM_LIMIT = 56 * 1024 * 1024
VMEM_LIMIT_CONV = 62 * 1024 * 1024


def _cparams(*sem, vmem=VMEM_LIMIT):
    return pltpu.CompilerParams(dimension_semantics=sem, vmem_limit_bytes=vmem)


def _rms(xf, g):
    ms = jnp.mean(xf * xf, axis=-1, keepdims=True)
    return xf * lax.rsqrt(ms + RMS_EPS) * g


def _dot(a, b):
    return jnp.dot(a, b, preferred_element_type=F32)


def _layer_spec(block, layer, tail_map):
    return pl.BlockSpec((None,) + block, lambda *idx: (layer,) + tail_map(*idx))


def _norm_matmul_kernel(x_ref, g_ref, w_ref, o_ref):
    xn = _rms(x_ref[...], g_ref[...]).astype(BF16)
    o_ref[...] = _dot(xn, w_ref[...]).astype(o_ref.dtype)


def memory_kv(mem2d, g, w):
    layers, d, n = w.shape
    m = mem2d.shape[0]
    return pl.pallas_call(
        _norm_matmul_kernel,
        grid=(layers,),
        in_specs=[
            pl.BlockSpec((m, d), lambda i: (0, 0)),
            pl.BlockSpec((None, 1, d), lambda i: (i, 0, 0)),
            pl.BlockSpec((None, d, n), lambda i: (i, 0, 0)),
        ],
        out_specs=pl.BlockSpec((None, m, n), lambda i: (i, 0, 0)),
        out_shape=jax.ShapeDtypeStruct((layers, m, n), BF16),
        compiler_params=_cparams("parallel"),
        name="memory_kv",
    )(mem2d, g.reshape(layers, 1, d), w)


def _xattn(x, kv_ref, wq_ref, wo_ref, g1_ref, g2_ref):
    xn = _rms(x, g1_ref[...]).astype(BF16)
    q = (_dot(xn, wq_ref[...]) * (XA_HEAD_DIM ** -0.5)).astype(BF16)
    heads = []
    for h in range(XA_HEADS):
        lo = h * XA_HEAD_DIM
        qh = q[:, lo:lo + XA_HEAD_DIM]
        kh = kv_ref[0, :, lo:lo + XA_HEAD_DIM]
        vh = kv_ref[0, :, D_MODEL + lo:D_MODEL + lo + XA_HEAD_DIM]
        s = lax.dot_general(qh, kh, (((1,), (1,)), ((), ())), preferred_element_type=F32)
        e = jnp.exp(s - jnp.max(s, axis=-1, keepdims=True))
        p = e / jnp.sum(e, axis=-1, keepdims=True)
        heads.append(_dot(p.astype(BF16), vh).astype(BF16))
    o = jnp.concatenate(heads, axis=-1)
    y = _dot(o, wo_ref[...])
    return x + _rms(y, g2_ref[...])


def _xattn_kernel(x_ref, kv_ref, wq_ref, wo_ref, g1_ref, g2_ref, o_ref):
    o_ref[0] = _xattn(x_ref[0], kv_ref, wq_ref, wo_ref, g1_ref, g2_ref)


def _xattn_specs(d, layer):
    return [
        pl.BlockSpec((None, 1, N_MEM, 2 * d), lambda bi, i: (layer, bi, 0, 0)),
        _layer_spec((d, d), layer, lambda bi, i: (0, 0)),
        _layer_spec((d, d), layer, lambda bi, i: (0, 0)),
        pl.BlockSpec((1, d), lambda bi, i: (0, 0)),
        pl.BlockSpec((1, d), lambda bi, i: (0, 0)),
    ]


def xattn_block(x, kv, wq, wo, layer, g1, g2, *, tm):
    b, l, d = x.shape
    return pl.pallas_call(
        _xattn_kernel,
        grid=(b, l // tm),
        in_specs=[pl.BlockSpec((1, tm, d), lambda bi, i: (bi, i, 0))] + _xattn_specs(d, layer),
        out_specs=pl.BlockSpec((1, tm, d), lambda bi, i: (bi, i, 0)),
        out_shape=jax.ShapeDtypeStruct(x.shape, F32),
        compiler_params=_cparams("parallel", "parallel"),
        name="xattn_block",
    )(x, kv, wq, wo, g1.reshape(1, d), g2.reshape(1, d))


def _mlp_kernel(x_ref, w1_ref, w2_ref, g1_ref, g2_ref, o_ref, h_ref, *, tf):
    x = x_ref[...]
    xn = _rms(x, g1_ref[...]).astype(BF16)
    for c in range(h_ref.shape[1] // tf):
        cols = slice(c * tf, (c + 1) * tf)
        h = jnp.maximum(_dot(xn, w1_ref[:, cols]), 0.0)
        h_ref[:, cols] = (h * h).astype(BF16)
    o_ref[...] = x + _rms(_dot(h_ref[...], w2_ref[...]), g2_ref[...])


def mlp_block(x2d, w1, w2, layer, g1, g2, *, tm, tf):
    m, d = x2d.shape
    ff = w1.shape[-1]
    resident = dict(pipeline_mode=pl.Buffered(1))
    return pl.pallas_call(
        functools.partial(_mlp_kernel, tf=tf),
        grid=(m // tm,),
        in_specs=[
            pl.BlockSpec((tm, d), lambda i: (i, 0)),
            pl.BlockSpec((None, d, ff), lambda i: (layer, 0, 0), **resident),
            pl.BlockSpec((None, ff, d), lambda i: (layer, 0, 0), **resident),
            pl.BlockSpec((1, d), lambda i: (0, 0)),
            pl.BlockSpec((1, d), lambda i: (0, 0)),
        ],
        out_specs=pl.BlockSpec((tm, d), lambda i: (i, 0)),
        out_shape=jax.ShapeDtypeStruct((m, d), F32),
        scratch_shapes=[pltpu.VMEM((tm, ff), BF16)],
        compiler_params=_cparams("parallel"),
        name="mlp_block",
    )(x2d, w1, w2, g1.reshape(1, d), g2.reshape(1, d))


def _halo_specs(tm, seq_len, width, col):
    r = tm // POOL_HALO
    last = seq_len // POOL_HALO - 1

    def prev_map(bi, i, *_):
        return (bi, jnp.maximum(i * r - 1, 0), col(*_))

    def main_map(bi, i, *_):
        return (bi, i, col(*_))

    def next_map(bi, i, *_):
        return (bi, jnp.minimum((i + 1) * r, last), col(*_))

    return [
        pl.BlockSpec((1, POOL_HALO, width), prev_map),
        pl.BlockSpec((1, tm, width), main_map),
        pl.BlockSpec((1, POOL_HALO, width), next_map),
    ]


def _with_halo(prev_ref, main_ref, next_ref):
    i = pl.program_id(1)
    prev = jnp.where(i > 0, prev_ref[0], 0.0)
    nxt = jnp.where(i < pl.num_programs(1) - 1, next_ref[0], 0.0)
    return jnp.concatenate([prev, main_ref[0], nxt], axis=0)


def _shift_rows(v, k):
    return pltpu.roll(v, k % v.shape[0], 0)


def _even_mixer_kernel(xp_ref, x_ref, xn_ref, gi_ref, wi_ref, wg_ref, ps_ref, cw_ref, wo_ref, g_ref, o_ref,
                       *, tm, seq_len):
    x_ext = _with_halo(xp_ref, x_ref, xn_ref)
    ext = _dot(_rms(x_ext, gi_ref[...]).astype(BF16), wi_ref[...])
    lo, hi = POOL_HALO, POOL_HALO + tm
    t = pl.program_id(1) * tm + lax.broadcasted_iota(jnp.int32, (tm, 1), 0)
    parts = []
    for gi, win in enumerate(POOL_WINDOWS):
        half = win // 2
        u = ext[:, gi * POOL_GROUP_DIM:(gi + 1) * POOL_GROUP_DIM]
        s = u + _shift_rows(u, 1)
        step = 1
        while 2 * step < win:
            s = _shift_rows(s, step) + _shift_rows(s, -step)
            step *= 2
        cnt = (jnp.minimum(t + half, seq_len) - jnp.maximum(t - half, 0)).astype(F32)
        pooled = s[lo:hi] / cnt - u[lo:hi]
        y = _dot(pooled.astype(BF16), wg_ref[gi])
        parts.append((y * ps_ref[:, gi * POOL_GROUP_DIM:(gi + 1) * POOL_GROUP_DIM]).astype(BF16))
    b_gate = ext[lo:hi, D_POOL:D_POOL + D_CONV]
    ch = ext[:, D_POOL + D_CONV:D_POOL + 2 * D_CONV] * ext[:, D_POOL + 2 * D_CONV:D_POOL + 3 * D_CONV]
    conv = cw_ref[0:1, :] * _shift_rows(ch, 1) + cw_ref[1:2, :] * ch + cw_ref[2:3, :] * _shift_rows(ch, -1)
    parts.append((b_gate * conv[lo:hi]).astype(BF16))
    mix = _dot(jnp.concatenate(parts, axis=-1), wo_ref[...])
    o_ref[0] = x_ref[0] + _rms(mix, g_ref[...])


def even_mixer(x, g_in, w_in, w_group, pool_scale, conv_w, w_out, layer, g, *, tm):
    b, l, d = x.shape
    kern = functools.partial(_even_mixer_kernel, tm=tm, seq_len=l)
    return pl.pallas_call(
        kern,
        grid=(b, l // tm),
        in_specs=_halo_specs(tm, l, d, lambda: 0) + [
            pl.BlockSpec((1, d), lambda bi, i: (0, 0)),
            _layer_spec(w_in.shape[1:], layer, lambda bi, i: (0, 0)),
            _layer_spec(w_group.shape[1:], layer, lambda bi, i: (0, 0, 0)),
            pl.BlockSpec((1, D_POOL), lambda bi, i: (0, 0)),
            pl.BlockSpec((3, D_CONV), lambda bi, i: (0, 0)),
            _layer_spec((d, d), layer, lambda bi, i: (0, 0)),
            pl.BlockSpec((1, d), lambda bi, i: (0, 0)),
        ],
        out_specs=pl.BlockSpec((1, tm, d), lambda bi, i: (bi, i, 0)),
        out_shape=jax.ShapeDtypeStruct(x.shape, F32),
        compiler_params=_cparams("parallel", "parallel"),
        name="even_mixer",
    )(x, x, x, g_in.reshape(1, d), w_in, w_group, pool_scale.reshape(1, D_POOL), conv_w, w_out, g.reshape(1, d))


def _cmul(ar, ai, br, bi):
    return ar * br - ai * bi, ar * bi + ai * br


def _s5_toeplitz_kernel(m_ref, o_ref):
    tc, hd = S5_CHUNK, S5_GROUP_DIM
    for g in range(m_ref.shape[0]):
        m = m_ref[g]
        rows = [m[:, (tc - 1 - s) * hd:(2 * tc - 1 - s) * hd] for s in range(tc)]
        o_ref[g] = jnp.concatenate(rows, axis=0).astype(o_ref.dtype)


def s5_toeplitz(lag_table):
    g, hd, w = lag_table.shape
    k = w // 2
    return pl.pallas_call(
        _s5_toeplitz_kernel,
        grid=(g // S5_SUPER,),
        in_specs=[pl.BlockSpec((S5_SUPER, hd, w), lambda i: (i, 0, 0))],
        out_specs=pl.BlockSpec((S5_SUPER, k, k), lambda i: (i, 0, 0)),
        out_shape=jax.ShapeDtypeStruct((g, k, k), BF16),
        compiler_params=_cparams("parallel"),
        name="s5_toeplitz",
    )(lag_table)


def s5_matrices(lam_re, lam_im, log_dt, b_re, b_im, c_re, c_im):
    tc, hd, n = S5_CHUNK, S5_GROUP_DIM, S5_STATE
    hp = lax.Precision.HIGHEST
    lr = jnp.minimum(lam_re, -1e-4)
    li = lam_im
    dt = jnp.exp(log_dt)[..., None]
    taus = jnp.arange(tc + 1, dtype=F32)[:, None, None, None]
    rmag = jnp.exp(lr[None] * dt[None] * (tc - taus))
    rang = li[None] * dt[None] * (tc - taus)
    pwr_re, pwr_im = rmag * jnp.cos(rang), rmag * jnp.sin(rang)
    mag = jnp.exp(lr[None] * dt[None] * taus)
    ang = li[None] * dt[None] * taus
    pw_re, pw_im = mag * jnp.cos(ang), mag * jnp.sin(ang)
    nr, ni = pw_re[1] - 1.0, pw_im[1]
    den = lr * lr + li * li
    coef_re, coef_im = (nr * lr + ni * li) / den, (ni * lr - nr * li) / den
    cb_re, cb_im = _cmul(coef_re[..., None], coef_im[..., None], b_re[None], b_im[None])
    cbt_re, cbt_im = _cmul(coef_re[:, :, None, :], coef_im[:, :, None, :],
                           jnp.swapaxes(b_re, 1, 2)[None], jnp.swapaxes(b_im, 1, 2)[None])
    def lag_kernels(d, p_re, p_im):
        q_re, q_im = _cmul(p_re[..., None], p_im[..., None], cb_re[d][None], cb_im[d][None])
        c_cat = jnp.concatenate([c_re[d], -c_im[d]], axis=-1)
        return jnp.einsum('gkn,tgnj->gjtk', c_cat, jnp.concatenate([q_re, q_im], axis=2), precision=hp)

    m_fwd = lag_kernels(0, pw_re[:tc, 0], pw_im[:tc, 0])
    m_bwd = lag_kernels(1, pwr_re[1:tc + 1, 1], pwr_im[1:tc + 1, 1])
    lag_table = jnp.concatenate([m_bwd[:, :, :tc - 1], m_bwd[:, :, tc - 1:] + m_fwd[:, :, :1], m_fwd[:, :, 1:],
                                 jnp.zeros_like(m_fwd[:, :, :1])], axis=2)
    intra = s5_toeplitz(lag_table.reshape(S5_GROUPS, hd, 2 * tc * hd))
    sf_re, sf_im = _cmul(pwr_re[1:tc + 1, 0][:, :, None, :], pwr_im[1:tc + 1, 0][:, :, None, :],
                         cbt_re[0][None], cbt_im[0][None])
    sb_re, sb_im = _cmul(pw_re[:tc, 1][:, :, None, :], pw_im[:tc, 1][:, :, None, :],
                         cbt_re[1][None], cbt_im[1][None])
    summ = jnp.concatenate([sf_re, sb_re, sf_im, sb_im], axis=-1)
    summ = jnp.swapaxes(summ, 0, 1).reshape(S5_GROUPS, tc * hd, 4 * n)
    gtn = lambda z: jnp.swapaxes(z, 0, 1)[:, :, None, :]
    qf_re, qf_im = _cmul(c_re[0][:, None], c_im[0][:, None], gtn(pw_re[1:tc + 1, 0]), gtn(pw_im[1:tc + 1, 0]))
    qb_re, qb_im = _cmul(c_re[1][:, None], c_im[1][:, None], gtn(pwr_re[:tc, 1]), gtn(pwr_im[:tc, 1]))
    carry = jnp.concatenate([qf_re, qb_re, -qf_im, -qb_im], axis=-1)
    carry = carry.reshape(S5_GROUPS, tc * hd, 4 * n)
    a_re = jnp.concatenate([pw_re[tc, 0], pw_re[tc, 1]], axis=-1)
    a_im = jnp.concatenate([pw_im[tc, 0], pw_im[tc, 1]], axis=-1)
    return intra, summ, carry, a_re, a_im


def s5_regroup_matrix():
    n = S5_SUPER * LANES
    src = np.arange(n)
    tb, g, h = src // LANES, (src % LANES) // S5_GROUP_DIM, src % S5_GROUP_DIM
    perm = np.zeros((n, n), np.float32)
    perm[src, g * LANES + tb * S5_GROUP_DIM + h] = 1.0
    return jnp.asarray(perm, BF16)


def _s5_regroup_kernel(p_ref, w_ref, perm_ref, u_ref, sr_ref, si_ref, *, n_chunks):
    n2 = 2 * S5_STATE
    for q in range(S5_CHUNK // S5_SUPER):
        rows_t = [p_ref[pl.ds(q * S5_SUPER + tb, n_chunks, stride=S5_CHUNK), :].astype(BF16) for tb in range(S5_SUPER)]
        grouped = _dot(jnp.concatenate(rows_t, axis=-1), perm_ref[...]).astype(BF16)
        for g in range(S5_SUPER):
            u_ref[g, :, q * LANES:(q + 1) * LANES] = grouped[:, g * LANES:(g + 1) * LANES]
    for g in range(S5_SUPER):
        s = _dot(u_ref[g], w_ref[g])
        sr_ref[pl.ds(g, n_chunks, stride=S5_SUPER), :] = s[:, :n2]
        si_ref[pl.ds(g, n_chunks, stride=S5_SUPER), :] = s[:, n2:]


def _s5_state_spec(c, n2, n_tiles):
    return pl.BlockSpec((None, c * S5_SUPER, n2), lambda a, bi: (bi * n_tiles + a, 0, 0))


def _s5_group_spec(rows, cols):
    return pl.BlockSpec((S5_SUPER, rows, cols), lambda a, bi: (a, 0, 0))


def s5_summary(p, summ):
    b, l, _ = p.shape
    c = l // S5_CHUNK
    n_tiles = S5_GROUPS // S5_SUPER
    k = S5_CHUNK * S5_GROUP_DIM
    n2 = 2 * S5_STATE
    shape = jax.ShapeDtypeStruct((b * n_tiles, c * S5_SUPER, n2), F32)
    kern = functools.partial(_s5_regroup_kernel, n_chunks=c)
    perm = s5_regroup_matrix()
    return pl.pallas_call(
        kern,
        grid=(n_tiles, b),
        in_specs=[
            pl.BlockSpec((None, l, LANES), lambda a, bi: (bi, 0, a)),
            _s5_group_spec(k, 2 * n2),
            pl.BlockSpec(perm.shape, lambda a, bi: (0, 0)),
        ],
        out_specs=[pl.BlockSpec((None, S5_SUPER, c, k), lambda a, bi: (bi * n_tiles + a, 0, 0, 0)),
                   _s5_state_spec(c, n2, n_tiles), _s5_state_spec(c, n2, n_tiles)],
        out_shape=[jax.ShapeDtypeStruct((b * n_tiles, S5_SUPER, c, k), BF16), shape, shape],
        compiler_params=_cparams("parallel", "parallel"),
        name="s5_summary",
    )(p, summ, perm)


def _s5_scan(sr_ref, si_ref, are_ref, aim_ref, or_ref, oi_ref, n_chunks):
    n2 = 2 * S5_STATE
    a_re = are_ref[...]
    a_im = aim_ref[...]
    fwd_lane = lax.broadcasted_iota(jnp.int32, (SUBLANES, n2), 1) < S5_STATE

    def body(i, state):
        x_re, x_im = state
        rows_i = pl.ds(pl.multiple_of(i * SUBLANES, SUBLANES), SUBLANES)
        rows_r = pl.ds(pl.multiple_of((n_chunks - 1 - i) * SUBLANES, SUBLANES), SUBLANES)
        or_ref[rows_i, 0:S5_STATE] = x_re[:, :S5_STATE]
        oi_ref[rows_i, 0:S5_STATE] = x_im[:, :S5_STATE]
        or_ref[rows_r, S5_STATE:n2] = x_re[:, S5_STATE:]
        oi_ref[rows_r, S5_STATE:n2] = x_im[:, S5_STATE:]
        s_re = jnp.where(fwd_lane, sr_ref[rows_i, :], sr_ref[rows_r, :])
        s_im = jnp.where(fwd_lane, si_ref[rows_i, :], si_ref[rows_r, :])
        return (a_re * x_re - a_im * x_im + s_re, a_re * x_im + a_im * x_re + s_im)

    zero = jnp.zeros((SUBLANES, n2), F32)
    lax.fori_loop(0, n_chunks, body, (zero, zero))


def _s5_output_kernel(u_ref, sr_ref, si_ref, are_ref, aim_ref, wi_ref, wc_ref, perm_ref, o_ref,
                      y_ref, xr_ref, xi_ref, *, n_chunks):
    _s5_scan(sr_ref, si_ref, are_ref, aim_ref, xr_ref, xi_ref, n_chunks)
    back = (((1,), (1,)), ((), ()))
    for g in range(S5_SUPER):
        rows = pl.ds(g, n_chunks, stride=S5_SUPER)
        xin = jnp.concatenate([xr_ref[rows, :], xi_ref[rows, :]], axis=-1).astype(BF16)
        y_ref[g] = _dot(u_ref[g], wi_ref[g]) + lax.dot_general(xin, wc_ref[g], back, preferred_element_type=F32)
    for q in range(S5_CHUNK // S5_SUPER):
        y = jnp.concatenate([y_ref[g, :, q * LANES:(q + 1) * LANES] for g in range(S5_SUPER)], axis=-1)
        hi = y.astype(BF16)
        lo = (y - hi.astype(F32)).astype(BF16)
        tok = (lax.dot_general(hi, perm_ref[...], back, preferred_element_type=F32)
               + lax.dot_general(lo, perm_ref[...], back, preferred_element_type=F32))
        for tb in range(S5_SUPER):
            o_ref[pl.ds(q * S5_SUPER + tb, n_chunks, stride=S5_CHUNK), :] = tok[:, tb * LANES:(tb + 1) * LANES]


def s5_output(u, s_re, s_im, a_re, a_im, intra, carry, *, batch):
    tiles, _, c, k = u.shape
    n_tiles = tiles // batch
    n2 = 2 * S5_STATE
    l = c * S5_CHUNK
    kern = functools.partial(_s5_output_kernel, n_chunks=c)
    perm = s5_regroup_matrix()
    decay_spec = pl.BlockSpec((S5_SUPER, n2), lambda a, bi: (a, 0))
    state_scr = pltpu.VMEM((c * S5_SUPER, n2), F32)
    return pl.pallas_call(
        kern,
        grid=(n_tiles, batch),
        in_specs=[
            pl.BlockSpec((None, S5_SUPER, c, k), lambda a, bi: (bi * n_tiles + a, 0, 0, 0)),
            _s5_state_spec(c, n2, n_tiles),
            _s5_state_spec(c, n2, n_tiles),
            decay_spec,
            decay_spec,
            _s5_group_spec(k, k),
            _s5_group_spec(k, 2 * n2),
            pl.BlockSpec(perm.shape, lambda a, bi: (0, 0)),
        ],
        out_specs=pl.BlockSpec((None, l, LANES), lambda a, bi: (bi, 0, a)),
        out_shape=jax.ShapeDtypeStruct((batch, l, D_S5), F32),
        scratch_shapes=[pltpu.VMEM((S5_SUPER, c, k), F32), state_scr, state_scr],
        compiler_params=_cparams("parallel", "parallel"),
        name="s5_output",
    )(u, s_re, s_im, a_re, a_im, intra, carry, perm)


def s5_core(p, mats):
    intra, summ, carry, a_re, a_im = mats
    u, s_re, s_im = s5_summary(p, summ.astype(BF16))
    return s5_output(u, s_re, s_im, a_re, a_im, intra.astype(BF16), carry.astype(BF16), batch=p.shape[0])


def _to_time_tiles(y):
    nf, c = DFT_FAST, y.shape[-1]
    slabs = [y[s * nf:(s + 1) * nf].reshape(DFT_TILES, SUBLANES, c) for s in range(y.shape[0] // nf)]
    return jnp.concatenate(slabs, axis=1)


def _from_time_tiles(v):
    c = v.shape[-1]
    slabs = [v[:, s * SUBLANES:(s + 1) * SUBLANES, :].reshape(DFT_FAST, c) for s in range(v.shape[1] // SUBLANES)]
    return jnp.concatenate(slabs, axis=0)


def _odd_in_kernel(xp_ref, x_ref, xn_ref, g_ref, w_ref, cw_ref, cb_ref, s5_ref, gates_ref, *, tm):
    ext = _with_halo(xp_ref, x_ref, xn_ref)
    p = _dot(_rms(ext, g_ref[...]).astype(BF16), w_ref[...])
    lo, hi = POOL_HALO, POOL_HALO + tm
    s5_ref[0] = p[lo:hi, :D_S5]
    ph = p[:, D_S5:]
    y = cw_ref[0:1, :] * _shift_rows(ph, 1) + cw_ref[1:2, :] * ph + cw_ref[2:3, :] * _shift_rows(ph, -1)
    y = y[lo:hi] + cb_ref[...]
    for k in range(3):
        for ct in range(D_HYENA // LANES):
            col = k * D_HYENA + ct * LANES
            gates_ref[k, ct] = _to_time_tiles(y[:, col:col + LANES])


def odd_in(x, g, w, layer, conv_w, conv_b, *, tm):
    b, l, d = x.shape
    n = w.shape[-1]
    rows = tm // DFT_FAST * SUBLANES
    cts = D_HYENA // LANES
    kern = functools.partial(_odd_in_kernel, tm=tm)
    return pl.pallas_call(
        kern,
        grid=(b, l // tm),
        in_specs=_halo_specs(tm, l, d, lambda: 0) + [
            pl.BlockSpec((1, d), lambda bi, i: (0, 0)),
            _layer_spec((d, n), layer, lambda bi, i: (0, 0)),
            pl.BlockSpec((3, n - D_S5), lambda bi, i: (0, 0)),
            pl.BlockSpec((1, n - D_S5), lambda bi, i: (0, 0)),
        ],
        out_specs=[pl.BlockSpec((1, tm, D_S5), lambda bi, i: (bi, i, 0)),
                   pl.BlockSpec((3, None, cts, DFT_TILES, rows, LANES), lambda bi, i: (0, bi, 0, 0, i, 0))],
        out_shape=[jax.ShapeDtypeStruct((b, l, D_S5), F32),
                   jax.ShapeDtypeStruct((3, b, cts, DFT_TILES, l // DFT_FAST * SUBLANES, LANES), F32)],
        compiler_params=_cparams("parallel", "parallel"),
        name="odd_in",
    )(x, x, x, g.reshape(1, d), w, conv_w, conv_b.reshape(1, -1))


def _filter_mlp_kernel(bands_ref, w1t_ref, w1cs_ref, b1_ref, w2_ref, b2_ref, w3h_ref, w3l_ref, fr_ref, dl_ref,
                       eo_ref, asum_ref, *, tl, seq_len):
    i = pl.program_id(0)
    hp = lax.Precision.HIGHEST
    c = D_HYENA
    half = tl // 2
    first = i * tl + lax.broadcasted_iota(jnp.int32, (half, 1), 0)

    def positions(width):
        right = lax.broadcasted_iota(jnp.int32, (1, width), 1) >= width // 2
        return (first + jnp.where(right, half, 0)).astype(F32)

    feat = bands_ref.shape[1]
    ang = (2.0 * math.pi / seq_len) * positions(feat) * bands_ref[...]
    is_cos = lax.broadcasted_iota(jnp.int32, (1, feat), 1) % (2 * HYENA_BANDS) < HYENA_BANDS
    cs = jnp.sin(ang + jnp.where(is_cos, 0.5 * math.pi, math.pi))
    fr = fr_ref[...]
    z = (positions(w1t_ref.shape[1]) / (seq_len - 1.0)) * w1t_ref[...] + jnp.dot(
        cs, w1cs_ref[...], precision=hp, preferred_element_type=F32)
    h = jnp.sin(fr * (z + b1_ref[...]))
    h = jnp.sin(fr * (jnp.dot(h, w2_ref[...], precision=hp, preferred_element_type=F32) + b2_ref[...]))
    h_hi = h.astype(BF16)
    h_lo = (h - h_hi.astype(F32)).astype(BF16)
    h = jnp.concatenate([_dot(h_hi, w3h_ref[r]) + (_dot(h_lo, w3h_ref[r]) + _dot(h_hi, w3l_ref[r])) for r in range(2)],
                        axis=0)
    t_norm = (i * tl + lax.broadcasted_iota(jnp.int32, (tl, 1), 0)).astype(F32) / (seq_len - 1.0)
    decay = jnp.exp(-t_norm * dl_ref[...])

    @pl.when(i == 0)
    def _():
        asum_ref[...] = jnp.zeros_like(asum_ref)

    for o in range(h.shape[1] // (2 * c)):
        fwd = h[:, 2 * o * c:(2 * o + 1) * c] * decay
        bwd = h[:, (2 * o + 1) * c:(2 * o + 2) * c] * decay
        for part, val in enumerate((fwd + bwd, fwd - bwd)):
            for ct in range(c // LANES):
                eo_ref[((2 * o + part) * c) // LANES + ct] = _to_time_tiles(val[:, ct * LANES:(ct + 1) * LANES])
        asum_ref[:, o * c:(o + 1) * c] += jnp.sum(jnp.abs(fwd) + jnp.abs(bwd), axis=0, keepdims=True)


def hyena_filter_mlp(seq_len, w1, b1, w2, b2, w3, freq, *, tl):
    n_out = w3.shape[1]
    bands = jnp.linspace(1e-4, HYENA_BANDS - 1, HYENA_BANDS, dtype=F32)
    bands2 = jnp.concatenate([bands, bands])[None, :]
    deltas = jnp.abs(jnp.linspace(math.log(HYENA_TARGET) / HYENA_LONG_DECAY_PCT,
                                  math.log(HYENA_TARGET) / HYENA_SHORT_DECAY_PCT, D_HYENA, dtype=F32))[None, :]
    kern = functools.partial(_filter_mlp_kernel, tl=tl, seq_len=seq_len)
    full = lambda a: pl.BlockSpec(a.shape, lambda i: (0,) * a.ndim)
    w3_hi = w3.astype(BF16)
    w3_lo = (w3 - w3_hi.astype(F32)).astype(BF16)
    twice = lambda v: jnp.concatenate([v, v])[None, :]
    blockdiag = lambda m: jnp.concatenate([jnp.pad(m, ((0, 0), (0, m.shape[1]))),
                                           jnp.pad(m, ((0, 0), (m.shape[1], 0)))], axis=0)
    per_half = lambda m: jnp.stack([jnp.pad(m, ((0, m.shape[0]), (0, 0))), jnp.pad(m, ((m.shape[0], 0), (0, 0)))])
    args = [twice(bands2[0]), twice(w1[0]), blockdiag(w1[1:]), twice(b1), blockdiag(w2), twice(b2),
            per_half(w3_hi), per_half(w3_lo), twice(freq), deltas]
    return pl.pallas_call(
        kern,
        grid=(seq_len // tl,),
        in_specs=[full(a) for a in args],
        out_specs=[pl.BlockSpec((n_out // LANES, DFT_TILES, tl // DFT_FAST * SUBLANES, LANES), lambda i: (0, 0, i, 0)),
                   pl.BlockSpec((1, n_out // 2), lambda i: (0, 0))],
        out_shape=[jax.ShapeDtypeStruct((n_out // LANES, DFT_TILES, seq_len // DFT_FAST * SUBLANES, LANES), F32),
                   jax.ShapeDtypeStruct((1, n_out // 2), F32)],
        compiler_params=_cparams("arbitrary"),
        name="hyena_filter_mlp",
    )(*args)


def dft_tables(seq_len):
    n = 2 * seq_len
    nf = DFT_FAST
    ns = n // nf
    k_a = np.arange(ns)[:, None]
    n_s = np.arange(ns // 2)[None, :]
    w_s = np.exp(-2j * np.pi * ((k_a * n_s) % ns) / ns)
    n_f = np.arange(nf)[:, None]
    tw = np.exp(-2j * np.pi * ((n_f * np.arange(ns)[None, :]) % n) / n)
    c32 = lambda z: (jnp.asarray(z.real, F32), jnp.asarray(z.imag, F32))
    ws_re, ws_im = c32(w_s)
    tw_re, tw_im = c32(tw)
    g_re = tw_re[:, :, None] * ws_re[None] - tw_im[:, :, None] * ws_im[None]
    g_im = tw_re[:, :, None] * ws_im[None] + tw_im[:, :, None] * ws_re[None]
    fwd_a = jnp.concatenate([jnp.concatenate([g_re, -g_im], axis=2),
                             jnp.concatenate([g_im, g_re], axis=2)], axis=1)
    gt_re = jnp.swapaxes(g_re, 1, 2) * (1.0 / n)
    gt_im = jnp.swapaxes(g_im, 1, 2) * (1.0 / n)
    inv_a = jnp.concatenate([jnp.concatenate([gt_re, gt_im], axis=2),
                             jnp.concatenate([-gt_im, gt_re], axis=2)], axis=1)
    kk = np.arange(nf)
    w_f = np.exp(-2j * np.pi * ((kk[:, None] * kk[None, :]) % nf) / nf)
    f_re, f_im = c32(w_f)
    fwd_b = jnp.concatenate([jnp.concatenate([f_re, -f_im], axis=1),
                             jnp.concatenate([f_im, f_re], axis=1)], axis=0)
    inv_b = jnp.concatenate([jnp.concatenate([f_re, f_im], axis=1),
                             jnp.concatenate([-f_im, f_re], axis=1)], axis=0)
    kept = min(ns, -(-(ns // 2 + 1) // SUBLANES) * SUBLANES)
    fwd_a_real = jnp.concatenate([fwd_a[:, :kept, :ns // 2], fwd_a[:, ns:ns + kept, :ns // 2]], axis=1)
    m_re, m_im = jnp.roll(f_re, -1, axis=0), jnp.roll(f_im, -1, axis=0)
    mirror_part = jnp.stack([jnp.concatenate([m_re, m_im], axis=1), jnp.concatenate([m_im, -m_re], axis=1)])
    return dict(fwd_a=fwd_a.astype(BF16), fwd_a_real=fwd_a_real.astype(BF16),
                inv_a=inv_a.astype(BF16), fwd_b=fwd_b.astype(BF16), inv_b=inv_b.astype(BF16),
                fwd_b_part=fwd_b.reshape(2, nf, 2 * nf).astype(BF16), mirror_b_part=mirror_part.astype(BF16))


def _filter_dft_kernel(x_ref, ga_ref, fb_ref, fbm_ref, nrm_ref, k_ref, *scr):
    nf = DFT_FAST
    tiles = x_ref.shape[0]
    kept = ga_ref.shape[1] // 2
    n_in = ga_ref.shape[2]
    upper = pl.program_id(3)
    inv_norm = 1.0 / (nrm_ref[...] + 1e-6)

    def slab(start):
        return jnp.concatenate([_lane_cat([scr[2 * t + part][pl.ds(start, nf), :] for t in range(tiles)])
                                for part in range(2)], axis=0).astype(BF16)

    def put(row, mat_ref, a):
        y = _dot(mat_ref[...], a) * inv_norm
        for t in range(tiles):
            k_ref[t, pl.ds(row, nf), :] = y[:, t * LANES:(t + 1) * LANES]

    def divisor(trips):
        return max(u for u in (9, 8, 7, 5, 4, 3, 2, 1) if trips % u == 0)

    @pl.when(upper == 0)
    def _():
        def stage_a(q, carry):
            for j in range(SUBLANES):
                rows = pl.ds(j, n_in, stride=SUBLANES)
                x = _lane_cat([x_ref[t, q, rows, :] for t in range(tiles)]).astype(BF16)
                n_f = q * SUBLANES + j
                a = _dot(ga_ref[n_f], x)
                dst = pl.ds(n_f, kept, stride=DFT_PITCH)
                for t in range(tiles):
                    scr[2 * t][dst, :] = a[:kept, t * LANES:(t + 1) * LANES]
                    scr[2 * t + 1][dst, :] = a[kept:, t * LANES:(t + 1) * LANES]
            return carry

        lax.fori_loop(0, DFT_TILES, stage_a, 0, unroll=DFT_UNROLL // SUBLANES)

        def direct(k, carry):
            put(pl.multiple_of(k * nf, nf), fb_ref, slab(pl.multiple_of(k * DFT_PITCH, SUBLANES)))
            return carry

        lax.fori_loop(0, n_in, direct, 0, unroll=divisor(n_in))

    @pl.when(upper == 1)
    def _():
        put(0, fb_ref, slab(n_in * DFT_PITCH))

        def mirrored(k, carry):
            put(pl.multiple_of((n_in - k) * nf, nf), fbm_ref, slab(pl.multiple_of(k * DFT_PITCH, SUBLANES)))
            return carry

        lax.fori_loop(1, n_in, mirrored, 0, unroll=divisor(n_in - 1))


def hyena_filter_dft(eo, asum, tables, *, tiles=2):
    col_tiles, q_tiles, rows, _ = eo.shape
    l = q_tiles * rows
    c = D_HYENA
    groups = c // LANES // tiles
    orders = col_tiles // (2 * groups * tiles)
    ga, fb, fbm = tables['fwd_a_real'], tables['fwd_b_part'], tables['mirror_b_part']
    kept = ga.shape[1] // 2
    scr = pltpu.VMEM((kept * DFT_PITCH, LANES), F32)
    part_spec = pl.BlockSpec((None,) + fb.shape[1:], lambda o, part, j, h: (part, 0, 0))
    return pl.pallas_call(
        _filter_dft_kernel,
        grid=(orders, 2, groups, 2),
        in_specs=[
            pl.BlockSpec((tiles,) + eo.shape[1:], lambda o, part, j, h: ((2 * o + part) * groups + j, 0, 0, 0)),
            pl.BlockSpec(ga.shape, lambda o, part, j, h: (0, 0, 0)),
            part_spec,
            part_spec,
            pl.BlockSpec((1, tiles * LANES), lambda o, part, j, h: (0, o * groups + j)),
        ],
        out_specs=pl.BlockSpec((None, None, tiles, l, LANES), lambda o, part, j, h: (o, part, j, h, 0)),
        out_shape=jax.ShapeDtypeStruct((orders, 2, groups * tiles, 2 * l, LANES), F32),
        scratch_shapes=[scr] * (2 * tiles),
        compiler_params=_cparams("parallel", "parallel", "parallel", "arbitrary", vmem=VMEM_LIMIT_CONV),
        name="hyena_filter_dft",
    )(eo, ga, fb, fbm, asum)


def _lane_cat(parts):
    return parts[0] if len(parts) == 1 else jnp.concatenate(parts, axis=-1)


def _hyena_conv_kernel(x_ref, ga_ref, k_ref, fb_ref, ib_ref, gi_ref, gate_ref, bias_ref, o_ref, *scr,
                       n_a, n_b, slabs):
    nf = DFT_FAST
    tiles, qt = x_ref.shape[1], x_ref.shape[2]
    half = x_ref.shape[3] // SUBLANES
    ns = 2 * half
    step = pl.program_id(1)

    @pl.when(step < n_a)
    def _():
        for qj in range(qt * SUBLANES):
            q, j = divmod(qj, SUBLANES)
            rows = pl.ds(j, half, stride=SUBLANES)
            x = _lane_cat([jnp.concatenate([x_ref[0, t, q, rows, :], x_ref[1, t, q, rows, :]], axis=0)
                           for t in range(tiles)])
            a = _dot(ga_ref[qj], x.astype(BF16))
            dst = pl.ds(step * (qt * SUBLANES) + qj, ns, stride=DFT_PITCH)
            for t in range(tiles):
                scr[2 * t][dst, :] = a[:ns, t * LANES:(t + 1) * LANES]
                scr[2 * t + 1][dst, :] = a[ns:, t * LANES:(t + 1) * LANES]

    @pl.when((step >= n_a) & (step < n_a + n_b))
    def _():
        for s in range(slabs):
            k_a = (step - n_a) * slabs + s
            rows = pl.ds(pl.multiple_of(k_a * DFT_PITCH, SUBLANES), nf)
            a = jnp.concatenate([_lane_cat([scr[2 * t + part][rows, :] for t in range(tiles)]) for part in range(2)], axis=0)
            x = _dot(fb_ref[...], a.astype(BF16))
            xr, xi = x[:nf], x[nf:]
            kr = _lane_cat([k_ref[0, t, s * nf:(s + 1) * nf, :] for t in range(tiles)]) + bias_ref[...]
            ki = _lane_cat([k_ref[1, t, s * nf:(s + 1) * nf, :] for t in range(tiles)])
            y = jnp.concatenate([xr * kr - xi * ki, xr * ki + xi * kr], axis=0).astype(BF16)
            cc = _dot(ib_ref[...], y)
            for t in range(tiles):
                scr[2 * t][rows, :] = cc[:nf, t * LANES:(t + 1) * LANES]
                scr[2 * t + 1][rows, :] = cc[nf:, t * LANES:(t + 1) * LANES]

    @pl.when(step >= n_a + n_b)
    def _():
        first = (step - (n_a + n_b)) * (qt * SUBLANES)
        for qj in range(qt * SUBLANES):
            q, j = divmod(qj, SUBLANES)
            src = pl.ds(first + qj, ns, stride=DFT_PITCH)
            cc = jnp.concatenate([_lane_cat([scr[2 * t + part][src, :] for t in range(tiles)]) for part in range(2)], axis=0)
            y = _dot(gi_ref[qj], cc.astype(BF16))
            dst = pl.ds(j, half, stride=SUBLANES)
            for t in range(tiles):
                o_ref[0, t, q, dst, :] = y[:half, t * LANES:(t + 1) * LANES]
                o_ref[1, t, q, dst, :] = y[half:, t * LANES:(t + 1) * LANES]
        o_ref[...] = gate_ref[...] * o_ref[...]


def hyena_long_conv(u6, u_idx, gate6, gate_idx, kf, order, bias, tables, *, slabs=8, tiles=2, qt=2):
    _, _, cts, q_tiles, rows, _ = u6.shape
    ns = 2 * rows // SUBLANES
    n_a = q_tiles // qt
    n_b = ns // slabs
    ga, gi, fb, ib = tables['fwd_a'], tables['inv_a'], tables['fwd_b'], tables['inv_b']
    q_fwd = lambda s: jnp.minimum(s, n_a - 1)
    q_mid = lambda s: jnp.clip(s - n_a, 0, n_b - 1)
    q_inv = lambda s: jnp.clip(s - n_a - n_b, 0, n_a - 1)

    def seq_spec(idx, q_of):
        return pl.BlockSpec((None, 2, tiles, qt, rows, LANES), lambda c, s: (idx, 0, c, q_of(s), 0, 0))

    scr = pltpu.VMEM((ns * DFT_PITCH, LANES), F32)
    kern = functools.partial(_hyena_conv_kernel, n_a=n_a, n_b=n_b, slabs=slabs)
    return pl.pallas_call(
        kern,
        grid=(cts // tiles, 2 * n_a + n_b),
        in_specs=[
            seq_spec(u_idx, q_fwd),
            pl.BlockSpec((qt * SUBLANES,) + ga.shape[1:], lambda c, s: (q_fwd(s), 0, 0)),
            pl.BlockSpec((None, 2, tiles, slabs * DFT_FAST, LANES), lambda c, s: (order, 0, c, q_mid(s), 0)),
            pl.BlockSpec(fb.shape, lambda c, s: (0, 0)),
            pl.BlockSpec(ib.shape, lambda c, s: (0, 0)),
            pl.BlockSpec((qt * SUBLANES,) + gi.shape[1:], lambda c, s: (q_inv(s), 0, 0)),
            seq_spec(gate_idx, q_inv),
            pl.BlockSpec((1, tiles * LANES), lambda c, s: (0, c)),
        ],
        out_specs=pl.BlockSpec((2, tiles, qt, rows, LANES), lambda c, s: (0, c, q_inv(s), 0, 0)),
        out_shape=jax.ShapeDtypeStruct((2, cts, q_tiles, rows, LANES), F32),
        scratch_shapes=[scr] * (2 * tiles),
        compiler_params=_cparams("arbitrary", "arbitrary", vmem=VMEM_LIMIT_CONV),
        name="hyena_long_conv",
    )(u6, ga, kf, fb, ib, gi, gate6, bias.reshape(1, -1))


def hyena_filter_spectra(seq_len, tables, w1, b1, w2, b2, w3, freq):
    eo, asum = hyena_filter_mlp(seq_len, w1, b1, w2, b2, w3, freq, tl=FILTER_ROW_TILE)
    return hyena_filter_dft(eo, asum, tables)


def hyena_mixer(gates, kf, bias, tables):
    z = hyena_long_conv(gates, 2, gates, 1, kf, 0, bias[0], tables)
    return hyena_long_conv(z[None], 0, gates, 0, kf, 1, bias[1], tables)


def _odd_out_xattn_kernel(ys_ref, u_ref, hy_ref, x_ref, d_ref, wglu_ref, wo_ref, g_ref,
                          kv_ref, wq_ref, xwo_ref, g1_ref, g2_ref, o_ref):
    y = ys_ref[0] + d_ref[...] * u_ref[0]
    c0 = math.sqrt(2.0 / math.pi)
    gl = 0.5 * y * (1.0 + jnp.tanh(c0 * (y + 0.044715 * (y * y * y))))
    z = _dot(gl.astype(BF16), wglu_ref[...])
    s5 = gl * (1.0 / (1.0 + jnp.exp(-z)))
    hy = [_from_time_tiles(hy_ref[ct]).astype(BF16) for ct in range(hy_ref.shape[0])]
    mix = _dot(jnp.concatenate([s5.astype(BF16)] + hy, axis=-1), wo_ref[...])
    x = x_ref[0] + _rms(mix, g_ref[...])
    o_ref[0] = _xattn(x, kv_ref, wq_ref, xwo_ref, g1_ref, g2_ref)


def odd_out_xattn(ys, u_s5, hy, x, d_skip, w_glu, w_out, odd_layer, g, kv, wq, xwo, layer, g1, g2, *, tm):
    b, l, d = x.shape
    tok = lambda w: pl.BlockSpec((1, tm, w), lambda bi, i: (bi, i, 0))
    hy_spec = pl.BlockSpec((None, D_HYENA // LANES, DFT_TILES, tm // DFT_FAST * SUBLANES, LANES),
                           lambda bi, i: (bi, 0, 0, i, 0))
    return pl.pallas_call(
        _odd_out_xattn_kernel,
        grid=(b, l // tm),
        in_specs=[tok(D_S5), tok(D_S5), hy_spec, tok(d),
                  pl.BlockSpec((1, D_S5), lambda bi, i: (0, 0)),
                  _layer_spec((D_S5, D_S5), odd_layer, lambda bi, i: (0, 0)),
                  _layer_spec((d, d), odd_layer, lambda bi, i: (0, 0)),
                  pl.BlockSpec((1, d), lambda bi, i: (0, 0))] + _xattn_specs(d, layer),
        out_specs=tok(d),
        out_shape=jax.ShapeDtypeStruct(x.shape, F32),
        compiler_params=_cparams("parallel", "parallel"),
        name="odd_out_xattn",
    )(ys, u_s5, hy, x, d_skip.reshape(1, D_S5), w_glu, w_out, g.reshape(1, d),
      kv, wq, xwo, g1.reshape(1, d), g2.reshape(1, d))


def kernel(x, mem, norm_mix, norm_xattn, norm_mem, norm_mlp, xa_wq, xa_wk, xa_wv, xa_wo, mlp_w1, mlp_w2, ev_w_in, ev_pool_w, ev_pool_scale, ev_conv_w, ev_w_out, od_w_in, od_s5_lambda_re, od_s5_lambda_im, od_s5_log_dt, od_s5_b_re, od_s5_b_im, od_s5_c_re, od_s5_c_im, od_s5_d, od_s5_w_glu, od_hy_short_w, od_hy_short_b, od_hy_w1, od_hy_b1, od_hy_w2, od_hy_b2, od_hy_w3, od_hy_freq, od_hy_bias, od_w_out):
    b, l, d = x.shape
    depth = norm_mix.shape[0]
    assert b == 2, "the long convolution packs the two batch rows as one complex signal"
    tables = dft_tables(l)
    mem2d = mem.reshape(b * N_MEM, d)
    wq, wo = xa_wq.astype(BF16), xa_wo.astype(BF16)
    wkv = jnp.concatenate([xa_wk, xa_wv], axis=2).astype(BF16)
    kv = memory_kv(mem2d, norm_mem, wkv).reshape(depth, b, N_MEM, 2 * d)
    w1, w2 = mlp_w1.astype(BF16), mlp_w2.astype(BF16)
    ev_in, ev_out, ev_pool = ev_w_in.astype(BF16), ev_w_out.astype(BF16), ev_pool_w.astype(BF16)
    od_in, od_out, od_glu = od_w_in.astype(BF16), od_w_out.astype(BF16), od_s5_w_glu.astype(BF16)
    for i in range(depth):
        j = i // 2
        if i % 2 == 0:
            x = even_mixer(x, norm_mix[i, 0], ev_in, ev_pool, ev_pool_scale[j], ev_conv_w[j], ev_out, j,
                           norm_mix[i, 1], tm=ROW_TILE)
            x = xattn_block(x, kv, wq, wo, i, norm_xattn[i, 0], norm_xattn[i, 1], tm=ROW_TILE)
        else:
            u_s5, gates = odd_in(x, norm_mix[i, 0], od_in, j, od_hy_short_w[j], od_hy_short_b[j], tm=ROW_TILE)
            mats = s5_matrices(od_s5_lambda_re[j], od_s5_lambda_im[j], od_s5_log_dt[j], od_s5_b_re[j],
                               od_s5_b_im[j], od_s5_c_re[j], od_s5_c_im[j])
            ys = s5_core(u_s5, mats)
            kf = hyena_filter_spectra(l, tables, od_hy_w1[j], od_hy_b1[j], od_hy_w2[j], od_hy_b2[j],
                                      od_hy_w3[j], od_hy_freq[j])
            hy = hyena_mixer(gates, kf, od_hy_bias[j], tables)
            x = odd_out_xattn(ys, u_s5, hy, x, od_s5_d[j], od_glu, od_out, j, norm_mix[i, 1],
                              kv, wq, wo, i, norm_xattn[i, 0], norm_xattn[i, 1], tm=ROW_TILE)
        x = mlp_block(x.reshape(b * l, d), w1, w2, i, norm_mlp[i, 0], norm_mlp[i, 1],
                      tm=ROW_TILE, tf=MLP_HIDDEN_TILE).reshape(b, l, d)
    return x
```
